```python
import math
import jax
import jax.numpy as jnp
from jax import lax
import numpy as np

D_MODEL = 2048
BATCH = 2
SEQ = 8192
DEPTH = 2

CTX_LEN = 256
GRID_W = 64

MIX_W = D_MODEL // 2
N_BRANCH = 3
HY_SHORT = 3
HY_BANDS = 16
HY_EMB = 2 * HY_BANDS + 1
HY_FF = 64
HY_TARGET = 1e-2
HY_FAST_PCT = 0.3
HY_SLOW_PCT = 1.5
LRU_CONV = 4
LRU_BLOCKS = 8
LRU_BS = MIX_W // LRU_BLOCKS
LRU_C = 8.0
HG_HEADS = 8
HG_DK = MIX_W // HG_HEADS
HG_CHUNK = 64
N_EXPERTS = 16
EC_CAPACITY = 2
D_EXPERT = D_MODEL
EPS = 1e-6

HY_COLS = 3 * MIX_W
LRU_COLS = 2 * MIX_W
HG_COLS = 5 * MIX_W
GATE_COLS = N_BRANCH * D_MODEL
IN_COLS = HY_COLS + LRU_COLS + HG_COLS + GATE_COLS

kernel_name = 'hybrid_hyena_rglru_hgrn2_ecmoe_dit'


def rmsnorm(x, g):
    xf = x.astype(jnp.float32)
    y = xf * lax.rsqrt(jnp.mean(xf * xf, axis=-1, keepdims=True) + EPS)
    return y.astype(x.dtype) * g


def modulate(h, shift, scale):
    return h * (1.0 + scale) + shift


def short_conv(x, w, b, left):
    k_w = w.shape[0]
    n = x.shape[-2]
    pad = [(0, 0)] * (x.ndim - 2) + [(left, k_w - 1 - left), (0, 0)]
    xp = jnp.pad(x, pad)
    y = b
    for j in range(k_w):
        y = y + xp[..., j:j + n, :] * w[j]
    return y


def grid_conv(x, w, b, left, rows):
    bsz, n, ch = x.shape
    return short_conv(x.reshape(bsz, rows, GRID_W, ch), w, b, left).reshape(bsz, n, ch)


def hyena_filters(n, ff1, ff1_b, freq, ff2, ff2_b, ff3):
    pos = jnp.arange(n, dtype=jnp.float32)
    t = pos / (n - 1)
    fw = (2.0 * math.pi * pos / n)[:, None] * jnp.linspace(1e-4, HY_BANDS - 1, HY_BANDS, dtype=jnp.float32)[None, :]
    feats = jnp.concatenate([t[:, None], jnp.cos(fw), -jnp.sin(fw)], axis=-1)
    h = jnp.sin(freq * (feats @ ff1 + ff1_b))
    h = jnp.sin(freq * (h @ ff2 + ff2_b))
    h = (h @ ff3).astype(jnp.float32).reshape(n, 2, 2, MIX_W)
    deltas = jnp.abs(jnp.linspace(math.log(HY_TARGET) / HY_SLOW_PCT, math.log(HY_TARGET) / HY_FAST_PCT,
                                  MIX_W, dtype=jnp.float32))
    h = h * jnp.exp(-t[:, None] * deltas[None, :])[:, None, None, :]
    return h / jnp.sum(jnp.abs(h), axis=(0, 2), keepdims=True)


def long_conv(u, h_fwd, h_bwd, bias):
    n = u.shape[1]
    taps = jnp.concatenate([h_fwd, jnp.zeros_like(h_fwd[:1]), h_bwd[:0:-1]], axis=0)
    uf = u.astype(jnp.float32)
    y = jnp.fft.irfft(jnp.fft.rfft(uf, n=2 * n, axis=1) * jnp.fft.rfft(taps, axis=0)[None], n=2 * n, axis=1)[:, :n]
    return (y + uf * bias).astype(u.dtype)


def hyena_branch(z, filt, bias):
    x1, x2, v = jnp.split(z, 3, axis=-1)
    z1 = x1 * long_conv(v, filt[:, 0, 0], filt[:, 0, 1], bias[0])
    return x2 * long_conv(z1, filt[:, 1, 0], filt[:, 1, 1], bias[1])


def linear_scan(a, b, h0, reverse):
    if reverse:
        a, b = a[:, ::-1], b[:, ::-1]
    b = b.at[:, 0].add(a[:, 0] * h0)
    _, h = lax.associative_scan(lambda p, q: (p[0] * q[0], q[0] * p[1] + q[1]), (a, b), axis=1)
    final = h[:, -1]
    if reverse:
        h = h[:, ::-1]
    return h, final


def rglru_bidir(xc, p, h0s):
    bsz, n, w = xc.shape
    xb = xc.reshape(bsz, n, LRU_BLOCKS, LRU_BS)
    outs, finals = [], []
    for d in range(2):
        r = jax.nn.sigmoid(jnp.einsum('blnc,ncd->blnd', xb, p['lru_wa'][d]).reshape(bsz, n, w) + p['lru_ba'][d])
        i = jax.nn.sigmoid(jnp.einsum('blnc,ncd->blnd', xb, p['lru_wi'][d]).reshape(bsz, n, w) + p['lru_bi'][d])
        log_a = (-LRU_C * r * jax.nn.softplus(-p['lru_lambda'][d])).astype(jnp.float32)
        bterm = jnp.sqrt(-jnp.expm1(2.0 * log_a)) * (i * xc).astype(jnp.float32)
        h, hf = linear_scan(jnp.exp(log_a), bterm, h0s[d], d == 1)
        outs.append(h)
        finals.append(hf)
    return outs[0] + outs[1], (finals[0], finals[1])


def hgrn_chunk_scan(q, k, log_f, v, s0, reverse):
    if reverse:
        q, k, log_f, v = q[:, ::-1], k[:, ::-1], log_f[:, ::-1], v[:, ::-1]
    bsz, n, nh, dk = q.shape
    dv = v.shape[-1]
    nc = n // HG_CHUNK

    def chunks(t):
        return t.reshape(bsz, nc, HG_CHUNK, nh, t.shape[-1]).transpose(1, 0, 3, 2, 4)

    tril = jnp.tril(jnp.ones((HG_CHUNK, HG_CHUNK), dtype=bool))[:, :, None]

    def step(s, inp):
        qc, kc, lfc, vc = inp
        bc = jnp.cumsum(lfc, axis=2)
        diff = bc[:, :, :, None, :] - bc[:, :, None, :, :]
        dec = jnp.where(tril, jnp.exp(jnp.where(tril, diff, 0.0)), 0.0)
        att = jnp.einsum('bhtk,bhsk,bhtsk->bhts', qc, kc, dec)
        o = jnp.einsum('bhts,bhsv->bhtv', att, vc) + jnp.einsum('bhtk,bhkv->bhtv', qc * jnp.exp(bc), s)
        bl = bc[:, :, -1:]
        s = jnp.exp(bl[:, :, 0])[..., None] * s + jnp.einsum('bhsk,bhsv->bhkv', kc * jnp.exp(bl - bc), vc)
        return s, o

    s_fin, o = lax.scan(step, s0, (chunks(q), chunks(k), chunks(log_f), chunks(v)))
    o = o.transpose(1, 0, 3, 2, 4).reshape(bsz, n, nh, dv)
    if reverse:
        o = o[:, ::-1]
    return o, s_fin


def hgrn_bidir(z, lb, s0s):
    bsz, n, _ = z.shape
    q, f_fw, f_bw, v = jnp.split(z[..., :4 * MIX_W], 4, axis=-1)

    def heads(t):
        return t.astype(jnp.float32).reshape(bsz, n, HG_HEADS, HG_DK)

    lbh = lb.reshape(HG_HEADS, HG_DK)
    qh, vh = heads(q), heads(v)
    outs, states = [], []
    for d, f_pre in enumerate((f_fw, f_bw)):
        log_f = jnp.logaddexp(jnp.log(lbh), jnp.log1p(-lbh) + jax.nn.log_sigmoid(heads(f_pre)))
        o, s = hgrn_chunk_scan(qh, -jnp.expm1(log_f), log_f, vh, s0s[d], d == 1)
        outs.append(o)
        states.append(s)
    return outs[0] + outs[1], (states[0], states[1])


def hgrn_readout(o, z, norm_g):
    bsz, n, _ = z.shape
    y = rmsnorm(o, norm_g.reshape(HG_HEADS, HG_DK)).reshape(bsz, n, MIX_W).astype(z.dtype)
    return y * jax.nn.silu(z[..., 4 * MIX_W:])


def merge_branches(gate_pre, ys, w_branch, w_out):
    bsz, n, _ = gate_pre.shape
    gates = jax.nn.sigmoid(gate_pre).reshape(bsz, n, N_BRANCH, D_MODEL)
    acc = gates[:, :, 0] * (ys[0] @ w_branch[0])
    for j in range(1, N_BRANCH):
        acc = acc + gates[:, :, j] * (ys[j] @ w_branch[j])
    return acc @ w_out


def hybrid_mixer(u_lat, u_ctx, rows, lb, p, ctx_out):
    bsz = u_lat.shape[0]
    cuts = [HY_COLS, HY_COLS + LRU_COLS, HY_COLS + LRU_COLS + HG_COLS]
    hy_l, lru_l, hg_l, gate_l = jnp.split(u_lat @ p['w_in'], cuts, axis=-1)
    hy_c, lru_c, hg_c, gate_c = jnp.split(u_ctx @ p['w_in'], cuts, axis=-1)

    lx_c, lg_c = jnp.split(lru_c, 2, axis=-1)
    lx_l, lg_l = jnp.split(lru_l, 2, axis=-1)
    h0 = jnp.zeros((bsz, MIX_W), jnp.float32)
    hs_c, lru_state = rglru_bidir(short_conv(lx_c, p['lru_conv_w'], p['lru_conv_b'], 2), p, (h0, h0))
    hs_l, _ = rglru_bidir(grid_conv(lx_l, p['lru_conv_w'], p['lru_conv_b'], 2, rows), p, lru_state)
    y_lru_l = hs_l.astype(u_lat.dtype) * jax.nn.gelu(lg_l)

    s0 = jnp.zeros((bsz, HG_HEADS, HG_DK, HG_DK), jnp.float32)
    os_c, hg_state = hgrn_bidir(hg_c, lb, (s0, s0))
    os_l, _ = hgrn_bidir(hg_l, lb, hg_state)
    y_hg_l = hgrn_readout(os_l, hg_l, p['hg_norm_g'])

    filt_l = hyena_filters(u_lat.shape[1], p['hy_ff1'], p['hy_ff1_b'], p['hy_freq'], p['hy_ff2'], p['hy_ff2_b'], p['hy_ff3'])
    y_hy_l = hyena_branch(grid_conv(hy_l, p['hy_conv_w'], p['hy_conv_b'], 1, rows), filt_l, p['hy_bias'])

    y_lat = merge_branches(gate_l, (y_hy_l, y_lru_l, y_hg_l), p['w_branch'], p['w_out'])
    if not ctx_out:
        return y_lat, None

    filt_c = hyena_filters(u_ctx.shape[1], p['hy_ff1'], p['hy_ff1_b'], p['hy_freq'], p['hy_ff2'], p['hy_ff2_b'], p['hy_ff3'])
    y_hy_c = hyena_branch(short_conv(hy_c, p['hy_conv_w'], p['hy_conv_b'], 1), filt_c, p['hy_bias'])
    y_lru_c = hs_c.astype(u_ctx.dtype) * jax.nn.gelu(lg_c)
    y_hg_c = hgrn_readout(os_c, hg_c, p['hg_norm_g'])
    y_ctx = merge_branches(gate_c, (y_hy_c, y_lru_c, y_hg_c), p['w_branch'], p['w_out'])
    return y_lat, y_ctx


def ec_moe(h, w_router, w_gate, w_up, w_down):
    bsz, n, d = h.shape
    cap = EC_CAPACITY * n // N_EXPERTS
    aff = jax.nn.softmax((h @ w_router).astype(jnp.float32), axis=-1)
    g, idx = lax.top_k(jnp.swapaxes(aff, 1, 2), cap)
    flat = idx + (jnp.arange(bsz, dtype=idx.dtype) * n)[:, None, None]
    hf = h.reshape(bsz * n, d)
    xe = hf[flat]
    hid = jax.nn.silu(jnp.einsum('becd,edf->becf', xe, w_gate)) * jnp.einsum('becd,edf->becf', xe, w_up)
    ye = jnp.einsum('becf,efd->becd', hid, w_down) * g[..., None].astype(h.dtype)
    return jnp.zeros_like(hf).at[flat.reshape(-1)].add(ye.reshape(-1, d)).reshape(bsz, n, d)


def setup_inputs(seed: int = 0) -> dict:
    key = jax.random.key(seed)
    ks = jax.random.split(key, 32)
    D = D_MODEL

    def nrm(k, shape, s):
        return jax.random.normal(k, shape, jnp.float32) * s

    u = jax.random.uniform(ks[22], (DEPTH, 2, MIX_W), jnp.float32, 0.9, 0.999)
    a = u ** (1.0 / LRU_C)
    return {
        'x': nrm(ks[0], (BATCH, SEQ, D), 1.0),
        'c': nrm(ks[1], (BATCH, D), 1.0),
        'ctx': nrm(ks[2], (BATCH, CTX_LEN, D), 1.0),
        'c_ctx': nrm(ks[3], (D,), 1.0),
        'w_mod': nrm(ks[4], (DEPTH, D, 6 * D), 0.5 * D ** -0.5),
        'b_mod': nrm(ks[5], (DEPTH, 6 * D), 0.01),
        'norm_g': 1.0 + nrm(ks[6], (DEPTH, 4, D), 0.02),
        'w_in': nrm(ks[7], (DEPTH, D, IN_COLS), D ** -0.5),
        'hy_conv_w': nrm(ks[8], (DEPTH, HY_SHORT, HY_COLS), 0.5),
        'hy_conv_b': nrm(ks[9], (DEPTH, HY_COLS), 0.01),
        'hy_ff1': nrm(ks[10], (DEPTH, HY_EMB, HY_FF), HY_EMB ** -0.5),
        'hy_ff1_b': nrm(ks[11], (DEPTH, HY_FF), 0.1),
        'hy_freq': 1.0 + nrm(ks[12], (DEPTH, HY_FF), 0.1),
        'hy_ff2': nrm(ks[13], (DEPTH, HY_FF, HY_FF), HY_FF ** -0.5),
        'hy_ff2_b': nrm(ks[14], (DEPTH, HY_FF), 0.1),
        'hy_ff3': nrm(ks[15], (DEPTH, HY_FF, 4 * MIX_W), HY_FF ** -0.5),
        'hy_bias': nrm(ks[16], (DEPTH, 2, MIX_W), 0.1),
        'lru_conv_w': nrm(ks[17], (DEPTH, LRU_CONV, MIX_W), 0.5),
        'lru_conv_b': nrm(ks[18], (DEPTH, MIX_W), 0.01),
        'lru_wa': nrm(ks[19], (DEPTH, 2, LRU_BLOCKS, LRU_BS, LRU_BS), LRU_BS ** -0.5),
        'lru_ba': nrm(ks[20], (DEPTH, 2, MIX_W), 0.01),
        'lru_wi': nrm(ks[21], (DEPTH, 2, LRU_BLOCKS, LRU_BS, LRU_BS), LRU_BS ** -0.5),
        'lru_bi': nrm(ks[23], (DEPTH, 2, MIX_W), 0.01),
        'lru_lambda': jnp.log(a) - jnp.log1p(-a),
        'hg_lb_logits': nrm(ks[24], (DEPTH, MIX_W), 0.1),
        'hg_norm_g': 1.0 + nrm(ks[25], (DEPTH, MIX_W), 0.02),
        'w_branch': nrm(ks[26], (DEPTH, N_BRANCH, MIX_W, D), MIX_W ** -0.5),
        'w_out': nrm(ks[27], (DEPTH, D, D), D ** -0.5),
        'w_router': nrm(ks[28], (DEPTH, D, N_EXPERTS), D ** -0.5),
        'w_gate': nrm(ks[29], (DEPTH, N_EXPERTS, D, D_EXPERT), D ** -0.5),
        'w_up': nrm(ks[30], (DEPTH, N_EXPERTS, D, D_EXPERT), D ** -0.5),
        'w_down': nrm(ks[31], (DEPTH, N_EXPERTS, D_EXPERT, D), D_EXPERT ** -0.5),
    }


def reference(x, c, ctx, c_ctx, w_mod, b_mod, norm_g, w_in, hy_conv_w, hy_conv_b, hy_ff1, hy_ff1_b, hy_freq,
              hy_ff2, hy_ff2_b, hy_ff3, hy_bias, lru_conv_w, lru_conv_b, lru_wa, lru_ba, lru_wi, lru_bi,
              lru_lambda, hg_lb_logits, hg_norm_g, w_branch, w_out, w_router, w_gate, w_up, w_down):
    rows = x.shape[1] // GRID_W
    gam = jax.nn.softmax(hg_lb_logits.astype(jnp.float32), axis=0)
    lb_all = jnp.maximum(jnp.cumsum(gam, axis=0) - gam[:1], 0.0)
    for l in range(DEPTH):
        last = l == DEPTH - 1
        p = {
            'w_in': w_in[l], 'hy_conv_w': hy_conv_w[l], 'hy_conv_b': hy_conv_b[l],
            'hy_ff1': hy_ff1[l], 'hy_ff1_b': hy_ff1_b[l], 'hy_freq': hy_freq[l], 'hy_ff2': hy_ff2[l],
            'hy_ff2_b': hy_ff2_b[l], 'hy_ff3': hy_ff3[l], 'hy_bias': hy_bias[l],
            'lru_conv_w': lru_conv_w[l], 'lru_conv_b': lru_conv_b[l], 'lru_wa': lru_wa[l], 'lru_ba': lru_ba[l],
            'lru_wi': lru_wi[l], 'lru_bi': lru_bi[l], 'lru_lambda': lru_lambda[l],
            'hg_norm_g': hg_norm_g[l], 'w_branch': w_branch[l], 'w_out': w_out[l],
        }
        mod_l = (jax.nn.silu(c) @ w_mod[l] + b_mod[l])[:, None, :]
        mod_c = jax.nn.silu(c_ctx) @ w_mod[l] + b_mod[l]
        sh1, sc1, gt1, sh2, sc2, gt2 = jnp.split(mod_l, 6, axis=-1)
        csh1, csc1, cgt1, csh2, csc2, cgt2 = jnp.split(mod_c, 6, axis=-1)

        u_lat = modulate(rmsnorm(x, norm_g[l, 0]), sh1, sc1)
        u_ctx = modulate(rmsnorm(ctx, norm_g[l, 0]), csh1, csc1)
        y_lat, y_ctx = hybrid_mixer(u_lat, u_ctx, rows, lb_all[l], p, not last)
        x = x + gt1 * rmsnorm(y_lat, norm_g[l, 1])
        h = modulate(rmsnorm(x, norm_g[l, 2]), sh2, sc2)
        x = x + gt2 * rmsnorm(ec_moe(h, w_router[l], w_gate[l], w_up[l], w_down[l]), norm_g[l, 3])
        if not last:
            ctx = ctx + cgt1 * rmsnorm(y_ctx, norm_g[l, 1])
            hc = modulate(rmsnorm(ctx, norm_g[l, 2]), csh2, csc2)
            ctx = ctx + cgt2 * rmsnorm(ec_moe(hc, w_router[l], w_gate[l], w_up[l], w_down[l]), norm_g[l, 3])
    return x
```

```python
import functools
import math

import jax
import jax.numpy as jnp
from jax import lax
from jax.experimental import pallas as pl
from jax.experimental.pallas import tpu as pltpu

D_MODEL = 2048
DEPTH = 2
GRID_W = 64
MIX_W = D_MODEL // 2
N_BRANCH = 3
HY_BANDS = 16
HY_TARGET = 1e-2
HY_FAST_PCT = 0.3
HY_SLOW_PCT = 1.5
LRU_BLOCKS = 8
LRU_BS = MIX_W // LRU_BLOCKS
LRU_C = 8.0
HG_HEADS = 8
HG_DK = MIX_W // HG_HEADS
HG_CHUNK = 64
N_EXPERTS = 16
EC_CAPACITY = 2
EPS = 1e-6
HY_COLS = 3 * MIX_W
LRU_COLS = 2 * MIX_W
HG_COLS = 5 * MIX_W

VMEM_LIMIT_BYTES = 48 * 1024 * 1024


def _mm_kernel(a_ref, b_ref, o_ref):
    o_ref[...] = jnp.dot(a_ref[...].astype(jnp.bfloat16), b_ref[...].astype(jnp.bfloat16),
                         preferred_element_type=jnp.float32).astype(o_ref.dtype)


def _pick_tile(n, pref):
    for t in (pref, 1024, 512, 256, 128):
        if t <= n and n % t == 0:
            return t
    return n


def mm(a, b, out_dtype=jnp.float32, tm=512, tn=1024):
    m, k = a.shape
    k2, n = b.shape
    assert k == k2
    tm = _pick_tile(m, tm)
    tn = _pick_tile(n, tn)
    return pl.pallas_call(
        _mm_kernel,
        grid=(m // tm, n // tn),
        in_specs=[pl.BlockSpec((tm, k), lambda i, j: (i, 0)),
                  pl.BlockSpec((k, tn), lambda i, j: (0, j))],
        out_specs=pl.BlockSpec((tm, tn), lambda i, j: (i, j)),
        out_shape=jax.ShapeDtypeStruct((m, n), out_dtype),
        compiler_params=pltpu.CompilerParams(
            dimension_semantics=("parallel", "arbitrary"),
            vmem_limit_bytes=VMEM_LIMIT_BYTES),
    )(a, b)


def mm3(a, b):
    lead = a.shape[:-1]
    return mm(a.reshape(-1, a.shape[-1]), b).reshape(*lead, b.shape[-1])


def rmsnorm(x, g):
    xf = x.astype(jnp.float32)
    y = xf * lax.rsqrt(jnp.mean(xf * xf, axis=-1, keepdims=True) + EPS)
    return y.astype(x.dtype) * g


def modulate(h, shift, scale):
    return h * (1.0 + scale) + shift


def short_conv(x, w, b, left):
    k_w = w.shape[0]
    n = x.shape[-2]
    pad = [(0, 0)] * (x.ndim - 2) + [(left, k_w - 1 - left), (0, 0)]
    xp = jnp.pad(x, pad)
    y = b
    for j in range(k_w):
        y = y + xp[..., j:j + n, :] * w[j]
    return y


def grid_conv(x, w, b, left, rows):
    bsz, n, ch = x.shape
    return short_conv(x.reshape(bsz, rows, GRID_W, ch), w, b, left).reshape(bsz, n, ch)


def hyena_filters(n, ff1, ff1_b, freq, ff2, ff2_b, ff3):
    pos = jnp.arange(n, dtype=jnp.float32)
    t = pos / (n - 1)
    fw = (2.0 * math.pi * pos / n)[:, None] * jnp.linspace(1e-4, HY_BANDS - 1, HY_BANDS, dtype=jnp.float32)[None, :]
    feats = jnp.concatenate([t[:, None], jnp.cos(fw), -jnp.sin(fw)], axis=-1)
    h = jnp.sin(freq * (feats @ ff1 + ff1_b))
    h = jnp.sin(freq * (h @ ff2 + ff2_b))
    h = (h @ ff3).astype(jnp.float32).reshape(n, 2, 2, MIX_W)
    deltas = jnp.abs(jnp.linspace(math.log(HY_TARGET) / HY_SLOW_PCT, math.log(HY_TARGET) / HY_FAST_PCT,
                                  MIX_W, dtype=jnp.float32))
    h = h * jnp.exp(-t[:, None] * deltas[None, :])[:, None, None, :]
    return h / jnp.sum(jnp.abs(h), axis=(0, 2), keepdims=True)


def long_conv(u, h_fwd, h_bwd, bias):
    n = u.shape[1]
    taps = jnp.concatenate([h_fwd, jnp.zeros_like(h_fwd[:1]), h_bwd[:0:-1]], axis=0)
    uf = u.astype(jnp.float32)
    y = jnp.fft.irfft(jnp.fft.rfft(uf, n=2 * n, axis=1) * jnp.fft.rfft(taps, axis=0)[None], n=2 * n, axis=1)[:, :n]
    return (y + uf * bias).astype(u.dtype)


def hyena_branch(z, filt, bias):
    x1, x2, v = jnp.split(z, 3, axis=-1)
    z1 = x1 * long_conv(v, filt[:, 0, 0], filt[:, 0, 1], bias[0])
    return x2 * long_conv(z1, filt[:, 1, 0], filt[:, 1, 1], bias[1])


def linear_scan(a, b, h0, reverse):
    if reverse:
        a, b = a[:, ::-1], b[:, ::-1]
    b = b.at[:, 0].add(a[:, 0] * h0)
    _, h = lax.associative_scan(lambda p, q: (p[0] * q[0], q[0] * p[1] + q[1]), (a, b), axis=1)
    final = h[:, -1]
    if reverse:
        h = h[:, ::-1]
    return h, final


def rglru_bidir(xc, p, h0s):
    bsz, n, w = xc.shape
    xb = xc.reshape(bsz, n, LRU_BLOCKS, LRU_BS)
    outs, finals = [], []
    for d in range(2):
        r = jax.nn.sigmoid(jnp.einsum('blnc,ncd->blnd', xb, p['lru_wa'][d]).reshape(bsz, n, w) + p['lru_ba'][d])
        i = jax.nn.sigmoid(jnp.einsum('blnc,ncd->blnd', xb, p['lru_wi'][d]).reshape(bsz, n, w) + p['lru_bi'][d])
        log_a = (-LRU_C * r * jax.nn.softplus(-p['lru_lambda'][d])).astype(jnp.float32)
        bterm = jnp.sqrt(-jnp.expm1(2.0 * log_a)) * (i * xc).astype(jnp.float32)
        h, hf = linear_scan(jnp.exp(log_a), bterm, h0s[d], d == 1)
        outs.append(h)
        finals.append(hf)
    return outs[0] + outs[1], (finals[0], finals[1])


def hgrn_chunk_scan(q, k, log_f, v, s0, reverse):
    if reverse:
        q, k, log_f, v = q[:, ::-1], k[:, ::-1], log_f[:, ::-1], v[:, ::-1]
    bsz, n, nh, dk = q.shape
    dv = v.shape[-1]
    nc = n // HG_CHUNK

    def chunks(t):
        return t.reshape(bsz, nc, HG_CHUNK, nh, t.shape[-1]).transpose(1, 0, 3, 2, 4)

    tril = jnp.tril(jnp.ones((HG_CHUNK, HG_CHUNK), dtype=bool))[:, :, None]

    def step(s, inp):
        qc, kc, lfc, vc = inp
        bc = jnp.cumsum(lfc, axis=2)
        diff = bc[:, :, :, None, :] - bc[:, :, None, :, :]
        dec = jnp.where(tril, jnp.exp(jnp.where(tril, diff, 0.0)), 0.0)
        att = jnp.einsum('bhtk,bhsk,bhtsk->bhts', qc, kc, dec)
        o = jnp.einsum('bhts,bhsv->bhtv', att, vc) + jnp.einsum('bhtk,bhkv->bhtv', qc * jnp.exp(bc), s)
        bl = bc[:, :, -1:]
        s = jnp.exp(bl[:, :, 0])[..., None] * s + jnp.einsum('bhsk,bhsv->bhkv', kc * jnp.exp(bl - bc), vc)
        return s, o

    s_fin, o = lax.scan(step, s0, (chunks(q), chunks(k), chunks(log_f), chunks(v)))
    o = o.transpose(1, 0, 3, 2, 4).reshape(bsz, n, nh, dv)
    if reverse:
        o = o[:, ::-1]
    return o, s_fin


def hgrn_bidir(z, lb, s0s):
    bsz, n, _ = z.shape
    q, f_fw, f_bw, v = jnp.split(z[..., :4 * MIX_W], 4, axis=-1)

    def heads(t):
        return t.astype(jnp.float32).reshape(bsz, n, HG_HEADS, HG_DK)

    lbh = lb.reshape(HG_HEADS, HG_DK)
    qh, vh = heads(q), heads(v)
    outs, states = [], []
    for d, f_pre in enumerate((f_fw, f_bw)):
        log_f = jnp.logaddexp(jnp.log(lbh), jnp.log1p(-lbh) + jax.nn.log_sigmoid(heads(f_pre)))
        o, s = hgrn_chunk_scan(qh, -jnp.expm1(log_f), log_f, vh, s0s[d], d == 1)
        outs.append(o)
        states.append(s)
    return outs[0] + outs[1], (states[0], states[1])


def hgrn_readout(o, z, norm_g):
    bsz, n, _ = z.shape
    y = rmsnorm(o, norm_g.reshape(HG_HEADS, HG_DK)).reshape(bsz, n, MIX_W).astype(z.dtype)
    return y * jax.nn.silu(z[..., 4 * MIX_W:])


def merge_branches(gate_pre, ys, w_branch, w_out):
    bsz, n, _ = gate_pre.shape
    gates = jax.nn.sigmoid(gate_pre).reshape(bsz, n, N_BRANCH, D_MODEL)
    acc = gates[:, :, 0] * mm3(ys[0], w_branch[0])
    for j in range(1, N_BRANCH):
        acc = acc + gates[:, :, j] * mm3(ys[j], w_branch[j])
    return mm3(acc, w_out)


def hybrid_mixer(u_lat, u_ctx, rows, lb, p, ctx_out):
    bsz = u_lat.shape[0]
    cuts = [HY_COLS, HY_COLS + LRU_COLS, HY_COLS + LRU_COLS + HG_COLS]
    hy_l, lru_l, hg_l, gate_l = jnp.split(mm3(u_lat, p['w_in']), cuts, axis=-1)
    hy_c, lru_c, hg_c, gate_c = jnp.split(mm3(u_ctx, p['w_in']), cuts, axis=-1)

    lx_c, lg_c = jnp.split(lru_c, 2, axis=-1)
    lx_l, lg_l = jnp.split(lru_l, 2, axis=-1)
    h0 = jnp.zeros((bsz, MIX_W), jnp.float32)
    hs_c, lru_state = rglru_bidir(short_conv(lx_c, p['lru_conv_w'], p['lru_conv_b'], 2), p, (h0, h0))
    hs_l, _ = rglru_bidir(grid_conv(lx_l, p['lru_conv_w'], p['lru_conv_b'], 2, rows), p, lru_state)
    y_lru_l = hs_l.astype(u_lat.dtype) * jax.nn.gelu(lg_l)

    s0 = jnp.zeros((bsz, HG_HEADS, HG_DK, HG_DK), jnp.float32)
    os_c, hg_state = hgrn_bidir(hg_c, lb, (s0, s0))
    os_l, _ = hgrn_bidir(hg_l, lb, hg_state)
    y_hg_l = hgrn_readout(os_l, hg_l, p['hg_norm_g'])

    filt_l = hyena_filters(u_lat.shape[1], p['hy_ff1'], p['hy_ff1_b'], p['hy_freq'], p['hy_ff2'], p['hy_ff2_b'], p['hy_ff3'])
    y_hy_l = hyena_branch(grid_conv(hy_l, p['hy_conv_w'], p['hy_conv_b'], 1, rows), filt_l, p['hy_bias'])

    y_lat = merge_branches(gate_l, (y_hy_l, y_lru_l, y_hg_l), p['w_branch'], p['w_out'])
    if not ctx_out:
        return y_lat, None

    filt_c = hyena_filters(u_ctx.shape[1], p['hy_ff1'], p['hy_ff1_b'], p['hy_freq'], p['hy_ff2'], p['hy_ff2_b'], p['hy_ff3'])
    y_hy_c = hyena_branch(short_conv(hy_c, p['hy_conv_w'], p['hy_conv_b'], 1), filt_c, p['hy_bias'])
    y_lru_c = hs_c.astype(u_ctx.dtype) * jax.nn.gelu(lg_c)
    y_hg_c = hgrn_readout(os_c, hg_c, p['hg_norm_g'])
    y_ctx = merge_branches(gate_c, (y_hy_c, y_lru_c, y_hg_c), p['w_branch'], p['w_out'])
    return y_lat, y_ctx


def ec_moe(h, w_router, w_gate, w_up, w_down):
    bsz, n, d = h.shape
    cap = EC_CAPACITY * n // N_EXPERTS
    aff = jax.nn.softmax((h @ w_router).astype(jnp.float32), axis=-1)
    g, idx = lax.top_k(jnp.swapaxes(aff, 1, 2), cap)
    flat = idx + (jnp.arange(bsz, dtype=idx.dtype) * n)[:, None, None]
    hf = h.reshape(bsz * n, d)
    xe = hf[flat]
    yes = []
    for e in range(N_EXPERTS):
        x_e = xe[:, e].reshape(bsz * cap, d)
        hid = jax.nn.silu(mm(x_e, w_gate[e])) * mm(x_e, w_up[e])
        yes.append(mm(hid, w_down[e]).reshape(bsz, cap, d))
    ye = jnp.stack(yes, axis=1) * g[..., None].astype(h.dtype)
    return jnp.zeros_like(hf).at[flat.reshape(-1)].add(ye.reshape(-1, d)).reshape(bsz, n, d)


def kernel(x, c, ctx, c_ctx, w_mod, b_mod, norm_g, w_in, hy_conv_w, hy_conv_b, hy_ff1, hy_ff1_b, hy_freq,
           hy_ff2, hy_ff2_b, hy_ff3, hy_bias, lru_conv_w, lru_conv_b, lru_wa, lru_ba, lru_wi, lru_bi,
           lru_lambda, hg_lb_logits, hg_norm_g, w_branch, w_out, w_router, w_gate, w_up, w_down):
    rows = x.shape[1] // GRID_W
    gam = jax.nn.softmax(hg_lb_logits.astype(jnp.float32), axis=0)
    lb_all = jnp.maximum(jnp.cumsum(gam, axis=0) - gam[:1], 0.0)
    for l in range(DEPTH):
        last = l == DEPTH - 1
        p = {
            'w_in': w_in[l], 'hy_conv_w': hy_conv_w[l], 'hy_conv_b': hy_conv_b[l],
            'hy_ff1': hy_ff1[l], 'hy_ff1_b': hy_ff1_b[l], 'hy_freq': hy_freq[l], 'hy_ff2': hy_ff2[l],
            'hy_ff2_b': hy_ff2_b[l], 'hy_ff3': hy_ff3[l], 'hy_bias': hy_bias[l],
            'lru_conv_w': lru_conv_w[l], 'lru_conv_b': lru_conv_b[l], 'lru_wa': lru_wa[l], 'lru_ba': lru_ba[l],
            'lru_wi': lru_wi[l], 'lru_bi': lru_bi[l], 'lru_lambda': lru_lambda[l],
            'hg_norm_g': hg_norm_g[l], 'w_branch': w_branch[l], 'w_out': w_out[l],
        }
        mod_l = (jax.nn.silu(c) @ w_mod[l] + b_mod[l])[:, None, :]
        mod_c = jax.nn.silu(c_ctx) @ w_mod[l] + b_mod[l]
        sh1, sc1, gt1, sh2, sc2, gt2 = jnp.split(mod_l, 6, axis=-1)
        csh1, csc1, cgt1, csh2, csc2, cgt2 = jnp.split(mod_c, 6, axis=-1)

        u_lat = modulate(rmsnorm(x, norm_g[l, 0]), sh1, sc1)
        u_ctx = modulate(rmsnorm(ctx, norm_g[l, 0]), csh1, csc1)
        y_lat, y_ctx = hybrid_mixer(u_lat, u_ctx, rows, lb_all[l], p, not last)
        x = x + gt1 * rmsnorm(y_lat, norm_g[l, 1])
        h = modulate(rmsnorm(x, norm_g[l, 2]), sh2, sc2)
        x = x + gt2 * rmsnorm(ec_moe(h, w_router[l], w_gate[l], w_up[l], w_down[l]), norm_g[l, 3])
        if not last:
            ctx = ctx + cgt1 * rmsnorm(y_ctx, norm_g[l, 1])
            hc = modulate(rmsnorm(ctx, norm_g[l, 2]), csh2, csc2)
            ctx = ctx + cgt2 * rmsnorm(ec_moe(hc, w_router[l], w_gate[l], w_up[l], w_down[l]), norm_g[l, 3])
    return x
```

```python
import functools
import math

import numpy as np
import jax
import jax.numpy as jnp
from jax import lax
from jax.experimental import pallas as pl
from jax.experimental.pallas import tpu as pltpu

D_MODEL = 2048
DEPTH = 2
GRID_W = 64
MIX_W = D_MODEL // 2
N_BRANCH = 3
HY_BANDS = 16
HY_TARGET = 1e-2
HY_FAST_PCT = 0.3
HY_SLOW_PCT = 1.5
LRU_BLOCKS = 8
LRU_BS = MIX_W // LRU_BLOCKS
LRU_C = 8.0
HG_HEADS = 8
HG_DK = MIX_W // HG_HEADS
N_EXPERTS = 16
EC_CAPACITY = 2
EPS = 1e-6
HY_COL0 = 0
LRU_COL0 = 3 * MIX_W
HG_COL0 = 5 * MIX_W
GATE_COL0 = 10 * MIX_W
IN_COLS = GATE_COL0 + N_BRANCH * D_MODEL

SUBLANES = 8
LANES = 128
SEQ_T = 256
HG_C = 128
HG_LEVELS = 7
ROW_T = 512
FFT_R = 128
FFT_L = FFT_R * FFT_R
FFT_KB = 8
ROUTER_PAD = LANES
FFN_TF = 256
VMEM_LIMIT_BYTES = 48 * 1024 * 1024


def _cparams(*sem):
    return pltpu.CompilerParams(dimension_semantics=sem, vmem_limit_bytes=VMEM_LIMIT_BYTES)


def _pick_tile(n, pref):
    for t in (pref, 1024, 512, 256, 128):
        if t <= n and n % t == 0:
            return t
    return n


def _row_block(b, j, n_ctx, n_lat, reverse):
    if reverse:
        kc, kl = n_ctx - 1 - j, n_ctx + n_lat - 1 - j
    else:
        kc, kl = j, j - n_ctx
    return jnp.where(j < n_ctx, b * n_ctx + kc, 2 * n_ctx + b * n_lat + kl)


def _mm_kernel(a_ref, b_ref, o_ref):
    o_ref[...] = jnp.dot(a_ref[...].astype(jnp.bfloat16), b_ref[...].astype(jnp.bfloat16),
                         preferred_element_type=jnp.float32).astype(o_ref.dtype)


def mm(a, b, out_dtype=jnp.float32, tm=512, tn=1024):
    m, k = a.shape
    n = b.shape[1]
    tm = _pick_tile(m, tm)
    tn = _pick_tile(n, tn)
    return pl.pallas_call(
        _mm_kernel,
        grid=(m // tm, n // tn),
        in_specs=[pl.BlockSpec((tm, k), lambda i, j: (i, 0)),
                  pl.BlockSpec((k, tn), lambda i, j: (0, j))],
        out_specs=pl.BlockSpec((tm, tn), lambda i, j: (i, j)),
        out_shape=jax.ShapeDtypeStruct((m, n), out_dtype),
        compiler_params=_cparams("parallel", "arbitrary"),
    )(a, b)


def _mod_index(i, n_ctx_tiles, tiles_per_batch):
    return jnp.where(i < n_ctx_tiles, 2, (i - n_ctx_tiles) // tiles_per_batch)


def _in_proj_kernel(x_ref, g_ref, sh_ref, sc_ref, w_ref, o_ref, u_ref):
    @pl.when(pl.program_id(1) == 0)
    def _():
        x = x_ref[...]
        y = x * lax.rsqrt(jnp.mean(x * x, axis=-1, keepdims=True) + EPS) * g_ref[...]
        u_ref[...] = (y * (1.0 + sc_ref[...]) + sh_ref[...]).astype(u_ref.dtype)

    o_ref[...] = jnp.dot(u_ref[...], w_ref[...], preferred_element_type=jnp.float32).astype(o_ref.dtype)


def in_proj(x, g, sh, sc, w, n_ctx_rows, n_lat_rows, tn=1024):
    rows, d = x.shape
    n = w.shape[1]
    idx = functools.partial(_mod_index, n_ctx_tiles=2 * n_ctx_rows // ROW_T, tiles_per_batch=n_lat_rows // ROW_T)
    vec = pl.BlockSpec((None, 1, d), lambda i, j: (idx(i), 0, 0))
    return pl.pallas_call(
        _in_proj_kernel,
        grid=(rows // ROW_T, n // tn),
        in_specs=[pl.BlockSpec((ROW_T, d), lambda i, j: (i, 0)),
                  pl.BlockSpec((1, d), lambda i, j: (0, 0)), vec, vec,
                  pl.BlockSpec((d, tn), lambda i, j: (0, j))],
        out_specs=pl.BlockSpec((ROW_T, tn), lambda i, j: (i, j)),
        out_shape=jax.ShapeDtypeStruct((rows, n), jnp.bfloat16),
        scratch_shapes=[pltpu.VMEM((ROW_T, d), jnp.bfloat16)],
        compiler_params=_cparams("parallel", "arbitrary"),
    )(x, g.reshape(1, d), sh, sc, w)


def _gelu_tanh(x):
    return 0.5 * x * (1.0 + jnp.tanh(math.sqrt(2.0 / math.pi) * (x + 0.044715 * (x * x * x))))


def _conv_pos(is_ctx, shape):
    t = lax.broadcasted_iota(jnp.int32, shape, 0)
    pos = jnp.where(is_ctx, t, t & (GRID_W - 1))
    last = jnp.where(is_ctx, SEQ_T - 1, GRID_W - 1)
    return pos, last


def _lru_pass_kernel(x_ref, cw_ref, cb_ref, w_ref, gb_ref, lam_ref, *rest, reverse, final):
    if final:
        lg_ref, hprev_ref, out_ref, h_ref, a_ref, b_ref = rest
    else:
        out_ref, h_ref, a_ref, b_ref = rest
    j = pl.program_id(1)

    @pl.when(j == 0)
    def _():
        h_ref[...] = jnp.zeros_like(h_ref)

    x = x_ref[...].astype(jnp.float32)
    pos, last = _conv_pos(j == 0, x.shape)
    cw = cw_ref[...]
    xc = cb_ref[...] + cw[2:3] * x
    xc = xc + cw[0:1] * jnp.where(pos >= 2, pltpu.roll(x, 2, 0), 0.0)
    xc = xc + cw[1:2] * jnp.where(pos >= 1, pltpu.roll(x, 1, 0), 0.0)
    xc = xc + cw[3:4] * jnp.where(pos < last, pltpu.roll(x, SEQ_T - 1, 0), 0.0)

    xb = xc.astype(jnp.bfloat16)
    lam = lam_ref[...]
    sp = jnp.maximum(-lam, 0.0) + jnp.log(1.0 + jnp.exp(-jnp.abs(lam)))
    row = lax.broadcasted_iota(jnp.int32, (SEQ_T, LRU_BS), 0) & (SUBLANES - 1)
    for n in range(LRU_BLOCKS):
        sl = slice(n * LRU_BS, (n + 1) * LRU_BS)
        pre = jnp.dot(xb[:, sl], w_ref[n], preferred_element_type=jnp.float32) + gb_ref[n]
        gate = 1.0 / (1.0 + jnp.exp(-pre))
        r, i = gate[:, :LRU_BS], gate[:, LRU_BS:]
        a = jnp.exp(-LRU_C * r * sp[:, sl])
        b = jnp.sqrt(1.0 - a * a) * (i * xc[:, sl])
        for d in (1, 2, 4):
            if reverse:
                m = row < SUBLANES - d
                a_sh, b_sh = pltpu.roll(a, SEQ_T - d, 0), pltpu.roll(b, SEQ_T - d, 0)
            else:
                m = row >= d
                a_sh, b_sh = pltpu.roll(a, d, 0), pltpu.roll(b, d, 0)
            b = jnp.where(m, a * b_sh + b, b)
            a = jnp.where(m, a * a_sh, a)
        a_ref[:, sl] = a
        b_ref[:, sl] = b

    h = h_ref[...]
    n_tiles = SEQ_T // SUBLANES
    for k in range(n_tiles):
        kk = n_tiles - 1 - k if reverse else k
        rs = slice(kk * SUBLANES, (kk + 1) * SUBLANES)
        ht = a_ref[rs, :] * h + b_ref[rs, :]
        h = ht[0:1, :] if reverse else ht[SUBLANES - 1:SUBLANES, :]
        if final:
            ht = (ht + hprev_ref[rs, :]) * _gelu_tanh(lg_ref[rs, :].astype(jnp.float32))
        out_ref[rs, :] = ht.astype(out_ref.dtype)
    h_ref[...] = h


def lru_pass(z, conv_w, conv_b, wa, ba, wi, bi, lam, n_ctx_rows, n_lat_rows, reverse, h_prev=None):
    rows = z.shape[0]
    n_lat = n_lat_rows // SEQ_T
    cb = LRU_COL0 // MIX_W
    final = h_prev is not None
    rb = functools.partial(_row_block, n_ctx=n_ctx_rows // SEQ_T, n_lat=n_lat, reverse=reverse)

    def zspec(k):
        return pl.BlockSpec((SEQ_T, MIX_W), lambda b, j: (rb(b, j), cb + k))

    row_spec = pl.BlockSpec((SEQ_T, MIX_W), lambda b, j: (rb(b, j), 0))

    def const_spec(shape):
        return pl.BlockSpec(shape, lambda b, j: (0,) * len(shape))

    w2 = jnp.concatenate([wa, wi], axis=-1).astype(jnp.bfloat16)
    gb = jnp.concatenate([ba.reshape(LRU_BLOCKS, 1, LRU_BS), bi.reshape(LRU_BLOCKS, 1, LRU_BS)], axis=-1)
    in_specs = [zspec(0), const_spec((4, MIX_W)), const_spec((1, MIX_W)),
                const_spec((LRU_BLOCKS, LRU_BS, 2 * LRU_BS)), const_spec((LRU_BLOCKS, 1, 2 * LRU_BS)),
                const_spec((1, MIX_W))]
    args = [z, conv_w, conv_b.reshape(1, MIX_W), w2, gb, lam.reshape(1, MIX_W)]
    if final:
        in_specs += [zspec(1), row_spec]
        args += [z, h_prev]
    return pl.pallas_call(
        functools.partial(_lru_pass_kernel, reverse=reverse, final=final),
        grid=(2, n_ctx_rows // SEQ_T + n_lat),
        in_specs=in_specs,
        out_specs=row_spec,
        out_shape=jax.ShapeDtypeStruct((rows, MIX_W), jnp.bfloat16 if final else jnp.float32),
        scratch_shapes=[pltpu.VMEM((1, MIX_W), jnp.float32),
                        pltpu.VMEM((SEQ_T, MIX_W), jnp.float32),
                        pltpu.VMEM((SEQ_T, MIX_W), jnp.float32)],
        compiler_params=_cparams("parallel", "arbitrary"),
    )(*args)


def lru_branch(z, p, n_ctx_rows, n_lat_rows):
    def one(d, h_prev):
        return lru_pass(z, p['lru_conv_w'], p['lru_conv_b'], p['lru_wa'][d], p['lru_ba'][d], p['lru_wi'][d],
                        p['lru_bi'][d], p['lru_lambda'][d], n_ctx_rows, n_lat_rows, d == 1, h_prev)
    return one(1, one(0, None))


def _hg_level_matrix(reverse):
    t = np.arange(HG_C)[:, None]
    s = np.arange(HG_C)[None, :]
    x = t ^ s
    lv = np.where(x > 0, np.floor(np.log2(np.maximum(x, 1))).astype(np.int32), -1)
    lv = np.where(s < t, lv, -1).astype(np.int32)
    return jnp.asarray(lv.T if reverse else lv)


def _hg_pass_kernel(lv_ref, lb_ref, q_ref, f_ref, v_ref, *rest, reverse, final):
    if final:
        og_ref, oprev_ref, g_ref, out_ref, st_ref, att_ref = rest
    else:
        out_ref, st_ref, att_ref = rest

    @pl.when(pl.program_id(1) == 0)
    def _():
        st_ref[...] = jnp.zeros_like(st_ref)

    x = f_ref[...].astype(jnp.float32)
    lb = lb_ref[...]
    e = jnp.exp(-jnp.abs(x))
    inv = 1.0 / (1.0 + e)
    pos = x >= 0
    sig = jnp.where(pos, inv, e * inv)
    sigm = jnp.where(pos, e * inv, inv)
    lsig = jnp.minimum(x, 0.0) - jnp.log(1.0 + e)
    lf = jnp.where(lb > 0, jnp.log(lb + (1.0 - lb) * sig), lsig)
    kk = (1.0 - lb) * sigm
    q = q_ref[...].astype(jnp.float32)
    v = v_ref[...]
    lv = lv_ref[...]

    row = lax.broadcasted_iota(jnp.int32, (HG_C, MIX_W), 0)
    c = lf
    bt = lf
    for lvl in range(HG_LEVELS + 1):
        m = 1 << lvl
        qm = (q * jnp.exp(c)).astype(jnp.bfloat16)
        km = (kk * jnp.exp(bt - c)).astype(jnp.bfloat16)
        if lvl < HG_LEVELS:
            for h in range(HG_HEADS):
                sl = slice(h * HG_DK, (h + 1) * HG_DK)
                d = lax.dot_general(qm[:, sl], km[:, sl], (((1,), (1,)), ((), ())),
                                    preferred_element_type=jnp.float32)
                if lvl == 0:
                    att_ref[h] = jnp.where(lv == 0, d, 0.0)
                else:
                    att_ref[h] = jnp.where(lv == lvl, d, att_ref[h])
            upper = (row & m) != 0
            down = pltpu.roll(bt, m, 0)
            up = pltpu.roll(bt, HG_C - m, 0)
            if reverse:
                c = c + jnp.where(upper, 0.0, up)
            else:
                c = c + jnp.where(upper, down, 0.0)
            bt = bt + jnp.where(upper, down, up)
        else:
            tot = jnp.exp(bt[0:1, :])
            diag = q * kk
            for h in range(HG_HEADS):
                sl = slice(h * HG_DK, (h + 1) * HG_DK)
                st = st_ref[h]
                vh = v[:, sl]
                o = lax.dot_general(qm[:, sl], st.astype(jnp.bfloat16), (((1,), (1,)), ((), ())),
                                    preferred_element_type=jnp.float32)
                o = o + jnp.dot(att_ref[h].astype(jnp.bfloat16), vh, preferred_element_type=jnp.float32)
                o = o + jnp.sum(diag[:, sl], axis=-1, keepdims=True) * vh.astype(jnp.float32)
                st_ref[h] = st * tot[:, sl] + lax.dot_general(
                    vh, km[:, sl], (((0,), (0,)), ((), ())), preferred_element_type=jnp.float32)
                if final:
                    o = o + oprev_ref[:, sl]
                    y = o * lax.rsqrt(jnp.mean(o * o, axis=-1, keepdims=True) + EPS) * g_ref[:, sl]
                    og = og_ref[:, sl].astype(jnp.float32)
                    out_ref[:, sl] = (y * (og / (1.0 + jnp.exp(-og)))).astype(out_ref.dtype)
                else:
                    out_ref[:, sl] = o


def hgrn_pass(z, lb, n_ctx_rows, n_lat_rows, reverse, o_prev=None, norm_g=None):
    rows = z.shape[0]
    n_ctx, n_lat = n_ctx_rows // HG_C, n_lat_rows // HG_C
    cb = HG_COL0 // MIX_W
    final = o_prev is not None
    rb = functools.partial(_row_block, n_ctx=n_ctx, n_lat=n_lat, reverse=reverse)

    def zspec(k):
        return pl.BlockSpec((HG_C, MIX_W), lambda b, j: (rb(b, j), cb + k))

    row_spec = pl.BlockSpec((HG_C, MIX_W), lambda b, j: (rb(b, j), 0))
    vec_spec = pl.BlockSpec((1, MIX_W), lambda b, j: (0, 0))
    in_specs = [pl.BlockSpec((HG_C, HG_C), lambda b, j: (0, 0)), vec_spec,
                zspec(0), zspec(2 if reverse else 1), zspec(3)]
    args = [_hg_level_matrix(reverse), lb.reshape(1, MIX_W), z, z, z]
    if final:
        in_specs += [zspec(4), row_spec, vec_spec]
        args += [z, o_prev, norm_g.reshape(1, MIX_W)]
    return pl.pallas_call(
        functools.partial(_hg_pass_kernel, reverse=reverse, final=final),
        grid=(2, n_ctx + n_lat),
        in_specs=in_specs,
        out_specs=row_spec,
        out_shape=jax.ShapeDtypeStruct((rows, MIX_W), jnp.bfloat16 if final else jnp.float32),
        scratch_shapes=[pltpu.VMEM((HG_HEADS, HG_DK, HG_DK), jnp.float32),
                        pltpu.VMEM((HG_HEADS, HG_C, HG_C), jnp.float32)],
        compiler_params=_cparams("parallel", "arbitrary"),
    )(*args)


def hgrn_branch(z, lb, norm_g, n_ctx_rows, n_lat_rows):
    o_f = hgrn_pass(z, lb, n_ctx_rows, n_lat_rows, False)
    return hgrn_pass(z, lb, n_ctx_rows, n_lat_rows, True, o_f, norm_g)


def _hy_conv_kernel(x_ref, w_ref, b_ref, out_ref, *, is_ctx):
    x = x_ref[...].astype(jnp.float32)
    t = lax.broadcasted_iota(jnp.int32, x.shape, 0)
    pos = t if is_ctx else t & (GRID_W - 1)
    last = SEQ_T - 1 if is_ctx else GRID_W - 1
    w = w_ref[...]
    y = b_ref[...] + w[1:2] * x
    y = y + w[0:1] * jnp.where(pos >= 1, pltpu.roll(x, 1, 0), 0.0)
    y = y + w[2:3] * jnp.where(pos < last, pltpu.roll(x, SEQ_T - 1, 0), 0.0)
    for g in range(3):
        out_ref[g] = y[:, g * MIX_W:(g + 1) * MIX_W].astype(out_ref.dtype)


def hy_conv(z, conv_w, conv_b, block0, n_rows, is_ctx):
    nb = n_rows // SEQ_T
    return pl.pallas_call(
        functools.partial(_hy_conv_kernel, is_ctx=is_ctx),
        grid=(2, nb),
        in_specs=[pl.BlockSpec((SEQ_T, 3 * MIX_W), lambda b, j: (block0 + b * nb + j, 0)),
                  pl.BlockSpec((3, 3 * MIX_W), lambda b, j: (0, 0)),
                  pl.BlockSpec((1, 3 * MIX_W), lambda b, j: (0, 0))],
        out_specs=pl.BlockSpec((3, None, SEQ_T, MIX_W), lambda b, j: (0, b, j, 0)),
        out_shape=jax.ShapeDtypeStruct((3, 2, n_rows, MIX_W), jnp.bfloat16),
        compiler_params=_cparams("parallel", "arbitrary"),
    )(z, conv_w, conv_b.reshape(1, 3 * MIX_W))


def _cis(num, den):
    ang = (2.0 * math.pi / den) * (num % den).astype(jnp.float32)
    return jnp.cos(ang), jnp.sin(ang)


def fft_matrices():
    r = FFT_R
    i = jnp.arange(r, dtype=jnp.int32)
    c, s = _cis(i[:, None] * i[None, :], r)
    half = r // 2
    fa_data = jnp.concatenate([jnp.concatenate([c[:, :half], s[:, :half]], axis=1),
                               jnp.concatenate([-s[:, :half], c[:, :half]], axis=1)], axis=0)
    fa_taps = jnp.concatenate([c, -s], axis=0)
    k1, k2, s2 = i[:, None, None], i[None, :, None], i[None, None, :]
    gc, gs = _cis(s2 * (r * k2 + k1), FFT_L)
    gr, gi = gc, -gs
    m1 = jnp.concatenate([jnp.concatenate([gr, -gi], axis=2), jnp.concatenate([gi, gr], axis=2)], axis=1)
    grt, git = jnp.swapaxes(gr, 1, 2), jnp.swapaxes(gi, 1, 2)
    m2 = jnp.concatenate([jnp.concatenate([grt, git], axis=2), jnp.concatenate([-git, grt], axis=2)], axis=1)
    er, ei = c[:half] / FFT_L, s[:half] / FFT_L
    e2 = jnp.concatenate([jnp.concatenate([er, -ei], axis=1), jnp.concatenate([ei, er], axis=1)], axis=0)
    bf = jnp.bfloat16
    return dict(fa_data=fa_data.astype(bf), fa_taps=fa_taps.astype(bf), m1=m1.astype(bf), m2=m2.astype(bf),
                e2=e2.astype(bf))


def _fft_a_kernel(f_ref, x_ref, o_ref, *, two):
    x = jnp.concatenate([x_ref[0], x_ref[1]], axis=0) if two else x_ref[...]
    o_ref[...] = jnp.dot(f_ref[...], x, preferred_element_type=jnp.float32).astype(o_ref.dtype)


def fft_stage_a(fmat, x, tn):
    two = x.ndim == 3
    cols = x.shape[-1]
    xspec = (pl.BlockSpec((2, FFT_R // 2, tn), lambda j: (0, 0, j)) if two
             else pl.BlockSpec((FFT_R, tn), lambda j: (0, j)))
    return pl.pallas_call(
        functools.partial(_fft_a_kernel, two=two),
        grid=(cols // tn,),
        in_specs=[pl.BlockSpec((2 * FFT_R, FFT_R), lambda j: (0, 0)), xspec],
        out_specs=pl.BlockSpec((2 * FFT_R, tn), lambda j: (0, j)),
        out_shape=jax.ShapeDtypeStruct((2 * FFT_R, cols), jnp.bfloat16),
        compiler_params=_cparams("parallel"),
    )(fmat, x)


def _fft_mid_kernel(m1_ref, a_ref, *rest, conv):
    if conv:
        m2_ref, h_ref, o_ref = rest
    else:
        o_ref, = rest
    r = FFT_R
    for k in range(FFT_KB):
        a = a_ref[:, k].reshape(2 * r, MIX_W)
        x = jnp.dot(m1_ref[k], a, preferred_element_type=jnp.float32)
        if conv:
            h = h_ref[k].astype(jnp.float32)
            xr, xi, hr, hi = x[:r], x[r:], h[:r], h[r:]
            zc = jnp.concatenate([xr * hr - xi * hi, xr * hi + xi * hr], axis=0).astype(jnp.bfloat16)
            p = jnp.dot(m2_ref[k], zc, preferred_element_type=jnp.float32)
            o_ref[:, k] = p.reshape(2, r, MIX_W).astype(o_ref.dtype)
        else:
            o_ref[k] = x.astype(o_ref.dtype)


def fft_mid(m1, a, m2=None, h=None):
    conv = h is not None
    r = FFT_R
    mspec = pl.BlockSpec((FFT_KB, 2 * r, 2 * r), lambda j: (j, 0, 0))
    aspec = pl.BlockSpec((2, FFT_KB, r, MIX_W), lambda j: (0, j, 0, 0))
    hspec = pl.BlockSpec((FFT_KB, 2 * r, MIX_W), lambda j: (j, 0, 0))
    if conv:
        in_specs, args = [mspec, aspec, mspec, hspec], (m1, a, m2, h)
        out_specs, out_shape = aspec, jax.ShapeDtypeStruct((2, r, r, MIX_W), jnp.bfloat16)
    else:
        in_specs, args = [mspec, aspec], (m1, a)
        out_specs, out_shape = hspec, jax.ShapeDtypeStruct((r, 2 * r, MIX_W), jnp.bfloat16)
    return pl.pallas_call(
        functools.partial(_fft_mid_kernel, conv=conv),
        grid=(r // FFT_KB,),
        in_specs=in_specs, out_specs=out_specs, out_shape=out_shape,
        compiler_params=_cparams("parallel"),
    )(*args)


def _fft_out_kernel(e_ref, p_ref, mul_ref, add_ref, bias_ref, o_ref):
    y = jnp.dot(e_ref[...], p_ref[...], preferred_element_type=jnp.float32)
    half = FFT_R // 2
    bias = bias_ref[...]
    for b in range(2):
        yb = y[b * half:(b + 1) * half] + add_ref[b].astype(jnp.float32) * bias
        o_ref[b] = (mul_ref[b].astype(jnp.float32) * yb).astype(o_ref.dtype)


def fft_out(e2, p, mul, add, bias_tiled, tn):
    cols = p.shape[-1]
    half = FFT_R // 2
    bspec = pl.BlockSpec((2, half, tn), lambda j: (0, 0, j))
    return pl.pallas_call(
        _fft_out_kernel,
        grid=(cols // tn,),
        in_specs=[pl.BlockSpec((FFT_R, 2 * FFT_R), lambda j: (0, 0)),
                  pl.BlockSpec((2 * FFT_R, tn), lambda j: (0, j)), bspec, bspec,
                  pl.BlockSpec((1, tn), lambda j: (0, 0))],
        out_specs=bspec,
        out_shape=jax.ShapeDtypeStruct((2, half, cols), jnp.bfloat16),
        compiler_params=_cparams("parallel"),
    )(e2, p, mul, add, bias_tiled)


def hyena_latent(xc, taps, bias, mats):
    r = FFT_R
    cols = r * MIX_W
    tn = 8192
    x1, x2, u = (xc[g].reshape(2, r // 2, cols) for g in range(3))
    for order, gate in enumerate((x1, x2)):
        ta = fft_stage_a(mats['fa_taps'], taps[order].astype(jnp.bfloat16).reshape(r, cols), tn)
        spec = fft_mid(mats['m1'], ta.reshape(2, r, r, MIX_W))
        a = fft_stage_a(mats['fa_data'], u, tn)
        p = fft_mid(mats['m1'], a.reshape(2, r, r, MIX_W), mats['m2'], spec)
        bt = jnp.tile(bias[order].reshape(1, MIX_W), (1, tn // MIX_W))
        u = fft_out(mats['e2'], p.reshape(2 * r, cols), gate, u, bt, tn)
    return u.reshape(2, FFT_L // 2, MIX_W)


def hyena_context(xc, taps, bias):
    n = xc.shape[2]
    length = 2 * n
    k = jnp.arange(length, dtype=jnp.int32)
    c, s = _cis(k[:, None] * k[None, :], length)
    fr, fi = c[:, :n], -s[:, :n]
    m_fwd = jnp.concatenate([jnp.concatenate([fr, -fi], axis=1), jnp.concatenate([fi, fr], axis=1)], axis=0)
    m_taps = jnp.concatenate([c, -s], axis=0)
    er, ei = c[:n] / length, s[:n] / length
    m_inv = jnp.concatenate([jnp.concatenate([er, -ei], axis=1), jnp.concatenate([ei, er], axis=1)], axis=0)
    x1, x2, u = (xc[g].astype(jnp.float32) for g in range(3))
    for order, gate in enumerate((x1, x2)):
        h = mm(m_taps, taps[order])
        w = mm(m_fwd, u.reshape(length, MIX_W))
        wr, wi, hr, hi = w[:length], w[length:], h[:length], h[length:]
        zc = jnp.concatenate([wr * hr - wi * hi, wr * hi + wi * hr], axis=0)
        y = mm(m_inv, zc).reshape(2, n, MIX_W)
        u = gate * (y + u * bias[order])
    return u.astype(jnp.bfloat16)


def hyena_filters(n, ff1, ff1_b, freq, ff2, ff2_b, ff3):
    pos = jnp.arange(n, dtype=jnp.float32)
    t = pos / (n - 1)
    fw = (2.0 * math.pi * pos / n)[:, None] * jnp.linspace(1e-4, HY_BANDS - 1, HY_BANDS, dtype=jnp.float32)[None, :]
    feats = jnp.concatenate([t[:, None], jnp.cos(fw), -jnp.sin(fw)], axis=-1)
    h = jnp.sin(freq * (feats @ ff1 + ff1_b))
    h = jnp.sin(freq * (h @ ff2 + ff2_b))
    h = mm(h, ff3).reshape(n, 2, 2, MIX_W)
    deltas = jnp.abs(jnp.linspace(math.log(HY_TARGET) / HY_SLOW_PCT, math.log(HY_TARGET) / HY_FAST_PCT,
                                  MIX_W, dtype=jnp.float32))
    h = h * jnp.exp(-t[:, None] * deltas[None, :])[:, None, None, :]
    return h / jnp.sum(jnp.abs(h), axis=(0, 2), keepdims=True)


def filter_taps(filt):
    zero = jnp.zeros_like(filt[:1, 0, 0])
    return jnp.stack([jnp.concatenate([filt[:, o, 0], zero, filt[:0:-1, o, 1]], axis=0) for o in range(2)])


def _merge_kernel(yh_ref, yl_ref, yg_ref, w_ref, g0_ref, g1_ref, g2_ref, o_ref):
    acc = None
    for y_ref, g_ref, j in ((yh_ref, g0_ref, 0), (yl_ref, g1_ref, 1), (yg_ref, g2_ref, 2)):
        gate = 1.0 / (1.0 + jnp.exp(-g_ref[...].astype(jnp.float32)))
        t = gate * jnp.dot(y_ref[...], w_ref[j], preferred_element_type=jnp.float32)
        acc = t if acc is None else acc + t
    o_ref[...] = acc.astype(o_ref.dtype)


def merge_branches(z, ys, w_branch, tn=1024):
    rows = z.shape[0]
    yspec = pl.BlockSpec((ROW_T, MIX_W), lambda i, j: (i, 0))
    nb = D_MODEL // tn

    def gspec(k):
        return pl.BlockSpec((ROW_T, tn), lambda i, j: (i, GATE_COL0 // tn + k * nb + j))

    return pl.pallas_call(
        _merge_kernel,
        grid=(rows // ROW_T, nb),
        in_specs=[yspec, yspec, yspec, pl.BlockSpec((N_BRANCH, MIX_W, tn), lambda i, j: (0, 0, j)),
                  gspec(0), gspec(1), gspec(2)],
        out_specs=pl.BlockSpec((ROW_T, tn), lambda i, j: (i, j)),
        out_shape=jax.ShapeDtypeStruct((rows, D_MODEL), jnp.bfloat16),
        compiler_params=_cparams("parallel", "arbitrary"),
    )(*ys, w_branch, z, z, z)


def _out_proj_kernel(a_ref, w_ref, x_ref, g1_ref, gt_ref, g2_ref, sh_ref, sc_ref, wr_ref, xo_ref, h_ref, lg_ref):
    y = jnp.dot(a_ref[...], w_ref[...], preferred_element_type=jnp.float32)
    y = y * lax.rsqrt(jnp.mean(y * y, axis=-1, keepdims=True) + EPS) * g1_ref[...]
    x = x_ref[...] + gt_ref[...] * y
    xo_ref[...] = x
    h = x * lax.rsqrt(jnp.mean(x * x, axis=-1, keepdims=True) + EPS) * g2_ref[...]
    h = (h * (1.0 + sc_ref[...]) + sh_ref[...]).astype(jnp.bfloat16)
    h_ref[...] = h
    lg_ref[...] = jnp.dot(h, wr_ref[...], preferred_element_type=jnp.float32)


def out_proj(acc, w_out, x, g1, gt, g2, sh, sc, w_router, n_ctx_rows, n_lat_rows):
    rows, d = x.shape
    tm = ROW_T // 2
    idx = functools.partial(_mod_index, n_ctx_tiles=2 * n_ctx_rows // tm, tiles_per_batch=n_lat_rows // tm)
    row = pl.BlockSpec((tm, d), lambda i: (i, 0))
    vec = pl.BlockSpec((1, d), lambda i: (0, 0))
    mod = pl.BlockSpec((None, 1, d), lambda i: (idx(i), 0, 0))
    wr = jnp.zeros((d, ROUTER_PAD), jnp.bfloat16).at[:, :N_EXPERTS].set(w_router.astype(jnp.bfloat16))
    return pl.pallas_call(
        _out_proj_kernel,
        grid=(rows // tm,),
        in_specs=[row, pl.BlockSpec((d, d), lambda i: (0, 0)), row, vec, mod, vec, mod, mod,
                  pl.BlockSpec((d, ROUTER_PAD), lambda i: (0, 0))],
        out_specs=[row, row, pl.BlockSpec((tm, ROUTER_PAD), lambda i: (i, 0))],
        out_shape=[jax.ShapeDtypeStruct((rows, d), jnp.float32), jax.ShapeDtypeStruct((rows, d), jnp.bfloat16),
                   jax.ShapeDtypeStruct((rows, ROUTER_PAD), jnp.float32)],
        compiler_params=_cparams("parallel"),
    )(acc, w_out, x, g1.reshape(1, d), gt, g2.reshape(1, d), sh, sc, wr)


def _ffn_kernel(x_ref, wg_ref, wu_ref, wd_ref, g_ref, o_ref):
    f = pl.program_id(2)
    x = x_ref[...]
    hg = jnp.dot(x, wg_ref[...].astype(jnp.bfloat16), preferred_element_type=jnp.float32)
    hu = jnp.dot(x, wu_ref[...].astype(jnp.bfloat16), preferred_element_type=jnp.float32)
    hid = (hg / (1.0 + jnp.exp(-hg)) * hu).astype(jnp.bfloat16)
    part = jnp.dot(hid, wd_ref[...].astype(jnp.bfloat16), preferred_element_type=jnp.float32)

    @pl.when(f == 0)
    def _():
        o_ref[...] = part

    @pl.when(f > 0)
    def _():
        o_ref[...] += part

    @pl.when(f == pl.num_programs(2) - 1)
    def _():
        o_ref[...] *= g_ref[...]


def expert_ffn(xe, w_gate, w_up, w_down, g):
    ne, r, d = xe.shape
    dff = w_gate.shape[2]
    tm = r // 2
    return pl.pallas_call(
        _ffn_kernel,
        grid=(ne, r // tm, dff // FFN_TF),
        in_specs=[pl.BlockSpec((None, tm, d), lambda e, m, f: (e, m, 0)),
                  pl.BlockSpec((None, d, FFN_TF), lambda e, m, f: (e, 0, f)),
                  pl.BlockSpec((None, d, FFN_TF), lambda e, m, f: (e, 0, f)),
                  pl.BlockSpec((None, FFN_TF, d), lambda e, m, f: (e, f, 0)),
                  pl.BlockSpec((None, tm, 1), lambda e, m, f: (e, m, 0))],
        out_specs=pl.BlockSpec((None, tm, d), lambda e, m, f: (e, m, 0)),
        out_shape=jax.ShapeDtypeStruct((ne, r, d), jnp.float32),
        compiler_params=_cparams("parallel", "arbitrary", "arbitrary"),
    )(xe, w_gate, w_up, w_down, g)


def route(logits, row0, bsz, n):
    cap = EC_CAPACITY * n // N_EXPERTS
    aff = jax.nn.softmax(logits[row0:row0 + bsz * n, :N_EXPERTS].reshape(bsz, n, N_EXPERTS), axis=-1)
    g, idx = lax.top_k(jnp.swapaxes(aff, 1, 2), cap)
    flat = idx + (row0 + jnp.arange(bsz, dtype=idx.dtype) * n)[:, None, None]
    return (jnp.swapaxes(g, 0, 1).reshape(N_EXPERTS, bsz * cap),
            jnp.swapaxes(flat, 0, 1).reshape(N_EXPERTS, bsz * cap))


def ec_moe(h, logits, w_gate, w_up, w_down, segments):
    gs, flats = zip(*(route(logits, *seg) for seg in segments))
    g = jnp.concatenate(gs, axis=1)
    flat = jnp.concatenate(flats, axis=1)
    ye = expert_ffn(h[flat], w_gate, w_up, w_down, g[..., None])
    return jnp.zeros((h.shape[0], D_MODEL), jnp.float32).at[flat.reshape(-1)].add(ye.reshape(-1, D_MODEL))


def _residual_kernel(x_ref, y_ref, g_ref, gt_ref, o_ref):
    y = y_ref[...]
    y = y * lax.rsqrt(jnp.mean(y * y, axis=-1, keepdims=True) + EPS) * g_ref[...]
    o_ref[...] = x_ref[...] + gt_ref[...] * y


def residual_norm(x, y, g, gt, n_ctx_rows, n_lat_rows):
    rows, d = x.shape
    idx = functools.partial(_mod_index, n_ctx_tiles=2 * n_ctx_rows // ROW_T, tiles_per_batch=n_lat_rows // ROW_T)
    row = pl.BlockSpec((ROW_T, d), lambda i: (i, 0))
    return pl.pallas_call(
        _residual_kernel,
        grid=(rows // ROW_T,),
        in_specs=[row, row, pl.BlockSpec((1, d), lambda i: (0, 0)),
                  pl.BlockSpec((None, 1, d), lambda i: (idx(i), 0, 0))],
        out_specs=row,
        out_shape=jax.ShapeDtypeStruct((rows, d), jnp.float32),
        compiler_params=_cparams("parallel"),
    )(x, y, g.reshape(1, d), gt)


def kernel(x, c, ctx, c_ctx, w_mod, b_mod, norm_g, w_in, hy_conv_w, hy_conv_b, hy_ff1, hy_ff1_b, hy_freq,
           hy_ff2, hy_ff2_b, hy_ff3, hy_bias, lru_conv_w, lru_conv_b, lru_wa, lru_ba, lru_wi, lru_bi,
           lru_lambda, hg_lb_logits, hg_norm_g, w_branch, w_out, w_router, w_gate, w_up, w_down):
    bsz, n_lat, d = x.shape
    n_ctx = ctx.shape[1]
    assert bsz == 2 and d == D_MODEL and n_ctx == SEQ_T and 2 * n_lat == FFT_L and n_lat % ROW_T == 0
    gam = jax.nn.softmax(hg_lb_logits.astype(jnp.float32), axis=0)
    lb_all = jnp.maximum(jnp.cumsum(gam, axis=0) - gam[:1], 0.0)
    mats = fft_matrices()
    cvec = jnp.zeros((SUBLANES, d), jnp.float32).at[:bsz].set(jax.nn.silu(c)).at[bsz].set(jax.nn.silu(c_ctx))
    n_ctx_all = bsz * n_ctx
    xa = jnp.concatenate([ctx.reshape(n_ctx_all, d), x.reshape(bsz * n_lat, d)], axis=0)
    for l in range(DEPTH):
        last = l == DEPTH - 1
        p = {'lru_conv_w': lru_conv_w[l], 'lru_conv_b': lru_conv_b[l], 'lru_wa': lru_wa[l], 'lru_ba': lru_ba[l],
             'lru_wi': lru_wi[l], 'lru_bi': lru_bi[l], 'lru_lambda': lru_lambda[l]}
        mod = (mm(cvec, w_mod[l])[:bsz + 1] + b_mod[l]).reshape(bsz + 1, 1, 6, d)
        sh1, sc1, gt1, sh2, sc2, gt2 = (mod[:, :, k] for k in range(6))

        z = in_proj(xa, norm_g[l, 0], sh1, sc1, w_in[l].astype(jnp.bfloat16), n_ctx, n_lat)

        y_lru = lru_branch(z, p, n_ctx, n_lat)
        y_hg = hgrn_branch(z, lb_all[l], hg_norm_g[l], n_ctx, n_lat)

        ffp = (hy_ff1[l], hy_ff1_b[l], hy_freq[l], hy_ff2[l], hy_ff2_b[l], hy_ff3[l])
        xc_l = hy_conv(z, hy_conv_w[l], hy_conv_b[l], n_ctx_all // SEQ_T, n_lat, False)
        y_hy_l = hyena_latent(xc_l, filter_taps(hyena_filters(n_lat, *ffp)), hy_bias[l], mats)
        xc_c = hy_conv(z, hy_conv_w[l], hy_conv_b[l], 0, n_ctx, True)
        y_hy_c = hyena_context(xc_c, filter_taps(hyena_filters(n_ctx, *ffp)), hy_bias[l])
        y_hy = jnp.concatenate([y_hy_c.reshape(n_ctx_all, MIX_W), y_hy_l.reshape(bsz * n_lat, MIX_W)], axis=0)

        acc = merge_branches(z, (y_hy, y_lru, y_hg), w_branch[l].astype(jnp.bfloat16))
        xa, h, logits = out_proj(acc, w_out[l].astype(jnp.bfloat16), xa, norm_g[l, 1], gt1, norm_g[l, 2], sh2, sc2,
                                 w_router[l], n_ctx, n_lat)
        segments = [(n_ctx_all, bsz, n_lat)] if last else [(n_ctx_all, bsz, n_lat), (0, bsz, n_ctx)]
        moe = ec_moe(h, logits, w_gate[l], w_up[l], w_down[l], segments)
        xa = residual_norm(xa, moe, norm_g[l, 3], gt2, n_ctx, n_lat)
    return xa[n_ctx_all:].reshape(bsz, n_lat, d)
```

```python
import functools
import math

import numpy as np
import jax
import jax.numpy as jnp
from jax import lax
from jax.experimental import pallas as pl
from jax.experimental.pallas import tpu as pltpu

D_MODEL = 2048
DEPTH = 2
GRID_W = 64
MIX_W = D_MODEL // 2
N_BRANCH = 3
HY_BANDS = 16
HY_TARGET = 1e-2
HY_FAST_PCT = 0.3
HY_SLOW_PCT = 1.5
LRU_BLOCKS = 8
LRU_BS = MIX_W // LRU_BLOCKS
LRU_C = 8.0
HG_HEADS = 8
HG_DK = MIX_W // HG_HEADS
N_EXPERTS = 16
EC_CAPACITY = 2
EPS = 1e-6
HY_COL0 = 0
LRU_COL0 = 3 * MIX_W
HG_COL0 = 5 * MIX_W
GATE_COL0 = 10 * MIX_W
IN_COLS = GATE_COL0 + N_BRANCH * D_MODEL

SUBLANES = 8
LANES = 128
SEQ_T = 256
HG_C = 128
HG_LEVELS = 7
ROW_T = 512
IN_PROJ_TM = 1536
FFT_R = 128
FFT_L = FFT_R * FFT_R
FFT_KB = 8
ROUTER_PAD = LANES
FFN_TF = 256
VMEM_LIMIT_BYTES = 48 * 1024 * 1024


def _cparams(*sem):
    return pltpu.CompilerParams(dimension_semantics=sem, vmem_limit_bytes=VMEM_LIMIT_BYTES)


def _pick_tile(n, pref):
    for t in (pref, 1024, 512, 256, 128):
        if t <= n and n % t == 0:
            return t
    return n


def _row_block(b, j, n_ctx, n_lat, reverse):
    if reverse:
        kc, kl = n_ctx - 1 - j, n_ctx + n_lat - 1 - j
    else:
        kc, kl = j, j - n_ctx
    return jnp.where(j < n_ctx, b * n_ctx + kc, 2 * n_ctx + b * n_lat + kl)


def _mm_kernel(a_ref, b_ref, o_ref):
    o_ref[...] = jnp.dot(a_ref[...].astype(jnp.bfloat16), b_ref[...].astype(jnp.bfloat16),
                         preferred_element_type=jnp.float32).astype(o_ref.dtype)


def mm(a, b, out_dtype=jnp.float32, tm=512, tn=1024, layer=None, name="mm"):
    m, k = a.shape
    n = b.shape[-1]
    tm = _pick_tile(m, tm)
    tn = _pick_tile(n, tn)
    if layer is None:
        bspec = pl.BlockSpec((k, tn), lambda i, j: (0, j))
    else:
        bspec = pl.BlockSpec((None, k, tn), lambda i, j: (layer, 0, j))
    return pl.pallas_call(
        _mm_kernel,
        grid=(m // tm, n // tn),
        in_specs=[pl.BlockSpec((tm, k), lambda i, j: (i, 0)), bspec],
        out_specs=pl.BlockSpec((tm, tn), lambda i, j: (i, j)),
        out_shape=jax.ShapeDtypeStruct((m, n), out_dtype),
        compiler_params=_cparams("parallel", "arbitrary"),
        name=name,
    )(a, b)


def _mod_index(i, n_ctx_tiles, tiles_per_batch):
    return jnp.where(i < n_ctx_tiles, 2, (i - n_ctx_tiles) // tiles_per_batch)


def _prenorm_kernel(x_ref, g_ref, sh_ref, sc_ref, u_ref):
    x = x_ref[...]
    y = x * lax.rsqrt(jnp.mean(x * x, axis=-1, keepdims=True) + EPS) * g_ref[...]
    u_ref[...] = (y * (1.0 + sc_ref[...]) + sh_ref[...]).astype(u_ref.dtype)


def prenorm(x, g, sh, sc, n_ctx_rows, n_lat_rows):
    rows, d = x.shape
    idx = functools.partial(_mod_index, n_ctx_tiles=2 * n_ctx_rows // ROW_T, tiles_per_batch=n_lat_rows // ROW_T)
    row = pl.BlockSpec((ROW_T, d), lambda i: (i, 0))
    vec = pl.BlockSpec((None, 1, d), lambda i: (idx(i), 0, 0))
    return pl.pallas_call(
        _prenorm_kernel,
        grid=(rows // ROW_T,),
        in_specs=[row, pl.BlockSpec((1, d), lambda i: (0, 0)), vec, vec],
        out_specs=row,
        out_shape=jax.ShapeDtypeStruct((rows, d), jnp.bfloat16),
        compiler_params=_cparams("parallel"),
        name="prenorm",
    )(x, g.reshape(1, d), sh, sc)


def _gelu_tanh(x):
    return 0.5 * x * (1.0 + jnp.tanh(math.sqrt(2.0 / math.pi) * (x + 0.044715 * (x * x * x))))


def _conv_pos(is_ctx, shape):
    t = lax.broadcasted_iota(jnp.int32, shape, 0)
    pos = jnp.where(is_ctx, t, t & (GRID_W - 1))
    last = jnp.where(is_ctx, SEQ_T - 1, GRID_W - 1)
    return pos, last


def _lru_pass_kernel(x_ref, cw_ref, cb_ref, w_ref, gb_ref, lam_ref, *rest, reverse, final):
    if final:
        lg_ref, hprev_ref, out_ref, h_ref, a_ref, b_ref = rest
    else:
        out_ref, h_ref, a_ref, b_ref = rest
    j = pl.program_id(1)

    @pl.when(j == 0)
    def _():
        h_ref[...] = jnp.zeros_like(h_ref)

    x = x_ref[...].astype(jnp.float32)
    pos, last = _conv_pos(j == 0, x.shape)
    cw = cw_ref[...]
    xc = cb_ref[...] + cw[2:3] * x
    xc = xc + cw[0:1] * jnp.where(pos >= 2, pltpu.roll(x, 2, 0), 0.0)
    xc = xc + cw[1:2] * jnp.where(pos >= 1, pltpu.roll(x, 1, 0), 0.0)
    xc = xc + cw[3:4] * jnp.where(pos < last, pltpu.roll(x, SEQ_T - 1, 0), 0.0)

    xb = xc.astype(jnp.bfloat16)
    lam = lam_ref[...]
    sp = jnp.maximum(-lam, 0.0) + jnp.log(1.0 + jnp.exp(-jnp.abs(lam)))
    row = lax.broadcasted_iota(jnp.int32, (SEQ_T, LRU_BS), 0) & (SUBLANES - 1)
    for n in range(LRU_BLOCKS):
        sl = slice(n * LRU_BS, (n + 1) * LRU_BS)
        pre = jnp.dot(xb[:, sl], w_ref[n], preferred_element_type=jnp.float32) + gb_ref[n]
        gate = 1.0 / (1.0 + jnp.exp(-pre))
        r, i = gate[:, :LRU_BS], gate[:, LRU_BS:]
        a = jnp.exp(-LRU_C * r * sp[:, sl])
        b = jnp.sqrt(1.0 - a * a) * (i * xc[:, sl])
        for d in (1, 2, 4):
            if reverse:
                m = row < SUBLANES - d
                a_sh, b_sh = pltpu.roll(a, SEQ_T - d, 0), pltpu.roll(b, SEQ_T - d, 0)
            else:
                m = row >= d
                a_sh, b_sh = pltpu.roll(a, d, 0), pltpu.roll(b, d, 0)
            b = jnp.where(m, a * b_sh + b, b)
            a = jnp.where(m, a * a_sh, a)
        a_ref[:, sl] = a
        b_ref[:, sl] = b

    h = h_ref[...]
    n_tiles = SEQ_T // SUBLANES
    for k in range(n_tiles):
        kk = n_tiles - 1 - k if reverse else k
        rs = slice(kk * SUBLANES, (kk + 1) * SUBLANES)
        ht = a_ref[rs, :] * h + b_ref[rs, :]
        h = ht[0:1, :] if reverse else ht[SUBLANES - 1:SUBLANES, :]
        if final:
            ht = (ht + hprev_ref[rs, :]) * _gelu_tanh(lg_ref[rs, :].astype(jnp.float32))
        out_ref[rs, :] = ht.astype(out_ref.dtype)
    h_ref[...] = h


def lru_pass(z, conv_w, conv_b, wa, ba, wi, bi, lam, n_ctx_rows, n_lat_rows, reverse, h_prev=None):
    rows = z.shape[0]
    n_lat = n_lat_rows // SEQ_T
    cb = LRU_COL0 // MIX_W
    final = h_prev is not None
    rb = functools.partial(_row_block, n_ctx=n_ctx_rows // SEQ_T, n_lat=n_lat, reverse=reverse)

    def zspec(k):
        return pl.BlockSpec((SEQ_T, MIX_W), lambda b, j: (rb(b, j), cb + k))

    row_spec = pl.BlockSpec((SEQ_T, MIX_W), lambda b, j: (rb(b, j), 0))

    def const_spec(shape):
        return pl.BlockSpec(shape, lambda b, j: (0,) * len(shape))

    w2 = jnp.concatenate([wa, wi], axis=-1).astype(jnp.bfloat16)
    gb = jnp.concatenate([ba.reshape(LRU_BLOCKS, 1, LRU_BS), bi.reshape(LRU_BLOCKS, 1, LRU_BS)], axis=-1)
    in_specs = [zspec(0), const_spec((4, MIX_W)), const_spec((1, MIX_W)),
                const_spec((LRU_BLOCKS, LRU_BS, 2 * LRU_BS)), const_spec((LRU_BLOCKS, 1, 2 * LRU_BS)),
                const_spec((1, MIX_W))]
    args = [z, conv_w, conv_b.reshape(1, MIX_W), w2, gb, lam.reshape(1, MIX_W)]
    if final:
        in_specs += [zspec(1), row_spec]
        args += [z, h_prev]
    return pl.pallas_call(
        functools.partial(_lru_pass_kernel, reverse=reverse, final=final),
        grid=(2, n_ctx_rows // SEQ_T + n_lat),
        in_specs=in_specs,
        out_specs=row_spec,
        out_shape=jax.ShapeDtypeStruct((rows, MIX_W), jnp.bfloat16 if final else jnp.float32),
        scratch_shapes=[pltpu.VMEM((1, MIX_W), jnp.float32),
                        pltpu.VMEM((SEQ_T, MIX_W), jnp.float32),
                        pltpu.VMEM((SEQ_T, MIX_W), jnp.float32)],
        compiler_params=_cparams("parallel", "arbitrary"),
        name="lru_bwd" if reverse else "lru_fwd",
    )(*args)


def lru_branch(z, p, n_ctx_rows, n_lat_rows):
    def one(d, h_prev):
        return lru_pass(z, p['lru_conv_w'], p['lru_conv_b'], p['lru_wa'][d], p['lru_ba'][d], p['lru_wi'][d],
                        p['lru_bi'][d], p['lru_lambda'][d], n_ctx_rows, n_lat_rows, d == 1, h_prev)
    return one(1, one(0, None))


def _hg_level_matrix(reverse):
    t = np.arange(HG_C)[:, None]
    s = np.arange(HG_C)[None, :]
    x = t ^ s
    lv = np.where(x > 0, np.floor(np.log2(np.maximum(x, 1))).astype(np.int32), -1)
    lv = np.where(s < t, lv, -1).astype(np.int32)
    return jnp.asarray(lv.T if reverse else lv)


def _hg_pass_kernel(lv_ref, lb_ref, q_ref, f_ref, v_ref, *rest, reverse, final):
    if final:
        og_ref, oprev_ref, g_ref, out_ref, st_ref, att_ref = rest
    else:
        out_ref, st_ref, att_ref = rest

    @pl.when(pl.program_id(1) == 0)
    def _():
        st_ref[...] = jnp.zeros_like(st_ref)

    x = f_ref[...].astype(jnp.float32)
    lb = lb_ref[...]
    e = jnp.exp(-jnp.abs(x))
    inv = 1.0 / (1.0 + e)
    pos = x >= 0
    sig = jnp.where(pos, inv, e * inv)
    sigm = jnp.where(pos, e * inv, inv)
    lsig = jnp.minimum(x, 0.0) - jnp.log(1.0 + e)
    lf = jnp.where(lb > 0, jnp.log(lb + (1.0 - lb) * sig), lsig)
    kk = (1.0 - lb) * sigm
    q = q_ref[...].astype(jnp.float32)
    v = v_ref[...]
    lv = lv_ref[...]

    row = lax.broadcasted_iota(jnp.int32, (HG_C, MIX_W), 0)
    c = lf
    bt = lf
    for lvl in range(HG_LEVELS + 1):
        m = 1 << lvl
        qm = (q * jnp.exp(c)).astype(jnp.bfloat16)
        km = (kk * jnp.exp(bt - c)).astype(jnp.bfloat16)
        if lvl < HG_LEVELS:
            for h in range(HG_HEADS):
                sl = slice(h * HG_DK, (h + 1) * HG_DK)
                d = lax.dot_general(qm[:, sl], km[:, sl], (((1,), (1,)), ((), ())),
                                    preferred_element_type=jnp.float32)
                if lvl == 0:
                    att_ref[h] = jnp.where(lv == 0, d, 0.0)
                else:
                    att_ref[h] = jnp.where(lv == lvl, d, att_ref[h])
            upper = (row & m) != 0
            down = pltpu.roll(bt, m, 0)
            up = pltpu.roll(bt, HG_C - m, 0)
            if reverse:
                c = c + jnp.where(upper, 0.0, up)
            else:
                c = c + jnp.where(upper, down, 0.0)
            bt = bt + jnp.where(upper, down, up)
        else:
            tot = jnp.exp(bt[0:1, :])
            diag = q * kk
            for h in range(HG_HEADS):
                sl = slice(h * HG_DK, (h + 1) * HG_DK)
                st = st_ref[h]
                vh = v[:, sl]
                o = lax.dot_general(qm[:, sl], st.astype(jnp.bfloat16), (((1,), (1,)), ((), ())),
                                    preferred_element_type=jnp.float32)
                o = o + jnp.dot(att_ref[h].astype(jnp.bfloat16), vh, preferred_element_type=jnp.float32)
                o = o + jnp.sum(diag[:, sl], axis=-1, keepdims=True) * vh.astype(jnp.float32)
                st_ref[h] = st * tot[:, sl] + lax.dot_general(
                    vh, km[:, sl], (((0,), (0,)), ((), ())), preferred_element_type=jnp.float32)
                if final:
                    o = o + oprev_ref[:, sl]
                    y = o * lax.rsqrt(jnp.mean(o * o, axis=-1, keepdims=True) + EPS) * g_ref[:, sl]
                    og = og_ref[:, sl].astype(jnp.float32)
                    out_ref[:, sl] = (y * (og / (1.0 + jnp.exp(-og)))).astype(out_ref.dtype)
                else:
                    out_ref[:, sl] = o


def hgrn_pass(z, lb, n_ctx_rows, n_lat_rows, reverse, o_prev=None, norm_g=None):
    rows = z.shape[0]
    n_ctx, n_lat = n_ctx_rows // HG_C, n_lat_rows // HG_C
    cb = HG_COL0 // MIX_W
    final = o_prev is not None
    rb = functools.partial(_row_block, n_ctx=n_ctx, n_lat=n_lat, reverse=reverse)

    def zspec(k):
        return pl.BlockSpec((HG_C, MIX_W), lambda b, j: (rb(b, j), cb + k))

    row_spec = pl.BlockSpec((HG_C, MIX_W), lambda b, j: (rb(b, j), 0))
    vec_spec = pl.BlockSpec((1, MIX_W), lambda b, j: (0, 0))
    in_specs = [pl.BlockSpec((HG_C, HG_C), lambda b, j: (0, 0)), vec_spec,
                zspec(0), zspec(2 if reverse else 1), zspec(3)]
    args = [_hg_level_matrix(reverse), lb.reshape(1, MIX_W), z, z, z]
    if final:
        in_specs += [zspec(4), row_spec, vec_spec]
        args += [z, o_prev, norm_g.reshape(1, MIX_W)]
    return pl.pallas_call(
        functools.partial(_hg_pass_kernel, reverse=reverse, final=final),
        grid=(2, n_ctx + n_lat),
        in_specs=in_specs,
        out_specs=row_spec,
        out_shape=jax.ShapeDtypeStruct((rows, MIX_W), jnp.bfloat16 if final else jnp.float32),
        scratch_shapes=[pltpu.VMEM((HG_HEADS, HG_DK, HG_DK), jnp.float32),
                        pltpu.VMEM((HG_HEADS, HG_C, HG_C), jnp.float32)],
        compiler_params=_cparams("parallel", "arbitrary"),
        name="hgrn_bwd" if reverse else "hgrn_fwd",
    )(*args)


def hgrn_branch(z, lb, norm_g, n_ctx_rows, n_lat_rows):
    o_f = hgrn_pass(z, lb, n_ctx_rows, n_lat_rows, False)
    return hgrn_pass(z, lb, n_ctx_rows, n_lat_rows, True, o_f, norm_g)


def _hy_conv_kernel(x_ref, w_ref, b_ref, out_ref, *, is_ctx):
    x = x_ref[...].astype(jnp.float32)
    t = lax.broadcasted_iota(jnp.int32, x.shape, 0)
    pos = t if is_ctx else t & (GRID_W - 1)
    last = SEQ_T - 1 if is_ctx else GRID_W - 1
    w = w_ref[...]
    y = b_ref[...] + w[1:2] * x
    y = y + w[0:1] * jnp.where(pos >= 1, pltpu.roll(x, 1, 0), 0.0)
    y = y + w[2:3] * jnp.where(pos < last, pltpu.roll(x, SEQ_T - 1, 0), 0.0)
    for g in range(3):
        out_ref[g] = y[:, g * MIX_W:(g + 1) * MIX_W].astype(out_ref.dtype)


def hy_conv(z, conv_w, conv_b, block0, n_rows, is_ctx):
    nb = n_rows // SEQ_T
    return pl.pallas_call(
        functools.partial(_hy_conv_kernel, is_ctx=is_ctx),
        grid=(2, nb),
        in_specs=[pl.BlockSpec((SEQ_T, 3 * MIX_W), lambda b, j: (block0 + b * nb + j, 0)),
                  pl.BlockSpec((3, 3 * MIX_W), lambda b, j: (0, 0)),
                  pl.BlockSpec((1, 3 * MIX_W), lambda b, j: (0, 0))],
        out_specs=pl.BlockSpec((3, None, SEQ_T, MIX_W), lambda b, j: (0, b, j, 0)),
        out_shape=jax.ShapeDtypeStruct((3, 2, n_rows, MIX_W), jnp.bfloat16),
        compiler_params=_cparams("parallel", "arbitrary"),
        name="hy_conv_ctx" if is_ctx else "hy_conv_lat",
    )(z, conv_w, conv_b.reshape(1, 3 * MIX_W))


def _cis(num, den):
    ang = (2.0 * math.pi / den) * (num % den).astype(jnp.float32)
    return jnp.cos(ang), jnp.sin(ang)


def fft_matrices():
    r = FFT_R
    i = jnp.arange(r, dtype=jnp.int32)
    c, s = _cis(i[:, None] * i[None, :], r)
    half = r // 2
    fa_data = jnp.concatenate([jnp.concatenate([c[:, :half], s[:, :half]], axis=1),
                               jnp.concatenate([-s[:, :half], c[:, :half]], axis=1)], axis=0)
    fa_taps = jnp.concatenate([c, -s], axis=0)
    k1, k2, s2 = i[:, None, None], i[None, :, None], i[None, None, :]
    gc, gs = _cis(s2 * (r * k2 + k1), FFT_L)
    gr, gi = gc, -gs
    m1 = jnp.concatenate([jnp.concatenate([gr, -gi], axis=2), jnp.concatenate([gi, gr], axis=2)], axis=1)
    grt, git = jnp.swapaxes(gr, 1, 2), jnp.swapaxes(gi, 1, 2)
    m2 = jnp.concatenate([jnp.concatenate([grt, git], axis=2), jnp.concatenate([-git, grt], axis=2)], axis=1)
    er, ei = c[:half] / FFT_L, s[:half] / FFT_L
    e2 = jnp.concatenate([jnp.concatenate([er, -ei], axis=1), jnp.concatenate([ei, er], axis=1)], axis=0)
    bf = jnp.bfloat16
    return dict(fa_data=fa_data.astype(bf), fa_taps=fa_taps.astype(bf), m1=m1.astype(bf), m2=m2.astype(bf),
                e2=e2.astype(bf))


def _fft_a_kernel(f_ref, x_ref, o_ref, *, two):
    x = jnp.concatenate([x_ref[0], x_ref[1]], axis=0) if two else x_ref[...]
    o_ref[...] = jnp.dot(f_ref[...], x, preferred_element_type=jnp.float32).astype(o_ref.dtype)


def fft_stage_a(fmat, x, tn):
    two = x.ndim == 3
    cols = x.shape[-1]
    xspec = (pl.BlockSpec((2, FFT_R // 2, tn), lambda j: (0, 0, j)) if two
             else pl.BlockSpec((FFT_R, tn), lambda j: (0, j)))
    return pl.pallas_call(
        functools.partial(_fft_a_kernel, two=two),
        grid=(cols // tn,),
        in_specs=[pl.BlockSpec((2 * FFT_R, FFT_R), lambda j: (0, 0)), xspec],
        out_specs=pl.BlockSpec((2 * FFT_R, tn), lambda j: (0, j)),
        out_shape=jax.ShapeDtypeStruct((2 * FFT_R, cols), jnp.bfloat16),
        compiler_params=_cparams("parallel"),
        name="fft_stage_a",
    )(fmat, x)


def _fft_mid_kernel(m1_ref, a_ref, *rest, conv):
    if conv:
        m2_ref, h_ref, o_ref = rest
    else:
        s_ref, o_ref = rest
    r = FFT_R
    for k in range(FFT_KB):
        a = a_ref[:, k].reshape(2 * r, MIX_W)
        x = jnp.dot(m1_ref[k], a, preferred_element_type=jnp.float32)
        if conv:
            h = h_ref[k].astype(jnp.float32)
            xr, xi, hr, hi = x[:r], x[r:], h[:r], h[r:]
            zc = jnp.concatenate([xr * hr - xi * hi, xr * hi + xi * hr], axis=0).astype(jnp.bfloat16)
            p = jnp.dot(m2_ref[k], zc, preferred_element_type=jnp.float32)
            o_ref[:, k] = p.reshape(2, r, MIX_W).astype(o_ref.dtype)
        else:
            o_ref[k] = (x * s_ref[...]).astype(o_ref.dtype)


def fft_mid(m1, a, m2=None, h=None, scale=None):
    conv = h is not None
    r = FFT_R
    mspec = pl.BlockSpec((FFT_KB, 2 * r, 2 * r), lambda j: (j, 0, 0))
    aspec = pl.BlockSpec((2, FFT_KB, r, MIX_W), lambda j: (0, j, 0, 0))
    hspec = pl.BlockSpec((FFT_KB, 2 * r, MIX_W), lambda j: (j, 0, 0))
    if conv:
        in_specs, args = [mspec, aspec, mspec, hspec], (m1, a, m2, h)
        out_specs, out_shape = aspec, jax.ShapeDtypeStruct((2, r, r, MIX_W), jnp.bfloat16)
    else:
        in_specs, args = [mspec, aspec, pl.BlockSpec((1, MIX_W), lambda j: (0, 0))], (m1, a, scale)
        out_specs, out_shape = hspec, jax.ShapeDtypeStruct((r, 2 * r, MIX_W), jnp.bfloat16)
    return pl.pallas_call(
        functools.partial(_fft_mid_kernel, conv=conv),
        grid=(r // FFT_KB,),
        in_specs=in_specs, out_specs=out_specs, out_shape=out_shape,
        compiler_params=_cparams("parallel"),
        name="fft_mid_conv" if conv else "fft_mid_filter",
    )(*args)


def _fft_out_kernel(e_ref, p_ref, mul_ref, add_ref, bias_ref, o_ref):
    y = jnp.dot(e_ref[...], p_ref[...], preferred_element_type=jnp.float32)
    half = FFT_R // 2
    bias = bias_ref[...]
    for b in range(2):
        yb = y[b * half:(b + 1) * half] + add_ref[b].astype(jnp.float32) * bias
        o_ref[b] = (mul_ref[b].astype(jnp.float32) * yb).astype(o_ref.dtype)


def fft_out(e2, p, mul, add, bias_tiled, tn):
    cols = p.shape[-1]
    half = FFT_R // 2
    bspec = pl.BlockSpec((2, half, tn), lambda j: (0, 0, j))
    return pl.pallas_call(
        _fft_out_kernel,
        grid=(cols // tn,),
        in_specs=[pl.BlockSpec((FFT_R, 2 * FFT_R), lambda j: (0, 0)),
                  pl.BlockSpec((2 * FFT_R, tn), lambda j: (0, j)), bspec, bspec,
                  pl.BlockSpec((1, tn), lambda j: (0, 0))],
        out_specs=bspec,
        out_shape=jax.ShapeDtypeStruct((2, half, cols), jnp.bfloat16),
        compiler_params=_cparams("parallel"),
        name="fft_out",
    )(e2, p, mul, add, bias_tiled)


def hyena_latent(xc, taps, inv_norm, bias, mats):
    r = FFT_R
    cols = r * MIX_W
    tn = 8192
    x1, x2, u = (xc[g].reshape(2, r // 2, cols) for g in range(3))
    for order, gate in enumerate((x1, x2)):
        ta = fft_stage_a(mats['fa_taps'], taps[order].reshape(r, cols), tn)
        spec = fft_mid(mats['m1'], ta.reshape(2, r, r, MIX_W), scale=inv_norm[order])
        a = fft_stage_a(mats['fa_data'], u, tn)
        p = fft_mid(mats['m1'], a.reshape(2, r, r, MIX_W), mats['m2'], spec)
        bt = jnp.tile(bias[order].reshape(1, MIX_W), (1, tn // MIX_W))
        u = fft_out(mats['e2'], p.reshape(2 * r, cols), gate, u, bt, tn)
    return u.reshape(2, FFT_L // 2, MIX_W)


def hyena_context(xc, taps, inv_norm, bias):
    n = xc.shape[2]
    length = 2 * n
    k = jnp.arange(length, dtype=jnp.int32)
    c, s = _cis(k[:, None] * k[None, :], length)
    fr, fi = c[:, :n], -s[:, :n]
    m_fwd = jnp.concatenate([jnp.concatenate([fr, -fi], axis=1), jnp.concatenate([fi, fr], axis=1)], axis=0)
    m_taps = jnp.concatenate([c, -s], axis=0)
    er, ei = c[:n] / length, s[:n] / length
    m_inv = jnp.concatenate([jnp.concatenate([er, -ei], axis=1), jnp.concatenate([ei, er], axis=1)], axis=0)
    x1, x2, u = (xc[g].astype(jnp.float32) for g in range(3))
    for order, gate in enumerate((x1, x2)):
        h = mm(m_taps, taps[order], name="ctx_dft_taps") * inv_norm[order]
        w = mm(m_fwd, u.reshape(length, MIX_W), name="ctx_dft_fwd")
        wr, wi, hr, hi = w[:length], w[length:], h[:length], h[length:]
        zc = jnp.concatenate([wr * hr - wi * hi, wr * hi + wi * hr], axis=0)
        y = mm(m_inv, zc, name="ctx_dft_inv").reshape(2, n, MIX_W)
        u = gate * (y + u * bias[order])
    return u.astype(jnp.bfloat16)


def hyena_features(n, ff1, ff1_b, freq, ff2, ff2_b):
    pos = jnp.arange(n, dtype=jnp.float32)
    t = pos / (n - 1)
    fw = (2.0 * math.pi * pos / n)[:, None] * jnp.linspace(1e-4, HY_BANDS - 1, HY_BANDS, dtype=jnp.float32)[None, :]
    feats = jnp.concatenate([t[:, None], jnp.cos(fw), -jnp.sin(fw)], axis=-1)
    h = jnp.sin(freq * (feats @ ff1 + ff1_b))
    h = jnp.sin(freq * (h @ ff2 + ff2_b))
    return jnp.stack([h, jnp.concatenate([h[:1], h[:0:-1]], axis=0)])


def _filter_kernel(feat_ref, w_ref, delta_ref, taps_ref, sum_ref, *, n, tm):
    dr = pl.program_id(1)
    i = pl.program_id(2)
    row = i * tm + lax.broadcasted_iota(jnp.int32, (tm, MIX_W), 0)
    lag = jnp.where(dr == 0, row, jnp.where(row == 0, 0, n - row))
    t = lag.astype(jnp.float32) / (n - 1)
    h = jnp.dot(feat_ref[...].astype(jnp.bfloat16), w_ref[...].astype(jnp.bfloat16),
                preferred_element_type=jnp.float32)
    h = h * jnp.exp(-t * delta_ref[...])

    @pl.when((dr == 0) & (i == 0))
    def _():
        sum_ref[...] = jnp.zeros_like(sum_ref)

    sum_ref[...] += jnp.sum(jnp.abs(h), axis=0, keepdims=True)
    taps_ref[...] = jnp.where((dr == 0) | (row != 0), h, 0.0).astype(taps_ref.dtype)


def hyena_filter_taps(feats, ff3):
    _, n, nf = feats.shape
    tm = min(n, 1024)
    nb = n // tm
    w = ff3.reshape(nf, 2, 2, MIX_W).transpose(1, 2, 0, 3)
    deltas = jnp.abs(jnp.linspace(math.log(HY_TARGET) / HY_SLOW_PCT, math.log(HY_TARGET) / HY_FAST_PCT,
                                  MIX_W, dtype=jnp.float32)).reshape(1, MIX_W)
    return pl.pallas_call(
        functools.partial(_filter_kernel, n=n, tm=tm),
        grid=(2, 2, nb),
        in_specs=[pl.BlockSpec((None, tm, nf), lambda o, dr, i: (dr, i, 0)),
                  pl.BlockSpec((None, None, nf, MIX_W), lambda o, dr, i: (o, dr, 0, 0)),
                  pl.BlockSpec((1, MIX_W), lambda o, dr, i: (0, 0))],
        out_specs=[pl.BlockSpec((None, tm, MIX_W), lambda o, dr, i: (o, dr * nb + i, 0)),
                   pl.BlockSpec((None, 1, MIX_W), lambda o, dr, i: (o, 0, 0))],
        out_shape=[jax.ShapeDtypeStruct((2, 2 * n, MIX_W), jnp.bfloat16),
                   jax.ShapeDtypeStruct((2, 1, MIX_W), jnp.float32)],
        compiler_params=_cparams("arbitrary", "arbitrary", "arbitrary"),
        name="hyena_filter",
    )(feats, w, deltas)


def _merge_kernel(yh_ref, yl_ref, yg_ref, w_ref, g0_ref, g1_ref, g2_ref, o_ref):
    acc = None
    for y_ref, g_ref, j in ((yh_ref, g0_ref, 0), (yl_ref, g1_ref, 1), (yg_ref, g2_ref, 2)):
        gate = 1.0 / (1.0 + jnp.exp(-g_ref[...].astype(jnp.float32)))
        t = gate * jnp.dot(y_ref[...], w_ref[j], preferred_element_type=jnp.float32)
        acc = t if acc is None else acc + t
    o_ref[...] = acc.astype(o_ref.dtype)


def merge_branches(z, ys, w_branch, layer):
    rows = z.shape[0]
    tm = ROW_T // 2
    yspec = pl.BlockSpec((tm, MIX_W), lambda i: (i, 0))

    def gspec(k):
        return pl.BlockSpec((tm, D_MODEL), lambda i: (i, GATE_COL0 // D_MODEL + k))

    return pl.pallas_call(
        _merge_kernel,
        grid=(rows // tm,),
        in_specs=[yspec, yspec, yspec,
                  pl.BlockSpec((None, N_BRANCH, MIX_W, D_MODEL), lambda i: (layer, 0, 0, 0)),
                  gspec(0), gspec(1), gspec(2)],
        out_specs=pl.BlockSpec((tm, D_MODEL), lambda i: (i, 0)),
        out_shape=jax.ShapeDtypeStruct((rows, D_MODEL), jnp.bfloat16),
        compiler_params=_cparams("parallel"),
        name="merge",
    )(*ys, w_branch, z, z, z)


def _out_proj_kernel(a_ref, w_ref, x_ref, g1_ref, gt_ref, g2_ref, sh_ref, sc_ref, wr_ref, xo_ref, h_ref, lg_ref):
    y = jnp.dot(a_ref[...], w_ref[...], preferred_element_type=jnp.float32)
    y = y * lax.rsqrt(jnp.mean(y * y, axis=-1, keepdims=True) + EPS) * g1_ref[...]
    x = x_ref[...] + gt_ref[...] * y
    xo_ref[...] = x
    h = x * lax.rsqrt(jnp.mean(x * x, axis=-1, keepdims=True) + EPS) * g2_ref[...]
    h = (h * (1.0 + sc_ref[...]) + sh_ref[...]).astype(jnp.bfloat16)
    h_ref[...] = h
    lg_ref[...] = jnp.dot(h, wr_ref[...], preferred_element_type=jnp.float32)


def out_proj(acc, w_out, layer, x, g1, gt, g2, sh, sc, w_router, n_ctx_rows, n_lat_rows):
    rows, d = x.shape
    tm = ROW_T // 2
    idx = functools.partial(_mod_index, n_ctx_tiles=2 * n_ctx_rows // tm, tiles_per_batch=n_lat_rows // tm)
    row = pl.BlockSpec((tm, d), lambda i: (i, 0))
    vec = pl.BlockSpec((1, d), lambda i: (0, 0))
    mod = pl.BlockSpec((None, 1, d), lambda i: (idx(i), 0, 0))
    wr = jnp.zeros((d, ROUTER_PAD), jnp.bfloat16).at[:, :N_EXPERTS].set(w_router.astype(jnp.bfloat16))
    return pl.pallas_call(
        _out_proj_kernel,
        grid=(rows // tm,),
        in_specs=[row, pl.BlockSpec((None, d, d), lambda i: (layer, 0, 0)), row, vec, mod, vec, mod, mod,
                  pl.BlockSpec((d, ROUTER_PAD), lambda i: (0, 0))],
        out_specs=[row, row, pl.BlockSpec((tm, ROUTER_PAD), lambda i: (i, 0))],
        out_shape=[jax.ShapeDtypeStruct((rows, d), jnp.float32), jax.ShapeDtypeStruct((rows, d), jnp.bfloat16),
                   jax.ShapeDtypeStruct((rows, ROUTER_PAD), jnp.float32)],
        compiler_params=_cparams("parallel"),
        name="out_proj",
    )(acc, w_out, x, g1.reshape(1, d), gt, g2.reshape(1, d), sh, sc, wr)


def _ffn_up_kernel(x_ref, wg_ref, wu_ref, h_ref):
    x = x_ref[...]
    hg = jnp.dot(x, wg_ref[...].astype(jnp.bfloat16), preferred_element_type=jnp.float32)
    hu = jnp.dot(x, wu_ref[...].astype(jnp.bfloat16), preferred_element_type=jnp.float32)
    h_ref[...] = (hg / (1.0 + jnp.exp(-hg)) * hu).astype(h_ref.dtype)


def _ffn_down_kernel(h_ref, wd_ref, g_ref, o_ref):
    y = jnp.dot(h_ref[...], wd_ref[...].astype(jnp.bfloat16), preferred_element_type=jnp.float32)
    o_ref[...] = y * g_ref[...]


def expert_ffn(xe, w_gate, w_up, w_down, layer, g):
    ne, r, d = xe.shape
    dff = w_gate.shape[-1]
    hid = pl.pallas_call(
        _ffn_up_kernel,
        grid=(ne, dff // FFN_TF),
        in_specs=[pl.BlockSpec((None, r, d), lambda e, f: (e, 0, 0)),
                  pl.BlockSpec((None, None, d, FFN_TF), lambda e, f: (layer, e, 0, f)),
                  pl.BlockSpec((None, None, d, FFN_TF), lambda e, f: (layer, e, 0, f))],
        out_specs=pl.BlockSpec((None, r, FFN_TF), lambda e, f: (e, 0, f)),
        out_shape=jax.ShapeDtypeStruct((ne, r, dff), jnp.bfloat16),
        compiler_params=_cparams("parallel", "arbitrary"),
        name="ffn_up",
    )(xe, w_gate, w_up)
    return pl.pallas_call(
        _ffn_down_kernel,
        grid=(ne, d // FFN_TF),
        in_specs=[pl.BlockSpec((None, r, dff), lambda e, j: (e, 0, 0)),
                  pl.BlockSpec((None, None, dff, FFN_TF), lambda e, j: (layer, e, 0, j)),
                  pl.BlockSpec((None, r, 1), lambda e, j: (e, 0, 0))],
        out_specs=pl.BlockSpec((None, r, FFN_TF), lambda e, j: (e, 0, j)),
        out_shape=jax.ShapeDtypeStruct((ne, r, d), jnp.float32),
        compiler_params=_cparams("parallel", "arbitrary"),
        name="ffn_down",
    )(hid, w_down, g)


def route(logits, row0, bsz, n):
    cap = EC_CAPACITY * n // N_EXPERTS
    aff = jax.nn.softmax(logits[row0:row0 + bsz * n, :N_EXPERTS].reshape(bsz, n, N_EXPERTS), axis=-1)
    g, idx = lax.top_k(jnp.swapaxes(aff, 1, 2), cap)
    flat = idx + (row0 + jnp.arange(bsz, dtype=idx.dtype) * n)[:, None, None]
    return (jnp.swapaxes(g, 0, 1).reshape(N_EXPERTS, bsz * cap),
            jnp.swapaxes(flat, 0, 1).reshape(N_EXPERTS, bsz * cap))


def ec_moe(h, logits, w_gate, w_up, w_down, layer, segments):
    gs, flats = zip(*(route(logits, *seg) for seg in segments))
    g = jnp.concatenate(gs, axis=1)
    flat = jnp.concatenate(flats, axis=1)
    ye = expert_ffn(h[flat], w_gate, w_up, w_down, layer, g[..., None])
    return jnp.zeros((h.shape[0], D_MODEL), jnp.float32).at[flat.reshape(-1)].add(ye.reshape(-1, D_MODEL))


def _residual_kernel(x_ref, y_ref, g_ref, gt_ref, o_ref):
    y = y_ref[...]
    y = y * lax.rsqrt(jnp.mean(y * y, axis=-1, keepdims=True) + EPS) * g_ref[...]
    o_ref[...] = x_ref[...] + gt_ref[...] * y


def residual_norm(x, y, g, gt, n_ctx_rows, n_lat_rows):
    rows, d = x.shape
    idx = functools.partial(_mod_index, n_ctx_tiles=2 * n_ctx_rows // ROW_T, tiles_per_batch=n_lat_rows // ROW_T)
    row = pl.BlockSpec((ROW_T, d), lambda i: (i, 0))
    return pl.pallas_call(
        _residual_kernel,
        grid=(rows // ROW_T,),
        in_specs=[row, row, pl.BlockSpec((1, d), lambda i: (0, 0)),
                  pl.BlockSpec((None, 1, d), lambda i: (idx(i), 0, 0))],
        out_specs=row,
        out_shape=jax.ShapeDtypeStruct((rows, d), jnp.float32),
        compiler_params=_cparams("parallel"),
        name="residual_norm",
    )(x, y, g.reshape(1, d), gt)


def kernel(x, c, ctx, c_ctx, w_mod, b_mod, norm_g, w_in, hy_conv_w, hy_conv_b, hy_ff1, hy_ff1_b, hy_freq,
           hy_ff2, hy_ff2_b, hy_ff3, hy_bias, lru_conv_w, lru_conv_b, lru_wa, lru_ba, lru_wi, lru_bi,
           lru_lambda, hg_lb_logits, hg_norm_g, w_branch, w_out, w_router, w_gate, w_up, w_down):
    bsz, n_lat, d = x.shape
    n_ctx = ctx.shape[1]
    assert bsz == 2 and d == D_MODEL and n_ctx == SEQ_T and 2 * n_lat == FFT_L and n_lat % ROW_T == 0
    gam = jax.nn.softmax(hg_lb_logits.astype(jnp.float32), axis=0)
    lb_all = jnp.maximum(jnp.cumsum(gam, axis=0) - gam[:1], 0.0)
    mats = fft_matrices()
    cvec = jnp.zeros((SUBLANES, d), jnp.float32).at[:bsz].set(jax.nn.silu(c)).at[bsz].set(jax.nn.silu(c_ctx))
    n_ctx_all = bsz * n_ctx
    xa = jnp.concatenate([ctx.reshape(n_ctx_all, d), x.reshape(bsz * n_lat, d)], axis=0)
    w_in_b, w_branch_b, w_out_b = (w.astype(jnp.bfloat16) for w in (w_in, w_branch, w_out))
    for l in range(DEPTH):
        last = l == DEPTH - 1
        p = {'lru_conv_w': lru_conv_w[l], 'lru_conv_b': lru_conv_b[l], 'lru_wa': lru_wa[l], 'lru_ba': lru_ba[l],
             'lru_wi': lru_wi[l], 'lru_bi': lru_bi[l], 'lru_lambda': lru_lambda[l]}
        mod = mm(cvec, w_mod, layer=l, name="adaln")[:bsz + 1] + b_mod[l]
        sh1, sc1, gt1, sh2, sc2, gt2 = (mod.reshape(bsz + 1, 1, 6, d)[:, :, k] for k in range(6))

        u = prenorm(xa, norm_g[l, 0], sh1, sc1, n_ctx, n_lat)
        z = mm(u, w_in_b, jnp.bfloat16, tm=IN_PROJ_TM, tn=1024, layer=l, name="in_proj")

        y_lru = lru_branch(z, p, n_ctx, n_lat)
        y_hg = hgrn_branch(z, lb_all[l], hg_norm_g[l], n_ctx, n_lat)

        ffp = (hy_ff1[l], hy_ff1_b[l], hy_freq[l], hy_ff2[l], hy_ff2_b[l])
        taps_l, norm_l = hyena_filter_taps(hyena_features(n_lat, *ffp), hy_ff3[l])
        taps_c, norm_c = hyena_filter_taps(hyena_features(n_ctx, *ffp), hy_ff3[l])
        xc_l = hy_conv(z, hy_conv_w[l], hy_conv_b[l], n_ctx_all // SEQ_T, n_lat, False)
        y_hy_l = hyena_latent(xc_l, taps_l, 1.0 / norm_l, hy_bias[l], mats)
        xc_c = hy_conv(z, hy_conv_w[l], hy_conv_b[l], 0, n_ctx, True)
        y_hy_c = hyena_context(xc_c, taps_c, 1.0 / norm_c, hy_bias[l])
        y_hy = jnp.concatenate([y_hy_c.reshape(n_ctx_all, MIX_W), y_hy_l.reshape(bsz * n_lat, MIX_W)], axis=0)

        acc = merge_branches(z, (y_hy, y_lru, y_hg), w_branch_b, l)
        xa, h, logits = out_proj(acc, w_out_b, l, xa, norm_g[l, 1], gt1, norm_g[l, 2], sh2, sc2,
                                 w_router[l], n_ctx, n_lat)
        segments = [(n_ctx_all, bsz, n_lat)] if last else [(n_ctx_all, bsz, n_lat), (0, bsz, n_ctx)]
        moe = ec_moe(h, logits, w_gate, w_up, w_down, l, segments)
        xa = residual_norm(xa, moe, norm_g[l, 3], gt2, n_ctx, n_lat)
    return xa[n_ctx_all:].reshape(bsz, n_lat, d)
```

```python
import functools
import math

import numpy as np
import jax
import jax.numpy as jnp
from jax import lax
from jax.experimental import pallas as pl
from jax.experimental.pallas import tpu as pltpu

D_MODEL = 2048
DEPTH = 2
GRID_W = 64
MIX_W = D_MODEL // 2
N_BRANCH = 3
HY_BANDS = 16
HY_TARGET = 1e-2
HY_FAST_PCT = 0.3
HY_SLOW_PCT = 1.5
LRU_BLOCKS = 8
LRU_BS = MIX_W // LRU_BLOCKS
LRU_C = 8.0
HG_HEADS = 8
HG_DK = MIX_W // HG_HEADS
N_EXPERTS = 16
EC_CAPACITY = 2
EPS = 1e-6
TINY = 1e-30
LOG2E = 1.4426950408889634
HY_COL0 = 0
LRU_COL0 = 3 * MIX_W
HG_COL0 = 5 * MIX_W
GATE_COL0 = 10 * MIX_W
IN_COLS = GATE_COL0 + N_BRANCH * D_MODEL

SUBLANES = 8
LANES = 128
SEQ_T = 256
HG_C = 128
HG_LEVELS = 7
ROW_T = 512
IN_PROJ_TM = 1536
FFT_R = 128
FFT_L = FFT_R * FFT_R
FFT_KB = 8
ROUTER_PAD = LANES
FFN_TF = 256
VMEM_LIMIT_BYTES = 48 * 1024 * 1024


def _cparams(*sem):
    return pltpu.CompilerParams(dimension_semantics=sem, vmem_limit_bytes=VMEM_LIMIT_BYTES)


def _pick_tile(n, pref):
    for t in (pref, 1024, 512, 256, 128):
        if t <= n and n % t == 0:
            return t
    return n


def _row_block(b, j, n_ctx, n_lat, reverse):
    if reverse:
        kc, kl = n_ctx - 1 - j, n_ctx + n_lat - 1 - j
    else:
        kc, kl = j, j - n_ctx
    return jnp.where(j < n_ctx, b * n_ctx + kc, 2 * n_ctx + b * n_lat + kl)


def _mm_kernel(a_ref, b_ref, o_ref):
    o_ref[...] = jnp.dot(a_ref[...].astype(jnp.bfloat16), b_ref[...].astype(jnp.bfloat16),
                         preferred_element_type=jnp.float32).astype(o_ref.dtype)


def mm(a, b, out_dtype=jnp.float32, tm=512, tn=1024, layer=None, name="mm"):
    m, k = a.shape
    n = b.shape[-1]
    tm = _pick_tile(m, tm)
    tn = _pick_tile(n, tn)
    if layer is None:
        bspec = pl.BlockSpec((k, tn), lambda i, j: (0, j))
    else:
        bspec = pl.BlockSpec((None, k, tn), lambda i, j: (layer, 0, j))
    return pl.pallas_call(
        _mm_kernel,
        grid=(m // tm, n // tn),
        in_specs=[pl.BlockSpec((tm, k), lambda i, j: (i, 0)), bspec],
        out_specs=pl.BlockSpec((tm, tn), lambda i, j: (i, j)),
        out_shape=jax.ShapeDtypeStruct((m, n), out_dtype),
        compiler_params=_cparams("parallel", "arbitrary"),
        name=name,
    )(a, b)


def _mod_index(i, n_ctx_tiles, tiles_per_batch):
    return jnp.where(i < n_ctx_tiles, 2, (i - n_ctx_tiles) // tiles_per_batch)


def _prenorm_kernel(x_ref, g_ref, sh_ref, sc_ref, u_ref):
    x = x_ref[...]
    y = x * lax.rsqrt(jnp.mean(x * x, axis=-1, keepdims=True) + EPS) * g_ref[...]
    u_ref[...] = (y * (1.0 + sc_ref[...]) + sh_ref[...]).astype(u_ref.dtype)


def prenorm(x, g, sh, sc, n_ctx_rows, n_lat_rows):
    rows, d = x.shape
    idx = functools.partial(_mod_index, n_ctx_tiles=2 * n_ctx_rows // ROW_T, tiles_per_batch=n_lat_rows // ROW_T)
    row = pl.BlockSpec((ROW_T, d), lambda i: (i, 0))
    vec = pl.BlockSpec((None, 1, d), lambda i: (idx(i), 0, 0))
    return pl.pallas_call(
        _prenorm_kernel,
        grid=(rows // ROW_T,),
        in_specs=[row, pl.BlockSpec((1, d), lambda i: (0, 0)), vec, vec],
        out_specs=row,
        out_shape=jax.ShapeDtypeStruct((rows, d), jnp.bfloat16),
        compiler_params=_cparams("parallel"),
        name="prenorm",
    )(x, g.reshape(1, d), sh, sc)


def _gelu_tanh(x):
    return 0.5 * x * (1.0 + jnp.tanh(math.sqrt(2.0 / math.pi) * (x + 0.044715 * (x * x * x))))


def _conv_pos(is_ctx, shape):
    t = lax.broadcasted_iota(jnp.int32, shape, 0)
    pos = jnp.where(is_ctx, t, t & (GRID_W - 1))
    last = jnp.where(is_ctx, SEQ_T - 1, GRID_W - 1)
    return pos, last


def _lru_pass_kernel(x_ref, cw_ref, cb_ref, w_ref, gb_ref, lam_ref, *rest, reverse, final):
    if final:
        lg_ref, hprev_ref, out_ref, h_ref, a_ref, b_ref = rest
    else:
        out_ref, h_ref, a_ref, b_ref = rest
    j = pl.program_id(1)

    @pl.when(j == 0)
    def _():
        h_ref[...] = jnp.zeros_like(h_ref)

    x = x_ref[...].astype(jnp.float32)
    pos, last = _conv_pos(j == 0, x.shape)
    cw = cw_ref[...]
    xc = cb_ref[...] + cw[2:3] * x
    xc = xc + cw[0:1] * jnp.where(pos >= 2, pltpu.roll(x, 2, 0), 0.0)
    xc = xc + cw[1:2] * jnp.where(pos >= 1, pltpu.roll(x, 1, 0), 0.0)
    xc = xc + cw[3:4] * jnp.where(pos < last, pltpu.roll(x, SEQ_T - 1, 0), 0.0)

    xb = xc.astype(jnp.bfloat16)
    lam = lam_ref[...]
    sp = jnp.maximum(-lam, 0.0) + jnp.log(1.0 + jnp.exp(-jnp.abs(lam)))
    n_tiles = SEQ_T // SUBLANES
    row = lax.broadcasted_iota(jnp.int32, (1, SUBLANES, LRU_BS), 1)
    for n in range(LRU_BLOCKS):
        sl = slice(n * LRU_BS, (n + 1) * LRU_BS)
        pre = jnp.dot(xb[:, sl], w_ref[n], preferred_element_type=jnp.float32) + gb_ref[n]
        gate = 1.0 / (1.0 + jnp.exp(-pre))
        r, i = gate[:, :LRU_BS], gate[:, LRU_BS:]
        a = jnp.exp(-LRU_C * r * sp[:, sl])
        om = 1.0 - a * a
        b = om * lax.rsqrt(jnp.maximum(om, TINY)) * (i * xc[:, sl])
        a = a.reshape(n_tiles, SUBLANES, LRU_BS)
        b = b.reshape(n_tiles, SUBLANES, LRU_BS)
        for d in (1, 2, 4):
            if reverse:
                m = row < SUBLANES - d
                a_sh, b_sh = pltpu.roll(a, SUBLANES - d, 1), pltpu.roll(b, SUBLANES - d, 1)
            else:
                m = row >= d
                a_sh, b_sh = pltpu.roll(a, d, 1), pltpu.roll(b, d, 1)
            b = jnp.where(m, a * b_sh + b, b)
            a = jnp.where(m, a * a_sh, a)
        a_ref[:, sl] = a.reshape(SEQ_T, LRU_BS)
        b_ref[:, sl] = b.reshape(SEQ_T, LRU_BS)

    h = h_ref[...]
    for k in range(n_tiles):
        kk = n_tiles - 1 - k if reverse else k
        rs = slice(kk * SUBLANES, (kk + 1) * SUBLANES)
        ht = a_ref[rs, :] * h + b_ref[rs, :]
        h = ht[0:1, :] if reverse else ht[SUBLANES - 1:SUBLANES, :]
        if final:
            ht = (ht + hprev_ref[rs, :]) * _gelu_tanh(lg_ref[rs, :].astype(jnp.float32))
        out_ref[rs, :] = ht.astype(out_ref.dtype)
    h_ref[...] = h


def lru_pass(z, conv_w, conv_b, wa, ba, wi, bi, lam, n_ctx_rows, n_lat_rows, reverse, h_prev=None):
    rows = z.shape[0]
    n_lat = n_lat_rows // SEQ_T
    cb = LRU_COL0 // MIX_W
    final = h_prev is not None
    rb = functools.partial(_row_block, n_ctx=n_ctx_rows // SEQ_T, n_lat=n_lat, reverse=reverse)

    def zspec(k):
        return pl.BlockSpec((SEQ_T, MIX_W), lambda b, j: (rb(b, j), cb + k))

    row_spec = pl.BlockSpec((SEQ_T, MIX_W), lambda b, j: (rb(b, j), 0))

    def const_spec(shape):
        return pl.BlockSpec(shape, lambda b, j: (0,) * len(shape))

    w2 = jnp.concatenate([wa, wi], axis=-1).astype(jnp.bfloat16)
    gb = jnp.concatenate([ba.reshape(LRU_BLOCKS, 1, LRU_BS), bi.reshape(LRU_BLOCKS, 1, LRU_BS)], axis=-1)
    in_specs = [zspec(0), const_spec((4, MIX_W)), const_spec((1, MIX_W)),
                const_spec((LRU_BLOCKS, LRU_BS, 2 * LRU_BS)), const_spec((LRU_BLOCKS, 1, 2 * LRU_BS)),
                const_spec((1, MIX_W))]
    args = [z, conv_w, conv_b.reshape(1, MIX_W), w2, gb, lam.reshape(1, MIX_W)]
    if final:
        in_specs += [zspec(1), row_spec]
        args += [z, h_prev]
    return pl.pallas_call(
        functools.partial(_lru_pass_kernel, reverse=reverse, final=final),
        grid=(2, n_ctx_rows // SEQ_T + n_lat),
        in_specs=in_specs,
        out_specs=row_spec,
        out_shape=jax.ShapeDtypeStruct((rows, MIX_W), jnp.bfloat16 if final else jnp.float32),
        scratch_shapes=[pltpu.VMEM((1, MIX_W), jnp.float32),
                        pltpu.VMEM((SEQ_T, MIX_W), jnp.float32),
                        pltpu.VMEM((SEQ_T, MIX_W), jnp.float32)],
        compiler_params=_cparams("parallel", "arbitrary"),
        name="lru_bwd" if reverse else "lru_fwd",
    )(*args)


def lru_branch(z, p, n_ctx_rows, n_lat_rows):
    def one(d, h_prev):
        return lru_pass(z, p['lru_conv_w'], p['lru_conv_b'], p['lru_wa'][d], p['lru_ba'][d], p['lru_wi'][d],
                        p['lru_bi'][d], p['lru_lambda'][d], n_ctx_rows, n_lat_rows, d == 1, h_prev)
    return one(1, one(0, None))


def _hg_level_matrix(reverse):
    t = np.arange(HG_C)[:, None]
    s = np.arange(HG_C)[None, :]
    x = t ^ s
    lv = np.where(x > 0, np.floor(np.log2(np.maximum(x, 1))).astype(np.int32), -1)
    lv = np.where(s < t, lv, -1).astype(np.int32)
    return jnp.asarray(lv.T if reverse else lv)


def _hg_pass_kernel(lv_ref, lb_ref, q_ref, f_ref, v_ref, *rest, reverse, final):
    if final:
        og_ref, oprev_ref, g_ref, out_ref, st_ref, att_ref = rest
    else:
        out_ref, st_ref, att_ref = rest

    @pl.when(pl.program_id(1) == 0)
    def _():
        st_ref[...] = jnp.zeros_like(st_ref)

    x = f_ref[...].astype(jnp.float32)
    lb = lb_ref[...]
    e = jnp.exp(-jnp.abs(x))
    inv = 1.0 / (1.0 + e)
    pos = x >= 0
    sig = jnp.where(pos, inv, e * inv)
    sigm = jnp.where(pos, e * inv, inv)
    lsig = jnp.minimum(x, 0.0) - jnp.log(1.0 + e)
    lf = jnp.where(lb > 0, jnp.log(lb + (1.0 - lb) * sig), lsig)
    kk = (1.0 - lb) * sigm
    q = q_ref[...].astype(jnp.float32)
    v = v_ref[...]
    lv = lv_ref[...]

    n_tiles = HG_C // SUBLANES
    row = lax.broadcasted_iota(jnp.int32, (1, SUBLANES, MIX_W), 1)
    c = (lf * LOG2E).reshape(n_tiles, SUBLANES, MIX_W)
    bt = c
    for lvl in range(HG_LEVELS + 1):
        m = 1 << lvl
        qm = (q * jnp.exp2(c).reshape(HG_C, MIX_W)).astype(jnp.bfloat16)
        km = (kk * jnp.exp2(bt - c).reshape(HG_C, MIX_W)).astype(jnp.bfloat16)
        if lvl < HG_LEVELS:
            for h in range(HG_HEADS):
                sl = slice(h * HG_DK, (h + 1) * HG_DK)
                d = lax.dot_general(qm[:, sl], km[:, sl], (((1,), (1,)), ((), ())),
                                    preferred_element_type=jnp.float32)
                if lvl == 0:
                    att_ref[h] = jnp.where(lv == 0, d, 0.0)
                else:
                    att_ref[h] = jnp.where(lv == lvl, d, att_ref[h])
            if m < SUBLANES:
                upper = (row & m) != 0
                down = pltpu.roll(bt, m, 1)
                up = pltpu.roll(bt, SUBLANES - m, 1)
                if reverse:
                    c = c + jnp.where(upper, 0.0, up)
                else:
                    c = c + jnp.where(upper, down, 0.0)
                bt = bt + jnp.where(upper, down, up)
            else:
                k = m // SUBLANES
                pair = (n_tiles // (2 * k), 2, k, SUBLANES, MIX_W)
                c5, b5 = c.reshape(pair), bt.reshape(pair)
                tot2 = b5[:, 0] + b5[:, 1]
                if reverse:
                    c = jnp.stack([c5[:, 0] + b5[:, 1], c5[:, 1]], axis=1)
                else:
                    c = jnp.stack([c5[:, 0], c5[:, 1] + b5[:, 0]], axis=1)
                c = c.reshape(n_tiles, SUBLANES, MIX_W)
                bt = jnp.stack([tot2, tot2], axis=1).reshape(n_tiles, SUBLANES, MIX_W)
        else:
            tot = jnp.exp2(bt[0, 0:1, :])
            diag = q * kk
            for h in range(HG_HEADS):
                sl = slice(h * HG_DK, (h + 1) * HG_DK)
                st = st_ref[h]
                vh = v[:, sl]
                o = lax.dot_general(qm[:, sl], st.astype(jnp.bfloat16), (((1,), (1,)), ((), ())),
                                    preferred_element_type=jnp.float32)
                o = o + jnp.dot(att_ref[h].astype(jnp.bfloat16), vh, preferred_element_type=jnp.float32)
                o = o + jnp.sum(diag[:, sl], axis=-1, keepdims=True) * vh.astype(jnp.float32)
                st_ref[h] = st * tot[:, sl] + lax.dot_general(
                    vh, km[:, sl], (((0,), (0,)), ((), ())), preferred_element_type=jnp.float32)
                if final:
                    o = o + oprev_ref[:, sl]
                    y = o * lax.rsqrt(jnp.mean(o * o, axis=-1, keepdims=True) + EPS) * g_ref[:, sl]
                    og = og_ref[:, sl].astype(jnp.float32)
                    out_ref[:, sl] = (y * (og / (1.0 + jnp.exp(-og)))).astype(out_ref.dtype)
                else:
                    out_ref[:, sl] = o


def hgrn_pass(z, lb, n_ctx_rows, n_lat_rows, reverse, o_prev=None, norm_g=None):
    rows = z.shape[0]
    n_ctx, n_lat = n_ctx_rows // HG_C, n_lat_rows // HG_C
    cb = HG_COL0 // MIX_W
    final = o_prev is not None
    rb = functools.partial(_row_block, n_ctx=n_ctx, n_lat=n_lat, reverse=reverse)

    def zspec(k):
        return pl.BlockSpec((HG_C, MIX_W), lambda b, j: (rb(b, j), cb + k))

    row_spec = pl.BlockSpec((HG_C, MIX_W), lambda b, j: (rb(b, j), 0))
    vec_spec = pl.BlockSpec((1, MIX_W), lambda b, j: (0, 0))
    in_specs = [pl.BlockSpec((HG_C, HG_C), lambda b, j: (0, 0)), vec_spec,
                zspec(0), zspec(2 if reverse else 1), zspec(3)]
    args = [_hg_level_matrix(reverse), lb.reshape(1, MIX_W), z, z, z]
    if final:
        in_specs += [zspec(4), row_spec, vec_spec]
        args += [z, o_prev, norm_g.reshape(1, MIX_W)]
    return pl.pallas_call(
        functools.partial(_hg_pass_kernel, reverse=reverse, final=final),
        grid=(2, n_ctx + n_lat),
        in_specs=in_specs,
        out_specs=row_spec,
        out_shape=jax.ShapeDtypeStruct((rows, MIX_W), jnp.bfloat16 if final else jnp.float32),
        scratch_shapes=[pltpu.VMEM((HG_HEADS, HG_DK, HG_DK), jnp.float32),
                        pltpu.VMEM((HG_HEADS, HG_C, HG_C), jnp.float32)],
        compiler_params=_cparams("parallel", "arbitrary"),
        name="hgrn_bwd" if reverse else "hgrn_fwd",
    )(*args)


def hgrn_branch(z, lb, norm_g, n_ctx_rows, n_lat_rows):
    o_f = hgrn_pass(z, lb, n_ctx_rows, n_lat_rows, False)
    return hgrn_pass(z, lb, n_ctx_rows, n_lat_rows, True, o_f, norm_g)


def _hy_conv_kernel(x_ref, w_ref, b_ref, out_ref, *, is_ctx):
    x = x_ref[...].astype(jnp.float32)
    t = lax.broadcasted_iota(jnp.int32, x.shape, 0)
    pos = t if is_ctx else t & (GRID_W - 1)
    last = SEQ_T - 1 if is_ctx else GRID_W - 1
    w = w_ref[...]
    y = b_ref[...] + w[1:2] * x
    y = y + w[0:1] * jnp.where(pos >= 1, pltpu.roll(x, 1, 0), 0.0)
    y = y + w[2:3] * jnp.where(pos < last, pltpu.roll(x, SEQ_T - 1, 0), 0.0)
    for g in range(3):
        out_ref[g] = y[:, g * MIX_W:(g + 1) * MIX_W].astype(out_ref.dtype)


def hy_conv(z, conv_w, conv_b, block0, n_rows, is_ctx):
    nb = n_rows // SEQ_T
    return pl.pallas_call(
        functools.partial(_hy_conv_kernel, is_ctx=is_ctx),
        grid=(2, nb),
        in_specs=[pl.BlockSpec((SEQ_T, 3 * MIX_W), lambda b, j: (block0 + b * nb + j, 0)),
                  pl.BlockSpec((3, 3 * MIX_W), lambda b, j: (0, 0)),
                  pl.BlockSpec((1, 3 * MIX_W), lambda b, j: (0, 0))],
        out_specs=pl.BlockSpec((3, None, SEQ_T, MIX_W), lambda b, j: (0, b, j, 0)),
        out_shape=jax.ShapeDtypeStruct((3, 2, n_rows, MIX_W), jnp.bfloat16),
        compiler_params=_cparams("parallel", "arbitrary"),
        name="hy_conv_ctx" if is_ctx else "hy_conv_lat",
    )(z, conv_w, conv_b.reshape(1, 3 * MIX_W))


def _cis(num, den):
    ang = (2.0 * math.pi / den) * (num % den).astype(jnp.float32)
    return jnp.cos(ang), jnp.sin(ang)


def fft_matrices():
    r = FFT_R
    i = jnp.arange(r, dtype=jnp.int32)
    c, s = _cis(i[:, None] * i[None, :], r)
    half = r // 2
    fa_data = jnp.concatenate([jnp.concatenate([c[:, :half], s[:, :half]], axis=1),
                               jnp.concatenate([-s[:, :half], c[:, :half]], axis=1)], axis=0)
    fa_taps = jnp.concatenate([c, -s], axis=0)
    k1, k2, s2 = i[:, None, None], i[None, :, None], i[None, None, :]
    gc, gs = _cis(s2 * (r * k2 + k1), FFT_L)
    gr, gi = gc, -gs
    m1 = jnp.concatenate([jnp.concatenate([gr, -gi], axis=2), jnp.concatenate([gi, gr], axis=2)], axis=1)
    grt, git = jnp.swapaxes(gr, 1, 2), jnp.swapaxes(gi, 1, 2)
    m2 = jnp.concatenate([jnp.concatenate([grt, git], axis=2), jnp.concatenate([-git, grt], axis=2)], axis=1)
    er, ei = c[:half] / FFT_L, s[:half] / FFT_L
    e2 = jnp.concatenate([jnp.concatenate([er, -ei], axis=1), jnp.concatenate([ei, er], axis=1)], axis=0)
    bf = jnp.bfloat16
    return dict(fa_data=fa_data.astype(bf), fa_taps=fa_taps.astype(bf), m1=m1.astype(bf), m2=m2.astype(bf),
                e2=e2.astype(bf))


def _fft_a_kernel(f_ref, x_ref, o_ref, *, two):
    x = jnp.concatenate([x_ref[0], x_ref[1]], axis=0) if two else x_ref[...]
    o_ref[...] = jnp.dot(f_ref[...], x, preferred_element_type=jnp.float32).astype(o_ref.dtype)


def fft_stage_a(fmat, x, g, tn):
    two = x.ndim == 4
    cols = x.shape[-1]
    xspec = (pl.BlockSpec((None, 2, FFT_R // 2, tn), lambda j: (g, 0, 0, j)) if two
             else pl.BlockSpec((None, FFT_R, tn), lambda j: (g, 0, j)))
    return pl.pallas_call(
        functools.partial(_fft_a_kernel, two=two),
        grid=(cols // tn,),
        in_specs=[pl.BlockSpec((2 * FFT_R, FFT_R), lambda j: (0, 0)), xspec],
        out_specs=pl.BlockSpec((2 * FFT_R, tn), lambda j: (0, j)),
        out_shape=jax.ShapeDtypeStruct((2 * FFT_R, cols), jnp.bfloat16),
        compiler_params=_cparams("parallel"),
        name="fft_stage_a",
    )(fmat, x)


def _fft_mid_kernel(m1_ref, a_ref, *rest, conv):
    if conv:
        m2_ref, h_ref, o_ref = rest
    else:
        s_ref, o_ref = rest
    r = FFT_R
    for k in range(FFT_KB):
        a = a_ref[:, k].reshape(2 * r, MIX_W)
        x = jnp.dot(m1_ref[k], a, preferred_element_type=jnp.float32)
        if conv:
            h = h_ref[k].astype(jnp.float32)
            xr, xi, hr, hi = x[:r], x[r:], h[:r], h[r:]
            zc = jnp.concatenate([xr * hr - xi * hi, xr * hi + xi * hr], axis=0).astype(jnp.bfloat16)
            p = jnp.dot(m2_ref[k], zc, preferred_element_type=jnp.float32)
            o_ref[:, k] = p.reshape(2, r, MIX_W).astype(o_ref.dtype)
        else:
            o_ref[k] = (x * s_ref[...]).astype(o_ref.dtype)


def fft_mid(m1, a, m2=None, h=None, scale=None):
    conv = h is not None
    r = FFT_R
    mspec = pl.BlockSpec((FFT_KB, 2 * r, 2 * r), lambda j: (j, 0, 0))
    aspec = pl.BlockSpec((2, FFT_KB, r, MIX_W), lambda j: (0, j, 0, 0))
    hspec = pl.BlockSpec((FFT_KB, 2 * r, MIX_W), lambda j: (j, 0, 0))
    if conv:
        in_specs, args = [mspec, aspec, mspec, hspec], (m1, a, m2, h)
        out_specs, out_shape = aspec, jax.ShapeDtypeStruct((2, r, r, MIX_W), jnp.bfloat16)
    else:
        in_specs, args = [mspec, aspec, pl.BlockSpec((1, MIX_W), lambda j: (0, 0))], (m1, a, scale)
        out_specs, out_shape = hspec, jax.ShapeDtypeStruct((r, 2 * r, MIX_W), jnp.bfloat16)
    return pl.pallas_call(
        functools.partial(_fft_mid_kernel, conv=conv),
        grid=(r // FFT_KB,),
        in_specs=in_specs, out_specs=out_specs, out_shape=out_shape,
        compiler_params=_cparams("parallel"),
        name="fft_mid_conv" if conv else "fft_mid_filter",
    )(*args)


def _fft_out_kernel(e_ref, p_ref, mul_ref, add_ref, bias_ref, o_ref):
    y = jnp.dot(e_ref[...], p_ref[...], preferred_element_type=jnp.float32)
    half = FFT_R // 2
    bias = bias_ref[...]
    for b in range(2):
        yb = y[b * half:(b + 1) * half] + add_ref[b].astype(jnp.float32) * bias
        o_ref[b] = (mul_ref[b].astype(jnp.float32) * yb).astype(o_ref.dtype)


def fft_out(e2, p, mul, gm, add, ga, bias_tiled, tn):
    cols = p.shape[-1]
    half = FFT_R // 2

    def bspec(g):
        return pl.BlockSpec((None, 2, half, tn), lambda j: (g, 0, 0, j))

    return pl.pallas_call(
        _fft_out_kernel,
        grid=(cols // tn,),
        in_specs=[pl.BlockSpec((FFT_R, 2 * FFT_R), lambda j: (0, 0)),
                  pl.BlockSpec((2 * FFT_R, tn), lambda j: (0, j)), bspec(gm), bspec(ga),
                  pl.BlockSpec((1, tn), lambda j: (0, 0))],
        out_specs=bspec(0),
        out_shape=jax.ShapeDtypeStruct((1, 2, half, cols), jnp.bfloat16),
        compiler_params=_cparams("parallel"),
        name="fft_out",
    )(e2, p, mul, add, bias_tiled)


def hyena_latent(xc, taps, inv_norm, bias, mats):
    r = FFT_R
    cols = r * MIX_W
    tn = 8192
    xs = xc.reshape(3, 2, r // 2, cols)
    ts = taps.reshape(2, r, cols)
    u, gu = xs, 2
    for order in range(2):
        ta = fft_stage_a(mats['fa_taps'], ts, order, tn)
        spec = fft_mid(mats['m1'], ta.reshape(2, r, r, MIX_W), scale=inv_norm[order])
        a = fft_stage_a(mats['fa_data'], u, gu, tn)
        p = fft_mid(mats['m1'], a.reshape(2, r, r, MIX_W), mats['m2'], spec)
        bt = jnp.tile(bias[order].reshape(1, MIX_W), (1, tn // MIX_W))
        u, gu = fft_out(mats['e2'], p.reshape(2 * r, cols), xs, order, u, gu, bt, tn), 0
    return u.reshape(2, FFT_L // 2, MIX_W)


def hyena_context(xc, taps, inv_norm, bias):
    n = xc.shape[2]
    length = 2 * n
    k = jnp.arange(length, dtype=jnp.int32)
    c, s = _cis(k[:, None] * k[None, :], length)
    fr, fi = c[:, :n], -s[:, :n]
    m_fwd = jnp.concatenate([jnp.concatenate([fr, -fi], axis=1), jnp.concatenate([fi, fr], axis=1)], axis=0)
    m_taps = jnp.concatenate([c, -s], axis=0)
    er, ei = c[:n] / length, s[:n] / length
    m_inv = jnp.concatenate([jnp.concatenate([er, -ei], axis=1), jnp.concatenate([ei, er], axis=1)], axis=0)
    x1, x2, u = (xc[g].astype(jnp.float32) for g in range(3))
    for order, gate in enumerate((x1, x2)):
        h = mm(m_taps, taps[order], name="ctx_dft_taps") * inv_norm[order]
        w = mm(m_fwd, u.reshape(length, MIX_W), name="ctx_dft_fwd")
        wr, wi, hr, hi = w[:length], w[length:], h[:length], h[length:]
        zc = jnp.concatenate([wr * hr - wi * hi, wr * hi + wi * hr], axis=0)
        y = mm(m_inv, zc, name="ctx_dft_inv").reshape(2, n, MIX_W)
        u = gate * (y + u * bias[order])
    return u.astype(jnp.bfloat16)


def hyena_features(n, ff1, ff1_b, freq, ff2, ff2_b):
    k = jnp.arange(n, dtype=jnp.int32)
    pos = jnp.stack([k, (n - k) % n]).astype(jnp.float32)
    t = pos / (n - 1)
    fw = (2.0 * math.pi * pos / n)[..., None] * jnp.linspace(1e-4, HY_BANDS - 1, HY_BANDS, dtype=jnp.float32)
    feats = jnp.concatenate([t[..., None], jnp.cos(fw), -jnp.sin(fw)], axis=-1)
    h = jnp.sin(freq * (feats @ ff1 + ff1_b))
    return jnp.sin(freq * (h @ ff2 + ff2_b))


def _filter_kernel(feat_ref, w_ref, delta_ref, taps_ref, sum_ref, *, n, tm):
    dr = pl.program_id(1)
    i = pl.program_id(2)
    row = i * tm + lax.broadcasted_iota(jnp.int32, (tm, MIX_W), 0)
    lag = jnp.where(dr == 0, row, jnp.where(row == 0, 0, n - row))
    t = lag.astype(jnp.float32) / (n - 1)
    h = jnp.dot(feat_ref[...].astype(jnp.bfloat16), w_ref[...].astype(jnp.bfloat16),
                preferred_element_type=jnp.float32)
    h = h * jnp.exp(-t * delta_ref[...])

    @pl.when((dr == 0) & (i == 0))
    def _():
        sum_ref[...] = jnp.zeros_like(sum_ref)

    sum_ref[...] += jnp.sum(jnp.abs(h), axis=0, keepdims=True)
    taps_ref[...] = jnp.where((dr == 0) | (row != 0), h, 0.0).astype(taps_ref.dtype)


def hyena_filter_taps(feats, ff3):
    _, n, nf = feats.shape
    tm = min(n, 1024)
    nb = n // tm
    w = ff3.reshape(nf, 2, 2, MIX_W).transpose(1, 2, 0, 3)
    deltas = jnp.abs(jnp.linspace(math.log(HY_TARGET) / HY_SLOW_PCT, math.log(HY_TARGET) / HY_FAST_PCT,
                                  MIX_W, dtype=jnp.float32)).reshape(1, MIX_W)
    return pl.pallas_call(
        functools.partial(_filter_kernel, n=n, tm=tm),
        grid=(2, 2, nb),
        in_specs=[pl.BlockSpec((None, tm, nf), lambda o, dr, i: (dr, i, 0)),
                  pl.BlockSpec((None, None, nf, MIX_W), lambda o, dr, i: (o, dr, 0, 0)),
                  pl.BlockSpec((1, MIX_W), lambda o, dr, i: (0, 0))],
        out_specs=[pl.BlockSpec((None, tm, MIX_W), lambda o, dr, i: (o, dr * nb + i, 0)),
                   pl.BlockSpec((None, 1, MIX_W), lambda o, dr, i: (o, 0, 0))],
        out_shape=[jax.ShapeDtypeStruct((2, 2 * n, MIX_W), jnp.bfloat16),
                   jax.ShapeDtypeStruct((2, 1, MIX_W), jnp.float32)],
        compiler_params=_cparams("arbitrary", "arbitrary", "arbitrary"),
        name="hyena_filter",
    )(feats, w, deltas)


def _merge_kernel(yh_ref, yl_ref, yg_ref, w_ref, g0_ref, g1_ref, g2_ref, o_ref):
    acc = None
    for y_ref, g_ref, j in ((yh_ref, g0_ref, 0), (yl_ref, g1_ref, 1), (yg_ref, g2_ref, 2)):
        gate = 1.0 / (1.0 + jnp.exp(-g_ref[...].astype(jnp.float32)))
        t = gate * jnp.dot(y_ref[...], w_ref[j], preferred_element_type=jnp.float32)
        acc = t if acc is None else acc + t
    o_ref[...] = acc.astype(o_ref.dtype)


def merge_branches(z, ys, w_branch, layer):
    rows = z.shape[0]
    tm = ROW_T // 2
    yspec = pl.BlockSpec((tm, MIX_W), lambda i: (i, 0))

    def gspec(k):
        return pl.BlockSpec((tm, D_MODEL), lambda i: (i, GATE_COL0 // D_MODEL + k))

    return pl.pallas_call(
        _merge_kernel,
        grid=(rows // tm,),
        in_specs=[yspec, yspec, yspec,
                  pl.BlockSpec((None, N_BRANCH, MIX_W, D_MODEL), lambda i: (layer, 0, 0, 0)),
                  gspec(0), gspec(1), gspec(2)],
        out_specs=pl.BlockSpec((tm, D_MODEL), lambda i: (i, 0)),
        out_shape=jax.ShapeDtypeStruct((rows, D_MODEL), jnp.bfloat16),
        compiler_params=_cparams("parallel"),
        name="merge",
    )(*ys, w_branch, z, z, z)


def _out_proj_kernel(a_ref, w_ref, x_ref, g1_ref, gt_ref, g2_ref, sh_ref, sc_ref, wr_ref, xo_ref, h_ref, lg_ref):
    y = jnp.dot(a_ref[...], w_ref[...], preferred_element_type=jnp.float32)
    y = y * lax.rsqrt(jnp.mean(y * y, axis=-1, keepdims=True) + EPS) * g1_ref[...]
    x = x_ref[...] + gt_ref[...] * y
    xo_ref[...] = x
    h = x * lax.rsqrt(jnp.mean(x * x, axis=-1, keepdims=True) + EPS) * g2_ref[...]
    h = (h * (1.0 + sc_ref[...]) + sh_ref[...]).astype(jnp.bfloat16)
    h_ref[...] = h
    lg_ref[...] = jnp.dot(h, wr_ref[...], preferred_element_type=jnp.float32)


def out_proj(acc, w_out, layer, x, g1, gt, g2, sh, sc, w_router, n_ctx_rows, n_lat_rows):
    rows, d = x.shape
    tm = ROW_T // 2
    idx = functools.partial(_mod_index, n_ctx_tiles=2 * n_ctx_rows // tm, tiles_per_batch=n_lat_rows // tm)
    row = pl.BlockSpec((tm, d), lambda i: (i, 0))
    vec = pl.BlockSpec((1, d), lambda i: (0, 0))
    mod = pl.BlockSpec((None, 1, d), lambda i: (idx(i), 0, 0))
    wr = jnp.zeros((d, ROUTER_PAD), jnp.bfloat16).at[:, :N_EXPERTS].set(w_router.astype(jnp.bfloat16))
    return pl.pallas_call(
        _out_proj_kernel,
        grid=(rows // tm,),
        in_specs=[row, pl.BlockSpec((None, d, d), lambda i: (layer, 0, 0)), row, vec, mod, vec, mod, mod,
                  pl.BlockSpec((d, ROUTER_PAD), lambda i: (0, 0))],
        out_specs=[row, row, pl.BlockSpec((tm, ROUTER_PAD), lambda i: (i, 0))],
        out_shape=[jax.ShapeDtypeStruct((rows, d), jnp.float32), jax.ShapeDtypeStruct((rows, d), jnp.bfloat16),
                   jax.ShapeDtypeStruct((rows, ROUTER_PAD), jnp.float32)],
        compiler_params=_cparams("parallel"),
        name="out_proj",
    )(acc, w_out, x, g1.reshape(1, d), gt, g2.reshape(1, d), sh, sc, wr)


def _ffn_up_kernel(x_ref, wg_ref, wu_ref, h_ref):
    x = x_ref[...]
    hg = jnp.dot(x, wg_ref[...].astype(jnp.bfloat16), preferred_element_type=jnp.float32)
    hu = jnp.dot(x, wu_ref[...].astype(jnp.bfloat16), preferred_element_type=jnp.float32)
    h_ref[...] = (hg / (1.0 + jnp.exp(-hg)) * hu).astype(h_ref.dtype)


def _ffn_down_kernel(h_ref, wd_ref, g_ref, o_ref):
    y = jnp.dot(h_ref[...], wd_ref[...].astype(jnp.bfloat16), preferred_element_type=jnp.float32)
    o_ref[...] = y * g_ref[...]


def expert_ffn(xe, w_gate, w_up, w_down, layer, g):
    ne, r, d = xe.shape
    dff = w_gate.shape[-1]
    hid = pl.pallas_call(
        _ffn_up_kernel,
        grid=(ne, dff // FFN_TF),
        in_specs=[pl.BlockSpec((None, r, d), lambda e, f: (e, 0, 0)),
                  pl.BlockSpec((None, None, d, FFN_TF), lambda e, f: (layer, e, 0, f)),
                  pl.BlockSpec((None, None, d, FFN_TF), lambda e, f: (layer, e, 0, f))],
        out_specs=pl.BlockSpec((None, r, FFN_TF), lambda e, f: (e, 0, f)),
        out_shape=jax.ShapeDtypeStruct((ne, r, dff), jnp.bfloat16),
        compiler_params=_cparams("parallel", "arbitrary"),
        name="ffn_up",
    )(xe, w_gate, w_up)
    return pl.pallas_call(
        _ffn_down_kernel,
        grid=(ne, d // FFN_TF),
        in_specs=[pl.BlockSpec((None, r, dff), lambda e, j: (e, 0, 0)),
                  pl.BlockSpec((None, None, dff, FFN_TF), lambda e, j: (layer, e, 0, j)),
                  pl.BlockSpec((None, r, 1), lambda e, j: (e, 0, 0))],
        out_specs=pl.BlockSpec((None, r, FFN_TF), lambda e, j: (e, 0, j)),
        out_shape=jax.ShapeDtypeStruct((ne, r, d), jnp.float32),
        compiler_params=_cparams("parallel", "arbitrary"),
        name="ffn_down",
    )(hid, w_down, g)


def route(logits, row0, bsz, n):
    cap = EC_CAPACITY * n // N_EXPERTS
    aff = jax.nn.softmax(logits[row0:row0 + bsz * n, :N_EXPERTS].reshape(bsz, n, N_EXPERTS), axis=-1)
    g, idx = lax.top_k(jnp.swapaxes(aff, 1, 2), cap)
    flat = idx + (row0 + jnp.arange(bsz, dtype=idx.dtype) * n)[:, None, None]
    return (jnp.swapaxes(g, 0, 1).reshape(N_EXPERTS, bsz * cap),
            jnp.swapaxes(flat, 0, 1).reshape(N_EXPERTS, bsz * cap))


def ec_moe(h, logits, w_gate, w_up, w_down, layer, segments):
    gs, flats = zip(*(route(logits, *seg) for seg in segments))
    g = jnp.concatenate(gs, axis=1)
    flat = jnp.concatenate(flats, axis=1)
    ye = expert_ffn(h[flat], w_gate, w_up, w_down, layer, g[..., None])
    return jnp.zeros((h.shape[0], D_MODEL), jnp.float32).at[flat.reshape(-1)].add(ye.reshape(-1, D_MODEL))


def _residual_kernel(x_ref, y_ref, g_ref, gt_ref, *rest, nxt):
    y = y_ref[...]
    y = y * lax.rsqrt(jnp.mean(y * y, axis=-1, keepdims=True) + EPS) * g_ref[...]
    x = x_ref[...] + gt_ref[...] * y
    if nxt:
        gn_ref, sh_ref, sc_ref, o_ref, u_ref = rest
        u = x * lax.rsqrt(jnp.mean(x * x, axis=-1, keepdims=True) + EPS) * gn_ref[...]
        u_ref[...] = (u * (1.0 + sc_ref[...]) + sh_ref[...]).astype(u_ref.dtype)
    else:
        o_ref, = rest
    o_ref[...] = x


def residual_norm(x, y, g, gt, n_ctx_rows, n_lat_rows, nxt=None):
    rows, d = x.shape
    n_ctx_tiles = 2 * n_ctx_rows // ROW_T
    idx = functools.partial(_mod_index, n_ctx_tiles=n_ctx_tiles, tiles_per_batch=n_lat_rows // ROW_T)
    skip = 0 if nxt else n_ctx_tiles
    row = pl.BlockSpec((ROW_T, d), lambda i: (i + skip, 0))
    vec = pl.BlockSpec((1, d), lambda i: (0, 0))
    mod = pl.BlockSpec((None, 1, d), lambda i: (idx(i + skip), 0, 0))
    out_row = pl.BlockSpec((ROW_T, d), lambda i: (i, 0))
    n_out = rows - skip * ROW_T
    in_specs, args = [row, row, vec, mod], [x, y, g.reshape(1, d), gt]
    out_specs, out_shape = out_row, jax.ShapeDtypeStruct((n_out, d), jnp.float32)
    if nxt:
        in_specs += [vec, mod, mod]
        args += [nxt[0].reshape(1, d), nxt[1], nxt[2]]
        out_specs = [out_row, out_row]
        out_shape = [out_shape, jax.ShapeDtypeStruct((n_out, d), jnp.bfloat16)]
    return pl.pallas_call(
        functools.partial(_residual_kernel, nxt=bool(nxt)),
        grid=(n_out // ROW_T,),
        in_specs=in_specs, out_specs=out_specs, out_shape=out_shape,
        compiler_params=_cparams("parallel"),
        name="residual_norm",
    )(*args)


def kernel(x, c, ctx, c_ctx, w_mod, b_mod, norm_g, w_in, hy_conv_w, hy_conv_b, hy_ff1, hy_ff1_b, hy_freq,
           hy_ff2, hy_ff2_b, hy_ff3, hy_bias, lru_conv_w, lru_conv_b, lru_wa, lru_ba, lru_wi, lru_bi,
           lru_lambda, hg_lb_logits, hg_norm_g, w_branch, w_out, w_router, w_gate, w_up, w_down):
    bsz, n_lat, d = x.shape
    n_ctx = ctx.shape[1]
    assert bsz == 2 and d == D_MODEL and n_ctx == SEQ_T and 2 * n_lat == FFT_L and n_lat % ROW_T == 0
    gam = jax.nn.softmax(hg_lb_logits.astype(jnp.float32), axis=0)
    lb_all = jnp.maximum(jnp.cumsum(gam, axis=0) - gam[:1], 0.0)
    mats = fft_matrices()
    cvec = jnp.zeros((SUBLANES, d), jnp.float32).at[:bsz].set(jax.nn.silu(c)).at[bsz].set(jax.nn.silu(c_ctx))
    n_ctx_all = bsz * n_ctx
    xa = jnp.concatenate([ctx.reshape(n_ctx_all, d), x.reshape(bsz * n_lat, d)], axis=0)
    w_in_b, w_branch_b, w_out_b = (w.astype(jnp.bfloat16) for w in (w_in, w_branch, w_out))
    mods = []
    for l in range(DEPTH):
        mod = mm(cvec, w_mod, layer=l, name="adaln")[:bsz + 1] + b_mod[l]
        mods.append([mod.reshape(bsz + 1, 1, 6, d)[:, :, k] for k in range(6)])
    u = prenorm(xa, norm_g[0, 0], mods[0][0], mods[0][1], n_ctx, n_lat)
    for l in range(DEPTH):
        last = l == DEPTH - 1
        p = {'lru_conv_w': lru_conv_w[l], 'lru_conv_b': lru_conv_b[l], 'lru_wa': lru_wa[l], 'lru_ba': lru_ba[l],
             'lru_wi': lru_wi[l], 'lru_bi': lru_bi[l], 'lru_lambda': lru_lambda[l]}
        sh1, sc1, gt1, sh2, sc2, gt2 = mods[l]

        z = mm(u, w_in_b, jnp.bfloat16, tm=IN_PROJ_TM, tn=1024, layer=l, name="in_proj")

        y_lru = lru_branch(z, p, n_ctx, n_lat)
        y_hg = hgrn_branch(z, lb_all[l], hg_norm_g[l], n_ctx, n_lat)

        ffp = (hy_ff1[l], hy_ff1_b[l], hy_freq[l], hy_ff2[l], hy_ff2_b[l])
        taps_l, norm_l = hyena_filter_taps(hyena_features(n_lat, *ffp), hy_ff3[l])
        taps_c, norm_c = hyena_filter_taps(hyena_features(n_ctx, *ffp), hy_ff3[l])
        xc_l = hy_conv(z, hy_conv_w[l], hy_conv_b[l], n_ctx_all // SEQ_T, n_lat, False)
        y_hy_l = hyena_latent(xc_l, taps_l, 1.0 / norm_l, hy_bias[l], mats)
        xc_c = hy_conv(z, hy_conv_w[l], hy_conv_b[l], 0, n_ctx, True)
        y_hy_c = hyena_context(xc_c, taps_c, 1.0 / norm_c, hy_bias[l])
        y_hy = jnp.concatenate([y_hy_c.reshape(n_ctx_all, MIX_W), y_hy_l.reshape(bsz * n_lat, MIX_W)], axis=0)

        acc = merge_branches(z, (y_hy, y_lru, y_hg), w_branch_b, l)
        xa, h, logits = out_proj(acc, w_out_b, l, xa, norm_g[l, 1], gt1, norm_g[l, 2], sh2, sc2,
                                 w_router[l], n_ctx, n_lat)
        segments = [(n_ctx_all, bsz, n_lat)] if last else [(n_ctx_all, bsz, n_lat), (0, bsz, n_ctx)]
        moe = ec_moe(h, logits, w_gate, w_up, w_down, l, segments)
        if last:
            xa = residual_norm(xa, moe, norm_g[l, 3], gt2, n_ctx, n_lat)
        else:
            nxt = (norm_g[l + 1, 0], mods[l + 1][0], mods[l + 1][1])
            xa, u = residual_norm(xa, moe, norm_g[l, 3], gt2, n_ctx, n_lat, nxt)
    return xa.reshape(bsz, n_lat, d)
```

```python
import functools
import math

import numpy as np
import jax
import jax.numpy as jnp
from jax import lax
from jax.experimental import pallas as pl
from jax.experimental.pallas import tpu as pltpu

D_MODEL = 2048
DEPTH = 2
GRID_W = 64
MIX_W = D_MODEL // 2
N_BRANCH = 3
HY_BANDS = 16
HY_TARGET = 1e-2
HY_FAST_PCT = 0.3
HY_SLOW_PCT = 1.5
LRU_BLOCKS = 8
LRU_BS = MIX_W // LRU_BLOCKS
LRU_C = 8.0
HG_HEADS = 8
HG_DK = MIX_W // HG_HEADS
N_EXPERTS = 16
EC_CAPACITY = 2
EPS = 1e-6
TINY = 1e-30
LOG2E = 1.4426950408889634
HY_COL0 = 0
LRU_COL0 = 3 * MIX_W
HG_COL0 = 5 * MIX_W
GATE_COL0 = 10 * MIX_W
IN_COLS = GATE_COL0 + N_BRANCH * D_MODEL

SUBLANES = 8
LANES = 128
SEQ_T = 256
HG_C = 128
HG_LEVELS = 7
ROW_T = 512
IN_PROJ_TM = 1536
FFT_R = 128
FFT_L = FFT_R * FFT_R
FFT_KB = 8
ROUTER_PAD = LANES
FFN_TF = 256
MOE_TT = 256
MOE_WIN = 64
MOE_GROUP = MOE_TT // MOE_WIN
MOE_MAXQ = N_EXPERTS * (MOE_TT // MOE_WIN + 1)
VMEM_LIMIT_BYTES = 48 * 1024 * 1024


def _cparams(*sem):
    return pltpu.CompilerParams(dimension_semantics=sem, vmem_limit_bytes=VMEM_LIMIT_BYTES)


def _pick_tile(n, pref):
    for t in (pref, 1024, 512, 256, 128):
        if t <= n and n % t == 0:
            return t
    return n


def _row_block(b, j, n_ctx, n_lat, reverse):
    if reverse:
        kc, kl = n_ctx - 1 - j, n_ctx + n_lat - 1 - j
    else:
        kc, kl = j, j - n_ctx
    return jnp.where(j < n_ctx, b * n_ctx + kc, 2 * n_ctx + b * n_lat + kl)


def _mm_kernel(a_ref, b_ref, o_ref):
    o_ref[...] = jnp.dot(a_ref[...].astype(jnp.bfloat16), b_ref[...].astype(jnp.bfloat16),
                         preferred_element_type=jnp.float32).astype(o_ref.dtype)


def mm(a, b, out_dtype=jnp.float32, tm=512, tn=1024, layer=None, name="mm"):
    m, k = a.shape
    n = b.shape[-1]
    tm = _pick_tile(m, tm)
    tn = _pick_tile(n, tn)
    if layer is None:
        bspec = pl.BlockSpec((k, tn), lambda i, j: (0, j))
    else:
        bspec = pl.BlockSpec((None, k, tn), lambda i, j: (layer, 0, j))
    return pl.pallas_call(
        _mm_kernel,
        grid=(m // tm, n // tn),
        in_specs=[pl.BlockSpec((tm, k), lambda i, j: (i, 0)), bspec],
        out_specs=pl.BlockSpec((tm, tn), lambda i, j: (i, j)),
        out_shape=jax.ShapeDtypeStruct((m, n), out_dtype),
        compiler_params=_cparams("parallel", "arbitrary"),
        name=name,
    )(a, b)


def _mod_index(i, n_ctx_tiles, tiles_per_batch):
    return jnp.where(i < n_ctx_tiles, 2, (i - n_ctx_tiles) // tiles_per_batch)


def _prenorm_kernel(x_ref, g_ref, sh_ref, sc_ref, u_ref):
    x = x_ref[...]
    y = x * lax.rsqrt(jnp.mean(x * x, axis=-1, keepdims=True) + EPS) * g_ref[...]
    u_ref[...] = (y * (1.0 + sc_ref[...]) + sh_ref[...]).astype(u_ref.dtype)


def prenorm(x, g, sh, sc, n_ctx_rows, n_lat_rows):
    rows, d = x.shape
    idx = functools.partial(_mod_index, n_ctx_tiles=2 * n_ctx_rows // ROW_T, tiles_per_batch=n_lat_rows // ROW_T)
    row = pl.BlockSpec((ROW_T, d), lambda i: (i, 0))
    vec = pl.BlockSpec((None, 1, d), lambda i: (idx(i), 0, 0))
    return pl.pallas_call(
        _prenorm_kernel,
        grid=(rows // ROW_T,),
        in_specs=[row, pl.BlockSpec((1, d), lambda i: (0, 0)), vec, vec],
        out_specs=row,
        out_shape=jax.ShapeDtypeStruct((rows, d), jnp.bfloat16),
        compiler_params=_cparams("parallel"),
        name="prenorm",
    )(x, g.reshape(1, d), sh, sc)


def _gelu_tanh(x):
    return 0.5 * x * (1.0 + jnp.tanh(math.sqrt(2.0 / math.pi) * (x + 0.044715 * (x * x * x))))


def _conv_pos(is_ctx, shape):
    t = lax.broadcasted_iota(jnp.int32, shape, 0)
    pos = jnp.where(is_ctx, t, t & (GRID_W - 1))
    last = jnp.where(is_ctx, SEQ_T - 1, GRID_W - 1)
    return pos, last


def _lru_pass_kernel(x_ref, cw_ref, cb_ref, w_ref, gb_ref, lam_ref, *rest, reverse, final):
    if final:
        lg_ref, hprev_ref, out_ref, h_ref, a_ref, b_ref = rest
    else:
        out_ref, h_ref, a_ref, b_ref = rest
    j = pl.program_id(1)

    @pl.when(j == 0)
    def _():
        h_ref[...] = jnp.zeros_like(h_ref)

    x = x_ref[...].astype(jnp.float32)
    pos, last = _conv_pos(j == 0, x.shape)
    cw = cw_ref[...]
    xc = cb_ref[...] + cw[2:3] * x
    xc = xc + cw[0:1] * jnp.where(pos >= 2, pltpu.roll(x, 2, 0), 0.0)
    xc = xc + cw[1:2] * jnp.where(pos >= 1, pltpu.roll(x, 1, 0), 0.0)
    xc = xc + cw[3:4] * jnp.where(pos < last, pltpu.roll(x, SEQ_T - 1, 0), 0.0)

    xb = xc.astype(jnp.bfloat16)
    lam = lam_ref[...]
    sp = jnp.maximum(-lam, 0.0) + jnp.log(1.0 + jnp.exp(-jnp.abs(lam)))
    n_tiles = SEQ_T // SUBLANES
    row = lax.broadcasted_iota(jnp.int32, (1, SUBLANES, LRU_BS), 1)
    for n in range(LRU_BLOCKS):
        sl = slice(n * LRU_BS, (n + 1) * LRU_BS)
        pre = jnp.dot(xb[:, sl], w_ref[n], preferred_element_type=jnp.float32) + gb_ref[n]
        gate = 1.0 / (1.0 + jnp.exp(-pre))
        r, i = gate[:, :LRU_BS], gate[:, LRU_BS:]
        a = jnp.exp(-LRU_C * r * sp[:, sl])
        om = 1.0 - a * a
        b = om * lax.rsqrt(jnp.maximum(om, TINY)) * (i * xc[:, sl])
        a = a.reshape(n_tiles, SUBLANES, LRU_BS)
        b = b.reshape(n_tiles, SUBLANES, LRU_BS)
        for d in (1, 2, 4):
            if reverse:
                m = row < SUBLANES - d
                a_sh, b_sh = pltpu.roll(a, SUBLANES - d, 1), pltpu.roll(b, SUBLANES - d, 1)
            else:
                m = row >= d
                a_sh, b_sh = pltpu.roll(a, d, 1), pltpu.roll(b, d, 1)
            b = jnp.where(m, a * b_sh + b, b)
            a = jnp.where(m, a * a_sh, a)
        a_ref[:, sl] = a.reshape(SEQ_T, LRU_BS)
        b_ref[:, sl] = b.reshape(SEQ_T, LRU_BS)

    h = h_ref[...]
    for k in range(n_tiles):
        kk = n_tiles - 1 - k if reverse else k
        rs = slice(kk * SUBLANES, (kk + 1) * SUBLANES)
        ht = a_ref[rs, :] * h + b_ref[rs, :]
        h = ht[0:1, :] if reverse else ht[SUBLANES - 1:SUBLANES, :]
        if final:
            ht = (ht + hprev_ref[rs, :]) * _gelu_tanh(lg_ref[rs, :].astype(jnp.float32))
        out_ref[rs, :] = ht.astype(out_ref.dtype)
    h_ref[...] = h


def lru_pass(z, conv_w, conv_b, wa, ba, wi, bi, lam, n_ctx_rows, n_lat_rows, reverse, h_prev=None):
    rows = z.shape[0]
    n_lat = n_lat_rows // SEQ_T
    cb = LRU_COL0 // MIX_W
    final = h_prev is not None
    rb = functools.partial(_row_block, n_ctx=n_ctx_rows // SEQ_T, n_lat=n_lat, reverse=reverse)

    def zspec(k):
        return pl.BlockSpec((SEQ_T, MIX_W), lambda b, j: (rb(b, j), cb + k))

    row_spec = pl.BlockSpec((SEQ_T, MIX_W), lambda b, j: (rb(b, j), 0))

    def const_spec(shape):
        return pl.BlockSpec(shape, lambda b, j: (0,) * len(shape))

    w2 = jnp.concatenate([wa, wi], axis=-1).astype(jnp.bfloat16)
    gb = jnp.concatenate([ba.reshape(LRU_BLOCKS, 1, LRU_BS), bi.reshape(LRU_BLOCKS, 1, LRU_BS)], axis=-1)
    in_specs = [zspec(0), const_spec((4, MIX_W)), const_spec((1, MIX_W)),
                const_spec((LRU_BLOCKS, LRU_BS, 2 * LRU_BS)), const_spec((LRU_BLOCKS, 1, 2 * LRU_BS)),
                const_spec((1, MIX_W))]
    args = [z, conv_w, conv_b.reshape(1, MIX_W), w2, gb, lam.reshape(1, MIX_W)]
    if final:
        in_specs += [zspec(1), row_spec]
        args += [z, h_prev]
    return pl.pallas_call(
        functools.partial(_lru_pass_kernel, reverse=reverse, final=final),
        grid=(2, n_ctx_rows // SEQ_T + n_lat),
        in_specs=in_specs,
        out_specs=row_spec,
        out_shape=jax.ShapeDtypeStruct((rows, MIX_W), jnp.bfloat16 if final else jnp.float32),
        scratch_shapes=[pltpu.VMEM((1, MIX_W), jnp.float32),
                        pltpu.VMEM((SEQ_T, MIX_W), jnp.float32),
                        pltpu.VMEM((SEQ_T, MIX_W), jnp.float32)],
        compiler_params=_cparams("parallel", "arbitrary"),
        name="lru_bwd" if reverse else "lru_fwd",
    )(*args)


def lru_branch(z, p, n_ctx_rows, n_lat_rows):
    def one(d, h_prev):
        return lru_pass(z, p['lru_conv_w'], p['lru_conv_b'], p['lru_wa'][d], p['lru_ba'][d], p['lru_wi'][d],
                        p['lru_bi'][d], p['lru_lambda'][d], n_ctx_rows, n_lat_rows, d == 1, h_prev)
    return one(1, one(0, None))


def _hg_level_matrix(reverse):
    t = np.arange(HG_C)[:, None]
    s = np.arange(HG_C)[None, :]
    x = t ^ s
    lv = np.where(x > 0, np.floor(np.log2(np.maximum(x, 1))).astype(np.int32), -1)
    lv = np.where(s < t, lv, -1).astype(np.int32)
    return jnp.asarray(lv.T if reverse else lv)


def _hg_pass_kernel(lv_ref, lb_ref, q_ref, f_ref, v_ref, *rest, reverse, final):
    if final:
        og_ref, oprev_ref, g_ref, out_ref, st_ref, att_ref = rest
    else:
        out_ref, st_ref, att_ref = rest

    @pl.when(pl.program_id(1) == 0)
    def _():
        st_ref[...] = jnp.zeros_like(st_ref)

    x = f_ref[...].astype(jnp.float32)
    lb = lb_ref[...]
    e = jnp.exp(-jnp.abs(x))
    inv = 1.0 / (1.0 + e)
    pos = x >= 0
    sig = jnp.where(pos, inv, e * inv)
    sigm = jnp.where(pos, e * inv, inv)
    lsig = jnp.minimum(x, 0.0) - jnp.log(1.0 + e)
    lf = jnp.where(lb > 0, jnp.log(lb + (1.0 - lb) * sig), lsig)
    kk = (1.0 - lb) * sigm
    q = q_ref[...].astype(jnp.float32)
    v = v_ref[...]
    lv = lv_ref[...]

    n_tiles = HG_C // SUBLANES
    row = lax.broadcasted_iota(jnp.int32, (1, SUBLANES, MIX_W), 1)
    c = (lf * LOG2E).reshape(n_tiles, SUBLANES, MIX_W)
    bt = c
    for lvl in range(HG_LEVELS + 1):
        m = 1 << lvl
        qm = (q * jnp.exp2(c).reshape(HG_C, MIX_W)).astype(jnp.bfloat16)
        km = (kk * jnp.exp2(bt - c).reshape(HG_C, MIX_W)).astype(jnp.bfloat16)
        if lvl < HG_LEVELS:
            for h in range(HG_HEADS):
                sl = slice(h * HG_DK, (h + 1) * HG_DK)
                d = lax.dot_general(qm[:, sl], km[:, sl], (((1,), (1,)), ((), ())),
                                    preferred_element_type=jnp.float32)
                if lvl == 0:
                    att_ref[h] = jnp.where(lv == 0, d, 0.0)
                else:
                    att_ref[h] = jnp.where(lv == lvl, d, att_ref[h])
            if m < SUBLANES:
                upper = (row & m) != 0
                down = pltpu.roll(bt, m, 1)
                up = pltpu.roll(bt, SUBLANES - m, 1)
                if reverse:
                    c = c + jnp.where(upper, 0.0, up)
                else:
                    c = c + jnp.where(upper, down, 0.0)
                bt = bt + jnp.where(upper, down, up)
            else:
                k = m // SUBLANES
                pair = (n_tiles // (2 * k), 2, k, SUBLANES, MIX_W)
                c5, b5 = c.reshape(pair), bt.reshape(pair)
                tot2 = b5[:, 0] + b5[:, 1]
                if reverse:
                    c = jnp.stack([c5[:, 0] + b5[:, 1], c5[:, 1]], axis=1)
                else:
                    c = jnp.stack([c5[:, 0], c5[:, 1] + b5[:, 0]], axis=1)
                c = c.reshape(n_tiles, SUBLANES, MIX_W)
                bt = jnp.stack([tot2, tot2], axis=1).reshape(n_tiles, SUBLANES, MIX_W)
        else:
            tot = jnp.exp2(bt[0, 0:1, :])
            diag = q * kk
            for h in range(HG_HEADS):
                sl = slice(h * HG_DK, (h + 1) * HG_DK)
                st = st_ref[h]
                vh = v[:, sl]
                o = lax.dot_general(qm[:, sl], st.astype(jnp.bfloat16), (((1,), (1,)), ((), ())),
                                    preferred_element_type=jnp.float32)
                o = o + jnp.dot(att_ref[h].astype(jnp.bfloat16), vh, preferred_element_type=jnp.float32)
                o = o + jnp.sum(diag[:, sl], axis=-1, keepdims=True) * vh.astype(jnp.float32)
                st_ref[h] = st * tot[:, sl] + lax.dot_general(
                    vh, km[:, sl], (((0,), (0,)), ((), ())), preferred_element_type=jnp.float32)
                if final:
                    o = o + oprev_ref[:, sl]
                    y = o * lax.rsqrt(jnp.mean(o * o, axis=-1, keepdims=True) + EPS) * g_ref[:, sl]
                    og = og_ref[:, sl].astype(jnp.float32)
                    out_ref[:, sl] = (y * (og / (1.0 + jnp.exp(-og)))).astype(out_ref.dtype)
                else:
                    out_ref[:, sl] = o


def hgrn_pass(z, lb, n_ctx_rows, n_lat_rows, reverse, o_prev=None, norm_g=None):
    rows = z.shape[0]
    n_ctx, n_lat = n_ctx_rows // HG_C, n_lat_rows // HG_C
    cb = HG_COL0 // MIX_W
    final = o_prev is not None
    rb = functools.partial(_row_block, n_ctx=n_ctx, n_lat=n_lat, reverse=reverse)

    def zspec(k):
        return pl.BlockSpec((HG_C, MIX_W), lambda b, j: (rb(b, j), cb + k))

    row_spec = pl.BlockSpec((HG_C, MIX_W), lambda b, j: (rb(b, j), 0))
    vec_spec = pl.BlockSpec((1, MIX_W), lambda b, j: (0, 0))
    in_specs = [pl.BlockSpec((HG_C, HG_C), lambda b, j: (0, 0)), vec_spec,
                zspec(0), zspec(2 if reverse else 1), zspec(3)]
    args = [_hg_level_matrix(reverse), lb.reshape(1, MIX_W), z, z, z]
    if final:
        in_specs += [zspec(4), row_spec, vec_spec]
        args += [z, o_prev, norm_g.reshape(1, MIX_W)]
    return pl.pallas_call(
        functools.partial(_hg_pass_kernel, reverse=reverse, final=final),
        grid=(2, n_ctx + n_lat),
        in_specs=in_specs,
        out_specs=row_spec,
        out_shape=jax.ShapeDtypeStruct((rows, MIX_W), jnp.bfloat16 if final else jnp.float32),
        scratch_shapes=[pltpu.VMEM((HG_HEADS, HG_DK, HG_DK), jnp.float32),
                        pltpu.VMEM((HG_HEADS, HG_C, HG_C), jnp.float32)],
        compiler_params=_cparams("parallel", "arbitrary"),
        name="hgrn_bwd" if reverse else "hgrn_fwd",
    )(*args)


def hgrn_branch(z, lb, norm_g, n_ctx_rows, n_lat_rows):
    o_f = hgrn_pass(z, lb, n_ctx_rows, n_lat_rows, False)
    return hgrn_pass(z, lb, n_ctx_rows, n_lat_rows, True, o_f, norm_g)


def _hy_conv_kernel(x_ref, w_ref, b_ref, out_ref, *, is_ctx):
    x = x_ref[...].astype(jnp.float32)
    t = lax.broadcasted_iota(jnp.int32, x.shape, 0)
    pos = t if is_ctx else t & (GRID_W - 1)
    last = SEQ_T - 1 if is_ctx else GRID_W - 1
    w = w_ref[...]
    y = b_ref[...] + w[1:2] * x
    y = y + w[0:1] * jnp.where(pos >= 1, pltpu.roll(x, 1, 0), 0.0)
    y = y + w[2:3] * jnp.where(pos < last, pltpu.roll(x, SEQ_T - 1, 0), 0.0)
    for g in range(3):
        out_ref[g] = y[:, g * MIX_W:(g + 1) * MIX_W].astype(out_ref.dtype)


def hy_conv(z, conv_w, conv_b, block0, n_rows, is_ctx):
    nb = n_rows // SEQ_T
    return pl.pallas_call(
        functools.partial(_hy_conv_kernel, is_ctx=is_ctx),
        grid=(2, nb),
        in_specs=[pl.BlockSpec((SEQ_T, 3 * MIX_W), lambda b, j: (block0 + b * nb + j, 0)),
                  pl.BlockSpec((3, 3 * MIX_W), lambda b, j: (0, 0)),
                  pl.BlockSpec((1, 3 * MIX_W), lambda b, j: (0, 0))],
        out_specs=pl.BlockSpec((3, None, SEQ_T, MIX_W), lambda b, j: (0, b, j, 0)),
        out_shape=jax.ShapeDtypeStruct((3, 2, n_rows, MIX_W), jnp.bfloat16),
        compiler_params=_cparams("parallel", "arbitrary"),
        name="hy_conv_ctx" if is_ctx else "hy_conv_lat",
    )(z, conv_w, conv_b.reshape(1, 3 * MIX_W))


def _cis(num, den):
    ang = (2.0 * math.pi / den) * (num % den).astype(jnp.float32)
    return jnp.cos(ang), jnp.sin(ang)


def fft_matrices():
    r = FFT_R
    i = jnp.arange(r, dtype=jnp.int32)
    c, s = _cis(i[:, None] * i[None, :], r)
    half = r // 2
    fa_data = jnp.concatenate([jnp.concatenate([c[:, :half], s[:, :half]], axis=1),
                               jnp.concatenate([-s[:, :half], c[:, :half]], axis=1)], axis=0)
    fa_taps = jnp.concatenate([c, -s], axis=0)
    k1, k2, s2 = i[:, None, None], i[None, :, None], i[None, None, :]
    gc, gs = _cis(s2 * (r * k2 + k1), FFT_L)
    gr, gi = gc, -gs
    m1 = jnp.concatenate([jnp.concatenate([gr, -gi], axis=2), jnp.concatenate([gi, gr], axis=2)], axis=1)
    grt, git = jnp.swapaxes(gr, 1, 2), jnp.swapaxes(gi, 1, 2)
    m2 = jnp.concatenate([jnp.concatenate([grt, git], axis=2), jnp.concatenate([-git, grt], axis=2)], axis=1)
    er, ei = c[:half] / FFT_L, s[:half] / FFT_L
    e2 = jnp.concatenate([jnp.concatenate([er, -ei], axis=1), jnp.concatenate([ei, er], axis=1)], axis=0)
    bf = jnp.bfloat16
    return dict(fa_data=fa_data.astype(bf), fa_taps=fa_taps.astype(bf), m1=m1.astype(bf), m2=m2.astype(bf),
                e2=e2.astype(bf))


def _fft_a_kernel(f_ref, x_ref, o_ref, *, two):
    x = jnp.concatenate([x_ref[0], x_ref[1]], axis=0) if two else x_ref[...]
    o_ref[...] = jnp.dot(f_ref[...], x, preferred_element_type=jnp.float32).astype(o_ref.dtype)


def fft_stage_a(fmat, x, g, tn):
    two = x.ndim == 4
    cols = x.shape[-1]
    xspec = (pl.BlockSpec((None, 2, FFT_R // 2, tn), lambda j: (g, 0, 0, j)) if two
             else pl.BlockSpec((None, FFT_R, tn), lambda j: (g, 0, j)))
    return pl.pallas_call(
        functools.partial(_fft_a_kernel, two=two),
        grid=(cols // tn,),
        in_specs=[pl.BlockSpec((2 * FFT_R, FFT_R), lambda j: (0, 0)), xspec],
        out_specs=pl.BlockSpec((2 * FFT_R, tn), lambda j: (0, j)),
        out_shape=jax.ShapeDtypeStruct((2 * FFT_R, cols), jnp.bfloat16),
        compiler_params=_cparams("parallel"),
        name="fft_stage_a",
    )(fmat, x)


def _fft_mid_kernel(m1_ref, a_ref, *rest, conv):
    if conv:
        m2_ref, h_ref, o_ref = rest
    else:
        s_ref, o_ref = rest
    r = FFT_R
    for k in range(FFT_KB):
        a = a_ref[:, k].reshape(2 * r, MIX_W)
        x = jnp.dot(m1_ref[k], a, preferred_element_type=jnp.float32)
        if conv:
            h = h_ref[k].astype(jnp.float32)
            xr, xi, hr, hi = x[:r], x[r:], h[:r], h[r:]
            zc = jnp.concatenate([xr * hr - xi * hi, xr * hi + xi * hr], axis=0).astype(jnp.bfloat16)
            p = jnp.dot(m2_ref[k], zc, preferred_element_type=jnp.float32)
            o_ref[:, k] = p.reshape(2, r, MIX_W).astype(o_ref.dtype)
        else:
            o_ref[k] = (x * s_ref[...]).astype(o_ref.dtype)


def fft_mid(m1, a, m2=None, h=None, scale=None):
    conv = h is not None
    r = FFT_R
    mspec = pl.BlockSpec((FFT_KB, 2 * r, 2 * r), lambda j: (j, 0, 0))
    aspec = pl.BlockSpec((2, FFT_KB, r, MIX_W), lambda j: (0, j, 0, 0))
    hspec = pl.BlockSpec((FFT_KB, 2 * r, MIX_W), lambda j: (j, 0, 0))
    if conv:
        in_specs, args = [mspec, aspec, mspec, hspec], (m1, a, m2, h)
        out_specs, out_shape = aspec, jax.ShapeDtypeStruct((2, r, r, MIX_W), jnp.bfloat16)
    else:
        in_specs, args = [mspec, aspec, pl.BlockSpec((1, MIX_W), lambda j: (0, 0))], (m1, a, scale)
        out_specs, out_shape = hspec, jax.ShapeDtypeStruct((r, 2 * r, MIX_W), jnp.bfloat16)
    return pl.pallas_call(
        functools.partial(_fft_mid_kernel, conv=conv),
        grid=(r // FFT_KB,),
        in_specs=in_specs, out_specs=out_specs, out_shape=out_shape,
        compiler_params=_cparams("parallel"),
        name="fft_mid_conv" if conv else "fft_mid_filter",
    )(*args)


def _fft_out_kernel(e_ref, p_ref, mul_ref, add_ref, bias_ref, o_ref):
    y = jnp.dot(e_ref[...], p_ref[...], preferred_element_type=jnp.float32)
    half = FFT_R // 2
    bias = bias_ref[...]
    for b in range(2):
        yb = y[b * half:(b + 1) * half] + add_ref[b].astype(jnp.float32) * bias
        o_ref[b] = (mul_ref[b].astype(jnp.float32) * yb).astype(o_ref.dtype)


def fft_out(e2, p, mul, gm, add, ga, bias_tiled, tn):
    cols = p.shape[-1]
    half = FFT_R // 2

    def bspec(g):
        return pl.BlockSpec((None, 2, half, tn), lambda j: (g, 0, 0, j))

    return pl.pallas_call(
        _fft_out_kernel,
        grid=(cols // tn,),
        in_specs=[pl.BlockSpec((FFT_R, 2 * FFT_R), lambda j: (0, 0)),
                  pl.BlockSpec((2 * FFT_R, tn), lambda j: (0, j)), bspec(gm), bspec(ga),
                  pl.BlockSpec((1, tn), lambda j: (0, 0))],
        out_specs=bspec(0),
        out_shape=jax.ShapeDtypeStruct((1, 2, half, cols), jnp.bfloat16),
        compiler_params=_cparams("parallel"),
        name="fft_out",
    )(e2, p, mul, add, bias_tiled)


def hyena_latent(xc, taps, inv_norm, bias, mats):
    r = FFT_R
    cols = r * MIX_W
    tn = 8192
    xs = xc.reshape(3, 2, r // 2, cols)
    ts = taps.reshape(2, r, cols)
    u, gu = xs, 2
    for order in range(2):
        ta = fft_stage_a(mats['fa_taps'], ts, order, tn)
        spec = fft_mid(mats['m1'], ta.reshape(2, r, r, MIX_W), scale=inv_norm[order])
        a = fft_stage_a(mats['fa_data'], u, gu, tn)
        p = fft_mid(mats['m1'], a.reshape(2, r, r, MIX_W), mats['m2'], spec)
        bt = jnp.tile(bias[order].reshape(1, MIX_W), (1, tn // MIX_W))
        u, gu = fft_out(mats['e2'], p.reshape(2 * r, cols), xs, order, u, gu, bt, tn), 0
    return u.reshape(2, FFT_L // 2, MIX_W)


def hyena_context(xc, taps, inv_norm, bias):
    n = xc.shape[2]
    length = 2 * n
    k = jnp.arange(length, dtype=jnp.int32)
    c, s = _cis(k[:, None] * k[None, :], length)
    fr, fi = c[:, :n], -s[:, :n]
    m_fwd = jnp.concatenate([jnp.concatenate([fr, -fi], axis=1), jnp.concatenate([fi, fr], axis=1)], axis=0)
    m_taps = jnp.concatenate([c, -s], axis=0)
    er, ei = c[:n] / length, s[:n] / length
    m_inv = jnp.concatenate([jnp.concatenate([er, -ei], axis=1), jnp.concatenate([ei, er], axis=1)], axis=0)
    x1, x2, u = (xc[g].astype(jnp.float32) for g in range(3))
    for order, gate in enumerate((x1, x2)):
        h = mm(m_taps, taps[order], name="ctx_dft_taps") * inv_norm[order]
        w = mm(m_fwd, u.reshape(length, MIX_W), name="ctx_dft_fwd")
        wr, wi, hr, hi = w[:length], w[length:], h[:length], h[length:]
        zc = jnp.concatenate([wr * hr - wi * hi, wr * hi + wi * hr], axis=0)
        y = mm(m_inv, zc, name="ctx_dft_inv").reshape(2, n, MIX_W)
        u = gate * (y + u * bias[order])
    return u.astype(jnp.bfloat16)


def hyena_features(n, ff1, ff1_b, freq, ff2, ff2_b):
    k = jnp.arange(n, dtype=jnp.int32)
    pos = jnp.stack([k, (n - k) % n]).astype(jnp.float32)
    t = pos / (n - 1)
    fw = (2.0 * math.pi * pos / n)[..., None] * jnp.linspace(1e-4, HY_BANDS - 1, HY_BANDS, dtype=jnp.float32)
    feats = jnp.concatenate([t[..., None], jnp.cos(fw), -jnp.sin(fw)], axis=-1)
    h = jnp.sin(freq * (feats @ ff1 + ff1_b))
    return jnp.sin(freq * (h @ ff2 + ff2_b))


def _filter_kernel(feat_ref, w_ref, delta_ref, taps_ref, sum_ref, *, n, tm):
    dr = pl.program_id(1)
    i = pl.program_id(2)
    row = i * tm + lax.broadcasted_iota(jnp.int32, (tm, MIX_W), 0)
    lag = jnp.where(dr == 0, row, jnp.where(row == 0, 0, n - row))
    t = lag.astype(jnp.float32) / (n - 1)
    h = jnp.dot(feat_ref[...].astype(jnp.bfloat16), w_ref[...].astype(jnp.bfloat16),
                preferred_element_type=jnp.float32)
    h = h * jnp.exp(-t * delta_ref[...])

    @pl.when((dr == 0) & (i == 0))
    def _():
        sum_ref[...] = jnp.zeros_like(sum_ref)

    sum_ref[...] += jnp.sum(jnp.abs(h), axis=0, keepdims=True)
    taps_ref[...] = jnp.where((dr == 0) | (row != 0), h, 0.0).astype(taps_ref.dtype)


def hyena_filter_taps(feats, ff3):
    _, n, nf = feats.shape
    tm = min(n, 1024)
    nb = n // tm
    w = ff3.reshape(nf, 2, 2, MIX_W).transpose(1, 2, 0, 3)
    deltas = jnp.abs(jnp.linspace(math.log(HY_TARGET) / HY_SLOW_PCT, math.log(HY_TARGET) / HY_FAST_PCT,
                                  MIX_W, dtype=jnp.float32)).reshape(1, MIX_W)
    return pl.pallas_call(
        functools.partial(_filter_kernel, n=n, tm=tm),
        grid=(2, 2, nb),
        in_specs=[pl.BlockSpec((None, tm, nf), lambda o, dr, i: (dr, i, 0)),
                  pl.BlockSpec((None, None, nf, MIX_W), lambda o, dr, i: (o, dr, 0, 0)),
                  pl.BlockSpec((1, MIX_W), lambda o, dr, i: (0, 0))],
        out_specs=[pl.BlockSpec((None, tm, MIX_W), lambda o, dr, i: (o, dr * nb + i, 0)),
                   pl.BlockSpec((None, 1, MIX_W), lambda o, dr, i: (o, 0, 0))],
        out_shape=[jax.ShapeDtypeStruct((2, 2 * n, MIX_W), jnp.bfloat16),
                   jax.ShapeDtypeStruct((2, 1, MIX_W), jnp.float32)],
        compiler_params=_cparams("arbitrary", "arbitrary", "arbitrary"),
        name="hyena_filter",
    )(feats, w, deltas)


def _merge_kernel(yh_ref, yl_ref, yg_ref, w_ref, g0_ref, g1_ref, g2_ref, o_ref):
    acc = None
    for y_ref, g_ref, j in ((yh_ref, g0_ref, 0), (yl_ref, g1_ref, 1), (yg_ref, g2_ref, 2)):
        gate = 1.0 / (1.0 + jnp.exp(-g_ref[...].astype(jnp.float32)))
        t = gate * jnp.dot(y_ref[...], w_ref[j], preferred_element_type=jnp.float32)
        acc = t if acc is None else acc + t
    o_ref[...] = acc.astype(o_ref.dtype)


def merge_branches(z, ys, w_branch, layer):
    rows = z.shape[0]
    tm = ROW_T // 2
    yspec = pl.BlockSpec((tm, MIX_W), lambda i: (i, 0))

    def gspec(k):
        return pl.BlockSpec((tm, D_MODEL), lambda i: (i, GATE_COL0 // D_MODEL + k))

    return pl.pallas_call(
        _merge_kernel,
        grid=(rows // tm,),
        in_specs=[yspec, yspec, yspec,
                  pl.BlockSpec((None, N_BRANCH, MIX_W, D_MODEL), lambda i: (layer, 0, 0, 0)),
                  gspec(0), gspec(1), gspec(2)],
        out_specs=pl.BlockSpec((tm, D_MODEL), lambda i: (i, 0)),
        out_shape=jax.ShapeDtypeStruct((rows, D_MODEL), jnp.bfloat16),
        compiler_params=_cparams("parallel"),
        name="merge",
    )(*ys, w_branch, z, z, z)


def _out_proj_kernel(a_ref, w_ref, x_ref, g1_ref, gt_ref, g2_ref, sh_ref, sc_ref, wr_ref, xo_ref, h_ref, lg_ref):
    y = jnp.dot(a_ref[...], w_ref[...], preferred_element_type=jnp.float32)
    y = y * lax.rsqrt(jnp.mean(y * y, axis=-1, keepdims=True) + EPS) * g1_ref[...]
    x = x_ref[...] + gt_ref[...] * y
    xo_ref[...] = x
    h = x * lax.rsqrt(jnp.mean(x * x, axis=-1, keepdims=True) + EPS) * g2_ref[...]
    h = (h * (1.0 + sc_ref[...]) + sh_ref[...]).astype(jnp.bfloat16)
    h_ref[...] = h
    lg_ref[...] = jnp.dot(h, wr_ref[...], preferred_element_type=jnp.float32)


def out_proj(acc, w_out, layer, x, g1, gt, g2, sh, sc, w_router, n_ctx_rows, n_lat_rows):
    rows, d = x.shape
    tm = ROW_T // 2
    idx = functools.partial(_mod_index, n_ctx_tiles=2 * n_ctx_rows // tm, tiles_per_batch=n_lat_rows // tm)
    row = pl.BlockSpec((tm, d), lambda i: (i, 0))
    vec = pl.BlockSpec((1, d), lambda i: (0, 0))
    mod = pl.BlockSpec((None, 1, d), lambda i: (idx(i), 0, 0))
    wr = jnp.zeros((d, ROUTER_PAD), jnp.bfloat16).at[:, :N_EXPERTS].set(w_router.astype(jnp.bfloat16))
    return pl.pallas_call(
        _out_proj_kernel,
        grid=(rows // tm,),
        in_specs=[row, pl.BlockSpec((None, d, d), lambda i: (layer, 0, 0)), row, vec, mod, vec, mod, mod,
                  pl.BlockSpec((d, ROUTER_PAD), lambda i: (0, 0))],
        out_specs=[row, row, pl.BlockSpec((tm, ROUTER_PAD), lambda i: (i, 0))],
        out_shape=[jax.ShapeDtypeStruct((rows, d), jnp.float32), jax.ShapeDtypeStruct((rows, d), jnp.bfloat16),
                   jax.ShapeDtypeStruct((rows, ROUTER_PAD), jnp.float32)],
        compiler_params=_cparams("parallel"),
        name="out_proj",
    )(acc, w_out, x, g1.reshape(1, d), gt, g2.reshape(1, d), sh, sc, wr)


def _ffn_up_kernel(x_ref, wg_ref, wu_ref, h_ref):
    x = x_ref[...]
    hg = jnp.dot(x, wg_ref[...].astype(jnp.bfloat16), preferred_element_type=jnp.float32)
    hu = jnp.dot(x, wu_ref[...].astype(jnp.bfloat16), preferred_element_type=jnp.float32)
    h_ref[...] = (hg / (1.0 + jnp.exp(-hg)) * hu).astype(h_ref.dtype)


def _ffn_down_kernel(h_ref, wd_ref, g_ref, o_ref):
    y = jnp.dot(h_ref[...], wd_ref[...].astype(jnp.bfloat16), preferred_element_type=jnp.float32)
    o_ref[...] = (y * g_ref[...]).astype(o_ref.dtype)


def expert_ffn(xe, w_gate, w_up, w_down, layer, g):
    ne, r, d = xe.shape
    dff = w_gate.shape[-1]
    hid = pl.pallas_call(
        _ffn_up_kernel,
        grid=(ne, dff // FFN_TF),
        in_specs=[pl.BlockSpec((None, r, d), lambda e, f: (e, 0, 0)),
                  pl.BlockSpec((None, None, d, FFN_TF), lambda e, f: (layer, e, 0, f)),
                  pl.BlockSpec((None, None, d, FFN_TF), lambda e, f: (layer, e, 0, f))],
        out_specs=pl.BlockSpec((None, r, FFN_TF), lambda e, f: (e, 0, f)),
        out_shape=jax.ShapeDtypeStruct((ne, r, dff), jnp.bfloat16),
        compiler_params=_cparams("parallel", "arbitrary"),
        name="ffn_up",
    )(xe, w_gate, w_up)
    return pl.pallas_call(
        _ffn_down_kernel,
        grid=(ne, d // FFN_TF),
        in_specs=[pl.BlockSpec((None, r, dff), lambda e, j: (e, 0, 0)),
                  pl.BlockSpec((None, None, dff, FFN_TF), lambda e, j: (layer, e, 0, j)),
                  pl.BlockSpec((None, r, 1), lambda e, j: (e, 0, 0))],
        out_specs=pl.BlockSpec((None, r, FFN_TF), lambda e, j: (e, 0, j)),
        out_shape=jax.ShapeDtypeStruct((ne, r, d), jnp.bfloat16),
        compiler_params=_cparams("parallel", "arbitrary"),
        name="ffn_down",
    )(hid, w_down, g)


def route(logits, row0, bsz, n):
    cap = EC_CAPACITY * n // N_EXPERTS
    aff = jax.nn.softmax(logits[row0:row0 + bsz * n, :N_EXPERTS].reshape(bsz, n, N_EXPERTS), axis=-1)
    g, idx = lax.top_k(jnp.swapaxes(aff, 1, 2), cap)
    flat = idx + (row0 + jnp.arange(bsz, dtype=idx.dtype) * n)[:, None, None]
    return (jnp.swapaxes(g, 0, 1).reshape(N_EXPERTS, bsz * cap),
            jnp.swapaxes(flat, 0, 1).reshape(N_EXPERTS, bsz * cap))


def ec_moe(h, logits, w_gate, w_up, w_down, layer, segments):
    gs, flats = zip(*(route(logits, *seg) for seg in segments))
    flat, g = lax.sort_key_val(jnp.concatenate(flats, axis=1), jnp.concatenate(gs, axis=1), dimension=1)
    return expert_ffn(h[flat], w_gate, w_up, w_down, layer, g[..., None]), flat


def moe_plan(flat, n_rows):
    ne, r = flat.shape
    nwin = r // MOE_WIN
    n_tiles = n_rows // MOE_TT
    bounds = jnp.arange(n_tiles + 1, dtype=flat.dtype) * MOE_TT
    p = jnp.sum(flat[:, :, None] < bounds[None, None, :], axis=1, dtype=jnp.int32)
    p0, p1 = p[:, :-1].T, p[:, 1:].T
    w0 = p0 // MOE_WIN
    nw = jnp.where(p1 > p0, (p1 - 1) // MOE_WIN - w0 + 1, 0)
    off = jnp.cumsum(nw, axis=1)
    q = jnp.arange(MOE_MAXQ, dtype=jnp.int32)
    e_of_q = jnp.minimum(jnp.sum(q[None, :, None] >= off[:, None, :], axis=2, dtype=jnp.int32), ne - 1)
    w = jnp.take_along_axis(w0, e_of_q, axis=1) + q[None, :] - jnp.take_along_axis(off - nw, e_of_q, axis=1)
    items = jnp.clip(e_of_q * nwin + w, 0, ne * nwin - 1)
    return items.astype(jnp.int32), off[:, -1].astype(jnp.int32)


def _combine_kernel(items_ref, count_ref, tok_ref, ye_ref, x_ref, g_ref, gt_ref, *rest, nxt, skip):
    if nxt:
        gn_ref, sh_ref, sc_ref, o_ref, u_ref, stage, sem, acc_ref = rest
    else:
        o_ref, stage, sem, acc_ref = rest
    i = pl.program_id(0)
    t = i + skip
    n = count_ref[t]
    n_groups = (n + MOE_GROUP - 1) // MOE_GROUP

    @pl.when(i == 0)
    def _():
        stage[...] = jnp.zeros_like(stage)

    def window_copy(g, buf, k):
        item = items_ref[t, jnp.minimum(MOE_GROUP * g + k, MOE_MAXQ - 1)]
        return pltpu.make_async_copy(ye_ref.at[pl.ds(item * MOE_WIN, MOE_WIN)],
                                     stage.at[buf, pl.ds(k * MOE_WIN, MOE_WIN)], sem.at[buf])

    def for_group(g, buf, wait):
        for k in range(MOE_GROUP):
            @pl.when(MOE_GROUP * g + k < n)
            def _():
                cp = window_copy(g, buf, k)
                cp.wait() if wait else cp.start()

    acc_ref[...] = jnp.zeros_like(acc_ref)

    @pl.when(n_groups > 0)
    def _():
        for_group(0, 0, False)

    row = t * MOE_TT + lax.broadcasted_iota(jnp.int32, (MOE_TT, MOE_GROUP * MOE_WIN), 0)

    def body(g, carry):
        buf = g % 2

        @pl.when(g + 1 < n_groups)
        def _():
            for_group(g + 1, 1 - buf, False)

        for_group(g, buf, True)
        toks = []
        for k in range(MOE_GROUP):
            qk = MOE_GROUP * g + k
            item = items_ref[t, jnp.minimum(qk, MOE_MAXQ - 1)]
            toks.append(jnp.where(qk < n, tok_ref[pl.ds(item, 1), :], -1))
        onehot = jnp.where(row == jnp.concatenate(toks, axis=1), 1.0, 0.0).astype(jnp.bfloat16)
        acc_ref[...] += jnp.dot(onehot, stage[buf], preferred_element_type=jnp.float32)
        return carry

    lax.fori_loop(0, n_groups, body, 0)

    y = acc_ref[...]
    y = y * lax.rsqrt(jnp.mean(y * y, axis=-1, keepdims=True) + EPS) * g_ref[...]
    x = x_ref[...] + gt_ref[...] * y
    if nxt:
        u = x * lax.rsqrt(jnp.mean(x * x, axis=-1, keepdims=True) + EPS) * gn_ref[...]
        u_ref[...] = (u * (1.0 + sc_ref[...]) + sh_ref[...]).astype(u_ref.dtype)
    o_ref[...] = x


def moe_combine(x, ye, flat, g, gt, n_ctx_rows, n_lat_rows, nxt=None):
    rows, d = x.shape
    ne, r, _ = ye.shape
    items, count = moe_plan(flat, rows)
    n_ctx_tiles = 2 * n_ctx_rows // MOE_TT
    idx = functools.partial(_mod_index, n_ctx_tiles=n_ctx_tiles, tiles_per_batch=n_lat_rows // MOE_TT)
    skip = 0 if nxt else n_ctx_tiles
    row = pl.BlockSpec((MOE_TT, d), lambda i, *_: (i + skip, 0))
    vec = pl.BlockSpec((1, d), lambda i, *_: (0, 0))
    mod = pl.BlockSpec((None, 1, d), lambda i, *_: (idx(i + skip), 0, 0))
    out_row = pl.BlockSpec((MOE_TT, d), lambda i, *_: (i, 0))
    n_out = rows - skip * MOE_TT
    in_specs = [pl.BlockSpec((ne * r // MOE_WIN, MOE_WIN), lambda i, *_: (0, 0)),
                pl.BlockSpec(memory_space=pl.ANY), row, vec, mod]
    args = [flat.reshape(ne * r // MOE_WIN, MOE_WIN), ye.reshape(ne * r, d), x, g.reshape(1, d), gt]
    out_specs, out_shape = out_row, jax.ShapeDtypeStruct((n_out, d), jnp.float32)
    if nxt:
        in_specs += [vec, mod, mod]
        args += [nxt[0].reshape(1, d), nxt[1], nxt[2]]
        out_specs = [out_row, out_row]
        out_shape = [out_shape, jax.ShapeDtypeStruct((n_out, d), jnp.bfloat16)]
    return pl.pallas_call(
        functools.partial(_combine_kernel, nxt=bool(nxt), skip=skip),
        grid_spec=pltpu.PrefetchScalarGridSpec(
            num_scalar_prefetch=2,
            grid=(n_out // MOE_TT,),
            in_specs=in_specs, out_specs=out_specs,
            scratch_shapes=[pltpu.VMEM((2, MOE_GROUP * MOE_WIN, d), jnp.bfloat16),
                            pltpu.SemaphoreType.DMA((2,)),
                            pltpu.VMEM((MOE_TT, d), jnp.float32)]),
        out_shape=out_shape,
        compiler_params=_cparams("arbitrary"),
        name="moe_combine",
    )(items, count, *args)


def kernel(x, c, ctx, c_ctx, w_mod, b_mod, norm_g, w_in, hy_conv_w, hy_conv_b, hy_ff1, hy_ff1_b, hy_freq,
           hy_ff2, hy_ff2_b, hy_ff3, hy_bias, lru_conv_w, lru_conv_b, lru_wa, lru_ba, lru_wi, lru_bi,
           lru_lambda, hg_lb_logits, hg_norm_g, w_branch, w_out, w_router, w_gate, w_up, w_down):
    bsz, n_lat, d = x.shape
    n_ctx = ctx.shape[1]
    assert bsz == 2 and d == D_MODEL and n_ctx == SEQ_T and 2 * n_lat == FFT_L and n_lat % ROW_T == 0
    gam = jax.nn.softmax(hg_lb_logits.astype(jnp.float32), axis=0)
    lb_all = jnp.maximum(jnp.cumsum(gam, axis=0) - gam[:1], 0.0)
    mats = fft_matrices()
    cvec = jnp.zeros((SUBLANES, d), jnp.float32).at[:bsz].set(jax.nn.silu(c)).at[bsz].set(jax.nn.silu(c_ctx))
    n_ctx_all = bsz * n_ctx
    xa = jnp.concatenate([ctx.reshape(n_ctx_all, d), x.reshape(bsz * n_lat, d)], axis=0)
    w_in_b, w_branch_b, w_out_b = (w.astype(jnp.bfloat16) for w in (w_in, w_branch, w_out))
    mods = []
    for l in range(DEPTH):
        mod = mm(cvec, w_mod, layer=l, name="adaln")[:bsz + 1] + b_mod[l]
        mods.append([mod.reshape(bsz + 1, 1, 6, d)[:, :, k] for k in range(6)])
    u = prenorm(xa, norm_g[0, 0], mods[0][0], mods[0][1], n_ctx, n_lat)
    for l in range(DEPTH):
        last = l == DEPTH - 1
        p = {'lru_conv_w': lru_conv_w[l], 'lru_conv_b': lru_conv_b[l], 'lru_wa': lru_wa[l], 'lru_ba': lru_ba[l],
             'lru_wi': lru_wi[l], 'lru_bi': lru_bi[l], 'lru_lambda': lru_lambda[l]}
        sh1, sc1, gt1, sh2, sc2, gt2 = mods[l]

        z = mm(u, w_in_b, jnp.bfloat16, tm=IN_PROJ_TM, tn=1024, layer=l, name="in_proj")

        y_lru = lru_branch(z, p, n_ctx, n_lat)
        y_hg = hgrn_branch(z, lb_all[l], hg_norm_g[l], n_ctx, n_lat)

        ffp = (hy_ff1[l], hy_ff1_b[l], hy_freq[l], hy_ff2[l], hy_ff2_b[l])
        taps_l, norm_l = hyena_filter_taps(hyena_features(n_lat, *ffp), hy_ff3[l])
        taps_c, norm_c = hyena_filter_taps(hyena_features(n_ctx, *ffp), hy_ff3[l])
        xc_l = hy_conv(z, hy_conv_w[l], hy_conv_b[l], n_ctx_all // SEQ_T, n_lat, False)
        y_hy_l = hyena_latent(xc_l, taps_l, 1.0 / norm_l, hy_bias[l], mats)
        xc_c = hy_conv(z, hy_conv_w[l], hy_conv_b[l], 0, n_ctx, True)
        y_hy_c = hyena_context(xc_c, taps_c, 1.0 / norm_c, hy_bias[l])
        y_hy = jnp.concatenate([y_hy_c.reshape(n_ctx_all, MIX_W), y_hy_l.reshape(bsz * n_lat, MIX_W)], axis=0)

        acc = merge_branches(z, (y_hy, y_lru, y_hg), w_branch_b, l)
        xa, h, logits = out_proj(acc, w_out_b, l, xa, norm_g[l, 1], gt1, norm_g[l, 2], sh2, sc2,
                                 w_router[l], n_ctx, n_lat)
        segments = [(n_ctx_all, bsz, n_lat)] if last else [(n_ctx_all, bsz, n_lat), (0, bsz, n_ctx)]
        ye, flat = ec_moe(h, logits, w_gate, w_up, w_down, l, segments)
        if last:
            xa = moe_combine(xa, ye, flat, norm_g[l, 3], gt2, n_ctx, n_lat)
        else:
            nxt = (norm_g[l + 1, 0], mods[l + 1][0], mods[l + 1][1])
            xa, u = moe_combine(xa, ye, flat, norm_g[l, 3], gt2, n_ctx, n_lat, nxt)
    return xa.reshape(bsz, n_lat, d)
```

```python
import functools
import math

import numpy as np
import jax
import jax.numpy as jnp
from jax import lax
from jax.experimental import pallas as pl
from jax.experimental.pallas import tpu as pltpu

D_MODEL = 2048
DEPTH = 2
GRID_W = 64
MIX_W = D_MODEL // 2
N_BRANCH = 3
HY_BANDS = 16
HY_TARGET = 1e-2
HY_FAST_PCT = 0.3
HY_SLOW_PCT = 1.5
LRU_BLOCKS = 8
LRU_BS = MIX_W // LRU_BLOCKS
LRU_C = 8.0
HG_HEADS = 8
HG_DK = MIX_W // HG_HEADS
N_EXPERTS = 16
EC_CAPACITY = 2
EPS = 1e-6
TINY = 1e-30
LOG2E = 1.4426950408889634
HY_COL0 = 0
LRU_COL0 = 3 * MIX_W
HG_COL0 = 5 * MIX_W
GATE_COL0 = 10 * MIX_W
IN_COLS = GATE_COL0 + N_BRANCH * D_MODEL

SUBLANES = 8
LANES = 128
SEQ_T = 256
HG_C = 128
HG_LEVELS = 7
HG_HB = 8
ROW_T = 512
IN_PROJ_TM = 1536
FFT_R = 128
FFT_L = FFT_R * FFT_R
FFT_KB = 8
FFT_SB = 16
ROUTER_PAD = LANES
FFN_TF = 256
MOE_TT = 256
MOE_WIN = 64
MOE_GROUP = 8
MOE_NBUF = 3
MOE_MAXQ = N_EXPERTS * (MOE_TT // MOE_WIN + 1)
VMEM_LIMIT_BYTES = 48 * 1024 * 1024


def _cparams(*sem):
    return pltpu.CompilerParams(dimension_semantics=sem, vmem_limit_bytes=VMEM_LIMIT_BYTES)


def _pick_tile(n, pref):
    for t in (pref, 1024, 512, 256, 128):
        if t <= n and n % t == 0:
            return t
    return n


def _row_block(b, j, n_ctx, n_lat, reverse):
    if reverse:
        kc, kl = n_ctx - 1 - j, n_ctx + n_lat - 1 - j
    else:
        kc, kl = j, j - n_ctx
    return jnp.where(j < n_ctx, b * n_ctx + kc, 2 * n_ctx + b * n_lat + kl)


def _mm_kernel(a_ref, b_ref, o_ref):
    o_ref[...] = jnp.dot(a_ref[...].astype(jnp.bfloat16), b_ref[...].astype(jnp.bfloat16),
                         preferred_element_type=jnp.float32).astype(o_ref.dtype)


def mm(a, b, out_dtype=jnp.float32, tm=512, tn=1024, layer=None, name="mm"):
    m, k = a.shape
    n = b.shape[-1]
    tm = _pick_tile(m, tm)
    tn = _pick_tile(n, tn)
    if layer is None:
        bspec = pl.BlockSpec((k, tn), lambda i, j: (0, j))
    else:
        bspec = pl.BlockSpec((None, k, tn), lambda i, j: (layer, 0, j))
    return pl.pallas_call(
        _mm_kernel,
        grid=(m // tm, n // tn),
        in_specs=[pl.BlockSpec((tm, k), lambda i, j: (i, 0)), bspec],
        out_specs=pl.BlockSpec((tm, tn), lambda i, j: (i, j)),
        out_shape=jax.ShapeDtypeStruct((m, n), out_dtype),
        compiler_params=_cparams("parallel", "arbitrary"),
        name=name,
    )(a, b)


def _mod_index(i, n_ctx_tiles, tiles_per_batch):
    return jnp.where(i < n_ctx_tiles, 2, (i - n_ctx_tiles) // tiles_per_batch)


def _prenorm_kernel(x_ref, g_ref, sh_ref, sc_ref, u_ref):
    x = x_ref[...]
    y = x * lax.rsqrt(jnp.mean(x * x, axis=-1, keepdims=True) + EPS) * g_ref[...]
    u_ref[...] = (y * (1.0 + sc_ref[...]) + sh_ref[...]).astype(u_ref.dtype)


def prenorm(x, g, sh, sc, n_ctx_rows, n_lat_rows):
    rows, d = x.shape
    idx = functools.partial(_mod_index, n_ctx_tiles=2 * n_ctx_rows // ROW_T, tiles_per_batch=n_lat_rows // ROW_T)
    row = pl.BlockSpec((ROW_T, d), lambda i: (i, 0))
    vec = pl.BlockSpec((None, 1, d), lambda i: (idx(i), 0, 0))
    return pl.pallas_call(
        _prenorm_kernel,
        grid=(rows // ROW_T,),
        in_specs=[row, pl.BlockSpec((1, d), lambda i: (0, 0)), vec, vec],
        out_specs=row,
        out_shape=jax.ShapeDtypeStruct((rows, d), jnp.bfloat16),
        compiler_params=_cparams("parallel"),
        name="prenorm",
    )(x, g.reshape(1, d), sh, sc)


def _gelu_tanh(x):
    return 0.5 * x * (1.0 + jnp.tanh(math.sqrt(2.0 / math.pi) * (x + 0.044715 * (x * x * x))))


def _conv_pos(is_ctx, shape):
    t = lax.broadcasted_iota(jnp.int32, shape, 0)
    pos = jnp.where(is_ctx, t, t & (GRID_W - 1))
    last = jnp.where(is_ctx, SEQ_T - 1, GRID_W - 1)
    return pos, last


def _lru_pass_kernel(x_ref, cw_ref, cb_ref, w_ref, gb_ref, lam_ref, *rest, reverse, final):
    if final:
        lg_ref, hprev_ref, out_ref, h_ref, a_ref, b_ref = rest
    else:
        out_ref, h_ref, a_ref, b_ref = rest
    j = pl.program_id(1)

    @pl.when(j == 0)
    def _():
        h_ref[...] = jnp.zeros_like(h_ref)

    x = x_ref[...].astype(jnp.float32)
    pos, last = _conv_pos(j == 0, x.shape)
    cw = cw_ref[...]
    xc = cb_ref[...] + cw[2:3] * x
    xc = xc + cw[0:1] * jnp.where(pos >= 2, pltpu.roll(x, 2, 0), 0.0)
    xc = xc + cw[1:2] * jnp.where(pos >= 1, pltpu.roll(x, 1, 0), 0.0)
    xc = xc + cw[3:4] * jnp.where(pos < last, pltpu.roll(x, SEQ_T - 1, 0), 0.0)

    xb = xc.astype(jnp.bfloat16)
    lam = lam_ref[...]
    sp = jnp.maximum(-lam, 0.0) + jnp.log(1.0 + jnp.exp(-jnp.abs(lam)))
    n_tiles = SEQ_T // SUBLANES
    row = lax.broadcasted_iota(jnp.int32, (1, SUBLANES, LRU_BS), 1)
    for n in range(LRU_BLOCKS):
        sl = slice(n * LRU_BS, (n + 1) * LRU_BS)
        pre = jnp.dot(xb[:, sl], w_ref[n], preferred_element_type=jnp.float32) + gb_ref[n]
        gate = 1.0 / (1.0 + jnp.exp(-pre))
        r, i = gate[:, :LRU_BS], gate[:, LRU_BS:]
        a = jnp.exp(-LRU_C * r * sp[:, sl])
        om = 1.0 - a * a
        b = om * lax.rsqrt(jnp.maximum(om, TINY)) * (i * xc[:, sl])
        a = a.reshape(n_tiles, SUBLANES, LRU_BS)
        b = b.reshape(n_tiles, SUBLANES, LRU_BS)
        for d in (1, 2, 4):
            if reverse:
                m = row < SUBLANES - d
                a_sh, b_sh = pltpu.roll(a, SUBLANES - d, 1), pltpu.roll(b, SUBLANES - d, 1)
            else:
                m = row >= d
                a_sh, b_sh = pltpu.roll(a, d, 1), pltpu.roll(b, d, 1)
            b = jnp.where(m, a * b_sh + b, b)
            a = jnp.where(m, a * a_sh, a)
        a_ref[:, sl] = a.reshape(SEQ_T, LRU_BS)
        b_ref[:, sl] = b.reshape(SEQ_T, LRU_BS)

    h = h_ref[...]
    for k in range(n_tiles):
        kk = n_tiles - 1 - k if reverse else k
        rs = slice(kk * SUBLANES, (kk + 1) * SUBLANES)
        ht = a_ref[rs, :] * h + b_ref[rs, :]
        h = ht[0:1, :] if reverse else ht[SUBLANES - 1:SUBLANES, :]
        if final:
            ht = (ht + hprev_ref[rs, :]) * _gelu_tanh(lg_ref[rs, :].astype(jnp.float32))
        out_ref[rs, :] = ht.astype(out_ref.dtype)
    h_ref[...] = h


def lru_pass(z, conv_w, conv_b, wa, ba, wi, bi, lam, n_ctx_rows, n_lat_rows, reverse, h_prev=None):
    rows = z.shape[0]
    n_lat = n_lat_rows // SEQ_T
    cb = LRU_COL0 // MIX_W
    final = h_prev is not None
    rb = functools.partial(_row_block, n_ctx=n_ctx_rows // SEQ_T, n_lat=n_lat, reverse=reverse)

    def zspec(k):
        return pl.BlockSpec((SEQ_T, MIX_W), lambda b, j: (rb(b, j), cb + k))

    row_spec = pl.BlockSpec((SEQ_T, MIX_W), lambda b, j: (rb(b, j), 0))

    def const_spec(shape):
        return pl.BlockSpec(shape, lambda b, j: (0,) * len(shape))

    w2 = jnp.concatenate([wa, wi], axis=-1).astype(jnp.bfloat16)
    gb = jnp.concatenate([ba.reshape(LRU_BLOCKS, 1, LRU_BS), bi.reshape(LRU_BLOCKS, 1, LRU_BS)], axis=-1)
    in_specs = [zspec(0), const_spec((4, MIX_W)), const_spec((1, MIX_W)),
                const_spec((LRU_BLOCKS, LRU_BS, 2 * LRU_BS)), const_spec((LRU_BLOCKS, 1, 2 * LRU_BS)),
                const_spec((1, MIX_W))]
    args = [z, conv_w, conv_b.reshape(1, MIX_W), w2, gb, lam.reshape(1, MIX_W)]
    if final:
        in_specs += [zspec(1), row_spec]
        args += [z, h_prev]
    return pl.pallas_call(
        functools.partial(_lru_pass_kernel, reverse=reverse, final=final),
        grid=(2, n_ctx_rows // SEQ_T + n_lat),
        in_specs=in_specs,
        out_specs=row_spec,
        out_shape=jax.ShapeDtypeStruct((rows, MIX_W), jnp.bfloat16 if final else jnp.float32),
        scratch_shapes=[pltpu.VMEM((1, MIX_W), jnp.float32),
                        pltpu.VMEM((SEQ_T, MIX_W), jnp.float32),
                        pltpu.VMEM((SEQ_T, MIX_W), jnp.float32)],
        compiler_params=_cparams("parallel", "arbitrary"),
        name="lru_bwd" if reverse else "lru_fwd",
    )(*args)


def lru_branch(z, p, n_ctx_rows, n_lat_rows):
    def one(d, h_prev):
        return lru_pass(z, p['lru_conv_w'], p['lru_conv_b'], p['lru_wa'][d], p['lru_ba'][d], p['lru_wi'][d],
                        p['lru_bi'][d], p['lru_lambda'][d], n_ctx_rows, n_lat_rows, d == 1, h_prev)
    return one(1, one(0, None))


def _hg_level_matrix(reverse):
    t = np.arange(HG_C)[:, None]
    s = np.arange(HG_C)[None, :]
    x = t ^ s
    lv = np.where(x > 0, np.floor(np.log2(np.maximum(x, 1))).astype(np.int32), -1)
    lv = np.where(s < t, lv, -1).astype(np.int32)
    return jnp.asarray(lv.T if reverse else lv)


def _hg_pass_kernel(lv_ref, lb_ref, q_ref, f_ref, v_ref, *rest, reverse, final):
    if final:
        og_ref, oprev_ref, g_ref, out_ref, st_ref = rest
    else:
        out_ref, st_ref = rest

    @pl.when(pl.program_id(1) == 0)
    def _():
        st_ref[...] = jnp.zeros_like(st_ref)

    lv = lv_ref[...]
    n_tiles = HG_C // SUBLANES
    wb = HG_HB * HG_DK
    row = lax.broadcasted_iota(jnp.int32, (1, SUBLANES, wb), 1)
    for hb in range(HG_HEADS // HG_HB):
        sl = slice(hb * wb, (hb + 1) * wb)
        heads = [(hb * HG_HB + j, slice(j * HG_DK, (j + 1) * HG_DK)) for j in range(HG_HB)]
        x = f_ref[:, sl].astype(jnp.float32)
        lb = lb_ref[:, sl]
        e = jnp.exp(-jnp.abs(x))
        inv = 1.0 / (1.0 + e)
        pos = x >= 0
        sig = jnp.where(pos, inv, e * inv)
        sigm = jnp.where(pos, e * inv, inv)
        lsig = jnp.minimum(x, 0.0) - jnp.log(1.0 + e)
        lf = jnp.where(lb > 0, jnp.log(lb + (1.0 - lb) * sig), lsig)
        kk = (1.0 - lb) * sigm
        q = q_ref[:, sl].astype(jnp.float32)
        v = v_ref[:, sl]

        c = (lf * LOG2E).reshape(n_tiles, SUBLANES, wb)
        bt = c
        att = [None] * HG_HB
        for lvl in range(HG_LEVELS):
            m = 1 << lvl
            qm = (q * jnp.exp2(c).reshape(HG_C, wb)).astype(jnp.bfloat16)
            km = (kk * jnp.exp2(bt - c).reshape(HG_C, wb)).astype(jnp.bfloat16)
            for j, (_, hs) in enumerate(heads):
                d = lax.dot_general(qm[:, hs], km[:, hs], (((1,), (1,)), ((), ())),
                                    preferred_element_type=jnp.float32)
                att[j] = jnp.where(lv == 0, d, 0.0) if lvl == 0 else jnp.where(lv == lvl, d, att[j])
            if m < SUBLANES:
                upper = (row & m) != 0
                down = pltpu.roll(bt, m, 1)
                up = pltpu.roll(bt, SUBLANES - m, 1)
                if reverse:
                    c = c + jnp.where(upper, 0.0, up)
                else:
                    c = c + jnp.where(upper, down, 0.0)
                bt = bt + jnp.where(upper, down, up)
            else:
                k = m // SUBLANES
                pair = (n_tiles // (2 * k), 2, k, SUBLANES, wb)
                c5, b5 = c.reshape(pair), bt.reshape(pair)
                tot2 = b5[:, 0] + b5[:, 1]
                if reverse:
                    c = jnp.stack([c5[:, 0] + b5[:, 1], c5[:, 1]], axis=1)
                else:
                    c = jnp.stack([c5[:, 0], c5[:, 1] + b5[:, 0]], axis=1)
                c = c.reshape(n_tiles, SUBLANES, wb)
                bt = jnp.stack([tot2, tot2], axis=1).reshape(n_tiles, SUBLANES, wb)

        qm = (q * jnp.exp2(c).reshape(HG_C, wb)).astype(jnp.bfloat16)
        km = (kk * jnp.exp2(bt - c).reshape(HG_C, wb)).astype(jnp.bfloat16)
        tot = jnp.exp2(bt[0, 0:1, :])
        diag = q * kk
        for j, (h, hs) in enumerate(heads):
            osl = slice(h * HG_DK, (h + 1) * HG_DK)
            st = st_ref[h]
            vh = v[:, hs]
            o = lax.dot_general(qm[:, hs], st.astype(jnp.bfloat16), (((1,), (1,)), ((), ())),
                                preferred_element_type=jnp.float32)
            o = o + jnp.dot(att[j].astype(jnp.bfloat16), vh, preferred_element_type=jnp.float32)
            o = o + jnp.sum(diag[:, hs], axis=-1, keepdims=True) * vh.astype(jnp.float32)
            st_ref[h] = st * tot[:, hs] + lax.dot_general(vh, km[:, hs], (((0,), (0,)), ((), ())),
                                                          preferred_element_type=jnp.float32)
            if final:
                o = o + oprev_ref[:, osl]
                y = o * lax.rsqrt(jnp.mean(o * o, axis=-1, keepdims=True) + EPS) * g_ref[:, osl]
                og = og_ref[:, osl].astype(jnp.float32)
                out_ref[:, osl] = (y * (og / (1.0 + jnp.exp(-og)))).astype(out_ref.dtype)
            else:
                out_ref[:, osl] = o


def hgrn_pass(z, lb, n_ctx_rows, n_lat_rows, reverse, o_prev=None, norm_g=None):
    rows = z.shape[0]
    n_ctx, n_lat = n_ctx_rows // HG_C, n_lat_rows // HG_C
    cb = HG_COL0 // MIX_W
    final = o_prev is not None
    rb = functools.partial(_row_block, n_ctx=n_ctx, n_lat=n_lat, reverse=reverse)

    def zspec(k):
        return pl.BlockSpec((HG_C, MIX_W), lambda b, j: (rb(b, j), cb + k))

    row_spec = pl.BlockSpec((HG_C, MIX_W), lambda b, j: (rb(b, j), 0))
    vec_spec = pl.BlockSpec((1, MIX_W), lambda b, j: (0, 0))
    in_specs = [pl.BlockSpec((HG_C, HG_C), lambda b, j: (0, 0)), vec_spec,
                zspec(0), zspec(2 if reverse else 1), zspec(3)]
    args = [_hg_level_matrix(reverse), lb.reshape(1, MIX_W), z, z, z]
    if final:
        in_specs += [zspec(4), row_spec, vec_spec]
        args += [z, o_prev, norm_g.reshape(1, MIX_W)]
    return pl.pallas_call(
        functools.partial(_hg_pass_kernel, reverse=reverse, final=final),
        grid=(2, n_ctx + n_lat),
        in_specs=in_specs,
        out_specs=row_spec,
        out_shape=jax.ShapeDtypeStruct((rows, MIX_W), jnp.bfloat16 if final else jnp.float32),
        scratch_shapes=[pltpu.VMEM((HG_HEADS, HG_DK, HG_DK), jnp.float32)],
        compiler_params=_cparams("parallel", "arbitrary"),
        name="hgrn_bwd" if reverse else "hgrn_fwd",
    )(*args)


def hgrn_branch(z, lb, norm_g, n_ctx_rows, n_lat_rows):
    o_f = hgrn_pass(z, lb, n_ctx_rows, n_lat_rows, False)
    return hgrn_pass(z, lb, n_ctx_rows, n_lat_rows, True, o_f, norm_g)


def _hy_conv_kernel(x_ref, w_ref, b_ref, out_ref, *, is_ctx):
    x = x_ref[...].astype(jnp.float32)
    t = lax.broadcasted_iota(jnp.int32, x.shape, 0)
    pos = t if is_ctx else t & (GRID_W - 1)
    last = SEQ_T - 1 if is_ctx else GRID_W - 1
    w = w_ref[...]
    y = b_ref[...] + w[1:2] * x
    y = y + w[0:1] * jnp.where(pos >= 1, pltpu.roll(x, 1, 0), 0.0)
    y = y + w[2:3] * jnp.where(pos < last, pltpu.roll(x, SEQ_T - 1, 0), 0.0)
    for g in range(3):
        out_ref[g] = y[:, g * MIX_W:(g + 1) * MIX_W].astype(out_ref.dtype)


def hy_conv(z, conv_w, conv_b, block0, n_rows, is_ctx):
    nb = n_rows // SEQ_T
    return pl.pallas_call(
        functools.partial(_hy_conv_kernel, is_ctx=is_ctx),
        grid=(2, nb),
        in_specs=[pl.BlockSpec((SEQ_T, 3 * MIX_W), lambda b, j: (block0 + b * nb + j, 0)),
                  pl.BlockSpec((3, 3 * MIX_W), lambda b, j: (0, 0)),
                  pl.BlockSpec((1, 3 * MIX_W), lambda b, j: (0, 0))],
        out_specs=pl.BlockSpec((3, None, SEQ_T, MIX_W), lambda b, j: (0, b, j, 0)),
        out_shape=jax.ShapeDtypeStruct((3, 2, n_rows, MIX_W), jnp.bfloat16),
        compiler_params=_cparams("parallel", "arbitrary"),
        name="hy_conv_ctx" if is_ctx else "hy_conv_lat",
    )(z, conv_w, conv_b.reshape(1, 3 * MIX_W))


def _cis(num, den):
    ang = (2.0 * math.pi / den) * (num % den).astype(jnp.float32)
    return jnp.cos(ang), jnp.sin(ang)


def fft_matrices():
    r = FFT_R
    i = jnp.arange(r, dtype=jnp.int32)
    c, s = _cis(i[:, None] * i[None, :], r)
    half = r // 2
    fa_data = jnp.concatenate([jnp.concatenate([c[:, :half], s[:, :half]], axis=1),
                               jnp.concatenate([-s[:, :half], c[:, :half]], axis=1)], axis=0)
    fa_taps = jnp.concatenate([c, -s], axis=0)
    k1, k2, s2 = i[:, None, None], i[None, :, None], i[None, None, :]
    gc, gs = _cis(s2 * (r * k2 + k1), FFT_L)
    gr, gi = gc, -gs
    m1 = jnp.concatenate([jnp.concatenate([gr, -gi], axis=2), jnp.concatenate([gi, gr], axis=2)], axis=1)
    grt, git = jnp.swapaxes(gr, 1, 2), jnp.swapaxes(gi, 1, 2)
    m2 = jnp.concatenate([jnp.concatenate([grt, git], axis=2), jnp.concatenate([-git, grt], axis=2)], axis=1)
    er, ei = c[:half] / FFT_L, s[:half] / FFT_L
    e2 = jnp.concatenate([jnp.concatenate([er, -ei], axis=1), jnp.concatenate([ei, er], axis=1)], axis=0)
    bf = jnp.bfloat16
    return dict(fa_data=fa_data.astype(bf), fa_taps=fa_taps.astype(bf), m1=m1.astype(bf), m2=m2.astype(bf),
                e2=e2.astype(bf))


def _fft_a_kernel(f_ref, x_ref, o_ref, *, two):
    f = f_ref[...]
    res = []
    for s in range(FFT_SB):
        cs = slice(s * MIX_W, (s + 1) * MIX_W)
        x = jnp.concatenate([x_ref[0, :, cs], x_ref[1, :, cs]], axis=0) if two else x_ref[:, cs]
        res.append(jnp.dot(f, x, preferred_element_type=jnp.float32).astype(o_ref.dtype))
    o_ref[...] = jnp.swapaxes(jnp.stack(res), 0, 1).reshape(o_ref.shape)


def fft_stage_a(fmat, x, g):
    two = x.ndim == 4
    tn = FFT_SB * MIX_W
    xspec = (pl.BlockSpec((None, 2, FFT_R // 2, tn), lambda j: (g, 0, 0, j)) if two
             else pl.BlockSpec((None, FFT_R, tn), lambda j: (g, 0, j)))
    return pl.pallas_call(
        functools.partial(_fft_a_kernel, two=two),
        grid=(FFT_R // FFT_SB,),
        in_specs=[pl.BlockSpec((2 * FFT_R, FFT_R), lambda j: (0, 0)), xspec],
        out_specs=pl.BlockSpec((2, FFT_R, FFT_SB, MIX_W), lambda j: (0, 0, j, 0)),
        out_shape=jax.ShapeDtypeStruct((2, FFT_R, FFT_R, MIX_W), jnp.bfloat16),
        compiler_params=_cparams("parallel"),
        name="fft_stage_a",
    )(fmat, x)


def _fft_mid_kernel(m1_ref, a_ref, *rest, conv):
    if conv:
        m2_ref, h_ref, o_ref = rest
    else:
        s_ref, o_ref = rest
    r = FFT_R
    for k in range(FFT_KB):
        a = a_ref[:, k].reshape(2 * r, MIX_W)
        x = jnp.dot(m1_ref[k], a, preferred_element_type=jnp.float32)
        if conv:
            h = h_ref[k].astype(jnp.float32)
            xr, xi, hr, hi = x[:r], x[r:], h[:r], h[r:]
            zc = jnp.concatenate([xr * hr - xi * hi, xr * hi + xi * hr], axis=0).astype(jnp.bfloat16)
            p = jnp.dot(m2_ref[k], zc, preferred_element_type=jnp.float32)
            o_ref[:, k] = p.reshape(2, r, MIX_W).astype(o_ref.dtype)
        else:
            o_ref[k] = (x * s_ref[...]).astype(o_ref.dtype)


def fft_mid(m1, a, m2=None, h=None, scale=None):
    conv = h is not None
    r = FFT_R
    mspec = pl.BlockSpec((FFT_KB, 2 * r, 2 * r), lambda j: (j, 0, 0))
    aspec = pl.BlockSpec((2, FFT_KB, r, MIX_W), lambda j: (0, j, 0, 0))
    hspec = pl.BlockSpec((FFT_KB, 2 * r, MIX_W), lambda j: (j, 0, 0))
    if conv:
        in_specs, args = [mspec, aspec, mspec, hspec], (m1, a, m2, h)
        out_specs, out_shape = aspec, jax.ShapeDtypeStruct((2, r, r, MIX_W), jnp.bfloat16)
    else:
        in_specs, args = [mspec, aspec, pl.BlockSpec((1, MIX_W), lambda j: (0, 0))], (m1, a, scale)
        out_specs, out_shape = hspec, jax.ShapeDtypeStruct((r, 2 * r, MIX_W), jnp.bfloat16)
    return pl.pallas_call(
        functools.partial(_fft_mid_kernel, conv=conv),
        grid=(r // FFT_KB,),
        in_specs=in_specs, out_specs=out_specs, out_shape=out_shape,
        compiler_params=_cparams("parallel"),
        name="fft_mid_conv" if conv else "fft_mid_filter",
    )(*args)


def _fft_out_kernel(e_ref, p_ref, mul_ref, add_ref, bias_ref, o_ref):
    e = e_ref[...]
    half = FFT_R // 2
    bias = bias_ref[...]
    p = jnp.swapaxes(p_ref[...].reshape(2 * FFT_R, FFT_SB, MIX_W), 0, 1)
    for s in range(FFT_SB):
        cs = slice(s * MIX_W, (s + 1) * MIX_W)
        y = jnp.dot(e, p[s], preferred_element_type=jnp.float32)
        for b in range(2):
            yb = y[b * half:(b + 1) * half] + add_ref[b, :, cs].astype(jnp.float32) * bias
            o_ref[b, :, cs] = (mul_ref[b, :, cs].astype(jnp.float32) * yb).astype(o_ref.dtype)


def fft_out(e2, p, mul, gm, add, ga, bias):
    half = FFT_R // 2
    tn = FFT_SB * MIX_W

    def bspec(g):
        return pl.BlockSpec((None, 2, half, tn), lambda j: (g, 0, 0, j))

    return pl.pallas_call(
        _fft_out_kernel,
        grid=(FFT_R // FFT_SB,),
        in_specs=[pl.BlockSpec((FFT_R, 2 * FFT_R), lambda j: (0, 0)),
                  pl.BlockSpec((2, FFT_R, FFT_SB, MIX_W), lambda j: (0, 0, j, 0)), bspec(gm), bspec(ga),
                  pl.BlockSpec((1, MIX_W), lambda j: (0, 0))],
        out_specs=bspec(0),
        out_shape=jax.ShapeDtypeStruct((1, 2, half, FFT_R * MIX_W), jnp.bfloat16),
        compiler_params=_cparams("parallel"),
        name="fft_out",
    )(e2, p, mul, add, bias)


def hyena_latent(xc, taps, inv_norm, bias, mats):
    r = FFT_R
    cols = r * MIX_W
    xs = xc.reshape(3, 2, r // 2, cols)
    ts = taps.reshape(2, r, cols)
    u, gu = xs, 2
    for order in range(2):
        spec = fft_mid(mats['m1'], fft_stage_a(mats['fa_taps'], ts, order), scale=inv_norm[order])
        p = fft_mid(mats['m1'], fft_stage_a(mats['fa_data'], u, gu), mats['m2'], spec)
        u, gu = fft_out(mats['e2'], p, xs, order, u, gu, bias[order].reshape(1, MIX_W)), 0
    return u.reshape(2, FFT_L // 2, MIX_W)


def hyena_context(xc, taps, inv_norm, bias):
    n = xc.shape[2]
    length = 2 * n
    k = jnp.arange(length, dtype=jnp.int32)
    c, s = _cis(k[:, None] * k[None, :], length)
    fr, fi = c[:, :n], -s[:, :n]
    m_fwd = jnp.concatenate([jnp.concatenate([fr, -fi], axis=1), jnp.concatenate([fi, fr], axis=1)], axis=0)
    m_taps = jnp.concatenate([c, -s], axis=0)
    er, ei = c[:n] / length, s[:n] / length
    m_inv = jnp.concatenate([jnp.concatenate([er, -ei], axis=1), jnp.concatenate([ei, er], axis=1)], axis=0)
    x1, x2, u = (xc[g].astype(jnp.float32) for g in range(3))
    for order, gate in enumerate((x1, x2)):
        h = mm(m_taps, taps[order], name="ctx_dft_taps") * inv_norm[order]
        w = mm(m_fwd, u.reshape(length, MIX_W), name="ctx_dft_fwd")
        wr, wi, hr, hi = w[:length], w[length:], h[:length], h[length:]
        zc = jnp.concatenate([wr * hr - wi * hi, wr * hi + wi * hr], axis=0)
        y = mm(m_inv, zc, name="ctx_dft_inv").reshape(2, n, MIX_W)
        u = gate * (y + u * bias[order])
    return u.astype(jnp.bfloat16)


def hyena_features(n, ff1, ff1_b, freq, ff2, ff2_b):
    k = jnp.arange(n, dtype=jnp.int32)
    pos = jnp.stack([k, (n - k) % n]).astype(jnp.float32)
    t = pos / (n - 1)
    fw = (2.0 * math.pi * pos / n)[..., None] * jnp.linspace(1e-4, HY_BANDS - 1, HY_BANDS, dtype=jnp.float32)
    feats = jnp.concatenate([t[..., None], jnp.cos(fw), -jnp.sin(fw)], axis=-1)
    h = jnp.sin(freq * (feats @ ff1 + ff1_b))
    return jnp.sin(freq * (h @ ff2 + ff2_b))


def _filter_kernel(feat_ref, w_ref, delta_ref, taps_ref, sum_ref, *, n, tm):
    dr = pl.program_id(1)
    i = pl.program_id(2)
    row = i * tm + lax.broadcasted_iota(jnp.int32, (tm, MIX_W), 0)
    lag = jnp.where(dr == 0, row, jnp.where(row == 0, 0, n - row))
    t = lag.astype(jnp.float32) / (n - 1)
    h = jnp.dot(feat_ref[...].astype(jnp.bfloat16), w_ref[...].astype(jnp.bfloat16),
                preferred_element_type=jnp.float32)
    h = h * jnp.exp(-t * delta_ref[...])

    @pl.when((dr == 0) & (i == 0))
    def _():
        sum_ref[...] = jnp.zeros_like(sum_ref)

    sum_ref[...] += jnp.sum(jnp.abs(h), axis=0, keepdims=True)
    taps_ref[...] = jnp.where((dr == 0) | (row != 0), h, 0.0).astype(taps_ref.dtype)


def hyena_filter_taps(feats, ff3):
    _, n, nf = feats.shape
    tm = min(n, 1024)
    nb = n // tm
    w = ff3.reshape(nf, 2, 2, MIX_W).transpose(1, 2, 0, 3)
    deltas = jnp.abs(jnp.linspace(math.log(HY_TARGET) / HY_SLOW_PCT, math.log(HY_TARGET) / HY_FAST_PCT,
                                  MIX_W, dtype=jnp.float32)).reshape(1, MIX_W)
    return pl.pallas_call(
        functools.partial(_filter_kernel, n=n, tm=tm),
        grid=(2, 2, nb),
        in_specs=[pl.BlockSpec((None, tm, nf), lambda o, dr, i: (dr, i, 0)),
                  pl.BlockSpec((None, None, nf, MIX_W), lambda o, dr, i: (o, dr, 0, 0)),
                  pl.BlockSpec((1, MIX_W), lambda o, dr, i: (0, 0))],
        out_specs=[pl.BlockSpec((None, tm, MIX_W), lambda o, dr, i: (o, dr * nb + i, 0)),
                   pl.BlockSpec((None, 1, MIX_W), lambda o, dr, i: (o, 0, 0))],
        out_shape=[jax.ShapeDtypeStruct((2, 2 * n, MIX_W), jnp.bfloat16),
                   jax.ShapeDtypeStruct((2, 1, MIX_W), jnp.float32)],
        compiler_params=_cparams("arbitrary", "arbitrary", "arbitrary"),
        name="hyena_filter",
    )(feats, w, deltas)


def _merge_kernel(yh_ref, yl_ref, yg_ref, w_ref, g0_ref, g1_ref, g2_ref, o_ref):
    acc = None
    for y_ref, g_ref, j in ((yh_ref, g0_ref, 0), (yl_ref, g1_ref, 1), (yg_ref, g2_ref, 2)):
        gate = 1.0 / (1.0 + jnp.exp(-g_ref[...].astype(jnp.float32)))
        t = gate * jnp.dot(y_ref[...], w_ref[j], preferred_element_type=jnp.float32)
        acc = t if acc is None else acc + t
    o_ref[...] = acc.astype(o_ref.dtype)


def merge_branches(z, ys, w_branch, layer):
    rows = z.shape[0]
    tm = ROW_T // 2
    yspec = pl.BlockSpec((tm, MIX_W), lambda i: (i, 0))

    def gspec(k):
        return pl.BlockSpec((tm, D_MODEL), lambda i: (i, GATE_COL0 // D_MODEL + k))

    return pl.pallas_call(
        _merge_kernel,
        grid=(rows // tm,),
        in_specs=[yspec, yspec, yspec,
                  pl.BlockSpec((None, N_BRANCH, MIX_W, D_MODEL), lambda i: (layer, 0, 0, 0)),
                  gspec(0), gspec(1), gspec(2)],
        out_specs=pl.BlockSpec((tm, D_MODEL), lambda i: (i, 0)),
        out_shape=jax.ShapeDtypeStruct((rows, D_MODEL), jnp.bfloat16),
        compiler_params=_cparams("parallel"),
        name="merge",
    )(*ys, w_branch, z, z, z)


def _out_proj_kernel(a_ref, w_ref, x_ref, g1_ref, gt_ref, g2_ref, sh_ref, sc_ref, wr_ref, xo_ref, h_ref, lg_ref):
    y = jnp.dot(a_ref[...], w_ref[...], preferred_element_type=jnp.float32)
    y = y * lax.rsqrt(jnp.mean(y * y, axis=-1, keepdims=True) + EPS) * g1_ref[...]
    x = x_ref[...] + gt_ref[...] * y
    xo_ref[...] = x
    h = x * lax.rsqrt(jnp.mean(x * x, axis=-1, keepdims=True) + EPS) * g2_ref[...]
    h = (h * (1.0 + sc_ref[...]) + sh_ref[...]).astype(jnp.bfloat16)
    h_ref[...] = h
    lg_ref[...] = jnp.dot(h, wr_ref[...], preferred_element_type=jnp.float32)


def out_proj(acc, w_out, layer, x, g1, gt, g2, sh, sc, w_router, n_ctx_rows, n_lat_rows):
    rows, d = x.shape
    tm = ROW_T // 2
    idx = functools.partial(_mod_index, n_ctx_tiles=2 * n_ctx_rows // tm, tiles_per_batch=n_lat_rows // tm)
    row = pl.BlockSpec((tm, d), lambda i: (i, 0))
    vec = pl.BlockSpec((1, d), lambda i: (0, 0))
    mod = pl.BlockSpec((None, 1, d), lambda i: (idx(i), 0, 0))
    wr = jnp.zeros((d, ROUTER_PAD), jnp.bfloat16).at[:, :N_EXPERTS].set(w_router.astype(jnp.bfloat16))
    return pl.pallas_call(
        _out_proj_kernel,
        grid=(rows // tm,),
        in_specs=[row, pl.BlockSpec((None, d, d), lambda i: (layer, 0, 0)), row, vec, mod, vec, mod, mod,
                  pl.BlockSpec((d, ROUTER_PAD), lambda i: (0, 0))],
        out_specs=[row, row, pl.BlockSpec((tm, ROUTER_PAD), lambda i: (i, 0))],
        out_shape=[jax.ShapeDtypeStruct((rows, d), jnp.float32), jax.ShapeDtypeStruct((rows, d), jnp.bfloat16),
                   jax.ShapeDtypeStruct((rows, ROUTER_PAD), jnp.float32)],
        compiler_params=_cparams("parallel"),
        name="out_proj",
    )(acc, w_out, x, g1.reshape(1, d), gt, g2.reshape(1, d), sh, sc, wr)


def _ffn_up_kernel(x_ref, wg_ref, wu_ref, h_ref):
    x = x_ref[...]
    hg = jnp.dot(x, wg_ref[...].astype(jnp.bfloat16), preferred_element_type=jnp.float32)
    hu = jnp.dot(x, wu_ref[...].astype(jnp.bfloat16), preferred_element_type=jnp.float32)
    h_ref[...] = (hg / (1.0 + jnp.exp(-hg)) * hu).astype(h_ref.dtype)


def _ffn_down_kernel(h_ref, wd_ref, g_ref, o_ref):
    y = jnp.dot(h_ref[...], wd_ref[...].astype(jnp.bfloat16), preferred_element_type=jnp.float32)
    o_ref[...] = (y * g_ref[...]).astype(o_ref.dtype)


def expert_ffn(xe, w_gate, w_up, w_down, layer, g):
    ne, r, d = xe.shape
    dff = w_gate.shape[-1]
    hid = pl.pallas_call(
        _ffn_up_kernel,
        grid=(ne, dff // FFN_TF),
        in_specs=[pl.BlockSpec((None, r, d), lambda e, f: (e, 0, 0)),
                  pl.BlockSpec((None, None, d, FFN_TF), lambda e, f: (layer, e, 0, f)),
                  pl.BlockSpec((None, None, d, FFN_TF), lambda e, f: (layer, e, 0, f))],
        out_specs=pl.BlockSpec((None, r, FFN_TF), lambda e, f: (e, 0, f)),
        out_shape=jax.ShapeDtypeStruct((ne, r, dff), jnp.bfloat16),
        compiler_params=_cparams("parallel", "arbitrary"),
        name="ffn_up",
    )(xe, w_gate, w_up)
    return pl.pallas_call(
        _ffn_down_kernel,
        grid=(ne, d // FFN_TF),
        in_specs=[pl.BlockSpec((None, r, dff), lambda e, j: (e, 0, 0)),
                  pl.BlockSpec((None, None, dff, FFN_TF), lambda e, j: (layer, e, 0, j)),
                  pl.BlockSpec((None, r, 1), lambda e, j: (e, 0, 0))],
        out_specs=pl.BlockSpec((None, r, FFN_TF), lambda e, j: (e, 0, j)),
        out_shape=jax.ShapeDtypeStruct((ne, r, d), jnp.bfloat16),
        compiler_params=_cparams("parallel", "arbitrary"),
        name="ffn_down",
    )(hid, w_down, g)


def route(logits, row0, bsz, n):
    cap = EC_CAPACITY * n // N_EXPERTS
    aff = jax.nn.softmax(logits[row0:row0 + bsz * n, :N_EXPERTS].reshape(bsz, n, N_EXPERTS), axis=-1)
    g, idx = lax.top_k(jnp.swapaxes(aff, 1, 2), cap)
    flat = idx + (row0 + jnp.arange(bsz, dtype=idx.dtype) * n)[:, None, None]
    return (jnp.swapaxes(g, 0, 1).reshape(N_EXPERTS, bsz * cap),
            jnp.swapaxes(flat, 0, 1).reshape(N_EXPERTS, bsz * cap))


def ec_moe(h, logits, w_gate, w_up, w_down, layer, segments):
    gs, flats = zip(*(route(logits, *seg) for seg in segments))
    flat, g = lax.sort_key_val(jnp.concatenate(flats, axis=1), jnp.concatenate(gs, axis=1), dimension=1)
    return expert_ffn(h[flat], w_gate, w_up, w_down, layer, g[..., None]), flat


def moe_plan(flat, n_rows):
    ne, r = flat.shape
    nwin = r // MOE_WIN
    n_tiles = n_rows // MOE_TT
    bounds = jnp.arange(n_tiles + 1, dtype=flat.dtype) * MOE_TT
    p = jnp.sum(flat[:, :, None] < bounds[None, None, :], axis=1, dtype=jnp.int32)
    p0, p1 = p[:, :-1].T, p[:, 1:].T
    w0 = p0 // MOE_WIN
    nw = jnp.where(p1 > p0, (p1 - 1) // MOE_WIN - w0 + 1, 0)
    off = jnp.cumsum(nw, axis=1)
    q = jnp.arange(MOE_MAXQ, dtype=jnp.int32)
    e_of_q = jnp.minimum(jnp.sum(q[None, :, None] >= off[:, None, :], axis=2, dtype=jnp.int32), ne - 1)
    w = jnp.take_along_axis(w0, e_of_q, axis=1) + q[None, :] - jnp.take_along_axis(off - nw, e_of_q, axis=1)
    items = jnp.clip(e_of_q * nwin + w, 0, ne * nwin - 1)
    return items.astype(jnp.int32), off[:, -1].astype(jnp.int32)


def _combine_kernel(items_ref, count_ref, tok_ref, ye_ref, x_ref, g_ref, gt_ref, *rest, nxt, skip):
    if nxt:
        gn_ref, sh_ref, sc_ref, o_ref, u_ref, stage, sem, acc_ref = rest
    else:
        o_ref, stage, sem, acc_ref = rest
    i = pl.program_id(0)
    t = i + skip
    n = count_ref[t]
    n_groups = (n + MOE_GROUP - 1) // MOE_GROUP

    @pl.when(i == 0)
    def _():
        stage[...] = jnp.zeros_like(stage)

    def window_copy(g, buf, k):
        item = items_ref[t, jnp.minimum(MOE_GROUP * g + k, MOE_MAXQ - 1)]
        return pltpu.make_async_copy(ye_ref.at[pl.ds(item * MOE_WIN, MOE_WIN)],
                                     stage.at[buf, pl.ds(k * MOE_WIN, MOE_WIN)], sem.at[buf])

    def for_group(g, buf, wait):
        for k in range(MOE_GROUP):
            @pl.when(MOE_GROUP * g + k < n)
            def _():
                cp = window_copy(g, buf, k)
                cp.wait() if wait else cp.start()

    acc_ref[...] = jnp.zeros_like(acc_ref)

    for j in range(MOE_NBUF - 1):
        @pl.when(j < n_groups)
        def _():
            for_group(j, j, False)

    row = t * MOE_TT + lax.broadcasted_iota(jnp.int32, (MOE_TT, MOE_GROUP * MOE_WIN), 0)

    def body(g, carry):
        buf = g % MOE_NBUF
        ahead = g + MOE_NBUF - 1

        @pl.when(ahead < n_groups)
        def _():
            for_group(ahead, ahead % MOE_NBUF, False)

        for_group(g, buf, True)
        toks = []
        for k in range(MOE_GROUP):
            qk = MOE_GROUP * g + k
            item = items_ref[t, jnp.minimum(qk, MOE_MAXQ - 1)]
            toks.append(jnp.where(qk < n, tok_ref[pl.ds(item, 1), :], -1))
        onehot = jnp.where(row == jnp.concatenate(toks, axis=1), 1.0, 0.0).astype(jnp.bfloat16)
        acc_ref[...] += jnp.dot(onehot, stage[buf], preferred_element_type=jnp.float32)
        return carry

    lax.fori_loop(0, n_groups, body, 0)

    y = acc_ref[...]
    y = y * lax.rsqrt(jnp.mean(y * y, axis=-1, keepdims=True) + EPS) * g_ref[...]
    x = x_ref[...] + gt_ref[...] * y
    if nxt:
        u = x * lax.rsqrt(jnp.mean(x * x, axis=-1, keepdims=True) + EPS) * gn_ref[...]
        u_ref[...] = (u * (1.0 + sc_ref[...]) + sh_ref[...]).astype(u_ref.dtype)
    o_ref[...] = x


def moe_combine(x, ye, flat, g, gt, n_ctx_rows, n_lat_rows, nxt=None):
    rows, d = x.shape
    ne, r, _ = ye.shape
    items, count = moe_plan(flat, rows)
    n_ctx_tiles = 2 * n_ctx_rows // MOE_TT
    idx = functools.partial(_mod_index, n_ctx_tiles=n_ctx_tiles, tiles_per_batch=n_lat_rows // MOE_TT)
    skip = 0 if nxt else n_ctx_tiles
    row = pl.BlockSpec((MOE_TT, d), lambda i, *_: (i + skip, 0))
    vec = pl.BlockSpec((1, d), lambda i, *_: (0, 0))
    mod = pl.BlockSpec((None, 1, d), lambda i, *_: (idx(i + skip), 0, 0))
    out_row = pl.BlockSpec((MOE_TT, d), lambda i, *_: (i, 0))
    n_out = rows - skip * MOE_TT
    in_specs = [pl.BlockSpec((ne * r // MOE_WIN, MOE_WIN), lambda i, *_: (0, 0)),
                pl.BlockSpec(memory_space=pl.ANY), row, vec, mod]
    args = [flat.reshape(ne * r // MOE_WIN, MOE_WIN), ye.reshape(ne * r, d), x, g.reshape(1, d), gt]
    out_specs, out_shape = out_row, jax.ShapeDtypeStruct((n_out, d), jnp.float32)
    if nxt:
        in_specs += [vec, mod, mod]
        args += [nxt[0].reshape(1, d), nxt[1], nxt[2]]
        out_specs = [out_row, out_row]
        out_shape = [out_shape, jax.ShapeDtypeStruct((n_out, d), jnp.bfloat16)]
    return pl.pallas_call(
        functools.partial(_combine_kernel, nxt=bool(nxt), skip=skip),
        grid_spec=pltpu.PrefetchScalarGridSpec(
            num_scalar_prefetch=2,
            grid=(n_out // MOE_TT,),
            in_specs=in_specs, out_specs=out_specs,
            scratch_shapes=[pltpu.VMEM((MOE_NBUF, MOE_GROUP * MOE_WIN, d), jnp.bfloat16),
                            pltpu.SemaphoreType.DMA((MOE_NBUF,)),
                            pltpu.VMEM((MOE_TT, d), jnp.float32)]),
        out_shape=out_shape,
        compiler_params=_cparams("arbitrary"),
        name="moe_combine",
    )(items, count, *args)


def kernel(x, c, ctx, c_ctx, w_mod, b_mod, norm_g, w_in, hy_conv_w, hy_conv_b, hy_ff1, hy_ff1_b, hy_freq,
           hy_ff2, hy_ff2_b, hy_ff3, hy_bias, lru_conv_w, lru_conv_b, lru_wa, lru_ba, lru_wi, lru_bi,
           lru_lambda, hg_lb_logits, hg_norm_g, w_branch, w_out, w_router, w_gate, w_up, w_down):
    bsz, n_lat, d = x.shape
    n_ctx = ctx.shape[1]
    assert bsz == 2 and d == D_MODEL and n_ctx == SEQ_T and 2 * n_lat == FFT_L and n_lat % ROW_T == 0
    gam = jax.nn.softmax(hg_lb_logits.astype(jnp.float32), axis=0)
    lb_all = jnp.maximum(jnp.cumsum(gam, axis=0) - gam[:1], 0.0)
    mats = fft_matrices()
    cvec = jnp.zeros((SUBLANES, d), jnp.float32).at[:bsz].set(jax.nn.silu(c)).at[bsz].set(jax.nn.silu(c_ctx))
    n_ctx_all = bsz * n_ctx
    xa = jnp.concatenate([ctx.reshape(n_ctx_all, d), x.reshape(bsz * n_lat, d)], axis=0)
    w_in_b, w_branch_b, w_out_b = (w.astype(jnp.bfloat16) for w in (w_in, w_branch, w_out))
    mods = []
    for l in range(DEPTH):
        mod = mm(cvec, w_mod, layer=l, name="adaln")[:bsz + 1] + b_mod[l]
        mods.append([mod.reshape(bsz + 1, 1, 6, d)[:, :, k] for k in range(6)])
    u = prenorm(xa, norm_g[0, 0], mods[0][0], mods[0][1], n_ctx, n_lat)
    for l in range(DEPTH):
        last = l == DEPTH - 1
        p = {'lru_conv_w': lru_conv_w[l], 'lru_conv_b': lru_conv_b[l], 'lru_wa': lru_wa[l], 'lru_ba': lru_ba[l],
             'lru_wi': lru_wi[l], 'lru_bi': lru_bi[l], 'lru_lambda': lru_lambda[l]}
        sh1, sc1, gt1, sh2, sc2, gt2 = mods[l]

        z = mm(u, w_in_b, jnp.bfloat16, tm=IN_PROJ_TM, tn=1024, layer=l, name="in_proj")

        y_lru = lru_branch(z, p, n_ctx, n_lat)
        y_hg = hgrn_branch(z, lb_all[l], hg_norm_g[l], n_ctx, n_lat)

        ffp = (hy_ff1[l], hy_ff1_b[l], hy_freq[l], hy_ff2[l], hy_ff2_b[l])
        taps_l, norm_l = hyena_filter_taps(hyena_features(n_lat, *ffp), hy_ff3[l])
        taps_c, norm_c = hyena_filter_taps(hyena_features(n_ctx, *ffp), hy_ff3[l])
        xc_l = hy_conv(z, hy_conv_w[l], hy_conv_b[l], n_ctx_all // SEQ_T, n_lat, False)
        y_hy_l = hyena_latent(xc_l, taps_l, 1.0 / norm_l, hy_bias[l], mats)
        xc_c = hy_conv(z, hy_conv_w[l], hy_conv_b[l], 0, n_ctx, True)
        y_hy_c = hyena_context(xc_c, taps_c, 1.0 / norm_c, hy_bias[l])
        y_hy = jnp.concatenate([y_hy_c.reshape(n_ctx_all, MIX_W), y_hy_l.reshape(bsz * n_lat, MIX_W)], axis=0)

        acc = merge_branches(z, (y_hy, y_lru, y_hg), w_branch_b, l)
        xa, h, logits = out_proj(acc, w_out_b, l, xa, norm_g[l, 1], gt1, norm_g[l, 2], sh2, sc2,
                                 w_router[l], n_ctx, n_lat)
        segments = [(n_ctx_all, bsz, n_lat)] if last else [(n_ctx_all, bsz, n_lat), (0, bsz, n_ctx)]
        ye, flat = ec_moe(h, logits, w_gate, w_up, w_down, l, segments)
        if last:
            xa = moe_combine(xa, ye, flat, norm_g[l, 3], gt2, n_ctx, n_lat)
        else:
            nxt = (norm_g[l + 1, 0], mods[l + 1][0], mods[l + 1][1])
            xa, u = moe_combine(xa, ye, flat, norm_g[l, 3], gt2, n_ctx, n_lat, nxt)
    return xa.reshape(bsz, n_lat, d)
```

```python
import functools
import math

import numpy as np
import jax
import jax.numpy as jnp
from jax import lax
from jax.experimental import pallas as pl
from jax.experimental.pallas import tpu as pltpu

D_MODEL = 2048
DEPTH = 2
GRID_W = 64
MIX_W = D_MODEL // 2
N_BRANCH = 3
HY_BANDS = 16
HY_TARGET = 1e-2
HY_FAST_PCT = 0.3
HY_SLOW_PCT = 1.5
LRU_BLOCKS = 8
LRU_BS = MIX_W // LRU_BLOCKS
LRU_C = 8.0
HG_HEADS = 8
HG_DK = MIX_W // HG_HEADS
N_EXPERTS = 16
EC_CAPACITY = 2
EPS = 1e-6
TINY = 1e-30
LOG2E = 1.4426950408889634
HY_COL0 = 0
LRU_COL0 = 3 * MIX_W
HG_COL0 = 5 * MIX_W
GATE_COL0 = 10 * MIX_W
IN_COLS = GATE_COL0 + N_BRANCH * D_MODEL

SUBLANES = 8
LANES = 128
SEQ_T = 256
HG_C = 128
HG_LEVELS = 7
HG_HB = 8
ROW_T = 512
IN_PROJ_TM = 1536
FFT_R = 128
FFT_L = FFT_R * FFT_R
FFT_KB = 8
FFT_SB = 16
HY_PARTS = 4
ROUTER_PAD = LANES
FFN_TF = 256
MOE_TT = 512
MOE_WIN = 64
MOE_GROUP = 8
MOE_NBUF = 3
MOE_MAXQ = N_EXPERTS * (MOE_TT // MOE_WIN + 1)
VMEM_LIMIT_BYTES = 48 * 1024 * 1024


def _cparams(*sem):
    return pltpu.CompilerParams(dimension_semantics=sem, vmem_limit_bytes=VMEM_LIMIT_BYTES)


def _pick_tile(n, pref):
    for t in (pref, 1024, 512, 256, 128):
        if t <= n and n % t == 0:
            return t
    return n


def _row_block(b, j, n_ctx, n_lat, reverse):
    if reverse:
        kc, kl = n_ctx - 1 - j, n_ctx + n_lat - 1 - j
    else:
        kc, kl = j, j - n_ctx
    return jnp.where(j < n_ctx, b * n_ctx + kc, 2 * n_ctx + b * n_lat + kl)


def _mm_kernel(a_ref, b_ref, o_ref):
    o_ref[...] = jnp.dot(a_ref[...].astype(jnp.bfloat16), b_ref[...].astype(jnp.bfloat16),
                         preferred_element_type=jnp.float32).astype(o_ref.dtype)


def mm(a, b, out_dtype=jnp.float32, tm=512, tn=1024, layer=None, name="mm"):
    m, k = a.shape
    n = b.shape[-1]
    tm = _pick_tile(m, tm)
    tn = _pick_tile(n, tn)
    if layer is None:
        bspec = pl.BlockSpec((k, tn), lambda i, j: (0, j))
    else:
        bspec = pl.BlockSpec((None, k, tn), lambda i, j: (layer, 0, j))
    return pl.pallas_call(
        _mm_kernel,
        grid=(m // tm, n // tn),
        in_specs=[pl.BlockSpec((tm, k), lambda i, j: (i, 0)), bspec],
        out_specs=pl.BlockSpec((tm, tn), lambda i, j: (i, j)),
        out_shape=jax.ShapeDtypeStruct((m, n), out_dtype),
        compiler_params=_cparams("parallel", "arbitrary"),
        name=name,
    )(a, b)


def _mod_index(i, n_ctx_tiles, tiles_per_batch):
    return jnp.where(i < n_ctx_tiles, 2, (i - n_ctx_tiles) // tiles_per_batch)


def _prenorm_kernel(x_ref, g_ref, sh_ref, sc_ref, u_ref):
    x = x_ref[...]
    y = x * lax.rsqrt(jnp.mean(x * x, axis=-1, keepdims=True) + EPS) * g_ref[...]
    u_ref[...] = (y * (1.0 + sc_ref[...]) + sh_ref[...]).astype(u_ref.dtype)


def prenorm(x, g, sh, sc, n_ctx_rows, n_lat_rows):
    rows, d = x.shape
    idx = functools.partial(_mod_index, n_ctx_tiles=2 * n_ctx_rows // ROW_T, tiles_per_batch=n_lat_rows // ROW_T)
    row = pl.BlockSpec((ROW_T, d), lambda i: (i, 0))
    vec = pl.BlockSpec((None, 1, d), lambda i: (idx(i), 0, 0))
    return pl.pallas_call(
        _prenorm_kernel,
        grid=(rows // ROW_T,),
        in_specs=[row, pl.BlockSpec((1, d), lambda i: (0, 0)), vec, vec],
        out_specs=row,
        out_shape=jax.ShapeDtypeStruct((rows, d), jnp.bfloat16),
        compiler_params=_cparams("parallel"),
        name="prenorm",
    )(x, g.reshape(1, d), sh, sc)


def _gelu_tanh(x):
    return 0.5 * x * (1.0 + jnp.tanh(math.sqrt(2.0 / math.pi) * (x + 0.044715 * (x * x * x))))


def _conv_pos(is_ctx, shape):
    t = lax.broadcasted_iota(jnp.int32, shape, 0)
    pos = jnp.where(is_ctx, t, t & (GRID_W - 1))
    last = jnp.where(is_ctx, SEQ_T - 1, GRID_W - 1)
    return pos, last


def _lru_pass_kernel(x_ref, cw_ref, cb_ref, w_ref, gb_ref, lam_ref, *rest, reverse, final):
    if final:
        lg_ref, hprev_ref, out_ref, h_ref, a_ref, b_ref = rest
    else:
        out_ref, h_ref, a_ref, b_ref = rest
    j = pl.program_id(1)

    @pl.when(j == 0)
    def _():
        h_ref[...] = jnp.zeros_like(h_ref)

    x = x_ref[...].astype(jnp.float32)
    pos, last = _conv_pos(j == 0, x.shape)
    cw = cw_ref[...]
    xc = cb_ref[...] + cw[2:3] * x
    xc = xc + cw[0:1] * jnp.where(pos >= 2, pltpu.roll(x, 2, 0), 0.0)
    xc = xc + cw[1:2] * jnp.where(pos >= 1, pltpu.roll(x, 1, 0), 0.0)
    xc = xc + cw[3:4] * jnp.where(pos < last, pltpu.roll(x, SEQ_T - 1, 0), 0.0)

    xb = xc.astype(jnp.bfloat16)
    lam = lam_ref[...]
    sp = jnp.maximum(-lam, 0.0) + jnp.log(1.0 + jnp.exp(-jnp.abs(lam)))
    n_tiles = SEQ_T // SUBLANES
    row = lax.broadcasted_iota(jnp.int32, (1, SUBLANES, LRU_BS), 1)
    for n in range(LRU_BLOCKS):
        sl = slice(n * LRU_BS, (n + 1) * LRU_BS)
        pre = jnp.dot(xb[:, sl], w_ref[n], preferred_element_type=jnp.float32) + gb_ref[n]
        gate = 1.0 / (1.0 + jnp.exp(-pre))
        r, i = gate[:, :LRU_BS], gate[:, LRU_BS:]
        a = jnp.exp(-LRU_C * r * sp[:, sl])
        om = 1.0 - a * a
        b = om * lax.rsqrt(jnp.maximum(om, TINY)) * (i * xc[:, sl])
        a = a.reshape(n_tiles, SUBLANES, LRU_BS)
        b = b.reshape(n_tiles, SUBLANES, LRU_BS)
        for d in (1, 2, 4):
            if reverse:
                m = row < SUBLANES - d
                a_sh, b_sh = pltpu.roll(a, SUBLANES - d, 1), pltpu.roll(b, SUBLANES - d, 1)
            else:
                m = row >= d
                a_sh, b_sh = pltpu.roll(a, d, 1), pltpu.roll(b, d, 1)
            b = jnp.where(m, a * b_sh + b, b)
            a = jnp.where(m, a * a_sh, a)
        a_ref[:, sl] = a.reshape(SEQ_T, LRU_BS)
        b_ref[:, sl] = b.reshape(SEQ_T, LRU_BS)

    h = h_ref[...]
    for k in range(n_tiles):
        kk = n_tiles - 1 - k if reverse else k
        rs = slice(kk * SUBLANES, (kk + 1) * SUBLANES)
        ht = a_ref[rs, :] * h + b_ref[rs, :]
        h = ht[0:1, :] if reverse else ht[SUBLANES - 1:SUBLANES, :]
        if final:
            ht = (ht + hprev_ref[rs, :]) * _gelu_tanh(lg_ref[rs, :].astype(jnp.float32))
        out_ref[rs, :] = ht.astype(out_ref.dtype)
    h_ref[...] = h


def lru_pass(z, conv_w, conv_b, wa, ba, wi, bi, lam, n_ctx_rows, n_lat_rows, reverse, h_prev=None):
    rows = z.shape[0]
    n_lat = n_lat_rows // SEQ_T
    cb = LRU_COL0 // MIX_W
    final = h_prev is not None
    rb = functools.partial(_row_block, n_ctx=n_ctx_rows // SEQ_T, n_lat=n_lat, reverse=reverse)

    def zspec(k):
        return pl.BlockSpec((SEQ_T, MIX_W), lambda b, j: (rb(b, j), cb + k))

    row_spec = pl.BlockSpec((SEQ_T, MIX_W), lambda b, j: (rb(b, j), 0))

    def const_spec(shape):
        return pl.BlockSpec(shape, lambda b, j: (0,) * len(shape))

    w2 = jnp.concatenate([wa, wi], axis=-1).astype(jnp.bfloat16)
    gb = jnp.concatenate([ba.reshape(LRU_BLOCKS, 1, LRU_BS), bi.reshape(LRU_BLOCKS, 1, LRU_BS)], axis=-1)
    in_specs = [zspec(0), const_spec((4, MIX_W)), const_spec((1, MIX_W)),
                const_spec((LRU_BLOCKS, LRU_BS, 2 * LRU_BS)), const_spec((LRU_BLOCKS, 1, 2 * LRU_BS)),
                const_spec((1, MIX_W))]
    args = [z, conv_w, conv_b.reshape(1, MIX_W), w2, gb, lam.reshape(1, MIX_W)]
    if final:
        in_specs += [zspec(1), row_spec]
        args += [z, h_prev]
    return pl.pallas_call(
        functools.partial(_lru_pass_kernel, reverse=reverse, final=final),
        grid=(2, n_ctx_rows // SEQ_T + n_lat),
        in_specs=in_specs,
        out_specs=row_spec,
        out_shape=jax.ShapeDtypeStruct((rows, MIX_W), jnp.bfloat16 if final else jnp.float32),
        scratch_shapes=[pltpu.VMEM((1, MIX_W), jnp.float32),
                        pltpu.VMEM((SEQ_T, MIX_W), jnp.float32),
                        pltpu.VMEM((SEQ_T, MIX_W), jnp.float32)],
        compiler_params=_cparams("parallel", "arbitrary"),
        name="lru_bwd" if reverse else "lru_fwd",
    )(*args)


def lru_branch(z, p, n_ctx_rows, n_lat_rows):
    def one(d, h_prev):
        return lru_pass(z, p['lru_conv_w'], p['lru_conv_b'], p['lru_wa'][d], p['lru_ba'][d], p['lru_wi'][d],
                        p['lru_bi'][d], p['lru_lambda'][d], n_ctx_rows, n_lat_rows, d == 1, h_prev)
    return one(1, one(0, None))


def _hg_level_matrix(reverse):
    t = np.arange(HG_C)[:, None]
    s = np.arange(HG_C)[None, :]
    x = t ^ s
    lv = np.where(x > 0, np.floor(np.log2(np.maximum(x, 1))).astype(np.int32), -1)
    lv = np.where(s < t, lv, -1).astype(np.int32)
    return jnp.asarray(lv.T if reverse else lv)


def _hg_pass_kernel(lv_ref, lb_ref, q_ref, f_ref, v_ref, *rest, reverse, final):
    if final:
        og_ref, oprev_ref, g_ref, out_ref, st_ref = rest
    else:
        out_ref, st_ref = rest

    @pl.when(pl.program_id(1) == 0)
    def _():
        st_ref[...] = jnp.zeros_like(st_ref)

    lv = lv_ref[...]
    at_level = [lv == lvl for lvl in range(HG_LEVELS)]
    n_tiles = HG_C // SUBLANES
    wb = HG_HB * HG_DK
    row = lax.broadcasted_iota(jnp.int32, (1, SUBLANES, wb), 1)
    for hb in range(HG_HEADS // HG_HB):
        sl = slice(hb * wb, (hb + 1) * wb)
        heads = [(hb * HG_HB + j, slice(j * HG_DK, (j + 1) * HG_DK)) for j in range(HG_HB)]
        x = f_ref[:, sl].astype(jnp.float32)
        lb = lb_ref[:, sl]
        e = jnp.exp(-jnp.abs(x))
        inv = 1.0 / (1.0 + e)
        pos = x >= 0
        sig = jnp.where(pos, inv, e * inv)
        sigm = jnp.where(pos, e * inv, inv)
        lsig = jnp.minimum(x, 0.0) - jnp.log(1.0 + e)
        lf = jnp.where(lb > 0, jnp.log(lb + (1.0 - lb) * sig), lsig)
        kk = (1.0 - lb) * sigm
        q = q_ref[:, sl].astype(jnp.float32)
        v = v_ref[:, sl]

        c = (lf * LOG2E).reshape(n_tiles, SUBLANES, wb)
        bt = c
        att = [None] * HG_HB
        for lvl in range(HG_LEVELS):
            m = 1 << lvl
            qm = (q * jnp.exp2(c).reshape(HG_C, wb)).astype(jnp.bfloat16)
            km = (kk * jnp.exp2(bt - c).reshape(HG_C, wb)).astype(jnp.bfloat16)
            for j, (_, hs) in enumerate(heads):
                d = lax.dot_general(qm[:, hs], km[:, hs], (((1,), (1,)), ((), ())),
                                    preferred_element_type=jnp.float32)
                att[j] = jnp.where(at_level[lvl], d, 0.0 if lvl == 0 else att[j])
            if m < SUBLANES:
                upper = (row & m) != 0
                down = pltpu.roll(bt, m, 1)
                up = pltpu.roll(bt, SUBLANES - m, 1)
                if reverse:
                    c = c + jnp.where(upper, 0.0, up)
                else:
                    c = c + jnp.where(upper, down, 0.0)
                bt = bt + jnp.where(upper, down, up)
            else:
                k = m // SUBLANES
                pair = (n_tiles // (2 * k), 2, k, SUBLANES, wb)
                c5, b5 = c.reshape(pair), bt.reshape(pair)
                tot2 = b5[:, 0] + b5[:, 1]
                if reverse:
                    c = jnp.stack([c5[:, 0] + b5[:, 1], c5[:, 1]], axis=1)
                else:
                    c = jnp.stack([c5[:, 0], c5[:, 1] + b5[:, 0]], axis=1)
                c = c.reshape(n_tiles, SUBLANES, wb)
                bt = jnp.stack([tot2, tot2], axis=1).reshape(n_tiles, SUBLANES, wb)

        qm = (q * jnp.exp2(c).reshape(HG_C, wb)).astype(jnp.bfloat16)
        km = (kk * jnp.exp2(bt - c).reshape(HG_C, wb)).astype(jnp.bfloat16)
        tot = jnp.exp2(bt[0, 0:1, :])
        diag = q * kk
        for j, (h, hs) in enumerate(heads):
            osl = slice(h * HG_DK, (h + 1) * HG_DK)
            st = st_ref[h]
            vh = v[:, hs]
            o = lax.dot_general(qm[:, hs], st.astype(jnp.bfloat16), (((1,), (1,)), ((), ())),
                                preferred_element_type=jnp.float32)
            o = o + jnp.dot(att[j].astype(jnp.bfloat16), vh, preferred_element_type=jnp.float32)
            o = o + jnp.sum(diag[:, hs], axis=-1, keepdims=True) * vh.astype(jnp.float32)
            st_ref[h] = st * tot[:, hs] + lax.dot_general(vh, km[:, hs], (((0,), (0,)), ((), ())),
                                                          preferred_element_type=jnp.float32)
            if final:
                o = o + oprev_ref[:, osl]
                y = o * lax.rsqrt(jnp.mean(o * o, axis=-1, keepdims=True) + EPS) * g_ref[:, osl]
                og = og_ref[:, osl].astype(jnp.float32)
                out_ref[:, osl] = (y * (og / (1.0 + jnp.exp(-og)))).astype(out_ref.dtype)
            else:
                out_ref[:, osl] = o


def hgrn_pass(z, lb, n_ctx_rows, n_lat_rows, reverse, o_prev=None, norm_g=None):
    rows = z.shape[0]
    n_ctx, n_lat = n_ctx_rows // HG_C, n_lat_rows // HG_C
    cb = HG_COL0 // MIX_W
    final = o_prev is not None
    rb = functools.partial(_row_block, n_ctx=n_ctx, n_lat=n_lat, reverse=reverse)

    def zspec(k):
        return pl.BlockSpec((HG_C, MIX_W), lambda b, j: (rb(b, j), cb + k))

    row_spec = pl.BlockSpec((HG_C, MIX_W), lambda b, j: (rb(b, j), 0))
    vec_spec = pl.BlockSpec((1, MIX_W), lambda b, j: (0, 0))
    in_specs = [pl.BlockSpec((HG_C, HG_C), lambda b, j: (0, 0)), vec_spec,
                zspec(0), zspec(2 if reverse else 1), zspec(3)]
    args = [_hg_level_matrix(reverse), lb.reshape(1, MIX_W), z, z, z]
    if final:
        in_specs += [zspec(4), row_spec, vec_spec]
        args += [z, o_prev, norm_g.reshape(1, MIX_W)]
    return pl.pallas_call(
        functools.partial(_hg_pass_kernel, reverse=reverse, final=final),
        grid=(2, n_ctx + n_lat),
        in_specs=in_specs,
        out_specs=row_spec,
        out_shape=jax.ShapeDtypeStruct((rows, MIX_W), jnp.bfloat16 if final else jnp.float32),
        scratch_shapes=[pltpu.VMEM((HG_HEADS, HG_DK, HG_DK), jnp.float32)],
        compiler_params=_cparams("parallel", "arbitrary"),
        name="hgrn_bwd" if reverse else "hgrn_fwd",
    )(*args)


def hgrn_branch(z, lb, norm_g, n_ctx_rows, n_lat_rows):
    o_f = hgrn_pass(z, lb, n_ctx_rows, n_lat_rows, False)
    return hgrn_pass(z, lb, n_ctx_rows, n_lat_rows, True, o_f, norm_g)


def _hy_conv_kernel(x_ref, w_ref, b_ref, out_ref, *, is_ctx):
    x = x_ref[...].astype(jnp.float32)
    t = lax.broadcasted_iota(jnp.int32, x.shape, 0)
    pos = t if is_ctx else t & (GRID_W - 1)
    last = SEQ_T - 1 if is_ctx else GRID_W - 1
    w = w_ref[...]
    y = b_ref[...] + w[1:2] * x
    y = y + w[0:1] * jnp.where(pos >= 1, pltpu.roll(x, 1, 0), 0.0)
    y = y + w[2:3] * jnp.where(pos < last, pltpu.roll(x, SEQ_T - 1, 0), 0.0)
    for g in range(3):
        out_ref[g] = y[:, g * MIX_W:(g + 1) * MIX_W].astype(out_ref.dtype)


def hy_conv(z, conv_w, conv_b, block0, n_rows, is_ctx):
    nb = n_rows // SEQ_T
    return pl.pallas_call(
        functools.partial(_hy_conv_kernel, is_ctx=is_ctx),
        grid=(2, nb),
        in_specs=[pl.BlockSpec((SEQ_T, 3 * MIX_W), lambda b, j: (block0 + b * nb + j, 0)),
                  pl.BlockSpec((3, 3 * MIX_W), lambda b, j: (0, 0)),
                  pl.BlockSpec((1, 3 * MIX_W), lambda b, j: (0, 0))],
        out_specs=pl.BlockSpec((3, None, SEQ_T, MIX_W), lambda b, j: (0, b, j, 0)),
        out_shape=jax.ShapeDtypeStruct((3, 2, n_rows, MIX_W), jnp.bfloat16),
        compiler_params=_cparams("parallel", "arbitrary"),
        name="hy_conv_ctx" if is_ctx else "hy_conv_lat",
    )(z, conv_w, conv_b.reshape(1, 3 * MIX_W))


def _hy_conv_lat_kernel(*refs):
    x_refs, (w_ref, b_ref, out_ref) = refs[:HY_PARTS], refs[HY_PARTS:]
    x = jnp.concatenate([r[...] for r in x_refs], axis=0).astype(jnp.float32)
    n = x.shape[0]
    pos = lax.broadcasted_iota(jnp.int32, x.shape, 0) & (GRID_W - 1)
    w = w_ref[...]
    y = b_ref[...] + w[1:2] * x
    y = y + w[0:1] * jnp.where(pos >= 1, pltpu.roll(x, 1, 0), 0.0)
    y = y + w[2:3] * jnp.where(pos < GRID_W - 1, pltpu.roll(x, n - 1, 0), 0.0)
    y = y.astype(out_ref.dtype).reshape(n // FFT_R, FFT_R, MIX_W)
    out_ref[...] = jnp.swapaxes(y, 0, 1)


def hy_conv_latent(z, conv_w, conv_b, row0, n_rows):
    part = FFT_SB * FFT_R // HY_PARTS
    nb = n_rows // (FFT_SB * FFT_R)

    def xspec(k):
        return pl.BlockSpec((part, MIX_W), lambda g, b, j: ((row0 + b * n_rows) // part + j * HY_PARTS + k, g))

    return pl.pallas_call(
        _hy_conv_lat_kernel,
        grid=(3, 2, nb),
        in_specs=[xspec(k) for k in range(HY_PARTS)] + [
            pl.BlockSpec((3, MIX_W), lambda g, b, j: (0, g)), pl.BlockSpec((1, MIX_W), lambda g, b, j: (0, g))],
        out_specs=pl.BlockSpec((None, None, FFT_R, FFT_SB, MIX_W), lambda g, b, j: (g, b, 0, j, 0)),
        out_shape=jax.ShapeDtypeStruct((3, 2, FFT_R, n_rows // FFT_R, MIX_W), jnp.bfloat16),
        compiler_params=_cparams("parallel", "parallel", "arbitrary"),
        name="hy_conv_lat",
    )(*([z] * HY_PARTS), conv_w, conv_b.reshape(1, 3 * MIX_W))


def _cis(num, den):
    ang = (2.0 * math.pi / den) * (num % den).astype(jnp.float32)
    return jnp.cos(ang), jnp.sin(ang)


def fft_matrices():
    r = FFT_R
    i = jnp.arange(r, dtype=jnp.int32)
    c, s = _cis(i[:, None] * i[None, :], r)
    half = r // 2
    fa_data = jnp.concatenate([jnp.concatenate([c[:, :half], s[:, :half]], axis=1),
                               jnp.concatenate([-s[:, :half], c[:, :half]], axis=1)], axis=0)
    fa_taps = jnp.concatenate([c, -s], axis=0)
    k1, k2, s2 = i[:, None, None], i[None, :, None], i[None, None, :]
    gc, gs = _cis(s2 * (r * k2 + k1), FFT_L)
    gr, gi = gc, -gs
    m1 = jnp.concatenate([jnp.concatenate([gr, -gi], axis=2), jnp.concatenate([gi, gr], axis=2)], axis=1)
    grt, git = jnp.swapaxes(gr, 1, 2), jnp.swapaxes(gi, 1, 2)
    m2 = jnp.concatenate([jnp.concatenate([grt, git], axis=2), jnp.concatenate([-git, grt], axis=2)], axis=1)
    er, ei = c[:half] / FFT_L, s[:half] / FFT_L
    e2 = jnp.concatenate([jnp.concatenate([er, -ei], axis=1), jnp.concatenate([ei, er], axis=1)], axis=0)
    bf = jnp.bfloat16
    return dict(fa_data=fa_data.astype(bf), fa_taps=fa_taps.astype(bf), m1=m1.astype(bf), m2=m2.astype(bf),
                e2=e2.astype(bf))


def _fft_a_kernel(f_ref, x_ref, o_ref, *, two):
    f = f_ref[...]
    res = []
    for s in range(FFT_SB):
        x = jnp.concatenate([x_ref[0, s], x_ref[1, s]], axis=0) if two else x_ref[s]
        res.append(jnp.dot(f, x, preferred_element_type=jnp.float32).astype(o_ref.dtype))
    o_ref[...] = jnp.swapaxes(jnp.stack(res), 0, 1).reshape(o_ref.shape)


def fft_stage_a(fmat, x, g):
    two = x.ndim == 5
    xspec = (pl.BlockSpec((None, 2, FFT_SB, FFT_R // 2, MIX_W), lambda j: (g, 0, j, 0, 0)) if two
             else pl.BlockSpec((None, FFT_SB, FFT_R, MIX_W), lambda j: (g, j, 0, 0)))
    return pl.pallas_call(
        functools.partial(_fft_a_kernel, two=two),
        grid=(FFT_R // FFT_SB,),
        in_specs=[pl.BlockSpec((2 * FFT_R, FFT_R), lambda j: (0, 0)), xspec],
        out_specs=pl.BlockSpec((2, FFT_R, FFT_SB, MIX_W), lambda j: (0, 0, j, 0)),
        out_shape=jax.ShapeDtypeStruct((2, FFT_R, FFT_R, MIX_W), jnp.bfloat16),
        compiler_params=_cparams("parallel"),
        name="fft_stage_a",
    )(fmat, x)


def _fft_mid_kernel(m1_ref, a_ref, *rest, conv):
    if conv:
        m2_ref, h_ref, o_ref = rest
    else:
        s_ref, o_ref = rest
    r = FFT_R
    for k in range(FFT_KB):
        a = a_ref[:, k].reshape(2 * r, MIX_W)
        x = jnp.dot(m1_ref[k], a, preferred_element_type=jnp.float32)
        if conv:
            h = h_ref[k].astype(jnp.float32)
            xr, xi, hr, hi = x[:r], x[r:], h[:r], h[r:]
            zc = jnp.concatenate([xr * hr - xi * hi, xr * hi + xi * hr], axis=0).astype(jnp.bfloat16)
            p = jnp.dot(m2_ref[k], zc, preferred_element_type=jnp.float32)
            o_ref[:, k] = p.reshape(2, r, MIX_W).astype(o_ref.dtype)
        else:
            o_ref[k] = (x * s_ref[...]).astype(o_ref.dtype)


def fft_mid(m1, a, m2=None, h=None, scale=None):
    conv = h is not None
    r = FFT_R
    mspec = pl.BlockSpec((FFT_KB, 2 * r, 2 * r), lambda j: (j, 0, 0))
    aspec = pl.BlockSpec((2, FFT_KB, r, MIX_W), lambda j: (0, j, 0, 0))
    hspec = pl.BlockSpec((FFT_KB, 2 * r, MIX_W), lambda j: (j, 0, 0))
    if conv:
        in_specs, args = [mspec, aspec, mspec, hspec], (m1, a, m2, h)
        out_specs, out_shape = aspec, jax.ShapeDtypeStruct((2, r, r, MIX_W), jnp.bfloat16)
    else:
        in_specs, args = [mspec, aspec, pl.BlockSpec((1, MIX_W), lambda j: (0, 0))], (m1, a, scale)
        out_specs, out_shape = hspec, jax.ShapeDtypeStruct((r, 2 * r, MIX_W), jnp.bfloat16)
    return pl.pallas_call(
        functools.partial(_fft_mid_kernel, conv=conv),
        grid=(r // FFT_KB,),
        in_specs=in_specs, out_specs=out_specs, out_shape=out_shape,
        compiler_params=_cparams("parallel"),
        name="fft_mid_conv" if conv else "fft_mid_filter",
    )(*args)


def _fft_out_kernel(e_ref, p_ref, mul_ref, add_ref, bias_ref, o_ref):
    e = e_ref[...]
    half = FFT_R // 2
    bias = bias_ref[...]
    p = jnp.swapaxes(p_ref[...].reshape(2 * FFT_R, FFT_SB, MIX_W), 0, 1)
    for s in range(FFT_SB):
        y = jnp.dot(e, p[s], preferred_element_type=jnp.float32)
        for b in range(2):
            yb = y[b * half:(b + 1) * half] + add_ref[b, s].astype(jnp.float32) * bias
            o_ref[b, s] = (mul_ref[b, s].astype(jnp.float32) * yb).astype(o_ref.dtype)


def fft_out(e2, p, mul, gm, add, ga, bias):
    half = FFT_R // 2

    def bspec(g):
        return pl.BlockSpec((None, 2, FFT_SB, half, MIX_W), lambda j: (g, 0, j, 0, 0))

    return pl.pallas_call(
        _fft_out_kernel,
        grid=(FFT_R // FFT_SB,),
        in_specs=[pl.BlockSpec((FFT_R, 2 * FFT_R), lambda j: (0, 0)),
                  pl.BlockSpec((2, FFT_R, FFT_SB, MIX_W), lambda j: (0, 0, j, 0)), bspec(gm), bspec(ga),
                  pl.BlockSpec((1, MIX_W), lambda j: (0, 0))],
        out_specs=bspec(0),
        out_shape=jax.ShapeDtypeStruct((1, 2, FFT_R, half, MIX_W), jnp.bfloat16),
        compiler_params=_cparams("parallel"),
        name="fft_out",
    )(e2, p, mul, add, bias)


def hyena_latent(xs, taps, inv_norm, bias, mats):
    u, gu = xs, 2
    for order in range(2):
        spec = fft_mid(mats['m1'], fft_stage_a(mats['fa_taps'], taps, order), scale=inv_norm[order])
        p = fft_mid(mats['m1'], fft_stage_a(mats['fa_data'], u, gu), mats['m2'], spec)
        u, gu = fft_out(mats['e2'], p, xs, order, u, gu, bias[order].reshape(1, MIX_W)), 0
    return jnp.swapaxes(u[0], 1, 2).reshape(2, FFT_L // 2, MIX_W)


def hyena_context(xc, taps, inv_norm, bias):
    n = xc.shape[2]
    length = 2 * n
    k = jnp.arange(length, dtype=jnp.int32)
    c, s = _cis(k[:, None] * k[None, :], length)
    fr, fi = c[:, :n], -s[:, :n]
    m_fwd = jnp.concatenate([jnp.concatenate([fr, -fi], axis=1), jnp.concatenate([fi, fr], axis=1)], axis=0)
    m_taps = jnp.concatenate([c, -s], axis=0)
    er, ei = c[:n] / length, s[:n] / length
    m_inv = jnp.concatenate([jnp.concatenate([er, -ei], axis=1), jnp.concatenate([ei, er], axis=1)], axis=0)
    x1, x2, u = (xc[g].astype(jnp.float32) for g in range(3))
    for order, gate in enumerate((x1, x2)):
        h = mm(m_taps, taps[order], name="ctx_dft_taps") * inv_norm[order]
        w = mm(m_fwd, u.reshape(length, MIX_W), name="ctx_dft_fwd")
        wr, wi, hr, hi = w[:length], w[length:], h[:length], h[length:]
        zc = jnp.concatenate([wr * hr - wi * hi, wr * hi + wi * hr], axis=0)
        y = mm(m_inv, zc, name="ctx_dft_inv").reshape(2, n, MIX_W)
        u = gate * (y + u * bias[order])
    return u.astype(jnp.bfloat16)


def hyena_features(n, ff1, ff1_b, freq, ff2, ff2_b):
    k = jnp.arange(n, dtype=jnp.int32)
    pos = jnp.stack([k, (n - k) % n]).astype(jnp.float32)
    t = pos / (n - 1)
    fw = (2.0 * math.pi * pos / n)[..., None] * jnp.linspace(1e-4, HY_BANDS - 1, HY_BANDS, dtype=jnp.float32)
    feats = jnp.concatenate([t[..., None], jnp.cos(fw), -jnp.sin(fw)], axis=-1)
    h = jnp.sin(freq * (feats @ ff1 + ff1_b))
    return jnp.sin(freq * (h @ ff2 + ff2_b))


def _filter_kernel(feat_ref, w_ref, delta_ref, taps_ref, sum_ref, *, n, tm, split):
    dr = pl.program_id(1)
    i = pl.program_id(2)
    row = i * tm + lax.broadcasted_iota(jnp.int32, (tm, MIX_W), 0)
    lag = jnp.where(dr == 0, row, jnp.where(row == 0, 0, n - row))
    t = lag.astype(jnp.float32) / (n - 1)
    h = jnp.dot(feat_ref[...].astype(jnp.bfloat16), w_ref[...].astype(jnp.bfloat16),
                preferred_element_type=jnp.float32)
    h = h * jnp.exp(-t * delta_ref[...])

    @pl.when((dr == 0) & (i == 0))
    def _():
        sum_ref[...] = jnp.zeros_like(sum_ref)

    sum_ref[...] += jnp.sum(jnp.abs(h), axis=0, keepdims=True)
    taps = jnp.where((dr == 0) | (row != 0), h, 0.0).astype(taps_ref.dtype)
    if split:
        taps = jnp.swapaxes(taps.reshape(tm // FFT_R, FFT_R, MIX_W), 0, 1)
    taps_ref[...] = taps


def hyena_filter_taps(feats, ff3):
    _, n, nf = feats.shape
    split = 2 * n == FFT_L
    tm = FFT_SB * FFT_R if split else n
    nb = n // tm
    w = ff3.reshape(nf, 2, 2, MIX_W).transpose(1, 2, 0, 3)
    deltas = jnp.abs(jnp.linspace(math.log(HY_TARGET) / HY_SLOW_PCT, math.log(HY_TARGET) / HY_FAST_PCT,
                                  MIX_W, dtype=jnp.float32)).reshape(1, MIX_W)
    if split:
        tspec = pl.BlockSpec((None, FFT_R, FFT_SB, MIX_W), lambda o, dr, i: (o, 0, dr * nb + i, 0))
        tshape = jax.ShapeDtypeStruct((2, FFT_R, FFT_R, MIX_W), jnp.bfloat16)
    else:
        tspec = pl.BlockSpec((None, tm, MIX_W), lambda o, dr, i: (o, dr * nb + i, 0))
        tshape = jax.ShapeDtypeStruct((2, 2 * n, MIX_W), jnp.bfloat16)
    return pl.pallas_call(
        functools.partial(_filter_kernel, n=n, tm=tm, split=split),
        grid=(2, 2, nb),
        in_specs=[pl.BlockSpec((None, tm, nf), lambda o, dr, i: (dr, i, 0)),
                  pl.BlockSpec((None, None, nf, MIX_W), lambda o, dr, i: (o, dr, 0, 0)),
                  pl.BlockSpec((1, MIX_W), lambda o, dr, i: (0, 0))],
        out_specs=[tspec, pl.BlockSpec((None, 1, MIX_W), lambda o, dr, i: (o, 0, 0))],
        out_shape=[tshape, jax.ShapeDtypeStruct((2, 1, MIX_W), jnp.float32)],
        compiler_params=_cparams("arbitrary", "arbitrary", "arbitrary"),
        name="hyena_filter",
    )(feats, w, deltas)


def _merge_kernel(yh_ref, yl_ref, yg_ref, w_ref, g0_ref, g1_ref, g2_ref, o_ref):
    acc = None
    for y_ref, g_ref, j in ((yh_ref, g0_ref, 0), (yl_ref, g1_ref, 1), (yg_ref, g2_ref, 2)):
        gate = 1.0 / (1.0 + jnp.exp(-g_ref[...].astype(jnp.float32)))
        t = gate * jnp.dot(y_ref[...], w_ref[j], preferred_element_type=jnp.float32)
        acc = t if acc is None else acc + t
    o_ref[...] = acc.astype(o_ref.dtype)


def merge_branches(z, ys, w_branch, layer):
    rows = z.shape[0]
    tm = ROW_T // 2
    yspec = pl.BlockSpec((tm, MIX_W), lambda i: (i, 0))

    def gspec(k):
        return pl.BlockSpec((tm, D_MODEL), lambda i: (i, GATE_COL0 // D_MODEL + k))

    return pl.pallas_call(
        _merge_kernel,
        grid=(rows // tm,),
        in_specs=[yspec, yspec, yspec,
                  pl.BlockSpec((None, N_BRANCH, MIX_W, D_MODEL), lambda i: (layer, 0, 0, 0)),
                  gspec(0), gspec(1), gspec(2)],
        out_specs=pl.BlockSpec((tm, D_MODEL), lambda i: (i, 0)),
        out_shape=jax.ShapeDtypeStruct((rows, D_MODEL), jnp.bfloat16),
        compiler_params=_cparams("parallel"),
        name="merge",
    )(*ys, w_branch, z, z, z)


def _out_proj_kernel(a_ref, w_ref, x_ref, g1_ref, gt_ref, g2_ref, sh_ref, sc_ref, wr_ref, xo_ref, h_ref, lg_ref):
    y = jnp.dot(a_ref[...], w_ref[...], preferred_element_type=jnp.float32)
    y = y * lax.rsqrt(jnp.mean(y * y, axis=-1, keepdims=True) + EPS) * g1_ref[...]
    x = x_ref[...] + gt_ref[...] * y
    xo_ref[...] = x
    h = x * lax.rsqrt(jnp.mean(x * x, axis=-1, keepdims=True) + EPS) * g2_ref[...]
    h = (h * (1.0 + sc_ref[...]) + sh_ref[...]).astype(jnp.bfloat16)
    h_ref[...] = h
    lg_ref[...] = jnp.dot(h, wr_ref[...], preferred_element_type=jnp.float32)


def out_proj(acc, w_out, layer, x, g1, gt, g2, sh, sc, w_router, n_ctx_rows, n_lat_rows):
    rows, d = x.shape
    tm = ROW_T // 2
    idx = functools.partial(_mod_index, n_ctx_tiles=2 * n_ctx_rows // tm, tiles_per_batch=n_lat_rows // tm)
    row = pl.BlockSpec((tm, d), lambda i: (i, 0))
    vec = pl.BlockSpec((1, d), lambda i: (0, 0))
    mod = pl.BlockSpec((None, 1, d), lambda i: (idx(i), 0, 0))
    wr = jnp.zeros((d, ROUTER_PAD), jnp.bfloat16).at[:, :N_EXPERTS].set(w_router.astype(jnp.bfloat16))
    return pl.pallas_call(
        _out_proj_kernel,
        grid=(rows // tm,),
        in_specs=[row, pl.BlockSpec((None, d, d), lambda i: (layer, 0, 0)), row, vec, mod, vec, mod, mod,
                  pl.BlockSpec((d, ROUTER_PAD), lambda i: (0, 0))],
        out_specs=[row, row, pl.BlockSpec((tm, ROUTER_PAD), lambda i: (i, 0))],
        out_shape=[jax.ShapeDtypeStruct((rows, d), jnp.float32), jax.ShapeDtypeStruct((rows, d), jnp.bfloat16),
                   jax.ShapeDtypeStruct((rows, ROUTER_PAD), jnp.float32)],
        compiler_params=_cparams("parallel"),
        name="out_proj",
    )(acc, w_out, x, g1.reshape(1, d), gt, g2.reshape(1, d), sh, sc, wr)


def _ffn_up_kernel(x_ref, wg_ref, wu_ref, h_ref):
    x = x_ref[...]
    hg = jnp.dot(x, wg_ref[...].astype(jnp.bfloat16), preferred_element_type=jnp.float32)
    hu = jnp.dot(x, wu_ref[...].astype(jnp.bfloat16), preferred_element_type=jnp.float32)
    h_ref[...] = (hg / (1.0 + jnp.exp(-hg)) * hu).astype(h_ref.dtype)


def _ffn_down_kernel(h_ref, wd_ref, g_ref, o_ref):
    y = jnp.dot(h_ref[...], wd_ref[...].astype(jnp.bfloat16), preferred_element_type=jnp.float32)
    o_ref[...] = (y * g_ref[...]).astype(o_ref.dtype)


def expert_ffn(xe, w_gate, w_up, w_down, layer, g):
    ne, r, d = xe.shape
    dff = w_gate.shape[-1]
    hid = pl.pallas_call(
        _ffn_up_kernel,
        grid=(ne, dff // FFN_TF),
        in_specs=[pl.BlockSpec((None, r, d), lambda e, f: (e, 0, 0)),
                  pl.BlockSpec((None, None, d, FFN_TF), lambda e, f: (layer, e, 0, f)),
                  pl.BlockSpec((None, None, d, FFN_TF), lambda e, f: (layer, e, 0, f))],
        out_specs=pl.BlockSpec((None, r, FFN_TF), lambda e, f: (e, 0, f)),
        out_shape=jax.ShapeDtypeStruct((ne, r, dff), jnp.bfloat16),
        compiler_params=_cparams("parallel", "arbitrary"),
        name="ffn_up",
    )(xe, w_gate, w_up)
    return pl.pallas_call(
        _ffn_down_kernel,
        grid=(ne, d // FFN_TF),
        in_specs=[pl.BlockSpec((None, r, dff), lambda e, j: (e, 0, 0)),
                  pl.BlockSpec((None, None, dff, FFN_TF), lambda e, j: (layer, e, 0, j)),
                  pl.BlockSpec((None, r, 1), lambda e, j: (e, 0, 0))],
        out_specs=pl.BlockSpec((None, r, FFN_TF), lambda e, j: (e, 0, j)),
        out_shape=jax.ShapeDtypeStruct((ne, r, d), jnp.bfloat16),
        compiler_params=_cparams("parallel", "arbitrary"),
        name="ffn_down",
    )(hid, w_down, g)


def route(logits, row0, bsz, n):
    cap = EC_CAPACITY * n // N_EXPERTS
    aff = jax.nn.softmax(logits[row0:row0 + bsz * n, :N_EXPERTS].reshape(bsz, n, N_EXPERTS), axis=-1)
    g, idx = lax.top_k(jnp.swapaxes(aff, 1, 2), cap)
    flat = idx + (row0 + jnp.arange(bsz, dtype=idx.dtype) * n)[:, None, None]
    return (jnp.swapaxes(g, 0, 1).reshape(N_EXPERTS, bsz * cap),
            jnp.swapaxes(flat, 0, 1).reshape(N_EXPERTS, bsz * cap))


def ec_moe(h, logits, w_gate, w_up, w_down, layer, segments):
    gs, flats = zip(*(route(logits, *seg) for seg in segments))
    flat, g = lax.sort_key_val(jnp.concatenate(flats, axis=1), jnp.concatenate(gs, axis=1), dimension=1)
    return expert_ffn(h[flat], w_gate, w_up, w_down, layer, g[..., None]), flat


def moe_plan(flat, n_rows):
    ne, r = flat.shape
    nwin = r // MOE_WIN
    n_tiles = n_rows // MOE_TT
    bounds = jnp.arange(n_tiles + 1, dtype=flat.dtype) * MOE_TT
    p = jnp.sum(flat[:, :, None] < bounds[None, None, :], axis=1, dtype=jnp.int32)
    p0, p1 = p[:, :-1].T, p[:, 1:].T
    w0 = p0 // MOE_WIN
    nw = jnp.where(p1 > p0, (p1 - 1) // MOE_WIN - w0 + 1, 0)
    off = jnp.cumsum(nw, axis=1)
    q = jnp.arange(MOE_MAXQ, dtype=jnp.int32)
    e_of_q = jnp.minimum(jnp.sum(q[None, :, None] >= off[:, None, :], axis=2, dtype=jnp.int32), ne - 1)
    w = jnp.take_along_axis(w0, e_of_q, axis=1) + q[None, :] - jnp.take_along_axis(off - nw, e_of_q, axis=1)
    items = jnp.clip(e_of_q * nwin + w, 0, ne * nwin - 1)
    return items.astype(jnp.int32), off[:, -1].astype(jnp.int32)


def _combine_kernel(items_ref, count_ref, tok_ref, ye_ref, x_ref, g_ref, gt_ref, *rest, nxt, skip):
    if nxt:
        gn_ref, sh_ref, sc_ref, o_ref, u_ref, stage, sem, acc_ref = rest
    else:
        o_ref, stage, sem, acc_ref = rest
    i = pl.program_id(0)
    t = i + skip
    n = count_ref[t]
    n_groups = (n + MOE_GROUP - 1) // MOE_GROUP

    @pl.when(i == 0)
    def _():
        stage[...] = jnp.zeros_like(stage)

    def window_copy(g, buf, k):
        item = items_ref[t, jnp.minimum(MOE_GROUP * g + k, MOE_MAXQ - 1)]
        return pltpu.make_async_copy(ye_ref.at[pl.ds(item * MOE_WIN, MOE_WIN)],
                                     stage.at[buf, pl.ds(k * MOE_WIN, MOE_WIN)], sem.at[buf])

    def for_group(g, buf, wait):
        for k in range(MOE_GROUP):
            @pl.when(MOE_GROUP * g + k < n)
            def _():
                cp = window_copy(g, buf, k)
                cp.wait() if wait else cp.start()

    acc_ref[...] = jnp.zeros_like(acc_ref)

    for j in range(MOE_NBUF - 1):
        @pl.when(j < n_groups)
        def _():
            for_group(j, j, False)

    row = t * MOE_TT + lax.broadcasted_iota(jnp.int32, (MOE_TT, MOE_GROUP * MOE_WIN), 0)

    def body(g, carry):
        buf = g % MOE_NBUF
        ahead = g + MOE_NBUF - 1

        @pl.when(ahead < n_groups)
        def _():
            for_group(ahead, ahead % MOE_NBUF, False)

        for_group(g, buf, True)
        toks = []
        for k in range(MOE_GROUP):
            qk = MOE_GROUP * g + k
            item = items_ref[t, jnp.minimum(qk, MOE_MAXQ - 1)]
            toks.append(jnp.where(qk < n, tok_ref[pl.ds(item, 1), :], -1))
        onehot = jnp.where(row == jnp.concatenate(toks, axis=1), 1.0, 0.0).astype(jnp.bfloat16)
        acc_ref[...] += jnp.dot(onehot, stage[buf], preferred_element_type=jnp.float32)
        return carry

    lax.fori_loop(0, n_groups, body, 0)

    y = acc_ref[...]
    y = y * lax.rsqrt(jnp.mean(y * y, axis=-1, keepdims=True) + EPS) * g_ref[...]
    x = x_ref[...] + gt_ref[...] * y
    if nxt:
        u = x * lax.rsqrt(jnp.mean(x * x, axis=-1, keepdims=True) + EPS) * gn_ref[...]
        u_ref[...] = (u * (1.0 + sc_ref[...]) + sh_ref[...]).astype(u_ref.dtype)
    o_ref[...] = x


def moe_combine(x, ye, flat, g, gt, n_ctx_rows, n_lat_rows, nxt=None):
    rows, d = x.shape
    ne, r, _ = ye.shape
    items, count = moe_plan(flat, rows)
    n_ctx_tiles = 2 * n_ctx_rows // MOE_TT
    idx = functools.partial(_mod_index, n_ctx_tiles=n_ctx_tiles, tiles_per_batch=n_lat_rows // MOE_TT)
    skip = 0 if nxt else n_ctx_tiles
    row = pl.BlockSpec((MOE_TT, d), lambda i, *_: (i + skip, 0))
    vec = pl.BlockSpec((1, d), lambda i, *_: (0, 0))
    mod = pl.BlockSpec((None, 1, d), lambda i, *_: (idx(i + skip), 0, 0))
    out_row = pl.BlockSpec((MOE_TT, d), lambda i, *_: (i, 0))
    n_out = rows - skip * MOE_TT
    in_specs = [pl.BlockSpec((ne * r // MOE_WIN, MOE_WIN), lambda i, *_: (0, 0)),
                pl.BlockSpec(memory_space=pl.ANY), row, vec, mod]
    args = [flat.reshape(ne * r // MOE_WIN, MOE_WIN), ye.reshape(ne * r, d), x, g.reshape(1, d), gt]
    out_specs, out_shape = out_row, jax.ShapeDtypeStruct((n_out, d), jnp.float32)
    if nxt:
        in_specs += [vec, mod, mod]
        args += [nxt[0].reshape(1, d), nxt[1], nxt[2]]
        out_specs = [out_row, out_row]
        out_shape = [out_shape, jax.ShapeDtypeStruct((n_out, d), jnp.bfloat16)]
    return pl.pallas_call(
        functools.partial(_combine_kernel, nxt=bool(nxt), skip=skip),
        grid_spec=pltpu.PrefetchScalarGridSpec(
            num_scalar_prefetch=2,
            grid=(n_out // MOE_TT,),
            in_specs=in_specs, out_specs=out_specs,
            scratch_shapes=[pltpu.VMEM((MOE_NBUF, MOE_GROUP * MOE_WIN, d), jnp.bfloat16),
                            pltpu.SemaphoreType.DMA((MOE_NBUF,)),
                            pltpu.VMEM((MOE_TT, d), jnp.float32)]),
        out_shape=out_shape,
        compiler_params=_cparams("arbitrary"),
        name="moe_combine",
    )(items, count, *args)


def kernel(x, c, ctx, c_ctx, w_mod, b_mod, norm_g, w_in, hy_conv_w, hy_conv_b, hy_ff1, hy_ff1_b, hy_freq,
           hy_ff2, hy_ff2_b, hy_ff3, hy_bias, lru_conv_w, lru_conv_b, lru_wa, lru_ba, lru_wi, lru_bi,
           lru_lambda, hg_lb_logits, hg_norm_g, w_branch, w_out, w_router, w_gate, w_up, w_down):
    bsz, n_lat, d = x.shape
    n_ctx = ctx.shape[1]
    assert bsz == 2 and d == D_MODEL and n_ctx == SEQ_T and 2 * n_lat == FFT_L and n_lat % ROW_T == 0
    gam = jax.nn.softmax(hg_lb_logits.astype(jnp.float32), axis=0)
    lb_all = jnp.maximum(jnp.cumsum(gam, axis=0) - gam[:1], 0.0)
    mats = fft_matrices()
    cvec = jnp.zeros((SUBLANES, d), jnp.float32).at[:bsz].set(jax.nn.silu(c)).at[bsz].set(jax.nn.silu(c_ctx))
    n_ctx_all = bsz * n_ctx
    xa = jnp.concatenate([ctx.reshape(n_ctx_all, d), x.reshape(bsz * n_lat, d)], axis=0)
    w_in_b, w_branch_b, w_out_b = (w.astype(jnp.bfloat16) for w in (w_in, w_branch, w_out))
    mods = []
    for l in range(DEPTH):
        mod = mm(cvec, w_mod, layer=l, name="adaln")[:bsz + 1] + b_mod[l]
        mods.append([mod.reshape(bsz + 1, 1, 6, d)[:, :, k] for k in range(6)])
    u = prenorm(xa, norm_g[0, 0], mods[0][0], mods[0][1], n_ctx, n_lat)
    for l in range(DEPTH):
        last = l == DEPTH - 1
        p = {'lru_conv_w': lru_conv_w[l], 'lru_conv_b': lru_conv_b[l], 'lru_wa': lru_wa[l], 'lru_ba': lru_ba[l],
             'lru_wi': lru_wi[l], 'lru_bi': lru_bi[l], 'lru_lambda': lru_lambda[l]}
        sh1, sc1, gt1, sh2, sc2, gt2 = mods[l]

        z = mm(u, w_in_b, jnp.bfloat16, tm=IN_PROJ_TM, tn=1024, layer=l, name="in_proj")

        y_lru = lru_branch(z, p, n_ctx, n_lat)
        y_hg = hgrn_branch(z, lb_all[l], hg_norm_g[l], n_ctx, n_lat)

        ffp = (hy_ff1[l], hy_ff1_b[l], hy_freq[l], hy_ff2[l], hy_ff2_b[l])
        taps_l, norm_l = hyena_filter_taps(hyena_features(n_lat, *ffp), hy_ff3[l])
        taps_c, norm_c = hyena_filter_taps(hyena_features(n_ctx, *ffp), hy_ff3[l])
        xc_l = hy_conv_latent(z, hy_conv_w[l], hy_conv_b[l], n_ctx_all, n_lat)
        y_hy_l = hyena_latent(xc_l, taps_l, 1.0 / norm_l, hy_bias[l], mats)
        xc_c = hy_conv(z, hy_conv_w[l], hy_conv_b[l], 0, n_ctx, True)
        y_hy_c = hyena_context(xc_c, taps_c, 1.0 / norm_c, hy_bias[l])
        y_hy = jnp.concatenate([y_hy_c.reshape(n_ctx_all, MIX_W), y_hy_l.reshape(bsz * n_lat, MIX_W)], axis=0)

        acc = merge_branches(z, (y_hy, y_lru, y_hg), w_branch_b, l)
        xa, h, logits = out_proj(acc, w_out_b, l, xa, norm_g[l, 1], gt1, norm_g[l, 2], sh2, sc2,
                                 w_router[l], n_ctx, n_lat)
        segments = [(n_ctx_all, bsz, n_lat)] if last else [(n_ctx_all, bsz, n_lat), (0, bsz, n_ctx)]
        ye, flat = ec_moe(h, logits, w_gate, w_up, w_down, l, segments)
        if last:
            xa = moe_combine(xa, ye, flat, norm_g[l, 3], gt2, n_ctx, n_lat)
        else:
            nxt = (norm_g[l + 1, 0], mods[l + 1][0], mods[l + 1][1])
            xa, u = moe_combine(xa, ye, flat, norm_g[l, 3], gt2, n_ctx, n_lat, nxt)
    return xa.reshape(bsz, n_lat, d)
```

```python
import functools
import math

import numpy as np
import jax
import jax.numpy as jnp
from jax import lax
from jax.experimental import pallas as pl
from jax.experimental.pallas import tpu as pltpu

D_MODEL = 2048
DEPTH = 2
GRID_W = 64
MIX_W = D_MODEL // 2
N_BRANCH = 3
HY_BANDS = 16
HY_TARGET = 1e-2
HY_FAST_PCT = 0.3
HY_SLOW_PCT = 1.5
LRU_BLOCKS = 8
LRU_BS = MIX_W // LRU_BLOCKS
LRU_C = 8.0
HG_HEADS = 8
HG_DK = MIX_W // HG_HEADS
N_EXPERTS = 16
EC_CAPACITY = 2
EPS = 1e-6
TINY = 1e-30
LOG2E = 1.4426950408889634
HY_COL0 = 0
LRU_COL0 = 3 * MIX_W
HG_COL0 = 5 * MIX_W
GATE_COL0 = 10 * MIX_W
IN_COLS = GATE_COL0 + N_BRANCH * D_MODEL

SUBLANES = 8
LANES = 128
SEQ_T = 256
HG_C = 128
HG_LEVELS = 7
HG_HB = 8
ROW_T = 512
IN_PROJ_TM = 1536
FFT_R = 128
FFT_L = FFT_R * FFT_R
FFT_KB = 8
FFT_SB = 16
HY_PARTS = 4
ROUTER_PAD = LANES
FFN_TF = 256
FFN_TD = 512
MOE_TT = 512
MOE_WIN = 64
MOE_GROUP = 8
MOE_NBUF = 3
MOE_MAXQ = N_EXPERTS * (MOE_TT // MOE_WIN + 1)
VMEM_LIMIT_BYTES = 48 * 1024 * 1024


def _cparams(*sem):
    return pltpu.CompilerParams(dimension_semantics=sem, vmem_limit_bytes=VMEM_LIMIT_BYTES)


def _pick_tile(n, pref):
    for t in (pref, 1024, 512, 256, 128):
        if t <= n and n % t == 0:
            return t
    return n


def _row_block(b, j, n_ctx, n_lat, reverse):
    if reverse:
        kc, kl = n_ctx - 1 - j, n_ctx + n_lat - 1 - j
    else:
        kc, kl = j, j - n_ctx
    return jnp.where(j < n_ctx, b * n_ctx + kc, 2 * n_ctx + b * n_lat + kl)


def _mm_kernel(a_ref, b_ref, o_ref):
    o_ref[...] = jnp.dot(a_ref[...].astype(jnp.bfloat16), b_ref[...].astype(jnp.bfloat16),
                         preferred_element_type=jnp.float32).astype(o_ref.dtype)


def mm(a, b, out_dtype=jnp.float32, tm=512, tn=1024, layer=None, name="mm"):
    m, k = a.shape
    n = b.shape[-1]
    tm = _pick_tile(m, tm)
    tn = _pick_tile(n, tn)
    if layer is None:
        bspec = pl.BlockSpec((k, tn), lambda i, j: (0, j))
    else:
        bspec = pl.BlockSpec((None, k, tn), lambda i, j: (layer, 0, j))
    return pl.pallas_call(
        _mm_kernel,
        grid=(m // tm, n // tn),
        in_specs=[pl.BlockSpec((tm, k), lambda i, j: (i, 0)), bspec],
        out_specs=pl.BlockSpec((tm, tn), lambda i, j: (i, j)),
        out_shape=jax.ShapeDtypeStruct((m, n), out_dtype),
        compiler_params=_cparams("parallel", "arbitrary"),
        name=name,
    )(a, b)


def _mod_index(i, n_ctx_tiles, tiles_per_batch):
    return jnp.where(i < n_ctx_tiles, 2, (i - n_ctx_tiles) // tiles_per_batch)


def _prenorm_kernel(x_ref, g_ref, sh_ref, sc_ref, u_ref):
    x = x_ref[...]
    y = x * lax.rsqrt(jnp.mean(x * x, axis=-1, keepdims=True) + EPS) * g_ref[...]
    u_ref[...] = (y * (1.0 + sc_ref[...]) + sh_ref[...]).astype(u_ref.dtype)


def prenorm(x, g, sh, sc, n_ctx_rows, n_lat_rows):
    rows, d = x.shape
    idx = functools.partial(_mod_index, n_ctx_tiles=2 * n_ctx_rows // ROW_T, tiles_per_batch=n_lat_rows // ROW_T)
    row = pl.BlockSpec((ROW_T, d), lambda i: (i, 0))
    vec = pl.BlockSpec((None, 1, d), lambda i: (idx(i), 0, 0))
    return pl.pallas_call(
        _prenorm_kernel,
        grid=(rows // ROW_T,),
        in_specs=[row, pl.BlockSpec((1, d), lambda i: (0, 0)), vec, vec],
        out_specs=row,
        out_shape=jax.ShapeDtypeStruct((rows, d), jnp.bfloat16),
        compiler_params=_cparams("parallel"),
        name="prenorm",
    )(x, g.reshape(1, d), sh, sc)


def _gelu_tanh(x):
    return 0.5 * x * (1.0 + jnp.tanh(math.sqrt(2.0 / math.pi) * (x + 0.044715 * (x * x * x))))


def _conv_pos(is_ctx, shape):
    t = lax.broadcasted_iota(jnp.int32, shape, 0)
    pos = jnp.where(is_ctx, t, t & (GRID_W - 1))
    last = jnp.where(is_ctx, SEQ_T - 1, GRID_W - 1)
    return pos, last


def _lru_pass_kernel(x_ref, cw_ref, cb_ref, w_ref, gb_ref, lam_ref, *rest, reverse, final):
    if final:
        lg_ref, hprev_ref, out_ref, h_ref, a_ref, b_ref = rest
    else:
        out_ref, h_ref, a_ref, b_ref = rest
    j = pl.program_id(1)

    @pl.when(j == 0)
    def _():
        h_ref[...] = jnp.zeros_like(h_ref)

    x = x_ref[...].astype(jnp.float32)
    pos, last = _conv_pos(j == 0, x.shape)
    cw = cw_ref[...]
    xc = cb_ref[...] + cw[2:3] * x
    xc = xc + cw[0:1] * jnp.where(pos >= 2, pltpu.roll(x, 2, 0), 0.0)
    xc = xc + cw[1:2] * jnp.where(pos >= 1, pltpu.roll(x, 1, 0), 0.0)
    xc = xc + cw[3:4] * jnp.where(pos < last, pltpu.roll(x, SEQ_T - 1, 0), 0.0)

    xb = xc.astype(jnp.bfloat16)
    lam = lam_ref[...]
    sp = jnp.maximum(-lam, 0.0) + jnp.log(1.0 + jnp.exp(-jnp.abs(lam)))
    n_tiles = SEQ_T // SUBLANES
    row = lax.broadcasted_iota(jnp.int32, (1, SUBLANES, LRU_BS), 1)
    for n in range(LRU_BLOCKS):
        sl = slice(n * LRU_BS, (n + 1) * LRU_BS)
        pre = jnp.dot(xb[:, sl], w_ref[n], preferred_element_type=jnp.float32) + gb_ref[n]
        gate = 1.0 / (1.0 + jnp.exp(-pre))
        r, i = gate[:, :LRU_BS], gate[:, LRU_BS:]
        a = jnp.exp(-LRU_C * r * sp[:, sl])
        om = 1.0 - a * a
        b = om * lax.rsqrt(jnp.maximum(om, TINY)) * (i * xc[:, sl])
        a = a.reshape(n_tiles, SUBLANES, LRU_BS)
        b = b.reshape(n_tiles, SUBLANES, LRU_BS)
        for d in (1, 2, 4):
            if reverse:
                m = row < SUBLANES - d
                a_sh, b_sh = pltpu.roll(a, SUBLANES - d, 1), pltpu.roll(b, SUBLANES - d, 1)
            else:
                m = row >= d
                a_sh, b_sh = pltpu.roll(a, d, 1), pltpu.roll(b, d, 1)
            b = jnp.where(m, a * b_sh + b, b)
            a = jnp.where(m, a * a_sh, a)
        a_ref[:, sl] = a.reshape(SEQ_T, LRU_BS)
        b_ref[:, sl] = b.reshape(SEQ_T, LRU_BS)

    h = h_ref[...]
    for k in range(n_tiles):
        kk = n_tiles - 1 - k if reverse else k
        rs = slice(kk * SUBLANES, (kk + 1) * SUBLANES)
        ht = a_ref[rs, :] * h + b_ref[rs, :]
        h = ht[0:1, :] if reverse else ht[SUBLANES - 1:SUBLANES, :]
        if final:
            ht = (ht + hprev_ref[rs, :]) * _gelu_tanh(lg_ref[rs, :].astype(jnp.float32))
        out_ref[rs, :] = ht.astype(out_ref.dtype)
    h_ref[...] = h


def lru_pass(z, conv_w, conv_b, wa, ba, wi, bi, lam, n_ctx_rows, n_lat_rows, reverse, h_prev=None):
    rows = z.shape[0]
    n_lat = n_lat_rows // SEQ_T
    cb = LRU_COL0 // MIX_W
    final = h_prev is not None
    rb = functools.partial(_row_block, n_ctx=n_ctx_rows // SEQ_T, n_lat=n_lat, reverse=reverse)

    def zspec(k):
        return pl.BlockSpec((SEQ_T, MIX_W), lambda b, j: (rb(b, j), cb + k))

    row_spec = pl.BlockSpec((SEQ_T, MIX_W), lambda b, j: (rb(b, j), 0))

    def const_spec(shape):
        return pl.BlockSpec(shape, lambda b, j: (0,) * len(shape))

    w2 = jnp.concatenate([wa, wi], axis=-1).astype(jnp.bfloat16)
    gb = jnp.concatenate([ba.reshape(LRU_BLOCKS, 1, LRU_BS), bi.reshape(LRU_BLOCKS, 1, LRU_BS)], axis=-1)
    in_specs = [zspec(0), const_spec((4, MIX_W)), const_spec((1, MIX_W)),
                const_spec((LRU_BLOCKS, LRU_BS, 2 * LRU_BS)), const_spec((LRU_BLOCKS, 1, 2 * LRU_BS)),
                const_spec((1, MIX_W))]
    args = [z, conv_w, conv_b.reshape(1, MIX_W), w2, gb, lam.reshape(1, MIX_W)]
    if final:
        in_specs += [zspec(1), row_spec]
        args += [z, h_prev]
    return pl.pallas_call(
        functools.partial(_lru_pass_kernel, reverse=reverse, final=final),
        grid=(2, n_ctx_rows // SEQ_T + n_lat),
        in_specs=in_specs,
        out_specs=row_spec,
        out_shape=jax.ShapeDtypeStruct((rows, MIX_W), jnp.bfloat16 if final else jnp.float32),
        scratch_shapes=[pltpu.VMEM((1, MIX_W), jnp.float32),
                        pltpu.VMEM((SEQ_T, MIX_W), jnp.float32),
                        pltpu.VMEM((SEQ_T, MIX_W), jnp.float32)],
        compiler_params=_cparams("parallel", "arbitrary"),
        name="lru_bwd" if reverse else "lru_fwd",
    )(*args)


def lru_branch(z, p, n_ctx_rows, n_lat_rows):
    def one(d, h_prev):
        return lru_pass(z, p['lru_conv_w'], p['lru_conv_b'], p['lru_wa'][d], p['lru_ba'][d], p['lru_wi'][d],
                        p['lru_bi'][d], p['lru_lambda'][d], n_ctx_rows, n_lat_rows, d == 1, h_prev)
    return one(1, one(0, None))


def _hg_level_matrix(reverse):
    t = np.arange(HG_C)[:, None]
    s = np.arange(HG_C)[None, :]
    x = t ^ s
    lv = np.where(x > 0, np.floor(np.log2(np.maximum(x, 1))).astype(np.int32), -1)
    lv = np.where(s < t, lv, -1).astype(np.int32)
    return jnp.asarray(lv.T if reverse else lv)


def _hg_pass_kernel(lv_ref, lb_ref, q_ref, f_ref, v_ref, *rest, reverse, final):
    if final:
        og_ref, oprev_ref, g_ref, out_ref, st_ref = rest
    else:
        out_ref, st_ref = rest

    @pl.when(pl.program_id(1) == 0)
    def _():
        st_ref[...] = jnp.zeros_like(st_ref)

    lv = lv_ref[...]
    at_level = [lv == lvl for lvl in range(HG_LEVELS)]
    n_tiles = HG_C // SUBLANES
    wb = HG_HB * HG_DK
    row = lax.broadcasted_iota(jnp.int32, (1, SUBLANES, wb), 1)
    for hb in range(HG_HEADS // HG_HB):
        sl = slice(hb * wb, (hb + 1) * wb)
        heads = [(hb * HG_HB + j, slice(j * HG_DK, (j + 1) * HG_DK)) for j in range(HG_HB)]
        x = f_ref[:, sl].astype(jnp.float32)
        lb = lb_ref[:, sl]
        e = jnp.exp(-jnp.abs(x))
        inv = 1.0 / (1.0 + e)
        pos = x >= 0
        sig = jnp.where(pos, inv, e * inv)
        sigm = jnp.where(pos, e * inv, inv)
        lsig = jnp.minimum(x, 0.0) - jnp.log(1.0 + e)
        lf = jnp.where(lb > 0, jnp.log(lb + (1.0 - lb) * sig), lsig)
        kk = (1.0 - lb) * sigm
        q = q_ref[:, sl].astype(jnp.float32)
        v = v_ref[:, sl]

        c = (lf * LOG2E).reshape(n_tiles, SUBLANES, wb)
        bt = c
        att = [None] * HG_HB
        for lvl in range(HG_LEVELS):
            m = 1 << lvl
            qm = (q * jnp.exp2(c).reshape(HG_C, wb)).astype(jnp.bfloat16)
            km = (kk * jnp.exp2(bt - c).reshape(HG_C, wb)).astype(jnp.bfloat16)
            for j, (_, hs) in enumerate(heads):
                d = lax.dot_general(qm[:, hs], km[:, hs], (((1,), (1,)), ((), ())),
                                    preferred_element_type=jnp.float32)
                att[j] = jnp.where(at_level[lvl], d, 0.0 if lvl == 0 else att[j])
            if m < SUBLANES:
                upper = (row & m) != 0
                down = pltpu.roll(bt, m, 1)
                up = pltpu.roll(bt, SUBLANES - m, 1)
                if reverse:
                    c = c + jnp.where(upper, 0.0, up)
                else:
                    c = c + jnp.where(upper, down, 0.0)
                bt = bt + jnp.where(upper, down, up)
            else:
                k = m // SUBLANES
                pair = (n_tiles // (2 * k), 2, k, SUBLANES, wb)
                c5, b5 = c.reshape(pair), bt.reshape(pair)
                tot2 = b5[:, 0] + b5[:, 1]
                if reverse:
                    c = jnp.stack([c5[:, 0] + b5[:, 1], c5[:, 1]], axis=1)
                else:
                    c = jnp.stack([c5[:, 0], c5[:, 1] + b5[:, 0]], axis=1)
                c = c.reshape(n_tiles, SUBLANES, wb)
                bt = jnp.stack([tot2, tot2], axis=1).reshape(n_tiles, SUBLANES, wb)

        qm = (q * jnp.exp2(c).reshape(HG_C, wb)).astype(jnp.bfloat16)
        km = (kk * jnp.exp2(bt - c).reshape(HG_C, wb)).astype(jnp.bfloat16)
        tot = jnp.exp2(bt[0, 0:1, :])
        diag = q * kk
        for j, (h, hs) in enumerate(heads):
            osl = slice(h * HG_DK, (h + 1) * HG_DK)
            st = st_ref[h]
            vh = v[:, hs]
            o = lax.dot_general(qm[:, hs], st.astype(jnp.bfloat16), (((1,), (1,)), ((), ())),
                                preferred_element_type=jnp.float32)
            o = o + jnp.dot(att[j].astype(jnp.bfloat16), vh, preferred_element_type=jnp.float32)
            o = o + jnp.sum(diag[:, hs], axis=-1, keepdims=True) * vh.astype(jnp.float32)
            st_ref[h] = st * tot[:, hs] + lax.dot_general(vh, km[:, hs], (((0,), (0,)), ((), ())),
                                                          preferred_element_type=jnp.float32)
            if final:
                o = o + oprev_ref[:, osl]
                y = o * lax.rsqrt(jnp.mean(o * o, axis=-1, keepdims=True) + EPS) * g_ref[:, osl]
                og = og_ref[:, osl].astype(jnp.float32)
                out_ref[:, osl] = (y * (og / (1.0 + jnp.exp(-og)))).astype(out_ref.dtype)
            else:
                out_ref[:, osl] = o


def hgrn_pass(z, lb, n_ctx_rows, n_lat_rows, reverse, o_prev=None, norm_g=None):
    rows = z.shape[0]
    n_ctx, n_lat = n_ctx_rows // HG_C, n_lat_rows // HG_C
    cb = HG_COL0 // MIX_W
    final = o_prev is not None
    rb = functools.partial(_row_block, n_ctx=n_ctx, n_lat=n_lat, reverse=reverse)

    def zspec(k):
        return pl.BlockSpec((HG_C, MIX_W), lambda b, j: (rb(b, j), cb + k))

    row_spec = pl.BlockSpec((HG_C, MIX_W), lambda b, j: (rb(b, j), 0))
    vec_spec = pl.BlockSpec((1, MIX_W), lambda b, j: (0, 0))
    in_specs = [pl.BlockSpec((HG_C, HG_C), lambda b, j: (0, 0)), vec_spec,
                zspec(0), zspec(2 if reverse else 1), zspec(3)]
    args = [_hg_level_matrix(reverse), lb.reshape(1, MIX_W), z, z, z]
    if final:
        in_specs += [zspec(4), row_spec, vec_spec]
        args += [z, o_prev, norm_g.reshape(1, MIX_W)]
    return pl.pallas_call(
        functools.partial(_hg_pass_kernel, reverse=reverse, final=final),
        grid=(2, n_ctx + n_lat),
        in_specs=in_specs,
        out_specs=row_spec,
        out_shape=jax.ShapeDtypeStruct((rows, MIX_W), jnp.bfloat16 if final else jnp.float32),
        scratch_shapes=[pltpu.VMEM((HG_HEADS, HG_DK, HG_DK), jnp.float32)],
        compiler_params=_cparams("parallel", "arbitrary"),
        name="hgrn_bwd" if reverse else "hgrn_fwd",
    )(*args)


def hgrn_branch(z, lb, norm_g, n_ctx_rows, n_lat_rows):
    o_f = hgrn_pass(z, lb, n_ctx_rows, n_lat_rows, False)
    return hgrn_pass(z, lb, n_ctx_rows, n_lat_rows, True, o_f, norm_g)


def _hy_conv_kernel(x_ref, w_ref, b_ref, out_ref, *, is_ctx):
    x = x_ref[...].astype(jnp.float32)
    t = lax.broadcasted_iota(jnp.int32, x.shape, 0)
    pos = t if is_ctx else t & (GRID_W - 1)
    last = SEQ_T - 1 if is_ctx else GRID_W - 1
    w = w_ref[...]
    y = b_ref[...] + w[1:2] * x
    y = y + w[0:1] * jnp.where(pos >= 1, pltpu.roll(x, 1, 0), 0.0)
    y = y + w[2:3] * jnp.where(pos < last, pltpu.roll(x, SEQ_T - 1, 0), 0.0)
    for g in range(3):
        out_ref[g] = y[:, g * MIX_W:(g + 1) * MIX_W].astype(out_ref.dtype)


def hy_conv(z, conv_w, conv_b, block0, n_rows, is_ctx):
    nb = n_rows // SEQ_T
    return pl.pallas_call(
        functools.partial(_hy_conv_kernel, is_ctx=is_ctx),
        grid=(2, nb),
        in_specs=[pl.BlockSpec((SEQ_T, 3 * MIX_W), lambda b, j: (block0 + b * nb + j, 0)),
                  pl.BlockSpec((3, 3 * MIX_W), lambda b, j: (0, 0)),
                  pl.BlockSpec((1, 3 * MIX_W), lambda b, j: (0, 0))],
        out_specs=pl.BlockSpec((3, None, SEQ_T, MIX_W), lambda b, j: (0, b, j, 0)),
        out_shape=jax.ShapeDtypeStruct((3, 2, n_rows, MIX_W), jnp.bfloat16),
        compiler_params=_cparams("parallel", "arbitrary"),
        name="hy_conv_ctx" if is_ctx else "hy_conv_lat",
    )(z, conv_w, conv_b.reshape(1, 3 * MIX_W))


def _hy_conv_lat_kernel(*refs):
    x_refs, (w_ref, b_ref, out_ref) = refs[:HY_PARTS], refs[HY_PARTS:]
    x = jnp.concatenate([r[...] for r in x_refs], axis=0).astype(jnp.float32)
    n = x.shape[0]
    pos = lax.broadcasted_iota(jnp.int32, x.shape, 0) & (GRID_W - 1)
    w = w_ref[...]
    y = b_ref[...] + w[1:2] * x
    y = y + w[0:1] * jnp.where(pos >= 1, pltpu.roll(x, 1, 0), 0.0)
    y = y + w[2:3] * jnp.where(pos < GRID_W - 1, pltpu.roll(x, n - 1, 0), 0.0)
    y = y.astype(out_ref.dtype).reshape(n // FFT_R, FFT_R, MIX_W)
    out_ref[...] = jnp.swapaxes(y, 0, 1)


def hy_conv_latent(z, conv_w, conv_b, row0, n_rows):
    part = FFT_SB * FFT_R // HY_PARTS
    nb = n_rows // (FFT_SB * FFT_R)

    def xspec(k):
        return pl.BlockSpec((part, MIX_W), lambda g, b, j: ((row0 + b * n_rows) // part + j * HY_PARTS + k, g))

    return pl.pallas_call(
        _hy_conv_lat_kernel,
        grid=(3, 2, nb),
        in_specs=[xspec(k) for k in range(HY_PARTS)] + [
            pl.BlockSpec((3, MIX_W), lambda g, b, j: (0, g)), pl.BlockSpec((1, MIX_W), lambda g, b, j: (0, g))],
        out_specs=pl.BlockSpec((None, None, FFT_R, FFT_SB, MIX_W), lambda g, b, j: (g, b, 0, j, 0)),
        out_shape=jax.ShapeDtypeStruct((3, 2, FFT_R, n_rows // FFT_R, MIX_W), jnp.bfloat16),
        compiler_params=_cparams("parallel", "parallel", "arbitrary"),
        name="hy_conv_lat",
    )(*([z] * HY_PARTS), conv_w, conv_b.reshape(1, 3 * MIX_W))


def _cis(num, den):
    ang = (2.0 * math.pi / den) * (num % den).astype(jnp.float32)
    return jnp.cos(ang), jnp.sin(ang)


def fft_matrices():
    r = FFT_R
    i = jnp.arange(r, dtype=jnp.int32)
    c, s = _cis(i[:, None] * i[None, :], r)
    half = r // 2
    fa_data = jnp.concatenate([jnp.concatenate([c[:, :half], s[:, :half]], axis=1),
                               jnp.concatenate([-s[:, :half], c[:, :half]], axis=1)], axis=0)
    fa_taps = jnp.concatenate([c, -s], axis=0)
    k1, k2, s2 = i[:, None, None], i[None, :, None], i[None, None, :]
    gc, gs = _cis(s2 * (r * k2 + k1), FFT_L)
    gr, gi = gc, -gs
    m1 = jnp.concatenate([jnp.concatenate([gr, -gi], axis=2), jnp.concatenate([gi, gr], axis=2)], axis=1)
    er, ei = c[:half] / FFT_L, s[:half] / FFT_L
    e2 = jnp.concatenate([jnp.concatenate([er, -ei], axis=1), jnp.concatenate([ei, er], axis=1)], axis=0)
    bf = jnp.bfloat16
    return dict(fa_data=fa_data.astype(bf), fa_taps=fa_taps.astype(bf), m1=m1.astype(bf), e2=e2.astype(bf))


def _fft_a_kernel(f_ref, x_ref, o_ref, *, two):
    f = f_ref[...]
    res = []
    for s in range(FFT_SB):
        x = jnp.concatenate([x_ref[0, s], x_ref[1, s]], axis=0) if two else x_ref[s]
        res.append(jnp.dot(f, x, preferred_element_type=jnp.float32).astype(o_ref.dtype))
    o_ref[...] = jnp.swapaxes(jnp.stack(res), 0, 1).reshape(o_ref.shape)


def fft_stage_a(fmat, x, g):
    two = x.ndim == 5
    xspec = (pl.BlockSpec((None, 2, FFT_SB, FFT_R // 2, MIX_W), lambda j: (g, 0, j, 0, 0)) if two
             else pl.BlockSpec((None, FFT_SB, FFT_R, MIX_W), lambda j: (g, j, 0, 0)))
    return pl.pallas_call(
        functools.partial(_fft_a_kernel, two=two),
        grid=(FFT_R // FFT_SB,),
        in_specs=[pl.BlockSpec((2 * FFT_R, FFT_R), lambda j: (0, 0)), xspec],
        out_specs=pl.BlockSpec((2, FFT_R, FFT_SB, MIX_W), lambda j: (0, 0, j, 0)),
        out_shape=jax.ShapeDtypeStruct((2, FFT_R, FFT_R, MIX_W), jnp.bfloat16),
        compiler_params=_cparams("parallel"),
        name="fft_stage_a",
    )(fmat, x)


def _fft_mid_kernel(m1_ref, a_ref, *rest, conv):
    if conv:
        h_ref, o_ref = rest
    else:
        s_ref, o_ref = rest
    r = FFT_R
    for k in range(FFT_KB):
        a = a_ref[:, k].reshape(2 * r, MIX_W)
        x = jnp.dot(m1_ref[k], a, preferred_element_type=jnp.float32)
        if conv:
            h = h_ref[k].astype(jnp.float32)
            xr, xi, hr, hi = x[:r], x[r:], h[:r], h[r:]
            zc = jnp.concatenate([xr * hr - xi * hi, xr * hi + xi * hr], axis=0).astype(jnp.bfloat16)
            p = lax.dot_general(m1_ref[k], zc, (((0,), (0,)), ((), ())), preferred_element_type=jnp.float32)
            o_ref[:, k] = p.reshape(2, r, MIX_W).astype(o_ref.dtype)
        else:
            o_ref[k] = (x * s_ref[...]).astype(o_ref.dtype)


def fft_mid(m1, a, h=None, scale=None):
    conv = h is not None
    r = FFT_R
    mspec = pl.BlockSpec((FFT_KB, 2 * r, 2 * r), lambda j: (j, 0, 0))
    aspec = pl.BlockSpec((2, FFT_KB, r, MIX_W), lambda j: (0, j, 0, 0))
    hspec = pl.BlockSpec((FFT_KB, 2 * r, MIX_W), lambda j: (j, 0, 0))
    if conv:
        in_specs, args = [mspec, aspec, hspec], (m1, a, h)
        out_specs, out_shape = aspec, jax.ShapeDtypeStruct((2, r, r, MIX_W), jnp.bfloat16)
    else:
        in_specs, args = [mspec, aspec, pl.BlockSpec((1, MIX_W), lambda j: (0, 0))], (m1, a, scale)
        out_specs, out_shape = hspec, jax.ShapeDtypeStruct((r, 2 * r, MIX_W), jnp.bfloat16)
    return pl.pallas_call(
        functools.partial(_fft_mid_kernel, conv=conv),
        grid=(r // FFT_KB,),
        in_specs=in_specs, out_specs=out_specs, out_shape=out_shape,
        compiler_params=_cparams("parallel"),
        name="fft_mid_conv" if conv else "fft_mid_filter",
    )(*args)


def _fft_out_kernel(e_ref, p_ref, mul_ref, add_ref, bias_ref, o_ref):
    e = e_ref[...]
    half = FFT_R // 2
    bias = bias_ref[...]
    p = jnp.swapaxes(p_ref[...].reshape(2 * FFT_R, FFT_SB, MIX_W), 0, 1)
    for s in range(FFT_SB):
        y = jnp.dot(e, p[s], preferred_element_type=jnp.float32)
        for b in range(2):
            yb = y[b * half:(b + 1) * half] + add_ref[b, s].astype(jnp.float32) * bias
            o_ref[b, s] = (mul_ref[b, s].astype(jnp.float32) * yb).astype(o_ref.dtype)


def fft_out(e2, p, mul, gm, add, ga, bias):
    half = FFT_R // 2

    def bspec(g):
        return pl.BlockSpec((None, 2, FFT_SB, half, MIX_W), lambda j: (g, 0, j, 0, 0))

    return pl.pallas_call(
        _fft_out_kernel,
        grid=(FFT_R // FFT_SB,),
        in_specs=[pl.BlockSpec((FFT_R, 2 * FFT_R), lambda j: (0, 0)),
                  pl.BlockSpec((2, FFT_R, FFT_SB, MIX_W), lambda j: (0, 0, j, 0)), bspec(gm), bspec(ga),
                  pl.BlockSpec((1, MIX_W), lambda j: (0, 0))],
        out_specs=bspec(0),
        out_shape=jax.ShapeDtypeStruct((1, 2, FFT_R, half, MIX_W), jnp.bfloat16),
        compiler_params=_cparams("parallel"),
        name="fft_out",
    )(e2, p, mul, add, bias)


def hyena_latent(xs, taps, inv_norm, bias, mats):
    u, gu = xs, 2
    for order in range(2):
        spec = fft_mid(mats['m1'], fft_stage_a(mats['fa_taps'], taps, order), scale=inv_norm[order])
        p = fft_mid(mats['m1'], fft_stage_a(mats['fa_data'], u, gu), h=spec)
        u, gu = fft_out(mats['e2'], p, xs, order, u, gu, bias[order].reshape(1, MIX_W)), 0
    return jnp.swapaxes(u[0], 1, 2).reshape(2, FFT_L // 2, MIX_W)


def hyena_context(xc, taps, inv_norm, bias):
    n = xc.shape[2]
    length = 2 * n
    k = jnp.arange(length, dtype=jnp.int32)
    c, s = _cis(k[:, None] * k[None, :], length)
    fr, fi = c[:, :n], -s[:, :n]
    m_fwd = jnp.concatenate([jnp.concatenate([fr, -fi], axis=1), jnp.concatenate([fi, fr], axis=1)], axis=0)
    m_taps = jnp.concatenate([c, -s], axis=0)
    er, ei = c[:n] / length, s[:n] / length
    m_inv = jnp.concatenate([jnp.concatenate([er, -ei], axis=1), jnp.concatenate([ei, er], axis=1)], axis=0)
    x1, x2, u = (xc[g].astype(jnp.float32) for g in range(3))
    for order, gate in enumerate((x1, x2)):
        h = mm(m_taps, taps[order], name="ctx_dft_taps") * inv_norm[order]
        w = mm(m_fwd, u.reshape(length, MIX_W), name="ctx_dft_fwd")
        wr, wi, hr, hi = w[:length], w[length:], h[:length], h[length:]
        zc = jnp.concatenate([wr * hr - wi * hi, wr * hi + wi * hr], axis=0)
        y = mm(m_inv, zc, name="ctx_dft_inv").reshape(2, n, MIX_W)
        u = gate * (y + u * bias[order])
    return u.astype(jnp.bfloat16)


def hyena_features(n, ff1, ff1_b, freq, ff2, ff2_b):
    k = jnp.arange(n, dtype=jnp.int32)
    pos = jnp.stack([k, (n - k) % n]).astype(jnp.float32)
    t = pos / (n - 1)
    fw = (2.0 * math.pi * pos / n)[..., None] * jnp.linspace(1e-4, HY_BANDS - 1, HY_BANDS, dtype=jnp.float32)
    feats = jnp.concatenate([t[..., None], jnp.cos(fw), -jnp.sin(fw)], axis=-1)
    h = jnp.sin(freq * (feats @ ff1 + ff1_b))
    return jnp.sin(freq * (h @ ff2 + ff2_b))


def _filter_kernel(feat_ref, w_ref, delta_ref, taps_ref, sum_ref, *, n, tm, split):
    dr = pl.program_id(1)
    i = pl.program_id(2)
    row = i * tm + lax.broadcasted_iota(jnp.int32, (tm, MIX_W), 0)
    lag = jnp.where(dr == 0, row, jnp.where(row == 0, 0, n - row))
    t = lag.astype(jnp.float32) / (n - 1)
    h = jnp.dot(feat_ref[...].astype(jnp.bfloat16), w_ref[...].astype(jnp.bfloat16),
                preferred_element_type=jnp.float32)
    h = h * jnp.exp(-t * delta_ref[...])

    @pl.when((dr == 0) & (i == 0))
    def _():
        sum_ref[...] = jnp.zeros_like(sum_ref)

    sum_ref[...] += jnp.sum(jnp.abs(h), axis=0, keepdims=True)
    taps = jnp.where((dr == 0) | (row != 0), h, 0.0).astype(taps_ref.dtype)
    if split:
        taps = jnp.swapaxes(taps.reshape(tm // FFT_R, FFT_R, MIX_W), 0, 1)
    taps_ref[...] = taps


def hyena_filter_taps(feats, ff3):
    _, n, nf = feats.shape
    split = 2 * n == FFT_L
    tm = FFT_SB * FFT_R if split else n
    nb = n // tm
    w = ff3.reshape(nf, 2, 2, MIX_W).transpose(1, 2, 0, 3)
    deltas = jnp.abs(jnp.linspace(math.log(HY_TARGET) / HY_SLOW_PCT, math.log(HY_TARGET) / HY_FAST_PCT,
                                  MIX_W, dtype=jnp.float32)).reshape(1, MIX_W)
    if split:
        tspec = pl.BlockSpec((None, FFT_R, FFT_SB, MIX_W), lambda o, dr, i: (o, 0, dr * nb + i, 0))
        tshape = jax.ShapeDtypeStruct((2, FFT_R, FFT_R, MIX_W), jnp.bfloat16)
    else:
        tspec = pl.BlockSpec((None, tm, MIX_W), lambda o, dr, i: (o, dr * nb + i, 0))
        tshape = jax.ShapeDtypeStruct((2, 2 * n, MIX_W), jnp.bfloat16)
    return pl.pallas_call(
        functools.partial(_filter_kernel, n=n, tm=tm, split=split),
        grid=(2, 2, nb),
        in_specs=[pl.BlockSpec((None, tm, nf), lambda o, dr, i: (dr, i, 0)),
                  pl.BlockSpec((None, None, nf, MIX_W), lambda o, dr, i: (o, dr, 0, 0)),
                  pl.BlockSpec((1, MIX_W), lambda o, dr, i: (0, 0))],
        out_specs=[tspec, pl.BlockSpec((None, 1, MIX_W), lambda o, dr, i: (o, 0, 0))],
        out_shape=[tshape, jax.ShapeDtypeStruct((2, 1, MIX_W), jnp.float32)],
        compiler_params=_cparams("arbitrary", "arbitrary", "arbitrary"),
        name="hyena_filter",
    )(feats, w, deltas)


def _merge_kernel(yh_ref, yl_ref, yg_ref, w_ref, g0_ref, g1_ref, g2_ref, o_ref):
    acc = None
    for y_ref, g_ref, j in ((yh_ref, g0_ref, 0), (yl_ref, g1_ref, 1), (yg_ref, g2_ref, 2)):
        gate = 1.0 / (1.0 + jnp.exp(-g_ref[...].astype(jnp.float32)))
        t = gate * jnp.dot(y_ref[...], w_ref[j], preferred_element_type=jnp.float32)
        acc = t if acc is None else acc + t
    o_ref[...] = acc.astype(o_ref.dtype)


def merge_branches(z, ys, w_branch, layer):
    rows = z.shape[0]
    tm = ROW_T // 2
    yspec = pl.BlockSpec((tm, MIX_W), lambda i: (i, 0))

    def gspec(k):
        return pl.BlockSpec((tm, D_MODEL), lambda i: (i, GATE_COL0 // D_MODEL + k))

    return pl.pallas_call(
        _merge_kernel,
        grid=(rows // tm,),
        in_specs=[yspec, yspec, yspec,
                  pl.BlockSpec((None, N_BRANCH, MIX_W, D_MODEL), lambda i: (layer, 0, 0, 0)),
                  gspec(0), gspec(1), gspec(2)],
        out_specs=pl.BlockSpec((tm, D_MODEL), lambda i: (i, 0)),
        out_shape=jax.ShapeDtypeStruct((rows, D_MODEL), jnp.bfloat16),
        compiler_params=_cparams("parallel"),
        name="merge",
    )(*ys, w_branch, z, z, z)


def _out_proj_kernel(a_ref, w_ref, x_ref, g1_ref, gt_ref, g2_ref, sh_ref, sc_ref, wr_ref, xo_ref, h_ref, lg_ref):
    y = jnp.dot(a_ref[...], w_ref[...], preferred_element_type=jnp.float32)
    y = y * lax.rsqrt(jnp.mean(y * y, axis=-1, keepdims=True) + EPS) * g1_ref[...]
    x = x_ref[...] + gt_ref[...] * y
    xo_ref[...] = x
    h = x * lax.rsqrt(jnp.mean(x * x, axis=-1, keepdims=True) + EPS) * g2_ref[...]
    h = (h * (1.0 + sc_ref[...]) + sh_ref[...]).astype(jnp.bfloat16)
    h_ref[...] = h
    lg_ref[...] = jnp.dot(h, wr_ref[...], preferred_element_type=jnp.float32)


def out_proj(acc, w_out, layer, x, g1, gt, g2, sh, sc, w_router, n_ctx_rows, n_lat_rows):
    rows, d = x.shape
    tm = ROW_T // 2
    idx = functools.partial(_mod_index, n_ctx_tiles=2 * n_ctx_rows // tm, tiles_per_batch=n_lat_rows // tm)
    row = pl.BlockSpec((tm, d), lambda i: (i, 0))
    vec = pl.BlockSpec((1, d), lambda i: (0, 0))
    mod = pl.BlockSpec((None, 1, d), lambda i: (idx(i), 0, 0))
    wr = jnp.zeros((d, ROUTER_PAD), jnp.bfloat16).at[:, :N_EXPERTS].set(w_router.astype(jnp.bfloat16))
    return pl.pallas_call(
        _out_proj_kernel,
        grid=(rows // tm,),
        in_specs=[row, pl.BlockSpec((None, d, d), lambda i: (layer, 0, 0)), row, vec, mod, vec, mod, mod,
                  pl.BlockSpec((d, ROUTER_PAD), lambda i: (0, 0))],
        out_specs=[row, row, pl.BlockSpec((tm, ROUTER_PAD), lambda i: (i, 0))],
        out_shape=[jax.ShapeDtypeStruct((rows, d), jnp.float32), jax.ShapeDtypeStruct((rows, d), jnp.bfloat16),
                   jax.ShapeDtypeStruct((rows, ROUTER_PAD), jnp.float32)],
        compiler_params=_cparams("parallel"),
        name="out_proj",
    )(acc, w_out, x, g1.reshape(1, d), gt, g2.reshape(1, d), sh, sc, wr)


def _ffn_up_kernel(x_ref, wg_ref, wu_ref, h_ref):
    x = x_ref[...]
    hg = jnp.dot(x, wg_ref[...].astype(jnp.bfloat16), preferred_element_type=jnp.float32)
    hu = jnp.dot(x, wu_ref[...].astype(jnp.bfloat16), preferred_element_type=jnp.float32)
    h_ref[...] = (hg / (1.0 + jnp.exp(-hg)) * hu).astype(h_ref.dtype)


def _ffn_down_kernel(h_ref, wd_ref, g_ref, o_ref):
    y = jnp.dot(h_ref[...], wd_ref[...].astype(jnp.bfloat16), preferred_element_type=jnp.float32)
    o_ref[...] = (y * g_ref[...]).astype(o_ref.dtype)


def expert_ffn(xe, w_gate, w_up, w_down, layer, g):
    ne, r, d = xe.shape
    dff = w_gate.shape[-1]
    hid = pl.pallas_call(
        _ffn_up_kernel,
        grid=(ne, dff // FFN_TF),
        in_specs=[pl.BlockSpec((None, r, d), lambda e, f: (e, 0, 0)),
                  pl.BlockSpec((None, None, d, FFN_TF), lambda e, f: (layer, e, 0, f)),
                  pl.BlockSpec((None, None, d, FFN_TF), lambda e, f: (layer, e, 0, f))],
        out_specs=pl.BlockSpec((None, r, FFN_TF), lambda e, f: (e, 0, f)),
        out_shape=jax.ShapeDtypeStruct((ne, r, dff), jnp.bfloat16),
        compiler_params=_cparams("parallel", "arbitrary"),
        name="ffn_up",
    )(xe, w_gate, w_up)
    return pl.pallas_call(
        _ffn_down_kernel,
        grid=(ne, d // FFN_TD),
        in_specs=[pl.BlockSpec((None, r, dff), lambda e, j: (e, 0, 0)),
                  pl.BlockSpec((None, None, dff, FFN_TD), lambda e, j: (layer, e, 0, j)),
                  pl.BlockSpec((None, r, 1), lambda e, j: (e, 0, 0))],
        out_specs=pl.BlockSpec((None, r, FFN_TD), lambda e, j: (e, 0, j)),
        out_shape=jax.ShapeDtypeStruct((ne, r, d), jnp.bfloat16),
        compiler_params=_cparams("parallel", "arbitrary"),
        name="ffn_down",
    )(hid, w_down, g)


def route(logits, row0, bsz, n):
    cap = EC_CAPACITY * n // N_EXPERTS
    aff = jax.nn.softmax(logits[row0:row0 + bsz * n, :N_EXPERTS].reshape(bsz, n, N_EXPERTS), axis=-1)
    g, idx = lax.top_k(jnp.swapaxes(aff, 1, 2), cap)
    flat = idx + (row0 + jnp.arange(bsz, dtype=idx.dtype) * n)[:, None, None]
    return (jnp.swapaxes(g, 0, 1).reshape(N_EXPERTS, bsz * cap),
            jnp.swapaxes(flat, 0, 1).reshape(N_EXPERTS, bsz * cap))


def ec_moe(h, logits, w_gate, w_up, w_down, layer, segments):
    gs, flats = zip(*(route(logits, *seg) for seg in segments))
    flat, g = lax.sort_key_val(jnp.concatenate(flats, axis=1), jnp.concatenate(gs, axis=1), dimension=1)
    return expert_ffn(h[flat], w_gate, w_up, w_down, layer, g[..., None]), flat


def moe_plan(flat, n_rows):
    ne, r = flat.shape
    nwin = r // MOE_WIN
    n_tiles = n_rows // MOE_TT
    bounds = jnp.arange(n_tiles + 1, dtype=flat.dtype) * MOE_TT
    p = jnp.sum(flat[None, :, :] < bounds[:, None, None], axis=2, dtype=jnp.int32)
    p0, p1 = p[:-1], p[1:]
    w0 = p0 // MOE_WIN
    nw = jnp.where(p1 > p0, (p1 - 1) // MOE_WIN - w0 + 1, 0)
    off = jnp.cumsum(nw, axis=1)
    q = jnp.arange(MOE_MAXQ, dtype=jnp.int32)
    e_of_q = jnp.minimum(jnp.sum(q[None, :, None] >= off[:, None, :], axis=2, dtype=jnp.int32), ne - 1)
    pick = e_of_q[:, :, None] == jnp.arange(ne, dtype=jnp.int32)
    w = q[None, :] + jnp.sum(jnp.where(pick, (w0 - (off - nw))[:, None, :], 0), axis=2)
    items = jnp.clip(e_of_q * nwin + w, 0, ne * nwin - 1)
    return items.astype(jnp.int32), off[:, -1].astype(jnp.int32)


def _combine_kernel(items_ref, count_ref, tok_ref, ye_ref, x_ref, g_ref, gt_ref, *rest, nxt, skip):
    if nxt:
        gn_ref, sh_ref, sc_ref, o_ref, u_ref, stage, sem, acc_ref = rest
    else:
        o_ref, stage, sem, acc_ref = rest
    i = pl.program_id(0)
    t = i + skip
    n = count_ref[t]
    n_groups = (n + MOE_GROUP - 1) // MOE_GROUP

    @pl.when(i == 0)
    def _():
        stage[...] = jnp.zeros_like(stage)

    def window_copy(g, buf, k):
        item = items_ref[t, jnp.minimum(MOE_GROUP * g + k, MOE_MAXQ - 1)]
        return pltpu.make_async_copy(ye_ref.at[pl.ds(item * MOE_WIN, MOE_WIN)],
                                     stage.at[buf, pl.ds(k * MOE_WIN, MOE_WIN)], sem.at[buf])

    def for_group(g, buf, wait):
        for k in range(MOE_GROUP):
            @pl.when(MOE_GROUP * g + k < n)
            def _():
                cp = window_copy(g, buf, k)
                cp.wait() if wait else cp.start()

    acc_ref[...] = jnp.zeros_like(acc_ref)

    for j in range(MOE_NBUF - 1):
        @pl.when(j < n_groups)
        def _():
            for_group(j, j, False)

    row = t * MOE_TT + lax.broadcasted_iota(jnp.int32, (MOE_TT, MOE_GROUP * MOE_WIN), 0)

    def body(g, carry):
        buf = g % MOE_NBUF
        ahead = g + MOE_NBUF - 1

        @pl.when(ahead < n_groups)
        def _():
            for_group(ahead, ahead % MOE_NBUF, False)

        for_group(g, buf, True)
        toks = []
        for k in range(MOE_GROUP):
            qk = MOE_GROUP * g + k
            item = items_ref[t, jnp.minimum(qk, MOE_MAXQ - 1)]
            toks.append(jnp.where(qk < n, tok_ref[pl.ds(item, 1), :], -1))
        onehot = jnp.where(row == jnp.concatenate(toks, axis=1), 1.0, 0.0).astype(jnp.bfloat16)
        acc_ref[...] += jnp.dot(onehot, stage[buf], preferred_element_type=jnp.float32)
        return carry

    lax.fori_loop(0, n_groups, body, 0)

    y = acc_ref[...]
    y = y * lax.rsqrt(jnp.mean(y * y, axis=-1, keepdims=True) + EPS) * g_ref[...]
    x = x_ref[...] + gt_ref[...] * y
    if nxt:
        u = x * lax.rsqrt(jnp.mean(x * x, axis=-1, keepdims=True) + EPS) * gn_ref[...]
        u_ref[...] = (u * (1.0 + sc_ref[...]) + sh_ref[...]).astype(u_ref.dtype)
    o_ref[...] = x


def moe_combine(x, ye, flat, g, gt, n_ctx_rows, n_lat_rows, nxt=None):
    rows, d = x.shape
    ne, r, _ = ye.shape
    items, count = moe_plan(flat, rows)
    n_ctx_tiles = 2 * n_ctx_rows // MOE_TT
    idx = functools.partial(_mod_index, n_ctx_tiles=n_ctx_tiles, tiles_per_batch=n_lat_rows // MOE_TT)
    skip = 0 if nxt else n_ctx_tiles
    row = pl.BlockSpec((MOE_TT, d), lambda i, *_: (i + skip, 0))
    vec = pl.BlockSpec((1, d), lambda i, *_: (0, 0))
    mod = pl.BlockSpec((None, 1, d), lambda i, *_: (idx(i + skip), 0, 0))
    out_row = pl.BlockSpec((MOE_TT, d), lambda i, *_: (i, 0))
    n_out = rows - skip * MOE_TT
    in_specs = [pl.BlockSpec((ne * r // MOE_WIN, MOE_WIN), lambda i, *_: (0, 0)),
                pl.BlockSpec(memory_space=pl.ANY), row, vec, mod]
    args = [flat.reshape(ne * r // MOE_WIN, MOE_WIN), ye.reshape(ne * r, d), x, g.reshape(1, d), gt]
    out_specs, out_shape = out_row, jax.ShapeDtypeStruct((n_out, d), jnp.float32)
    if nxt:
        in_specs += [vec, mod, mod]
        args += [nxt[0].reshape(1, d), nxt[1], nxt[2]]
        out_specs = [out_row, out_row]
        out_shape = [out_shape, jax.ShapeDtypeStruct((n_out, d), jnp.bfloat16)]
    return pl.pallas_call(
        functools.partial(_combine_kernel, nxt=bool(nxt), skip=skip),
        grid_spec=pltpu.PrefetchScalarGridSpec(
            num_scalar_prefetch=2,
            grid=(n_out // MOE_TT,),
            in_specs=in_specs, out_specs=out_specs,
            scratch_shapes=[pltpu.VMEM((MOE_NBUF, MOE_GROUP * MOE_WIN, d), jnp.bfloat16),
                            pltpu.SemaphoreType.DMA((MOE_NBUF,)),
                            pltpu.VMEM((MOE_TT, d), jnp.float32)]),
        out_shape=out_shape,
        compiler_params=_cparams("arbitrary"),
        name="moe_combine",
    )(items, count, *args)


def kernel(x, c, ctx, c_ctx, w_mod, b_mod, norm_g, w_in, hy_conv_w, hy_conv_b, hy_ff1, hy_ff1_b, hy_freq,
           hy_ff2, hy_ff2_b, hy_ff3, hy_bias, lru_conv_w, lru_conv_b, lru_wa, lru_ba, lru_wi, lru_bi,
           lru_lambda, hg_lb_logits, hg_norm_g, w_branch, w_out, w_router, w_gate, w_up, w_down):
    bsz, n_lat, d = x.shape
    n_ctx = ctx.shape[1]
    assert bsz == 2 and d == D_MODEL and n_ctx == SEQ_T and 2 * n_lat == FFT_L and n_lat % ROW_T == 0
    gam = jax.nn.softmax(hg_lb_logits.astype(jnp.float32), axis=0)
    lb_all = jnp.maximum(jnp.cumsum(gam, axis=0) - gam[:1], 0.0)
    mats = fft_matrices()
    cvec = jnp.zeros((SUBLANES, d), jnp.float32).at[:bsz].set(jax.nn.silu(c)).at[bsz].set(jax.nn.silu(c_ctx))
    n_ctx_all = bsz * n_ctx
    xa = jnp.concatenate([ctx.reshape(n_ctx_all, d), x.reshape(bsz * n_lat, d)], axis=0)
    w_branch_b, w_out_b = w_branch.astype(jnp.bfloat16), w_out.astype(jnp.bfloat16)
    mods = []
    for l in range(DEPTH):
        mod = mm(cvec, w_mod, layer=l, name="adaln")[:bsz + 1] + b_mod[l]
        mods.append([mod.reshape(bsz + 1, 1, 6, d)[:, :, k] for k in range(6)])
    u = prenorm(xa, norm_g[0, 0], mods[0][0], mods[0][1], n_ctx, n_lat)
    for l in range(DEPTH):
        last = l == DEPTH - 1
        p = {'lru_conv_w': lru_conv_w[l], 'lru_conv_b': lru_conv_b[l], 'lru_wa': lru_wa[l], 'lru_ba': lru_ba[l],
             'lru_wi': lru_wi[l], 'lru_bi': lru_bi[l], 'lru_lambda': lru_lambda[l]}
        sh1, sc1, gt1, sh2, sc2, gt2 = mods[l]

        z = mm(u, w_in, jnp.bfloat16, tm=IN_PROJ_TM, tn=1024, layer=l, name="in_proj")

        y_lru = lru_branch(z, p, n_ctx, n_lat)
        y_hg = hgrn_branch(z, lb_all[l], hg_norm_g[l], n_ctx, n_lat)

        ffp = (hy_ff1[l], hy_ff1_b[l], hy_freq[l], hy_ff2[l], hy_ff2_b[l])
        taps_l, norm_l = hyena_filter_taps(hyena_features(n_lat, *ffp), hy_ff3[l])
        taps_c, norm_c = hyena_filter_taps(hyena_features(n_ctx, *ffp), hy_ff3[l])
        xc_l = hy_conv_latent(z, hy_conv_w[l], hy_conv_b[l], n_ctx_all, n_lat)
        y_hy_l = hyena_latent(xc_l, taps_l, 1.0 / norm_l, hy_bias[l], mats)
        xc_c = hy_conv(z, hy_conv_w[l], hy_conv_b[l], 0, n_ctx, True)
        y_hy_c = hyena_context(xc_c, taps_c, 1.0 / norm_c, hy_bias[l])
        y_hy = jnp.concatenate([y_hy_c.reshape(n_ctx_all, MIX_W), y_hy_l.reshape(bsz * n_lat, MIX_W)], axis=0)

        acc = merge_branches(z, (y_hy, y_lru, y_hg), w_branch_b, l)
        xa, h, logits = out_proj(acc, w_out_b, l, xa, norm_g[l, 1], gt1, norm_g[l, 2], sh2, sc2,
                                 w_router[l], n_ctx, n_lat)
        segments = [(n_ctx_all, bsz, n_lat)] if last else [(n_ctx_all, bsz, n_lat), (0, bsz, n_ctx)]
        ye, flat = ec_moe(h, logits, w_gate, w_up, w_down, l, segments)
        if last:
            xa = moe_combine(xa, ye, flat, norm_g[l, 3], gt2, n_ctx, n_lat)
        else:
            nxt = (norm_g[l + 1, 0], mods[l + 1][0], mods[l + 1][1])
            xa, u = moe_combine(xa, ye, flat, norm_g[l, 3], gt2, n_ctx, n_lat, nxt)
    return xa.reshape(bsz, n_lat, d)
```

```python
import functools
import math

import numpy as np
import jax
import jax.numpy as jnp
from jax import lax
from jax.experimental import pallas as pl
from jax.experimental.pallas import tpu as pltpu

D_MODEL = 2048
DEPTH = 2
GRID_W = 64
MIX_W = D_MODEL // 2
N_BRANCH = 3
HY_BANDS = 16
HY_TARGET = 1e-2
HY_FAST_PCT = 0.3
HY_SLOW_PCT = 1.5
LRU_BLOCKS = 8
LRU_BS = MIX_W // LRU_BLOCKS
LRU_C = 8.0
HG_HEADS = 8
HG_DK = MIX_W // HG_HEADS
N_EXPERTS = 16
EC_CAPACITY = 2
EPS = 1e-6
TINY = 1e-30
LOG2E = 1.4426950408889634
HY_COL0 = 0
LRU_COL0 = 3 * MIX_W
HG_COL0 = 5 * MIX_W
GATE_COL0 = 10 * MIX_W
IN_COLS = GATE_COL0 + N_BRANCH * D_MODEL

SUBLANES = 8
LANES = 128
SEQ_T = 256
HG_C = 128
HG_LEVELS = 7
HG_HB = 8
ROW_T = 512
IN_PROJ_TM = 1536
FFT_R = 128
FFT_L = FFT_R * FFT_R
FFT_KB = 8
FFT_SB = 16
HY_PARTS = 4
ROUTER_PAD = LANES
FFN_TF = 256
FFN_TD = 512
MOE_TT = 512
MOE_WIN = 64
MOE_GROUP = 8
MOE_NBUF = 3
MOE_MAXQ = N_EXPERTS * (MOE_TT // MOE_WIN + 1)
VMEM_LIMIT_BYTES = 48 * 1024 * 1024
FFT_OUT_VMEM_BYTES = 56 * 1024 * 1024


def _cparams(*sem, vmem=VMEM_LIMIT_BYTES):
    return pltpu.CompilerParams(dimension_semantics=sem, vmem_limit_bytes=vmem)


def _pick_tile(n, pref):
    for t in (pref, 1024, 512, 256, 128):
        if t <= n and n % t == 0:
            return t
    return n


def _row_block(b, j, n_ctx, n_lat, reverse):
    if reverse:
        kc, kl = n_ctx - 1 - j, n_ctx + n_lat - 1 - j
    else:
        kc, kl = j, j - n_ctx
    return jnp.where(j < n_ctx, b * n_ctx + kc, 2 * n_ctx + b * n_lat + kl)


def _mm_kernel(a_ref, b_ref, o_ref):
    o_ref[...] = jnp.dot(a_ref[...].astype(jnp.bfloat16), b_ref[...].astype(jnp.bfloat16),
                         preferred_element_type=jnp.float32).astype(o_ref.dtype)


def mm(a, b, out_dtype=jnp.float32, tm=512, tn=1024, layer=None, name="mm"):
    m, k = a.shape
    n = b.shape[-1]
    tm = _pick_tile(m, tm)
    tn = _pick_tile(n, tn)
    if layer is None:
        bspec = pl.BlockSpec((k, tn), lambda i, j: (0, j))
    else:
        bspec = pl.BlockSpec((None, k, tn), lambda i, j: (layer, 0, j))
    return pl.pallas_call(
        _mm_kernel,
        grid=(m // tm, n // tn),
        in_specs=[pl.BlockSpec((tm, k), lambda i, j: (i, 0)), bspec],
        out_specs=pl.BlockSpec((tm, tn), lambda i, j: (i, j)),
        out_shape=jax.ShapeDtypeStruct((m, n), out_dtype),
        compiler_params=_cparams("parallel", "arbitrary"),
        name=name,
    )(a, b)


def _mod_index(i, n_ctx_tiles, tiles_per_batch):
    return jnp.where(i < n_ctx_tiles, 2, (i - n_ctx_tiles) // tiles_per_batch)


def _split_row_specs(tile, width, n_ctx_tiles):
    return (pl.BlockSpec((tile, width), lambda i: (jnp.minimum(i, n_ctx_tiles - 1), 0)),
            pl.BlockSpec((tile, width), lambda i: (jnp.maximum(i - n_ctx_tiles, 0), 0)))


def _pick_rows(ctx_ref, lat_ref, n_ctx_tiles):
    return jnp.where(pl.program_id(0) < n_ctx_tiles, ctx_ref[...], lat_ref[...])


def _row_operand(x, tile):
    if isinstance(x, tuple):
        d = x[0].shape[1]
        return list(_split_row_specs(tile, d, x[0].shape[0] // tile)), list(x), x[0].shape[0] + x[1].shape[0], d
    return [pl.BlockSpec((tile, x.shape[1]), lambda i: (i, 0))], [x], x.shape[0], x.shape[1]


def _prenorm_kernel(*refs, n_ctx_tiles):
    *x_refs, g_ref, sh_ref, sc_ref, u_ref = refs
    x = _pick_rows(*x_refs, n_ctx_tiles) if len(x_refs) == 2 else x_refs[0][...]
    y = x * lax.rsqrt(jnp.mean(x * x, axis=-1, keepdims=True) + EPS) * g_ref[...]
    u_ref[...] = (y * (1.0 + sc_ref[...]) + sh_ref[...]).astype(u_ref.dtype)


def prenorm(x, g, sh, sc, n_ctx_rows, n_lat_rows):
    xspecs, xs, rows, d = _row_operand(x, ROW_T)
    n_ctx_tiles = 2 * n_ctx_rows // ROW_T
    idx = functools.partial(_mod_index, n_ctx_tiles=n_ctx_tiles, tiles_per_batch=n_lat_rows // ROW_T)
    vec = pl.BlockSpec((None, 1, d), lambda i: (idx(i), 0, 0))
    return pl.pallas_call(
        functools.partial(_prenorm_kernel, n_ctx_tiles=n_ctx_tiles),
        grid=(rows // ROW_T,),
        in_specs=[*xspecs, pl.BlockSpec((1, d), lambda i: (0, 0)), vec, vec],
        out_specs=pl.BlockSpec((ROW_T, d), lambda i: (i, 0)),
        out_shape=jax.ShapeDtypeStruct((rows, d), jnp.bfloat16),
        compiler_params=_cparams("parallel"),
        name="prenorm",
    )(*xs, g.reshape(1, d), sh, sc)


def _gelu_tanh(x):
    return 0.5 * x * (1.0 + jnp.tanh(math.sqrt(2.0 / math.pi) * (x + 0.044715 * (x * x * x))))


def _conv_pos(is_ctx, shape):
    t = lax.broadcasted_iota(jnp.int32, shape, 0)
    pos = jnp.where(is_ctx, t, t & (GRID_W - 1))
    last = jnp.where(is_ctx, SEQ_T - 1, GRID_W - 1)
    return pos, last


def _lru_pass_kernel(x_ref, cw_ref, cb_ref, w_ref, gb_ref, lam_ref, *rest, reverse, final):
    if final:
        lg_ref, hprev_ref, out_ref, h_ref, a_ref, b_ref = rest
    else:
        out_ref, h_ref, a_ref, b_ref = rest
    j = pl.program_id(1)

    @pl.when(j == 0)
    def _():
        h_ref[...] = jnp.zeros_like(h_ref)

    x = x_ref[...].astype(jnp.float32)
    pos, last = _conv_pos(j == 0, x.shape)
    cw = cw_ref[...]
    xc = cb_ref[...] + cw[2:3] * x
    xc = xc + cw[0:1] * jnp.where(pos >= 2, pltpu.roll(x, 2, 0), 0.0)
    xc = xc + cw[1:2] * jnp.where(pos >= 1, pltpu.roll(x, 1, 0), 0.0)
    xc = xc + cw[3:4] * jnp.where(pos < last, pltpu.roll(x, SEQ_T - 1, 0), 0.0)

    xb = xc.astype(jnp.bfloat16)
    lam = lam_ref[...]
    sp = jnp.maximum(-lam, 0.0) + jnp.log(1.0 + jnp.exp(-jnp.abs(lam)))
    n_tiles = SEQ_T // SUBLANES
    row = lax.broadcasted_iota(jnp.int32, (1, SUBLANES, LRU_BS), 1)
    for n in range(LRU_BLOCKS):
        sl = slice(n * LRU_BS, (n + 1) * LRU_BS)
        pre = jnp.dot(xb[:, sl], w_ref[n], preferred_element_type=jnp.float32) + gb_ref[n]
        gate = 1.0 / (1.0 + jnp.exp(-pre))
        r, i = gate[:, :LRU_BS], gate[:, LRU_BS:]
        a = jnp.exp(-LRU_C * r * sp[:, sl])
        om = 1.0 - a * a
        b = om * lax.rsqrt(jnp.maximum(om, TINY)) * (i * xc[:, sl])
        a = a.reshape(n_tiles, SUBLANES, LRU_BS)
        b = b.reshape(n_tiles, SUBLANES, LRU_BS)
        for d in (1, 2, 4):
            if reverse:
                m = row < SUBLANES - d
                a_sh, b_sh = pltpu.roll(a, SUBLANES - d, 1), pltpu.roll(b, SUBLANES - d, 1)
            else:
                m = row >= d
                a_sh, b_sh = pltpu.roll(a, d, 1), pltpu.roll(b, d, 1)
            b = jnp.where(m, a * b_sh + b, b)
            a = jnp.where(m, a * a_sh, a)
        a_ref[:, sl] = a.reshape(SEQ_T, LRU_BS)
        b_ref[:, sl] = b.reshape(SEQ_T, LRU_BS)

    h = h_ref[...]
    for k in range(n_tiles):
        kk = n_tiles - 1 - k if reverse else k
        rs = slice(kk * SUBLANES, (kk + 1) * SUBLANES)
        ht = a_ref[rs, :] * h + b_ref[rs, :]
        h = ht[0:1, :] if reverse else ht[SUBLANES - 1:SUBLANES, :]
        if final:
            ht = (ht + hprev_ref[rs, :]) * _gelu_tanh(lg_ref[rs, :].astype(jnp.float32))
        out_ref[rs, :] = ht.astype(out_ref.dtype)
    h_ref[...] = h


def lru_pass(z, conv_w, conv_b, wa, ba, wi, bi, lam, n_ctx_rows, n_lat_rows, reverse, h_prev=None):
    rows = z.shape[0]
    n_lat = n_lat_rows // SEQ_T
    cb = LRU_COL0 // MIX_W
    final = h_prev is not None
    rb = functools.partial(_row_block, n_ctx=n_ctx_rows // SEQ_T, n_lat=n_lat, reverse=reverse)

    def zspec(k):
        return pl.BlockSpec((SEQ_T, MIX_W), lambda b, j: (rb(b, j), cb + k))

    row_spec = pl.BlockSpec((SEQ_T, MIX_W), lambda b, j: (rb(b, j), 0))

    def const_spec(shape):
        return pl.BlockSpec(shape, lambda b, j: (0,) * len(shape))

    w2 = jnp.concatenate([wa, wi], axis=-1).astype(jnp.bfloat16)
    gb = jnp.concatenate([ba.reshape(LRU_BLOCKS, 1, LRU_BS), bi.reshape(LRU_BLOCKS, 1, LRU_BS)], axis=-1)
    in_specs = [zspec(0), const_spec((4, MIX_W)), const_spec((1, MIX_W)),
                const_spec((LRU_BLOCKS, LRU_BS, 2 * LRU_BS)), const_spec((LRU_BLOCKS, 1, 2 * LRU_BS)),
                const_spec((1, MIX_W))]
    args = [z, conv_w, conv_b.reshape(1, MIX_W), w2, gb, lam.reshape(1, MIX_W)]
    if final:
        in_specs += [zspec(1), row_spec]
        args += [z, h_prev]
    return pl.pallas_call(
        functools.partial(_lru_pass_kernel, reverse=reverse, final=final),
        grid=(2, n_ctx_rows // SEQ_T + n_lat),
        in_specs=in_specs,
        out_specs=row_spec,
        out_shape=jax.ShapeDtypeStruct((rows, MIX_W), jnp.bfloat16 if final else jnp.float32),
        scratch_shapes=[pltpu.VMEM((1, MIX_W), jnp.float32),
                        pltpu.VMEM((SEQ_T, MIX_W), jnp.float32),
                        pltpu.VMEM((SEQ_T, MIX_W), jnp.float32)],
        compiler_params=_cparams("parallel", "arbitrary"),
        name="lru_bwd" if reverse else "lru_fwd",
    )(*args)


def lru_branch(z, p, n_ctx_rows, n_lat_rows):
    def one(d, h_prev):
        return lru_pass(z, p['lru_conv_w'], p['lru_conv_b'], p['lru_wa'][d], p['lru_ba'][d], p['lru_wi'][d],
                        p['lru_bi'][d], p['lru_lambda'][d], n_ctx_rows, n_lat_rows, d == 1, h_prev)
    return one(1, one(0, None))


def _hg_level_matrix(reverse):
    t = np.arange(HG_C)[:, None]
    s = np.arange(HG_C)[None, :]
    x = t ^ s
    lv = np.where(x > 0, np.floor(np.log2(np.maximum(x, 1))).astype(np.int32), -1)
    lv = np.where(s < t, lv, -1).astype(np.int32)
    return jnp.asarray(lv.T if reverse else lv)


def _hg_pass_kernel(lv_ref, lb_ref, q_ref, f_ref, v_ref, *rest, reverse, final):
    if final:
        og_ref, oprev_ref, g_ref, out_ref, st_ref = rest
    else:
        out_ref, st_ref = rest

    @pl.when(pl.program_id(1) == 0)
    def _():
        st_ref[...] = jnp.zeros_like(st_ref)

    lv = lv_ref[...]
    at_level = [lv == lvl for lvl in range(HG_LEVELS)]
    n_tiles = HG_C // SUBLANES
    wb = HG_HB * HG_DK
    row = lax.broadcasted_iota(jnp.int32, (1, SUBLANES, wb), 1)
    for hb in range(HG_HEADS // HG_HB):
        sl = slice(hb * wb, (hb + 1) * wb)
        heads = [(hb * HG_HB + j, slice(j * HG_DK, (j + 1) * HG_DK)) for j in range(HG_HB)]
        x = f_ref[:, sl].astype(jnp.float32)
        lb = lb_ref[:, sl]
        e = jnp.exp(-jnp.abs(x))
        inv = 1.0 / (1.0 + e)
        pos = x >= 0
        sig = jnp.where(pos, inv, e * inv)
        sigm = jnp.where(pos, e * inv, inv)
        lsig = jnp.minimum(x, 0.0) - jnp.log(1.0 + e)
        lf = jnp.where(lb > 0, jnp.log(lb + (1.0 - lb) * sig), lsig)
        kk = (1.0 - lb) * sigm
        q = q_ref[:, sl].astype(jnp.float32)
        v = v_ref[:, sl]

        c = (lf * LOG2E).reshape(n_tiles, SUBLANES, wb)
        bt = c
        att = [None] * HG_HB
        for lvl in range(HG_LEVELS):
            m = 1 << lvl
            qm = (q * jnp.exp2(c).reshape(HG_C, wb)).astype(jnp.bfloat16)
            km = (kk * jnp.exp2(bt - c).reshape(HG_C, wb)).astype(jnp.bfloat16)
            for j, (_, hs) in enumerate(heads):
                d = lax.dot_general(qm[:, hs], km[:, hs], (((1,), (1,)), ((), ())),
                                    preferred_element_type=jnp.float32)
                att[j] = jnp.where(at_level[lvl], d, 0.0 if lvl == 0 else att[j])
            if m < SUBLANES:
                upper = (row & m) != 0
                down = pltpu.roll(bt, m, 1)
                up = pltpu.roll(bt, SUBLANES - m, 1)
                if reverse:
                    c = c + jnp.where(upper, 0.0, up)
                else:
                    c = c + jnp.where(upper, down, 0.0)
                bt = bt + jnp.where(upper, down, up)
            else:
                k = m // SUBLANES
                pair = (n_tiles // (2 * k), 2, k, SUBLANES, wb)
                c5, b5 = c.reshape(pair), bt.reshape(pair)
                tot2 = b5[:, 0] + b5[:, 1]
                if reverse:
                    c = jnp.stack([c5[:, 0] + b5[:, 1], c5[:, 1]], axis=1)
                else:
                    c = jnp.stack([c5[:, 0], c5[:, 1] + b5[:, 0]], axis=1)
                c = c.reshape(n_tiles, SUBLANES, wb)
                bt = jnp.stack([tot2, tot2], axis=1).reshape(n_tiles, SUBLANES, wb)

        qm = (q * jnp.exp2(c).reshape(HG_C, wb)).astype(jnp.bfloat16)
        km = (kk * jnp.exp2(bt - c).reshape(HG_C, wb)).astype(jnp.bfloat16)
        tot = jnp.exp2(bt[0, 0:1, :])
        diag = q * kk
        for j, (h, hs) in enumerate(heads):
            osl = slice(h * HG_DK, (h + 1) * HG_DK)
            st = st_ref[h]
            vh = v[:, hs]
            o = lax.dot_general(qm[:, hs], st.astype(jnp.bfloat16), (((1,), (1,)), ((), ())),
                                preferred_element_type=jnp.float32)
            o = o + jnp.dot(att[j].astype(jnp.bfloat16), vh, preferred_element_type=jnp.float32)
            o = o + jnp.sum(diag[:, hs], axis=-1, keepdims=True) * vh.astype(jnp.float32)
            st_ref[h] = st * tot[:, hs] + lax.dot_general(vh, km[:, hs], (((0,), (0,)), ((), ())),
                                                          preferred_element_type=jnp.float32)
            if final:
                o = o + oprev_ref[:, osl]
                y = o * lax.rsqrt(jnp.mean(o * o, axis=-1, keepdims=True) + EPS) * g_ref[:, osl]
                og = og_ref[:, osl].astype(jnp.float32)
                out_ref[:, osl] = (y * (og / (1.0 + jnp.exp(-og)))).astype(out_ref.dtype)
            else:
                out_ref[:, osl] = o


def hgrn_pass(z, lb, n_ctx_rows, n_lat_rows, reverse, o_prev=None, norm_g=None):
    rows = z.shape[0]
    n_ctx, n_lat = n_ctx_rows // HG_C, n_lat_rows // HG_C
    cb = HG_COL0 // MIX_W
    final = o_prev is not None
    rb = functools.partial(_row_block, n_ctx=n_ctx, n_lat=n_lat, reverse=reverse)

    def zspec(k):
        return pl.BlockSpec((HG_C, MIX_W), lambda b, j: (rb(b, j), cb + k))

    row_spec = pl.BlockSpec((HG_C, MIX_W), lambda b, j: (rb(b, j), 0))
    vec_spec = pl.BlockSpec((1, MIX_W), lambda b, j: (0, 0))
    in_specs = [pl.BlockSpec((HG_C, HG_C), lambda b, j: (0, 0)), vec_spec,
                zspec(0), zspec(2 if reverse else 1), zspec(3)]
    args = [_hg_level_matrix(reverse), lb.reshape(1, MIX_W), z, z, z]
    if final:
        in_specs += [zspec(4), row_spec, vec_spec]
        args += [z, o_prev, norm_g.reshape(1, MIX_W)]
    return pl.pallas_call(
        functools.partial(_hg_pass_kernel, reverse=reverse, final=final),
        grid=(2, n_ctx + n_lat),
        in_specs=in_specs,
        out_specs=row_spec,
        out_shape=jax.ShapeDtypeStruct((rows, MIX_W), jnp.bfloat16 if final else jnp.float32),
        scratch_shapes=[pltpu.VMEM((HG_HEADS, HG_DK, HG_DK), jnp.float32)],
        compiler_params=_cparams("parallel", "arbitrary"),
        name="hgrn_bwd" if reverse else "hgrn_fwd",
    )(*args)


def hgrn_branch(z, lb, norm_g, n_ctx_rows, n_lat_rows):
    o_f = hgrn_pass(z, lb, n_ctx_rows, n_lat_rows, False)
    return hgrn_pass(z, lb, n_ctx_rows, n_lat_rows, True, o_f, norm_g)


def _hy_conv_kernel(x_ref, w_ref, b_ref, out_ref, *, is_ctx):
    x = x_ref[...].astype(jnp.float32)
    t = lax.broadcasted_iota(jnp.int32, x.shape, 0)
    pos = t if is_ctx else t & (GRID_W - 1)
    last = SEQ_T - 1 if is_ctx else GRID_W - 1
    w = w_ref[...]
    y = b_ref[...] + w[1:2] * x
    y = y + w[0:1] * jnp.where(pos >= 1, pltpu.roll(x, 1, 0), 0.0)
    y = y + w[2:3] * jnp.where(pos < last, pltpu.roll(x, SEQ_T - 1, 0), 0.0)
    for g in range(3):
        out_ref[g] = y[:, g * MIX_W:(g + 1) * MIX_W].astype(out_ref.dtype)


def hy_conv(z, conv_w, conv_b, block0, n_rows, is_ctx):
    nb = n_rows // SEQ_T
    return pl.pallas_call(
        functools.partial(_hy_conv_kernel, is_ctx=is_ctx),
        grid=(2, nb),
        in_specs=[pl.BlockSpec((SEQ_T, 3 * MIX_W), lambda b, j: (block0 + b * nb + j, 0)),
                  pl.BlockSpec((3, 3 * MIX_W), lambda b, j: (0, 0)),
                  pl.BlockSpec((1, 3 * MIX_W), lambda b, j: (0, 0))],
        out_specs=pl.BlockSpec((3, None, SEQ_T, MIX_W), lambda b, j: (0, b, j, 0)),
        out_shape=jax.ShapeDtypeStruct((3, 2, n_rows, MIX_W), jnp.bfloat16),
        compiler_params=_cparams("parallel", "arbitrary"),
        name="hy_conv_ctx" if is_ctx else "hy_conv_lat",
    )(z, conv_w, conv_b.reshape(1, 3 * MIX_W))


def _hy_conv_lat_kernel(*refs):
    x_refs, (w_ref, b_ref, out_ref) = refs[:HY_PARTS], refs[HY_PARTS:]
    x = jnp.concatenate([r[...] for r in x_refs], axis=0).astype(jnp.float32)
    n = x.shape[0]
    pos = lax.broadcasted_iota(jnp.int32, x.shape, 0) & (GRID_W - 1)
    w = w_ref[...]
    y = b_ref[...] + w[1:2] * x
    y = y + w[0:1] * jnp.where(pos >= 1, pltpu.roll(x, 1, 0), 0.0)
    y = y + w[2:3] * jnp.where(pos < GRID_W - 1, pltpu.roll(x, n - 1, 0), 0.0)
    y = y.astype(out_ref.dtype).reshape(n // FFT_R, FFT_R, MIX_W)
    out_ref[...] = jnp.swapaxes(y, 0, 1)


def hy_conv_latent(z, conv_w, conv_b, row0, n_rows):
    part = FFT_SB * FFT_R // HY_PARTS
    nb = n_rows // (FFT_SB * FFT_R)

    def xspec(k):
        return pl.BlockSpec((part, MIX_W), lambda g, b, j: ((row0 + b * n_rows) // part + j * HY_PARTS + k, g))

    return pl.pallas_call(
        _hy_conv_lat_kernel,
        grid=(3, 2, nb),
        in_specs=[xspec(k) for k in range(HY_PARTS)] + [
            pl.BlockSpec((3, MIX_W), lambda g, b, j: (0, g)), pl.BlockSpec((1, MIX_W), lambda g, b, j: (0, g))],
        out_specs=pl.BlockSpec((None, None, FFT_R, FFT_SB, MIX_W), lambda g, b, j: (g, b, 0, j, 0)),
        out_shape=jax.ShapeDtypeStruct((3, 2, FFT_R, n_rows // FFT_R, MIX_W), jnp.bfloat16),
        compiler_params=_cparams("parallel", "parallel", "arbitrary"),
        name="hy_conv_lat",
    )(*([z] * HY_PARTS), conv_w, conv_b.reshape(1, 3 * MIX_W))


def _cis(num, den):
    ang = (2.0 * math.pi / den) * (num % den).astype(jnp.float32)
    return jnp.cos(ang), jnp.sin(ang)


def fft_matrices():
    r = FFT_R
    i = jnp.arange(r, dtype=jnp.int32)
    c, s = _cis(i[:, None] * i[None, :], r)
    half = r // 2
    fa_data = jnp.concatenate([jnp.concatenate([c[:, :half], s[:, :half]], axis=1),
                               jnp.concatenate([-s[:, :half], c[:, :half]], axis=1)], axis=0)
    fa_taps = jnp.concatenate([c, -s], axis=0)
    k1, k2, s2 = i[:, None, None], i[None, :, None], i[None, None, :]
    gc, gs = _cis(s2 * (r * k2 + k1), FFT_L)
    gr, gi = gc, -gs
    m1 = jnp.concatenate([jnp.concatenate([gr, -gi], axis=2), jnp.concatenate([gi, gr], axis=2)], axis=1)
    er, ei = c[:half] / FFT_L, s[:half] / FFT_L
    e2 = jnp.concatenate([jnp.concatenate([er, -ei], axis=1), jnp.concatenate([ei, er], axis=1)], axis=0)
    bf = jnp.bfloat16
    return dict(fa_data=fa_data.astype(bf), fa_taps=fa_taps.astype(bf), m1=m1.astype(bf), e2=e2.astype(bf))


def _fft_a_kernel(f_ref, x_ref, o_ref, *, two):
    f = f_ref[...]
    res = []
    for s in range(FFT_SB):
        x = jnp.concatenate([x_ref[0, s], x_ref[1, s]], axis=0) if two else x_ref[s]
        res.append(jnp.dot(f, x, preferred_element_type=jnp.float32).astype(o_ref.dtype))
    o_ref[...] = jnp.swapaxes(jnp.stack(res), 0, 1).reshape(o_ref.shape)


def fft_stage_a(fmat, x, g):
    two = x.ndim == 5
    xspec = (pl.BlockSpec((None, 2, FFT_SB, FFT_R // 2, MIX_W), lambda j: (g, 0, j, 0, 0)) if two
             else pl.BlockSpec((None, FFT_SB, FFT_R, MIX_W), lambda j: (g, j, 0, 0)))
    return pl.pallas_call(
        functools.partial(_fft_a_kernel, two=two),
        grid=(FFT_R // FFT_SB,),
        in_specs=[pl.BlockSpec((2 * FFT_R, FFT_R), lambda j: (0, 0)), xspec],
        out_specs=pl.BlockSpec((2, FFT_R, FFT_SB, MIX_W), lambda j: (0, 0, j, 0)),
        out_shape=jax.ShapeDtypeStruct((2, FFT_R, FFT_R, MIX_W), jnp.bfloat16),
        compiler_params=_cparams("parallel"),
        name="fft_stage_a",
    )(fmat, x)


def _fft_mid_kernel(m1_ref, a_ref, *rest, conv):
    if conv:
        h_ref, o_ref = rest
    else:
        s_ref, o_ref = rest
    r = FFT_R
    for k in range(FFT_KB):
        a = a_ref[:, k].reshape(2 * r, MIX_W)
        x = jnp.dot(m1_ref[k], a, preferred_element_type=jnp.float32)
        if conv:
            h = h_ref[k].astype(jnp.float32)
            xr, xi, hr, hi = x[:r], x[r:], h[:r], h[r:]
            zc = jnp.concatenate([xr * hr - xi * hi, xr * hi + xi * hr], axis=0).astype(jnp.bfloat16)
            p = lax.dot_general(m1_ref[k], zc, (((0,), (0,)), ((), ())), preferred_element_type=jnp.float32)
            o_ref[:, k] = p.reshape(2, r, MIX_W).astype(o_ref.dtype)
        else:
            o_ref[k] = (x * s_ref[...]).astype(o_ref.dtype)


def fft_mid(m1, a, h=None, scale=None):
    conv = h is not None
    r = FFT_R
    mspec = pl.BlockSpec((FFT_KB, 2 * r, 2 * r), lambda j: (j, 0, 0))
    aspec = pl.BlockSpec((2, FFT_KB, r, MIX_W), lambda j: (0, j, 0, 0))
    hspec = pl.BlockSpec((FFT_KB, 2 * r, MIX_W), lambda j: (j, 0, 0))
    if conv:
        in_specs, args = [mspec, aspec, hspec], (m1, a, h)
        out_specs, out_shape = aspec, jax.ShapeDtypeStruct((2, r, r, MIX_W), jnp.bfloat16)
    else:
        in_specs, args = [mspec, aspec, pl.BlockSpec((1, MIX_W), lambda j: (0, 0))], (m1, a, scale)
        out_specs, out_shape = hspec, jax.ShapeDtypeStruct((r, 2 * r, MIX_W), jnp.bfloat16)
    return pl.pallas_call(
        functools.partial(_fft_mid_kernel, conv=conv),
        grid=(r // FFT_KB,),
        in_specs=in_specs, out_specs=out_specs, out_shape=out_shape,
        compiler_params=_cparams("parallel"),
        name="fft_mid_conv" if conv else "fft_mid_filter",
    )(*args)


def _fft_out_kernel(e_ref, p_ref, mul_ref, add_ref, bias_ref, o_ref, *, time_order):
    e = e_ref[...]
    half = FFT_R // 2
    bias = bias_ref[...]
    p = jnp.swapaxes(p_ref[...].reshape(2 * FFT_R, FFT_SB, MIX_W), 0, 1)
    res = [[], []]
    for s in range(FFT_SB):
        y = jnp.dot(e, p[s], preferred_element_type=jnp.float32)
        for b in range(2):
            yb = y[b * half:(b + 1) * half] + add_ref[b, s].astype(jnp.float32) * bias
            res[b].append((mul_ref[b, s].astype(jnp.float32) * yb).astype(o_ref.dtype))
    for b in range(2):
        out = jnp.stack(res[b])
        o_ref[b] = jnp.swapaxes(out, 0, 1) if time_order else out


def fft_out(e2, p, mul, gm, add, ga, bias, time_order):
    half = FFT_R // 2

    def bspec(g):
        return pl.BlockSpec((None, 2, FFT_SB, half, MIX_W), lambda j: (g, 0, j, 0, 0))

    if time_order:
        out_spec = pl.BlockSpec((None, 2, half, FFT_SB, MIX_W), lambda j: (0, 0, 0, j, 0))
        out_shape = jax.ShapeDtypeStruct((1, 2, half, FFT_R, MIX_W), jnp.bfloat16)
    else:
        out_spec, out_shape = bspec(0), jax.ShapeDtypeStruct((1, 2, FFT_R, half, MIX_W), jnp.bfloat16)
    return pl.pallas_call(
        functools.partial(_fft_out_kernel, time_order=time_order),
        grid=(FFT_R // FFT_SB,),
        in_specs=[pl.BlockSpec((FFT_R, 2 * FFT_R), lambda j: (0, 0)),
                  pl.BlockSpec((2, FFT_R, FFT_SB, MIX_W), lambda j: (0, 0, j, 0)), bspec(gm), bspec(ga),
                  pl.BlockSpec((1, MIX_W), lambda j: (0, 0))],
        out_specs=out_spec,
        out_shape=out_shape,
        compiler_params=_cparams("parallel", vmem=FFT_OUT_VMEM_BYTES),
        name="fft_out",
    )(e2, p, mul, add, bias)


def hyena_latent(xs, taps, inv_norm, bias, mats):
    u, gu = xs, 2
    for order in range(2):
        spec = fft_mid(mats['m1'], fft_stage_a(mats['fa_taps'], taps, order), scale=inv_norm[order])
        p = fft_mid(mats['m1'], fft_stage_a(mats['fa_data'], u, gu), h=spec)
        u, gu = fft_out(mats['e2'], p, xs, order, u, gu, bias[order].reshape(1, MIX_W), order == 1), 0
    return u.reshape(2, FFT_L // 2, MIX_W)


def hyena_context(xc, taps, inv_norm, bias):
    n = xc.shape[2]
    length = 2 * n
    k = jnp.arange(length, dtype=jnp.int32)
    c, s = _cis(k[:, None] * k[None, :], length)
    fr, fi = c[:, :n], -s[:, :n]
    m_fwd = jnp.concatenate([jnp.concatenate([fr, -fi], axis=1), jnp.concatenate([fi, fr], axis=1)], axis=0)
    m_taps = jnp.concatenate([c, -s], axis=0)
    er, ei = c[:n] / length, s[:n] / length
    m_inv = jnp.concatenate([jnp.concatenate([er, -ei], axis=1), jnp.concatenate([ei, er], axis=1)], axis=0)
    x1, x2, u = (xc[g].astype(jnp.float32) for g in range(3))
    for order, gate in enumerate((x1, x2)):
        h = mm(m_taps, taps[order], name="ctx_dft_taps") * inv_norm[order]
        w = mm(m_fwd, u.reshape(length, MIX_W), name="ctx_dft_fwd")
        wr, wi, hr, hi = w[:length], w[length:], h[:length], h[length:]
        zc = jnp.concatenate([wr * hr - wi * hi, wr * hi + wi * hr], axis=0)
        y = mm(m_inv, zc, name="ctx_dft_inv").reshape(2, n, MIX_W)
        u = gate * (y + u * bias[order])
    return u.astype(jnp.bfloat16)


def hyena_features(n, ff1, ff1_b, freq, ff2, ff2_b):
    k = jnp.arange(n, dtype=jnp.int32)
    pos = jnp.stack([k, (n - k) % n]).astype(jnp.float32)
    t = pos / (n - 1)
    fw = (2.0 * math.pi * pos / n)[..., None] * jnp.linspace(1e-4, HY_BANDS - 1, HY_BANDS, dtype=jnp.float32)
    feats = jnp.concatenate([t[..., None], jnp.cos(fw), -jnp.sin(fw)], axis=-1)
    h = jnp.sin(freq * (feats @ ff1 + ff1_b))
    return jnp.sin(freq * (h @ ff2 + ff2_b))


def _filter_kernel(feat_ref, w_ref, delta_ref, taps_ref, sum_ref, *, n, tm, split):
    dr = pl.program_id(1)
    i = pl.program_id(2)
    row = i * tm + lax.broadcasted_iota(jnp.int32, (tm, MIX_W), 0)
    lag = jnp.where(dr == 0, row, jnp.where(row == 0, 0, n - row))
    t = lag.astype(jnp.float32) / (n - 1)
    h = jnp.dot(feat_ref[...].astype(jnp.bfloat16), w_ref[...].astype(jnp.bfloat16),
                preferred_element_type=jnp.float32)
    h = h * jnp.exp(-t * delta_ref[...])

    @pl.when((dr == 0) & (i == 0))
    def _():
        sum_ref[...] = jnp.zeros_like(sum_ref)

    sum_ref[...] += jnp.sum(jnp.abs(h), axis=0, keepdims=True)
    taps = jnp.where((dr == 0) | (row != 0), h, 0.0).astype(taps_ref.dtype)
    if split:
        taps = jnp.swapaxes(taps.reshape(tm // FFT_R, FFT_R, MIX_W), 0, 1)
    taps_ref[...] = taps


def hyena_filter_taps(feats, ff3):
    _, n, nf = feats.shape
    split = 2 * n == FFT_L
    tm = FFT_SB * FFT_R if split else n
    nb = n // tm
    w = ff3.reshape(nf, 2, 2, MIX_W).transpose(1, 2, 0, 3)
    deltas = jnp.abs(jnp.linspace(math.log(HY_TARGET) / HY_SLOW_PCT, math.log(HY_TARGET) / HY_FAST_PCT,
                                  MIX_W, dtype=jnp.float32)).reshape(1, MIX_W)
    if split:
        tspec = pl.BlockSpec((None, FFT_R, FFT_SB, MIX_W), lambda o, dr, i: (o, 0, dr * nb + i, 0))
        tshape = jax.ShapeDtypeStruct((2, FFT_R, FFT_R, MIX_W), jnp.bfloat16)
    else:
        tspec = pl.BlockSpec((None, tm, MIX_W), lambda o, dr, i: (o, dr * nb + i, 0))
        tshape = jax.ShapeDtypeStruct((2, 2 * n, MIX_W), jnp.bfloat16)
    return pl.pallas_call(
        functools.partial(_filter_kernel, n=n, tm=tm, split=split),
        grid=(2, 2, nb),
        in_specs=[pl.BlockSpec((None, tm, nf), lambda o, dr, i: (dr, i, 0)),
                  pl.BlockSpec((None, None, nf, MIX_W), lambda o, dr, i: (o, dr, 0, 0)),
                  pl.BlockSpec((1, MIX_W), lambda o, dr, i: (0, 0))],
        out_specs=[tspec, pl.BlockSpec((None, 1, MIX_W), lambda o, dr, i: (o, 0, 0))],
        out_shape=[tshape, jax.ShapeDtypeStruct((2, 1, MIX_W), jnp.float32)],
        compiler_params=_cparams("arbitrary", "arbitrary", "arbitrary"),
        name="hyena_filter",
    )(feats, w, deltas)


def _merge_kernel(yhc_ref, yhl_ref, yl_ref, yg_ref, w_ref, g0_ref, g1_ref, g2_ref, o_ref, *, n_ctx_tiles):
    ys = (_pick_rows(yhc_ref, yhl_ref, n_ctx_tiles), yl_ref[...], yg_ref[...])
    acc = None
    for j, (y, g_ref) in enumerate(zip(ys, (g0_ref, g1_ref, g2_ref))):
        gate = 1.0 / (1.0 + jnp.exp(-g_ref[...].astype(jnp.float32)))
        t = gate * jnp.dot(y, w_ref[j], preferred_element_type=jnp.float32)
        acc = t if acc is None else acc + t
    o_ref[...] = acc.astype(o_ref.dtype)


def merge_branches(z, y_hy_ctx, y_hy_lat, y_lru, y_hg, w_branch, layer):
    rows = z.shape[0]
    tm = ROW_T // 2
    n_ctx_tiles = y_hy_ctx.shape[0] // tm
    yspec = pl.BlockSpec((tm, MIX_W), lambda i: (i, 0))

    def gspec(k):
        return pl.BlockSpec((tm, D_MODEL), lambda i: (i, GATE_COL0 // D_MODEL + k))

    return pl.pallas_call(
        functools.partial(_merge_kernel, n_ctx_tiles=n_ctx_tiles),
        grid=(rows // tm,),
        in_specs=[*_split_row_specs(tm, MIX_W, n_ctx_tiles), yspec, yspec,
                  pl.BlockSpec((None, N_BRANCH, MIX_W, D_MODEL), lambda i: (layer, 0, 0, 0)),
                  gspec(0), gspec(1), gspec(2)],
        out_specs=pl.BlockSpec((tm, D_MODEL), lambda i: (i, 0)),
        out_shape=jax.ShapeDtypeStruct((rows, D_MODEL), jnp.bfloat16),
        compiler_params=_cparams("parallel"),
        name="merge",
    )(y_hy_ctx, y_hy_lat, y_lru, y_hg, w_branch, z, z, z)


def _out_proj_kernel(a_ref, w_ref, *refs, n_ctx_tiles):
    *x_refs, g1_ref, gt_ref, g2_ref, sh_ref, sc_ref, wr_ref, xo_ref, h_ref, lg_ref = refs
    x = _pick_rows(*x_refs, n_ctx_tiles) if len(x_refs) == 2 else x_refs[0][...]
    y = jnp.dot(a_ref[...], w_ref[...], preferred_element_type=jnp.float32)
    y = y * lax.rsqrt(jnp.mean(y * y, axis=-1, keepdims=True) + EPS) * g1_ref[...]
    x = x + gt_ref[...] * y
    xo_ref[...] = x
    h = x * lax.rsqrt(jnp.mean(x * x, axis=-1, keepdims=True) + EPS) * g2_ref[...]
    h = (h * (1.0 + sc_ref[...]) + sh_ref[...]).astype(jnp.bfloat16)
    h_ref[...] = h
    lg_ref[...] = jnp.dot(h, wr_ref[...], preferred_element_type=jnp.float32)


def out_proj(acc, w_out, layer, x, g1, gt, g2, sh, sc, w_router, n_ctx_rows, n_lat_rows):
    tm = ROW_T // 2
    xspecs, xs, rows, d = _row_operand(x, tm)
    n_ctx_tiles = 2 * n_ctx_rows // tm
    idx = functools.partial(_mod_index, n_ctx_tiles=n_ctx_tiles, tiles_per_batch=n_lat_rows // tm)
    row = pl.BlockSpec((tm, d), lambda i: (i, 0))
    vec = pl.BlockSpec((1, d), lambda i: (0, 0))
    mod = pl.BlockSpec((None, 1, d), lambda i: (idx(i), 0, 0))
    wr = jnp.zeros((d, ROUTER_PAD), jnp.bfloat16).at[:, :N_EXPERTS].set(w_router.astype(jnp.bfloat16))
    return pl.pallas_call(
        functools.partial(_out_proj_kernel, n_ctx_tiles=n_ctx_tiles),
        grid=(rows // tm,),
        in_specs=[row, pl.BlockSpec((None, d, d), lambda i: (layer, 0, 0)), *xspecs, vec, mod, vec, mod, mod,
                  pl.BlockSpec((d, ROUTER_PAD), lambda i: (0, 0))],
        out_specs=[row, row, pl.BlockSpec((tm, ROUTER_PAD), lambda i: (i, 0))],
        out_shape=[jax.ShapeDtypeStruct((rows, d), jnp.float32), jax.ShapeDtypeStruct((rows, d), jnp.bfloat16),
                   jax.ShapeDtypeStruct((rows, ROUTER_PAD), jnp.float32)],
        compiler_params=_cparams("parallel"),
        name="out_proj",
    )(acc, w_out, *xs, g1.reshape(1, d), gt, g2.reshape(1, d), sh, sc, wr)


def _ffn_up_kernel(x_ref, wg_ref, wu_ref, h_ref):
    x = x_ref[...]
    hg = jnp.dot(x, wg_ref[...].astype(jnp.bfloat16), preferred_element_type=jnp.float32)
    hu = jnp.dot(x, wu_ref[...].astype(jnp.bfloat16), preferred_element_type=jnp.float32)
    h_ref[...] = (hg / (1.0 + jnp.exp(-hg)) * hu).astype(h_ref.dtype)


def _ffn_down_kernel(h_ref, wd_ref, g_ref, o_ref):
    y = jnp.dot(h_ref[...], wd_ref[...].astype(jnp.bfloat16), preferred_element_type=jnp.float32)
    o_ref[...] = (y * g_ref[...]).astype(o_ref.dtype)


def expert_ffn(xe, w_gate, w_up, w_down, layer, g):
    ne, r, d = xe.shape
    dff = w_gate.shape[-1]
    hid = pl.pallas_call(
        _ffn_up_kernel,
        grid=(ne, dff // FFN_TF),
        in_specs=[pl.BlockSpec((None, r, d), lambda e, f: (e, 0, 0)),
                  pl.BlockSpec((None, None, d, FFN_TF), lambda e, f: (layer, e, 0, f)),
                  pl.BlockSpec((None, None, d, FFN_TF), lambda e, f: (layer, e, 0, f))],
        out_specs=pl.BlockSpec((None, r, FFN_TF), lambda e, f: (e, 0, f)),
        out_shape=jax.ShapeDtypeStruct((ne, r, dff), jnp.bfloat16),
        compiler_params=_cparams("parallel", "arbitrary"),
        name="ffn_up",
    )(xe, w_gate, w_up)
    return pl.pallas_call(
        _ffn_down_kernel,
        grid=(ne, d // FFN_TD),
        in_specs=[pl.BlockSpec((None, r, dff), lambda e, j: (e, 0, 0)),
                  pl.BlockSpec((None, None, dff, FFN_TD), lambda e, j: (layer, e, 0, j)),
                  pl.BlockSpec((None, r, 1), lambda e, j: (e, 0, 0))],
        out_specs=pl.BlockSpec((None, r, FFN_TD), lambda e, j: (e, 0, j)),
        out_shape=jax.ShapeDtypeStruct((ne, r, d), jnp.bfloat16),
        compiler_params=_cparams("parallel", "arbitrary"),
        name="ffn_down",
    )(hid, w_down, g)


def route(logits, row0, bsz, n):
    cap = EC_CAPACITY * n // N_EXPERTS
    aff = jax.nn.softmax(logits[row0:row0 + bsz * n, :N_EXPERTS].reshape(bsz, n, N_EXPERTS), axis=-1)
    g, idx = lax.top_k(jnp.swapaxes(aff, 1, 2), cap)
    flat = idx + (row0 + jnp.arange(bsz, dtype=idx.dtype) * n)[:, None, None]
    return (jnp.swapaxes(g, 0, 1).reshape(N_EXPERTS, bsz * cap),
            jnp.swapaxes(flat, 0, 1).reshape(N_EXPERTS, bsz * cap))


def ec_moe(h, logits, w_gate, w_up, w_down, layer, segments):
    gs, flats = zip(*(route(logits, *seg) for seg in segments))
    flat, g = lax.sort_key_val(jnp.concatenate(flats, axis=1), jnp.concatenate(gs, axis=1), dimension=1)
    return expert_ffn(h[flat], w_gate, w_up, w_down, layer, g[..., None]), flat


def moe_plan(flat, n_rows):
    ne, r = flat.shape
    nwin = r // MOE_WIN
    n_tiles = n_rows // MOE_TT
    bounds = jnp.arange(n_tiles + 1, dtype=flat.dtype) * MOE_TT
    p = jnp.sum(flat[None, :, :] < bounds[:, None, None], axis=2, dtype=jnp.int32)
    p0, p1 = p[:-1], p[1:]
    w0 = p0 // MOE_WIN
    nw = jnp.where(p1 > p0, (p1 - 1) // MOE_WIN - w0 + 1, 0)
    off = jnp.cumsum(nw, axis=1)
    q = jnp.arange(MOE_MAXQ, dtype=jnp.int32)
    e_of_q = jnp.minimum(jnp.sum(q[None, :, None] >= off[:, None, :], axis=2, dtype=jnp.int32), ne - 1)
    pick = e_of_q[:, :, None] == jnp.arange(ne, dtype=jnp.int32)
    w = q[None, :] + jnp.sum(jnp.where(pick, (w0 - (off - nw))[:, None, :], 0), axis=2)
    items = jnp.clip(e_of_q * nwin + w, 0, ne * nwin - 1)
    return items.astype(jnp.int32), off[:, -1].astype(jnp.int32)


def _combine_kernel(items_ref, count_ref, tok_ref, ye_ref, x_ref, g_ref, gt_ref, *rest, nxt, skip):
    if nxt:
        gn_ref, sh_ref, sc_ref, o_ref, u_ref, stage, sem, acc_ref = rest
    else:
        o_ref, stage, sem, acc_ref = rest
    i = pl.program_id(0)
    t = i + skip
    n = count_ref[t]
    n_groups = (n + MOE_GROUP - 1) // MOE_GROUP

    @pl.when(i == 0)
    def _():
        stage[...] = jnp.zeros_like(stage)

    def window_copy(g, buf, k):
        item = items_ref[t, jnp.minimum(MOE_GROUP * g + k, MOE_MAXQ - 1)]
        return pltpu.make_async_copy(ye_ref.at[pl.ds(item * MOE_WIN, MOE_WIN)],
                                     stage.at[buf, pl.ds(k * MOE_WIN, MOE_WIN)], sem.at[buf])

    def for_group(g, buf, wait):
        for k in range(MOE_GROUP):
            @pl.when(MOE_GROUP * g + k < n)
            def _():
                cp = window_copy(g, buf, k)
                cp.wait() if wait else cp.start()

    acc_ref[...] = jnp.zeros_like(acc_ref)

    for j in range(MOE_NBUF - 1):
        @pl.when(j < n_groups)
        def _():
            for_group(j, j, False)

    row = t * MOE_TT + lax.broadcasted_iota(jnp.int32, (MOE_TT, MOE_GROUP * MOE_WIN), 0)

    def body(g, carry):
        buf = g % MOE_NBUF
        ahead = g + MOE_NBUF - 1

        @pl.when(ahead < n_groups)
        def _():
            for_group(ahead, ahead % MOE_NBUF, False)

        for_group(g, buf, True)
        toks = []
        for k in range(MOE_GROUP):
            qk = MOE_GROUP * g + k
            item = items_ref[t, jnp.minimum(qk, MOE_MAXQ - 1)]
            toks.append(jnp.where(qk < n, tok_ref[pl.ds(item, 1), :], -1))
        onehot = jnp.where(row == jnp.concatenate(toks, axis=1), 1.0, 0.0).astype(jnp.bfloat16)
        acc_ref[...] += jnp.dot(onehot, stage[buf], preferred_element_type=jnp.float32)
        return carry

    lax.fori_loop(0, n_groups, body, 0)

    y = acc_ref[...]
    y = y * lax.rsqrt(jnp.mean(y * y, axis=-1, keepdims=True) + EPS) * g_ref[...]
    x = x_ref[...] + gt_ref[...] * y
    if nxt:
        u = x * lax.rsqrt(jnp.mean(x * x, axis=-1, keepdims=True) + EPS) * gn_ref[...]
        u_ref[...] = (u * (1.0 + sc_ref[...]) + sh_ref[...]).astype(u_ref.dtype)
    o_ref[...] = x


def moe_combine(x, ye, flat, g, gt, n_ctx_rows, n_lat_rows, nxt=None):
    rows, d = x.shape
    ne, r, _ = ye.shape
    items, count = moe_plan(flat, rows)
    n_ctx_tiles = 2 * n_ctx_rows // MOE_TT
    idx = functools.partial(_mod_index, n_ctx_tiles=n_ctx_tiles, tiles_per_batch=n_lat_rows // MOE_TT)
    skip = 0 if nxt else n_ctx_tiles
    row = pl.BlockSpec((MOE_TT, d), lambda i, *_: (i + skip, 0))
    vec = pl.BlockSpec((1, d), lambda i, *_: (0, 0))
    mod = pl.BlockSpec((None, 1, d), lambda i, *_: (idx(i + skip), 0, 0))
    out_row = pl.BlockSpec((MOE_TT, d), lambda i, *_: (i, 0))
    n_out = rows - skip * MOE_TT
    in_specs = [pl.BlockSpec((ne * r // MOE_WIN, MOE_WIN), lambda i, *_: (0, 0)),
                pl.BlockSpec(memory_space=pl.ANY), row, vec, mod]
    args = [flat.reshape(ne * r // MOE_WIN, MOE_WIN), ye.reshape(ne * r, d), x, g.reshape(1, d), gt]
    out_specs, out_shape = out_row, jax.ShapeDtypeStruct((n_out, d), jnp.float32)
    if nxt:
        in_specs += [vec, mod, mod]
        args += [nxt[0].reshape(1, d), nxt[1], nxt[2]]
        out_specs = [out_row, out_row]
        out_shape = [out_shape, jax.ShapeDtypeStruct((n_out, d), jnp.bfloat16)]
    return pl.pallas_call(
        functools.partial(_combine_kernel, nxt=bool(nxt), skip=skip),
        grid_spec=pltpu.PrefetchScalarGridSpec(
            num_scalar_prefetch=2,
            grid=(n_out // MOE_TT,),
            in_specs=in_specs, out_specs=out_specs,
            scratch_shapes=[pltpu.VMEM((MOE_NBUF, MOE_GROUP * MOE_WIN, d), jnp.bfloat16),
                            pltpu.SemaphoreType.DMA((MOE_NBUF,)),
                            pltpu.VMEM((MOE_TT, d), jnp.float32)]),
        out_shape=out_shape,
        compiler_params=_cparams("arbitrary"),
        name="moe_combine",
    )(items, count, *args)


def kernel(x, c, ctx, c_ctx, w_mod, b_mod, norm_g, w_in, hy_conv_w, hy_conv_b, hy_ff1, hy_ff1_b, hy_freq,
           hy_ff2, hy_ff2_b, hy_ff3, hy_bias, lru_conv_w, lru_conv_b, lru_wa, lru_ba, lru_wi, lru_bi,
           lru_lambda, hg_lb_logits, hg_norm_g, w_branch, w_out, w_router, w_gate, w_up, w_down):
    bsz, n_lat, d = x.shape
    n_ctx = ctx.shape[1]
    assert bsz == 2 and d == D_MODEL and n_ctx == SEQ_T and 2 * n_lat == FFT_L and n_lat % ROW_T == 0
    gam = jax.nn.softmax(hg_lb_logits.astype(jnp.float32), axis=0)
    lb_all = jnp.maximum(jnp.cumsum(gam, axis=0) - gam[:1], 0.0)
    mats = fft_matrices()
    cvec = jnp.zeros((SUBLANES, d), jnp.float32).at[:bsz].set(jax.nn.silu(c)).at[bsz].set(jax.nn.silu(c_ctx))
    n_ctx_all = bsz * n_ctx
    xa = (ctx.reshape(n_ctx_all, d), x.reshape(bsz * n_lat, d))
    w_branch_b, w_out_b = w_branch.astype(jnp.bfloat16), w_out.astype(jnp.bfloat16)
    mods = []
    for l in range(DEPTH):
        mod = mm(cvec, w_mod, layer=l, name="adaln")[:bsz + 1] + b_mod[l]
        mods.append([mod.reshape(bsz + 1, 1, 6, d)[:, :, k] for k in range(6)])
    u = prenorm(xa, norm_g[0, 0], mods[0][0], mods[0][1], n_ctx, n_lat)
    for l in range(DEPTH):
        last = l == DEPTH - 1
        p = {'lru_conv_w': lru_conv_w[l], 'lru_conv_b': lru_conv_b[l], 'lru_wa': lru_wa[l], 'lru_ba': lru_ba[l],
             'lru_wi': lru_wi[l], 'lru_bi': lru_bi[l], 'lru_lambda': lru_lambda[l]}
        sh1, sc1, gt1, sh2, sc2, gt2 = mods[l]

        z = mm(u, w_in, jnp.bfloat16, tm=IN_PROJ_TM, tn=1024, layer=l, name="in_proj")

        y_lru = lru_branch(z, p, n_ctx, n_lat)
        y_hg = hgrn_branch(z, lb_all[l], hg_norm_g[l], n_ctx, n_lat)

        ffp = (hy_ff1[l], hy_ff1_b[l], hy_freq[l], hy_ff2[l], hy_ff2_b[l])
        taps_l, norm_l = hyena_filter_taps(hyena_features(n_lat, *ffp), hy_ff3[l])
        taps_c, norm_c = hyena_filter_taps(hyena_features(n_ctx, *ffp), hy_ff3[l])
        xc_l = hy_conv_latent(z, hy_conv_w[l], hy_conv_b[l], n_ctx_all, n_lat)
        y_hy_l = hyena_latent(xc_l, taps_l, 1.0 / norm_l, hy_bias[l], mats)
        xc_c = hy_conv(z, hy_conv_w[l], hy_conv_b[l], 0, n_ctx, True)
        y_hy_c = hyena_context(xc_c, taps_c, 1.0 / norm_c, hy_bias[l])

        acc = merge_branches(z, y_hy_c.reshape(n_ctx_all, MIX_W), y_hy_l.reshape(bsz * n_lat, MIX_W), y_lru, y_hg,
                             w_branch_b, l)
        xa, h, logits = out_proj(acc, w_out_b, l, xa, norm_g[l, 1], gt1, norm_g[l, 2], sh2, sc2,
                                 w_router[l], n_ctx, n_lat)
        segments = [(n_ctx_all, bsz, n_lat)] if last else [(n_ctx_all, bsz, n_lat), (0, bsz, n_ctx)]
        ye, flat = ec_moe(h, logits, w_gate, w_up, w_down, l, segments)
        if last:
            xa = moe_combine(xa, ye, flat, norm_g[l, 3], gt2, n_ctx, n_lat)
        else:
            nxt = (norm_g[l + 1, 0], mods[l + 1][0], mods[l + 1][1])
            xa, u = moe_combine(xa, ye, flat, norm_g[l, 3], gt2, n_ctx, n_lat, nxt)
    return xa.reshape(bsz, n_lat, d)
```

```python
import functools
import math

import numpy as np
import jax
import jax.numpy as jnp
from jax import lax
from jax.experimental import pallas as pl
from jax.experimental.pallas import tpu as pltpu

D_MODEL = 2048
DEPTH = 2
GRID_W = 64
MIX_W = D_MODEL // 2
N_BRANCH = 3
HY_BANDS = 16
HY_TARGET = 1e-2
HY_FAST_PCT = 0.3
HY_SLOW_PCT = 1.5
LRU_BLOCKS = 8
LRU_BS = MIX_W // LRU_BLOCKS
LRU_C = 8.0
HG_HEADS = 8
HG_DK = MIX_W // HG_HEADS
N_EXPERTS = 16
EC_CAPACITY = 2
EPS = 1e-6
TINY = 1e-30
LOG2E = 1.4426950408889634
HY_COL0 = 0
LRU_COL0 = 3 * MIX_W
HG_COL0 = 5 * MIX_W
GATE_COL0 = 10 * MIX_W
IN_COLS = GATE_COL0 + N_BRANCH * D_MODEL

SUBLANES = 8
LANES = 128
SEQ_T = 256
HG_C = 128
HG_LEVELS = 7
HG_HB = 8
ROW_T = 512
IN_PROJ_TM = 1536
FFT_R = 128
FFT_L = FFT_R * FFT_R
FFT_KB = 8
FFT_SB = 16
HY_PARTS = 4
ROUTER_PAD = LANES
OUT_PROJ_PARTS = 2
FFN_TF = 256
FFN_TD = 512
MOE_TT = 512
MOE_WIN = 64
MOE_GROUP = 8
MOE_NBUF = 3
MOE_MAXQ = N_EXPERTS * (MOE_TT // MOE_WIN + 1)
VMEM_LIMIT_BYTES = 48 * 1024 * 1024
FFT_OUT_VMEM_BYTES = 56 * 1024 * 1024


def _cparams(*sem, vmem=VMEM_LIMIT_BYTES):
    return pltpu.CompilerParams(dimension_semantics=sem, vmem_limit_bytes=vmem)


def _pick_tile(n, pref):
    for t in (pref, 1024, 512, 256, 128):
        if t <= n and n % t == 0:
            return t
    return n


def _row_block(b, j, n_ctx, n_lat, reverse):
    if reverse:
        kc, kl = n_ctx - 1 - j, n_ctx + n_lat - 1 - j
    else:
        kc, kl = j, j - n_ctx
    return jnp.where(j < n_ctx, b * n_ctx + kc, 2 * n_ctx + b * n_lat + kl)


def _mm_kernel(a_ref, b_ref, o_ref):
    o_ref[...] = jnp.dot(a_ref[...].astype(jnp.bfloat16), b_ref[...].astype(jnp.bfloat16),
                         preferred_element_type=jnp.float32).astype(o_ref.dtype)


def mm(a, b, out_dtype=jnp.float32, tm=512, tn=1024, layer=None, name="mm"):
    m, k = a.shape
    n = b.shape[-1]
    tm = _pick_tile(m, tm)
    tn = _pick_tile(n, tn)
    if layer is None:
        bspec = pl.BlockSpec((k, tn), lambda i, j: (0, j))
    else:
        bspec = pl.BlockSpec((None, k, tn), lambda i, j: (layer, 0, j))
    return pl.pallas_call(
        _mm_kernel,
        grid=(m // tm, n // tn),
        in_specs=[pl.BlockSpec((tm, k), lambda i, j: (i, 0)), bspec],
        out_specs=pl.BlockSpec((tm, tn), lambda i, j: (i, j)),
        out_shape=jax.ShapeDtypeStruct((m, n), out_dtype),
        compiler_params=_cparams("parallel", "arbitrary"),
        name=name,
    )(a, b)


def _mod_index(i, n_ctx_tiles, tiles_per_batch):
    return jnp.where(i < n_ctx_tiles, 2, (i - n_ctx_tiles) // tiles_per_batch)


def _split_row_specs(tile, width, n_ctx_tiles):
    return (pl.BlockSpec((tile, width), lambda i: (jnp.minimum(i, n_ctx_tiles - 1), 0)),
            pl.BlockSpec((tile, width), lambda i: (jnp.maximum(i - n_ctx_tiles, 0), 0)))


def _pick_rows(ctx_ref, lat_ref, n_ctx_tiles):
    return jnp.where(pl.program_id(0) < n_ctx_tiles, ctx_ref[...], lat_ref[...])


def _row_operand(x, tile):
    if isinstance(x, tuple):
        d = x[0].shape[1]
        return list(_split_row_specs(tile, d, x[0].shape[0] // tile)), list(x), x[0].shape[0] + x[1].shape[0], d
    return [pl.BlockSpec((tile, x.shape[1]), lambda i: (i, 0))], [x], x.shape[0], x.shape[1]


def _prenorm_kernel(*refs, n_ctx_tiles):
    *x_refs, g_ref, sh_ref, sc_ref, u_ref = refs
    x = _pick_rows(*x_refs, n_ctx_tiles) if len(x_refs) == 2 else x_refs[0][...]
    y = x * lax.rsqrt(jnp.mean(x * x, axis=-1, keepdims=True) + EPS) * g_ref[...]
    u_ref[...] = (y * (1.0 + sc_ref[...]) + sh_ref[...]).astype(u_ref.dtype)


def prenorm(x, g, sh, sc, n_ctx_rows, n_lat_rows):
    xspecs, xs, rows, d = _row_operand(x, ROW_T)
    n_ctx_tiles = 2 * n_ctx_rows // ROW_T
    idx = functools.partial(_mod_index, n_ctx_tiles=n_ctx_tiles, tiles_per_batch=n_lat_rows // ROW_T)
    vec = pl.BlockSpec((None, 1, d), lambda i: (idx(i), 0, 0))
    return pl.pallas_call(
        functools.partial(_prenorm_kernel, n_ctx_tiles=n_ctx_tiles),
        grid=(rows // ROW_T,),
        in_specs=[*xspecs, pl.BlockSpec((1, d), lambda i: (0, 0)), vec, vec],
        out_specs=pl.BlockSpec((ROW_T, d), lambda i: (i, 0)),
        out_shape=jax.ShapeDtypeStruct((rows, d), jnp.bfloat16),
        compiler_params=_cparams("parallel"),
        name="prenorm",
    )(*xs, g.reshape(1, d), sh, sc)


def _gelu_tanh(x):
    return 0.5 * x * (1.0 + jnp.tanh(math.sqrt(2.0 / math.pi) * (x + 0.044715 * (x * x * x))))


def _conv_pos(is_ctx, shape):
    t = lax.broadcasted_iota(jnp.int32, shape, 0)
    pos = jnp.where(is_ctx, t, t & (GRID_W - 1))
    last = jnp.where(is_ctx, SEQ_T - 1, GRID_W - 1)
    return pos, last


def _lru_pass_kernel(x_ref, cw_ref, cb_ref, w_ref, gb_ref, lam_ref, *rest, reverse, final):
    if final:
        lg_ref, hprev_ref, out_ref, h_ref, a_ref, b_ref = rest
    else:
        out_ref, h_ref, a_ref, b_ref = rest
    j = pl.program_id(1)

    @pl.when(j == 0)
    def _():
        h_ref[...] = jnp.zeros_like(h_ref)

    x = x_ref[...].astype(jnp.float32)
    pos, last = _conv_pos(j == 0, x.shape)
    cw = cw_ref[...]
    xc = cb_ref[...] + cw[2:3] * x
    xc = xc + cw[0:1] * jnp.where(pos >= 2, pltpu.roll(x, 2, 0), 0.0)
    xc = xc + cw[1:2] * jnp.where(pos >= 1, pltpu.roll(x, 1, 0), 0.0)
    xc = xc + cw[3:4] * jnp.where(pos < last, pltpu.roll(x, SEQ_T - 1, 0), 0.0)

    xb = xc.astype(jnp.bfloat16)
    lam = lam_ref[...]
    sp = jnp.maximum(-lam, 0.0) + jnp.log(1.0 + jnp.exp(-jnp.abs(lam)))
    n_tiles = SEQ_T // SUBLANES
    row = lax.broadcasted_iota(jnp.int32, (1, SUBLANES, LRU_BS), 1)
    for n in range(LRU_BLOCKS):
        sl = slice(n * LRU_BS, (n + 1) * LRU_BS)
        pre = jnp.dot(xb[:, sl], w_ref[n], preferred_element_type=jnp.float32) + gb_ref[n]
        gate = 1.0 / (1.0 + jnp.exp(-pre))
        r, i = gate[:, :LRU_BS], gate[:, LRU_BS:]
        a = jnp.exp(-LRU_C * r * sp[:, sl])
        om = 1.0 - a * a
        b = om * lax.rsqrt(jnp.maximum(om, TINY)) * (i * xc[:, sl])
        a = a.reshape(n_tiles, SUBLANES, LRU_BS)
        b = b.reshape(n_tiles, SUBLANES, LRU_BS)
        for d in (1, 2, 4):
            if reverse:
                m = row < SUBLANES - d
                a_sh, b_sh = pltpu.roll(a, SUBLANES - d, 1), pltpu.roll(b, SUBLANES - d, 1)
            else:
                m = row >= d
                a_sh, b_sh = pltpu.roll(a, d, 1), pltpu.roll(b, d, 1)
            b = jnp.where(m, a * b_sh + b, b)
            a = jnp.where(m, a * a_sh, a)
        a_ref[:, sl] = a.reshape(SEQ_T, LRU_BS)
        b_ref[:, sl] = b.reshape(SEQ_T, LRU_BS)

    h = h_ref[...]
    for k in range(n_tiles):
        kk = n_tiles - 1 - k if reverse else k
        rs = slice(kk * SUBLANES, (kk + 1) * SUBLANES)
        ht = a_ref[rs, :] * h + b_ref[rs, :]
        h = ht[0:1, :] if reverse else ht[SUBLANES - 1:SUBLANES, :]
        if final:
            ht = (ht + hprev_ref[rs, :]) * _gelu_tanh(lg_ref[rs, :].astype(jnp.float32))
        out_ref[rs, :] = ht.astype(out_ref.dtype)
    h_ref[...] = h


def lru_pass(z, conv_w, conv_b, wa, ba, wi, bi, lam, n_ctx_rows, n_lat_rows, reverse, h_prev=None):
    rows = z.shape[0]
    n_lat = n_lat_rows // SEQ_T
    cb = LRU_COL0 // MIX_W
    final = h_prev is not None
    rb = functools.partial(_row_block, n_ctx=n_ctx_rows // SEQ_T, n_lat=n_lat, reverse=reverse)

    def zspec(k):
        return pl.BlockSpec((SEQ_T, MIX_W), lambda b, j: (rb(b, j), cb + k))

    row_spec = pl.BlockSpec((SEQ_T, MIX_W), lambda b, j: (rb(b, j), 0))

    def const_spec(shape):
        return pl.BlockSpec(shape, lambda b, j: (0,) * len(shape))

    w2 = jnp.concatenate([wa, wi], axis=-1).astype(jnp.bfloat16)
    gb = jnp.concatenate([ba.reshape(LRU_BLOCKS, 1, LRU_BS), bi.reshape(LRU_BLOCKS, 1, LRU_BS)], axis=-1)
    in_specs = [zspec(0), const_spec((4, MIX_W)), const_spec((1, MIX_W)),
                const_spec((LRU_BLOCKS, LRU_BS, 2 * LRU_BS)), const_spec((LRU_BLOCKS, 1, 2 * LRU_BS)),
                const_spec((1, MIX_W))]
    args = [z, conv_w, conv_b.reshape(1, MIX_W), w2, gb, lam.reshape(1, MIX_W)]
    if final:
        in_specs += [zspec(1), row_spec]
        args += [z, h_prev]
    return pl.pallas_call(
        functools.partial(_lru_pass_kernel, reverse=reverse, final=final),
        grid=(2, n_ctx_rows // SEQ_T + n_lat),
        in_specs=in_specs,
        out_specs=row_spec,
        out_shape=jax.ShapeDtypeStruct((rows, MIX_W), jnp.bfloat16 if final else jnp.float32),
        scratch_shapes=[pltpu.VMEM((1, MIX_W), jnp.float32),
                        pltpu.VMEM((SEQ_T, MIX_W), jnp.float32),
                        pltpu.VMEM((SEQ_T, MIX_W), jnp.float32)],
        compiler_params=_cparams("parallel", "arbitrary"),
        name="lru_bwd" if reverse else "lru_fwd",
    )(*args)


def lru_branch(z, p, n_ctx_rows, n_lat_rows):
    def one(d, h_prev):
        return lru_pass(z, p['lru_conv_w'], p['lru_conv_b'], p['lru_wa'][d], p['lru_ba'][d], p['lru_wi'][d],
                        p['lru_bi'][d], p['lru_lambda'][d], n_ctx_rows, n_lat_rows, d == 1, h_prev)
    return one(1, one(0, None))


def _hg_level_matrix(reverse):
    t = np.arange(HG_C)[:, None]
    s = np.arange(HG_C)[None, :]
    x = t ^ s
    lv = np.where(x > 0, np.floor(np.log2(np.maximum(x, 1))).astype(np.int32), -1)
    lv = np.where(s < t, lv, -1).astype(np.int32)
    return jnp.asarray(lv.T if reverse else lv)


def _hg_pass_kernel(lv_ref, lb_ref, q_ref, f_ref, v_ref, *rest, reverse, final):
    if final:
        og_ref, oprev_ref, g_ref, out_ref, st_ref = rest
    else:
        out_ref, st_ref = rest

    @pl.when(pl.program_id(1) == 0)
    def _():
        st_ref[...] = jnp.zeros_like(st_ref)

    lv = lv_ref[...]
    at_level = [lv == lvl for lvl in range(HG_LEVELS)]
    n_tiles = HG_C // SUBLANES
    wb = HG_HB * HG_DK
    row = lax.broadcasted_iota(jnp.int32, (1, SUBLANES, wb), 1)
    for hb in range(HG_HEADS // HG_HB):
        sl = slice(hb * wb, (hb + 1) * wb)
        heads = [(hb * HG_HB + j, slice(j * HG_DK, (j + 1) * HG_DK)) for j in range(HG_HB)]
        x = f_ref[:, sl].astype(jnp.float32)
        lb = lb_ref[:, sl]
        e = jnp.exp(-jnp.abs(x))
        inv = 1.0 / (1.0 + e)
        pos = x >= 0
        sig = jnp.where(pos, inv, e * inv)
        sigm = jnp.where(pos, e * inv, inv)
        lsig = jnp.minimum(x, 0.0) - jnp.log(1.0 + e)
        lf = jnp.where(lb > 0, jnp.log(lb + (1.0 - lb) * sig), lsig)
        kk = (1.0 - lb) * sigm
        q = q_ref[:, sl].astype(jnp.float32)
        v = v_ref[:, sl]

        c = (lf * LOG2E).reshape(n_tiles, SUBLANES, wb)
        bt = c
        att = [None] * HG_HB
        q3, k3 = (a.reshape(n_tiles, SUBLANES, wb) for a in (q, kk))
        for lvl in range(HG_LEVELS):
            m = 1 << lvl
            if m < SUBLANES:
                qm = (q * jnp.exp2(c).reshape(HG_C, wb)).astype(jnp.bfloat16)
                km = (kk * jnp.exp2(bt - c).reshape(HG_C, wb)).astype(jnp.bfloat16)
            else:
                k = m // SUBLANES
                pair = (n_tiles // (2 * k), 2, k, SUBLANES, wb)
                c5, b5, q5, k5 = (a.reshape(pair) for a in (c, bt, q3, k3))
                qi, ki = (0, 1) if reverse else (1, 0)
                parts = [None, None]
                parts[qi] = q5[:, qi] * jnp.exp2(c5[:, qi])
                parts[ki] = k5[:, ki] * jnp.exp2(b5[:, ki] - c5[:, ki])
                qm = km = jnp.stack(parts, axis=1).reshape(HG_C, wb).astype(jnp.bfloat16)
            for j, (_, hs) in enumerate(heads):
                d = lax.dot_general(qm[:, hs], km[:, hs], (((1,), (1,)), ((), ())),
                                    preferred_element_type=jnp.float32)
                att[j] = jnp.where(at_level[lvl], d, 0.0 if lvl == 0 else att[j])
            if m < SUBLANES:
                upper = (row & m) != 0
                down = pltpu.roll(bt, m, 1)
                up = pltpu.roll(bt, SUBLANES - m, 1)
                if reverse:
                    c = c + jnp.where(upper, 0.0, up)
                else:
                    c = c + jnp.where(upper, down, 0.0)
                bt = bt + jnp.where(upper, down, up)
            else:
                k = m // SUBLANES
                pair = (n_tiles // (2 * k), 2, k, SUBLANES, wb)
                c5, b5 = c.reshape(pair), bt.reshape(pair)
                tot2 = b5[:, 0] + b5[:, 1]
                if reverse:
                    c = jnp.stack([c5[:, 0] + b5[:, 1], c5[:, 1]], axis=1)
                else:
                    c = jnp.stack([c5[:, 0], c5[:, 1] + b5[:, 0]], axis=1)
                c = c.reshape(n_tiles, SUBLANES, wb)
                bt = jnp.stack([tot2, tot2], axis=1).reshape(n_tiles, SUBLANES, wb)

        qm = (q * jnp.exp2(c).reshape(HG_C, wb)).astype(jnp.bfloat16)
        km = (kk * jnp.exp2(bt - c).reshape(HG_C, wb)).astype(jnp.bfloat16)
        tot = jnp.exp2(bt[0, 0:1, :])
        diag = q * kk
        for j, (h, hs) in enumerate(heads):
            osl = slice(h * HG_DK, (h + 1) * HG_DK)
            st = st_ref[h]
            vh = v[:, hs]
            o = lax.dot_general(qm[:, hs], st.astype(jnp.bfloat16), (((1,), (1,)), ((), ())),
                                preferred_element_type=jnp.float32)
            o = o + jnp.dot(att[j].astype(jnp.bfloat16), vh, preferred_element_type=jnp.float32)
            o = o + jnp.sum(diag[:, hs], axis=-1, keepdims=True) * vh.astype(jnp.float32)
            st_ref[h] = st * tot[:, hs] + lax.dot_general(vh, km[:, hs], (((0,), (0,)), ((), ())),
                                                          preferred_element_type=jnp.float32)
            if final:
                o = o + oprev_ref[:, osl]
                y = o * lax.rsqrt(jnp.mean(o * o, axis=-1, keepdims=True) + EPS) * g_ref[:, osl]
                og = og_ref[:, osl].astype(jnp.float32)
                out_ref[:, osl] = (y * (og / (1.0 + jnp.exp(-og)))).astype(out_ref.dtype)
            else:
                out_ref[:, osl] = o


def hgrn_pass(z, lb, n_ctx_rows, n_lat_rows, reverse, o_prev=None, norm_g=None):
    rows = z.shape[0]
    n_ctx, n_lat = n_ctx_rows // HG_C, n_lat_rows // HG_C
    cb = HG_COL0 // MIX_W
    final = o_prev is not None
    rb = functools.partial(_row_block, n_ctx=n_ctx, n_lat=n_lat, reverse=reverse)

    def zspec(k):
        return pl.BlockSpec((HG_C, MIX_W), lambda b, j: (rb(b, j), cb + k))

    row_spec = pl.BlockSpec((HG_C, MIX_W), lambda b, j: (rb(b, j), 0))
    vec_spec = pl.BlockSpec((1, MIX_W), lambda b, j: (0, 0))
    in_specs = [pl.BlockSpec((HG_C, HG_C), lambda b, j: (0, 0)), vec_spec,
                zspec(0), zspec(2 if reverse else 1), zspec(3)]
    args = [_hg_level_matrix(reverse), lb.reshape(1, MIX_W), z, z, z]
    if final:
        in_specs += [zspec(4), row_spec, vec_spec]
        args += [z, o_prev, norm_g.reshape(1, MIX_W)]
    return pl.pallas_call(
        functools.partial(_hg_pass_kernel, reverse=reverse, final=final),
        grid=(2, n_ctx + n_lat),
        in_specs=in_specs,
        out_specs=row_spec,
        out_shape=jax.ShapeDtypeStruct((rows, MIX_W), jnp.bfloat16 if final else jnp.float32),
        scratch_shapes=[pltpu.VMEM((HG_HEADS, HG_DK, HG_DK), jnp.float32)],
        compiler_params=_cparams("parallel", "arbitrary"),
        name="hgrn_bwd" if reverse else "hgrn_fwd",
    )(*args)


def hgrn_branch(z, lb, norm_g, n_ctx_rows, n_lat_rows):
    o_f = hgrn_pass(z, lb, n_ctx_rows, n_lat_rows, False)
    return hgrn_pass(z, lb, n_ctx_rows, n_lat_rows, True, o_f, norm_g)


def _hy_conv_kernel(x_ref, w_ref, b_ref, out_ref, *, is_ctx):
    x = x_ref[...].astype(jnp.float32)
    t = lax.broadcasted_iota(jnp.int32, x.shape, 0)
    pos = t if is_ctx else t & (GRID_W - 1)
    last = SEQ_T - 1 if is_ctx else GRID_W - 1
    w = w_ref[...]
    y = b_ref[...] + w[1:2] * x
    y = y + w[0:1] * jnp.where(pos >= 1, pltpu.roll(x, 1, 0), 0.0)
    y = y + w[2:3] * jnp.where(pos < last, pltpu.roll(x, SEQ_T - 1, 0), 0.0)
    for g in range(3):
        out_ref[g] = y[:, g * MIX_W:(g + 1) * MIX_W].astype(out_ref.dtype)


def hy_conv(z, conv_w, conv_b, block0, n_rows, is_ctx):
    nb = n_rows // SEQ_T
    return pl.pallas_call(
        functools.partial(_hy_conv_kernel, is_ctx=is_ctx),
        grid=(2, nb),
        in_specs=[pl.BlockSpec((SEQ_T, 3 * MIX_W), lambda b, j: (block0 + b * nb + j, 0)),
                  pl.BlockSpec((3, 3 * MIX_W), lambda b, j: (0, 0)),
                  pl.BlockSpec((1, 3 * MIX_W), lambda b, j: (0, 0))],
        out_specs=pl.BlockSpec((3, None, SEQ_T, MIX_W), lambda b, j: (0, b, j, 0)),
        out_shape=jax.ShapeDtypeStruct((3, 2, n_rows, MIX_W), jnp.bfloat16),
        compiler_params=_cparams("parallel", "arbitrary"),
        name="hy_conv_ctx" if is_ctx else "hy_conv_lat",
    )(z, conv_w, conv_b.reshape(1, 3 * MIX_W))


def _hy_conv_lat_kernel(*refs):
    x_refs, (w_ref, b_ref, out_ref) = refs[:HY_PARTS], refs[HY_PARTS:]
    x = jnp.concatenate([r[...] for r in x_refs], axis=0).astype(jnp.float32)
    n = x.shape[0]
    pos = lax.broadcasted_iota(jnp.int32, x.shape, 0) & (GRID_W - 1)
    w = w_ref[...]
    y = b_ref[...] + w[1:2] * x
    y = y + w[0:1] * jnp.where(pos >= 1, pltpu.roll(x, 1, 0), 0.0)
    y = y + w[2:3] * jnp.where(pos < GRID_W - 1, pltpu.roll(x, n - 1, 0), 0.0)
    y = y.astype(out_ref.dtype).reshape(n // FFT_R, FFT_R, MIX_W)
    out_ref[...] = jnp.swapaxes(y, 0, 1)


def hy_conv_latent(z, conv_w, conv_b, row0, n_rows):
    part = FFT_SB * FFT_R // HY_PARTS
    nb = n_rows // (FFT_SB * FFT_R)

    def xspec(k):
        return pl.BlockSpec((part, MIX_W), lambda g, b, j: ((row0 + b * n_rows) // part + j * HY_PARTS + k, g))

    return pl.pallas_call(
        _hy_conv_lat_kernel,
        grid=(3, 2, nb),
        in_specs=[xspec(k) for k in range(HY_PARTS)] + [
            pl.BlockSpec((3, MIX_W), lambda g, b, j: (0, g)), pl.BlockSpec((1, MIX_W), lambda g, b, j: (0, g))],
        out_specs=pl.BlockSpec((None, None, FFT_R, FFT_SB, MIX_W), lambda g, b, j: (g, b, 0, j, 0)),
        out_shape=jax.ShapeDtypeStruct((3, 2, FFT_R, n_rows // FFT_R, MIX_W), jnp.bfloat16),
        compiler_params=_cparams("parallel", "parallel", "arbitrary"),
        name="hy_conv_lat",
    )(*([z] * HY_PARTS), conv_w, conv_b.reshape(1, 3 * MIX_W))


def _cis(num, den):
    ang = (2.0 * math.pi / den) * (num % den).astype(jnp.float32)
    return jnp.cos(ang), jnp.sin(ang)


def fft_matrices():
    r = FFT_R
    i = jnp.arange(r, dtype=jnp.int32)
    c, s = _cis(i[:, None] * i[None, :], r)
    half = r // 2
    fa_data = jnp.concatenate([jnp.concatenate([c[:, :half], s[:, :half]], axis=1),
                               jnp.concatenate([-s[:, :half], c[:, :half]], axis=1)], axis=0)
    fa_taps = jnp.concatenate([c, -s], axis=0)
    k1, k2, s2 = i[:, None, None], i[None, :, None], i[None, None, :]
    gc, gs = _cis(s2 * (r * k2 + k1), FFT_L)
    gr, gi = gc, -gs
    m1 = jnp.concatenate([jnp.concatenate([gr, -gi], axis=2), jnp.concatenate([gi, gr], axis=2)], axis=1)
    er, ei = c[:half] / FFT_L, s[:half] / FFT_L
    e2 = jnp.concatenate([jnp.concatenate([er, -ei], axis=1), jnp.concatenate([ei, er], axis=1)], axis=0)
    bf = jnp.bfloat16
    return dict(fa_data=fa_data.astype(bf), fa_taps=fa_taps.astype(bf), m1=m1.astype(bf), e2=e2.astype(bf))


def _fft_a_kernel(f_ref, x_ref, o_ref, *, two):
    f = f_ref[...]
    res = []
    for s in range(FFT_SB):
        x = jnp.concatenate([x_ref[0, s], x_ref[1, s]], axis=0) if two else x_ref[s]
        res.append(jnp.dot(f, x, preferred_element_type=jnp.float32).astype(o_ref.dtype))
    o_ref[...] = jnp.swapaxes(jnp.stack(res), 0, 1).reshape(o_ref.shape)


def fft_stage_a(fmat, x, g):
    two = x.ndim == 5
    xspec = (pl.BlockSpec((None, 2, FFT_SB, FFT_R // 2, MIX_W), lambda j: (g, 0, j, 0, 0)) if two
             else pl.BlockSpec((None, FFT_SB, FFT_R, MIX_W), lambda j: (g, j, 0, 0)))
    return pl.pallas_call(
        functools.partial(_fft_a_kernel, two=two),
        grid=(FFT_R // FFT_SB,),
        in_specs=[pl.BlockSpec((2 * FFT_R, FFT_R), lambda j: (0, 0)), xspec],
        out_specs=pl.BlockSpec((2, FFT_R, FFT_SB, MIX_W), lambda j: (0, 0, j, 0)),
        out_shape=jax.ShapeDtypeStruct((2, FFT_R, FFT_R, MIX_W), jnp.bfloat16),
        compiler_params=_cparams("parallel"),
        name="fft_stage_a",
    )(fmat, x)


def _fft_mid_kernel(m1_ref, a_ref, *rest, conv):
    if conv:
        h_ref, o_ref = rest
    else:
        s_ref, o_ref = rest
    r = FFT_R
    for k in range(FFT_KB):
        a = a_ref[:, k].reshape(2 * r, MIX_W)
        x = jnp.dot(m1_ref[k], a, preferred_element_type=jnp.float32)
        if conv:
            h = h_ref[k].astype(jnp.float32)
            xr, xi, hr, hi = x[:r], x[r:], h[:r], h[r:]
            zc = jnp.concatenate([xr * hr - xi * hi, xr * hi + xi * hr], axis=0).astype(jnp.bfloat16)
            p = lax.dot_general(m1_ref[k], zc, (((0,), (0,)), ((), ())), preferred_element_type=jnp.float32)
            o_ref[:, k] = p.reshape(2, r, MIX_W).astype(o_ref.dtype)
        else:
            o_ref[k] = (x * s_ref[...]).astype(o_ref.dtype)


def fft_mid(m1, a, h=None, scale=None):
    conv = h is not None
    r = FFT_R
    mspec = pl.BlockSpec((FFT_KB, 2 * r, 2 * r), lambda j: (j, 0, 0))
    aspec = pl.BlockSpec((2, FFT_KB, r, MIX_W), lambda j: (0, j, 0, 0))
    hspec = pl.BlockSpec((FFT_KB, 2 * r, MIX_W), lambda j: (j, 0, 0))
    if conv:
        in_specs, args = [mspec, aspec, hspec], (m1, a, h)
        out_specs, out_shape = aspec, jax.ShapeDtypeStruct((2, r, r, MIX_W), jnp.bfloat16)
    else:
        in_specs, args = [mspec, aspec, pl.BlockSpec((1, MIX_W), lambda j: (0, 0))], (m1, a, scale)
        out_specs, out_shape = hspec, jax.ShapeDtypeStruct((r, 2 * r, MIX_W), jnp.bfloat16)
    return pl.pallas_call(
        functools.partial(_fft_mid_kernel, conv=conv),
        grid=(r // FFT_KB,),
        in_specs=in_specs, out_specs=out_specs, out_shape=out_shape,
        compiler_params=_cparams("parallel"),
        name="fft_mid_conv" if conv else "fft_mid_filter",
    )(*args)


def _fft_out_kernel(e_ref, p_ref, mul_ref, add_ref, bias_ref, o_ref, *, time_order):
    e = e_ref[...]
    half = FFT_R // 2
    bias = bias_ref[...]
    p = jnp.swapaxes(p_ref[...].reshape(2 * FFT_R, FFT_SB, MIX_W), 0, 1)
    res = [[], []]
    for s in range(FFT_SB):
        y = jnp.dot(e, p[s], preferred_element_type=jnp.float32)
        for b in range(2):
            yb = y[b * half:(b + 1) * half] + add_ref[b, s].astype(jnp.float32) * bias
            res[b].append((mul_ref[b, s].astype(jnp.float32) * yb).astype(o_ref.dtype))
    for b in range(2):
        out = jnp.stack(res[b])
        o_ref[b] = jnp.swapaxes(out, 0, 1) if time_order else out


def fft_out(e2, p, mul, gm, add, ga, bias, time_order):
    half = FFT_R // 2

    def bspec(g):
        return pl.BlockSpec((None, 2, FFT_SB, half, MIX_W), lambda j: (g, 0, j, 0, 0))

    if time_order:
        out_spec = pl.BlockSpec((None, 2, half, FFT_SB, MIX_W), lambda j: (0, 0, 0, j, 0))
        out_shape = jax.ShapeDtypeStruct((1, 2, half, FFT_R, MIX_W), jnp.bfloat16)
    else:
        out_spec, out_shape = bspec(0), jax.ShapeDtypeStruct((1, 2, FFT_R, half, MIX_W), jnp.bfloat16)
    return pl.pallas_call(
        functools.partial(_fft_out_kernel, time_order=time_order),
        grid=(FFT_R // FFT_SB,),
        in_specs=[pl.BlockSpec((FFT_R, 2 * FFT_R), lambda j: (0, 0)),
                  pl.BlockSpec((2, FFT_R, FFT_SB, MIX_W), lambda j: (0, 0, j, 0)), bspec(gm), bspec(ga),
                  pl.BlockSpec((1, MIX_W), lambda j: (0, 0))],
        out_specs=out_spec,
        out_shape=out_shape,
        compiler_params=_cparams("parallel", vmem=FFT_OUT_VMEM_BYTES),
        name="fft_out",
    )(e2, p, mul, add, bias)


def hyena_spectra(taps, inv_norm, mats):
    return [fft_mid(mats['m1'], fft_stage_a(mats['fa_taps'], taps, order), scale=inv_norm[order])
            for order in range(2)]


def hyena_latent(xs, spectra, bias, mats):
    u, gu = xs, 2
    for order in range(2):
        p = fft_mid(mats['m1'], fft_stage_a(mats['fa_data'], u, gu), h=spectra[order])
        u, gu = fft_out(mats['e2'], p, xs, order, u, gu, bias[order].reshape(1, MIX_W), order == 1), 0
    return u.reshape(2, FFT_L // 2, MIX_W)


def hyena_context(xc, taps, inv_norm, bias):
    n = xc.shape[2]
    length = 2 * n
    k = jnp.arange(length, dtype=jnp.int32)
    c, s = _cis(k[:, None] * k[None, :], length)
    fr, fi = c[:, :n], -s[:, :n]
    m_fwd = jnp.concatenate([jnp.concatenate([fr, -fi], axis=1), jnp.concatenate([fi, fr], axis=1)], axis=0)
    m_taps = jnp.concatenate([c, -s], axis=0)
    er, ei = c[:n] / length, s[:n] / length
    m_inv = jnp.concatenate([jnp.concatenate([er, -ei], axis=1), jnp.concatenate([ei, er], axis=1)], axis=0)
    x1, x2, u = (xc[g].astype(jnp.float32) for g in range(3))
    for order, gate in enumerate((x1, x2)):
        h = mm(m_taps, taps[order], name="ctx_dft_taps") * inv_norm[order]
        w = mm(m_fwd, u.reshape(length, MIX_W), name="ctx_dft_fwd")
        wr, wi, hr, hi = w[:length], w[length:], h[:length], h[length:]
        zc = jnp.concatenate([wr * hr - wi * hi, wr * hi + wi * hr], axis=0)
        y = mm(m_inv, zc, name="ctx_dft_inv").reshape(2, n, MIX_W)
        u = gate * (y + u * bias[order])
    return u.astype(jnp.bfloat16)


def hyena_features(n, ff1, ff1_b, freq, ff2, ff2_b):
    k = jnp.arange(n, dtype=jnp.int32)
    pos = jnp.stack([k, (n - k) % n]).astype(jnp.float32)
    t = pos / (n - 1)
    fw = (2.0 * math.pi * pos / n)[..., None] * jnp.linspace(1e-4, HY_BANDS - 1, HY_BANDS, dtype=jnp.float32)
    feats = jnp.concatenate([t[..., None], jnp.cos(fw), -jnp.sin(fw)], axis=-1)
    h = jnp.sin(freq * (feats @ ff1 + ff1_b))
    return jnp.sin(freq * (h @ ff2 + ff2_b))


def _filter_kernel(feat_ref, w_ref, delta_ref, taps_ref, sum_ref, *, n, tm, split):
    dr = pl.program_id(1)
    i = pl.program_id(2)
    row = i * tm + lax.broadcasted_iota(jnp.int32, (tm, MIX_W), 0)
    lag = jnp.where(dr == 0, row, jnp.where(row == 0, 0, n - row))
    t = lag.astype(jnp.float32) / (n - 1)
    h = jnp.dot(feat_ref[...].astype(jnp.bfloat16), w_ref[...].astype(jnp.bfloat16),
                preferred_element_type=jnp.float32)
    h = h * jnp.exp(-t * delta_ref[...])

    @pl.when((dr == 0) & (i == 0))
    def _():
        sum_ref[...] = jnp.zeros_like(sum_ref)

    sum_ref[...] += jnp.sum(jnp.abs(h), axis=0, keepdims=True)
    taps = jnp.where((dr == 0) | (row != 0), h, 0.0).astype(taps_ref.dtype)
    if split:
        taps = jnp.swapaxes(taps.reshape(tm // FFT_R, FFT_R, MIX_W), 0, 1)
    taps_ref[...] = taps


def hyena_filter_taps(feats, ff3):
    _, n, nf = feats.shape
    split = 2 * n == FFT_L
    tm = FFT_SB * FFT_R if split else n
    nb = n // tm
    w = ff3.reshape(nf, 2, 2, MIX_W).transpose(1, 2, 0, 3)
    deltas = jnp.abs(jnp.linspace(math.log(HY_TARGET) / HY_SLOW_PCT, math.log(HY_TARGET) / HY_FAST_PCT,
                                  MIX_W, dtype=jnp.float32)).reshape(1, MIX_W)
    if split:
        tspec = pl.BlockSpec((None, FFT_R, FFT_SB, MIX_W), lambda o, dr, i: (o, 0, dr * nb + i, 0))
        tshape = jax.ShapeDtypeStruct((2, FFT_R, FFT_R, MIX_W), jnp.bfloat16)
    else:
        tspec = pl.BlockSpec((None, tm, MIX_W), lambda o, dr, i: (o, dr * nb + i, 0))
        tshape = jax.ShapeDtypeStruct((2, 2 * n, MIX_W), jnp.bfloat16)
    return pl.pallas_call(
        functools.partial(_filter_kernel, n=n, tm=tm, split=split),
        grid=(2, 2, nb),
        in_specs=[pl.BlockSpec((None, tm, nf), lambda o, dr, i: (dr, i, 0)),
                  pl.BlockSpec((None, None, nf, MIX_W), lambda o, dr, i: (o, dr, 0, 0)),
                  pl.BlockSpec((1, MIX_W), lambda o, dr, i: (0, 0))],
        out_specs=[tspec, pl.BlockSpec((None, 1, MIX_W), lambda o, dr, i: (o, 0, 0))],
        out_shape=[tshape, jax.ShapeDtypeStruct((2, 1, MIX_W), jnp.float32)],
        compiler_params=_cparams("arbitrary", "arbitrary", "arbitrary"),
        name="hyena_filter",
    )(feats, w, deltas)


def _merge_kernel(yhc_ref, yhl_ref, yl_ref, yg_ref, w_ref, g0_ref, g1_ref, g2_ref, o_ref, *, n_ctx_tiles):
    ys = (_pick_rows(yhc_ref, yhl_ref, n_ctx_tiles), yl_ref[...], yg_ref[...])
    acc = None
    for j, (y, g_ref) in enumerate(zip(ys, (g0_ref, g1_ref, g2_ref))):
        gate = 1.0 / (1.0 + jnp.exp(-g_ref[...].astype(jnp.float32)))
        t = gate * jnp.dot(y, w_ref[j], preferred_element_type=jnp.float32)
        acc = t if acc is None else acc + t
    o_ref[...] = acc.astype(o_ref.dtype)


def merge_branches(z, y_hy_ctx, y_hy_lat, y_lru, y_hg, w_branch, layer):
    rows = z.shape[0]
    tm = ROW_T // 2
    n_ctx_tiles = y_hy_ctx.shape[0] // tm
    yspec = pl.BlockSpec((tm, MIX_W), lambda i: (i, 0))

    def gspec(k):
        return pl.BlockSpec((tm, D_MODEL), lambda i: (i, GATE_COL0 // D_MODEL + k))

    return pl.pallas_call(
        functools.partial(_merge_kernel, n_ctx_tiles=n_ctx_tiles),
        grid=(rows // tm,),
        in_specs=[*_split_row_specs(tm, MIX_W, n_ctx_tiles), yspec, yspec,
                  pl.BlockSpec((None, N_BRANCH, MIX_W, D_MODEL), lambda i: (layer, 0, 0, 0)),
                  gspec(0), gspec(1), gspec(2)],
        out_specs=pl.BlockSpec((tm, D_MODEL), lambda i: (i, 0)),
        out_shape=jax.ShapeDtypeStruct((rows, D_MODEL), jnp.bfloat16),
        compiler_params=_cparams("parallel"),
        name="merge",
    )(y_hy_ctx, y_hy_lat, y_lru, y_hg, w_branch, z, z, z)


def _out_proj_kernel(a_ref, w_ref, *refs, n_ctx_tiles):
    *x_refs, g1_ref, gt_ref, g2_ref, sh_ref, sc_ref, wr_ref, xo_ref, h_ref, lg_ref = refs
    x_all = _pick_rows(*x_refs, n_ctx_tiles) if len(x_refs) == 2 else x_refs[0][...]
    w = w_ref[...]
    part = a_ref.shape[0] // OUT_PROJ_PARTS
    for r in range(OUT_PROJ_PARTS):
        rs = slice(r * part, (r + 1) * part)
        y = jnp.dot(a_ref[rs, :], w, preferred_element_type=jnp.float32)
        y = y * lax.rsqrt(jnp.mean(y * y, axis=-1, keepdims=True) + EPS) * g1_ref[...]
        x = x_all[rs] + gt_ref[...] * y
        xo_ref[rs, :] = x
        h = x * lax.rsqrt(jnp.mean(x * x, axis=-1, keepdims=True) + EPS) * g2_ref[...]
        h = (h * (1.0 + sc_ref[...]) + sh_ref[...]).astype(jnp.bfloat16)
        h_ref[rs, :] = h
        lg_ref[rs, :] = jnp.dot(h, wr_ref[...], preferred_element_type=jnp.float32)


def out_proj(acc, w_out, layer, x, g1, gt, g2, sh, sc, w_router, n_ctx_rows, n_lat_rows):
    tm = ROW_T // 2
    xspecs, xs, rows, d = _row_operand(x, tm)
    n_ctx_tiles = 2 * n_ctx_rows // tm
    idx = functools.partial(_mod_index, n_ctx_tiles=n_ctx_tiles, tiles_per_batch=n_lat_rows // tm)
    row = pl.BlockSpec((tm, d), lambda i: (i, 0))
    vec = pl.BlockSpec((1, d), lambda i: (0, 0))
    mod = pl.BlockSpec((None, 1, d), lambda i: (idx(i), 0, 0))
    wr = jnp.zeros((d, ROUTER_PAD), jnp.bfloat16).at[:, :N_EXPERTS].set(w_router.astype(jnp.bfloat16))
    return pl.pallas_call(
        functools.partial(_out_proj_kernel, n_ctx_tiles=n_ctx_tiles),
        grid=(rows // tm,),
        in_specs=[row, pl.BlockSpec((None, d, d), lambda i: (layer, 0, 0)), *xspecs, vec, mod, vec, mod, mod,
                  pl.BlockSpec((d, ROUTER_PAD), lambda i: (0, 0))],
        out_specs=[row, row, pl.BlockSpec((tm, ROUTER_PAD), lambda i: (i, 0))],
        out_shape=[jax.ShapeDtypeStruct((rows, d), jnp.float32), jax.ShapeDtypeStruct((rows, d), jnp.bfloat16),
                   jax.ShapeDtypeStruct((rows, ROUTER_PAD), jnp.float32)],
        compiler_params=_cparams("parallel"),
        name="out_proj",
    )(acc, w_out, *xs, g1.reshape(1, d), gt, g2.reshape(1, d), sh, sc, wr)


def _ffn_up_kernel(x_ref, wg_ref, wu_ref, h_ref):
    x = x_ref[...]
    hg = jnp.dot(x, wg_ref[...].astype(jnp.bfloat16), preferred_element_type=jnp.float32)
    hu = jnp.dot(x, wu_ref[...].astype(jnp.bfloat16), preferred_element_type=jnp.float32)
    h_ref[...] = (hg / (1.0 + jnp.exp(-hg)) * hu).astype(h_ref.dtype)


def _ffn_down_kernel(h_ref, wd_ref, g_ref, o_ref):
    y = jnp.dot(h_ref[...], wd_ref[...].astype(jnp.bfloat16), preferred_element_type=jnp.float32)
    o_ref[...] = (y * g_ref[...]).astype(o_ref.dtype)


def expert_ffn(xe, w_gate, w_up, w_down, layer, g):
    ne, r, d = xe.shape
    dff = w_gate.shape[-1]
    hid = pl.pallas_call(
        _ffn_up_kernel,
        grid=(ne, dff // FFN_TF),
        in_specs=[pl.BlockSpec((None, r, d), lambda e, f: (e, 0, 0)),
                  pl.BlockSpec((None, None, d, FFN_TF), lambda e, f: (layer, e, 0, f)),
                  pl.BlockSpec((None, None, d, FFN_TF), lambda e, f: (layer, e, 0, f))],
        out_specs=pl.BlockSpec((None, r, FFN_TF), lambda e, f: (e, 0, f)),
        out_shape=jax.ShapeDtypeStruct((ne, r, dff), jnp.bfloat16),
        compiler_params=_cparams("parallel", "arbitrary"),
        name="ffn_up",
    )(xe, w_gate, w_up)
    return pl.pallas_call(
        _ffn_down_kernel,
        grid=(ne, d // FFN_TD),
        in_specs=[pl.BlockSpec((None, r, dff), lambda e, j: (e, 0, 0)),
                  pl.BlockSpec((None, None, dff, FFN_TD), lambda e, j: (layer, e, 0, j)),
                  pl.BlockSpec((None, r, 1), lambda e, j: (e, 0, 0))],
        out_specs=pl.BlockSpec((None, r, FFN_TD), lambda e, j: (e, 0, j)),
        out_shape=jax.ShapeDtypeStruct((ne, r, d), jnp.bfloat16),
        compiler_params=_cparams("parallel", "arbitrary"),
        name="ffn_down",
    )(hid, w_down, g)


def route(logits, row0, bsz, n):
    cap = EC_CAPACITY * n // N_EXPERTS
    aff = jax.nn.softmax(logits[row0:row0 + bsz * n, :N_EXPERTS].reshape(bsz, n, N_EXPERTS), axis=-1)
    g, idx = lax.top_k(jnp.swapaxes(aff, 1, 2), cap)
    flat = idx + (row0 + jnp.arange(bsz, dtype=idx.dtype) * n)[:, None, None]
    return (jnp.swapaxes(g, 0, 1).reshape(N_EXPERTS, bsz * cap),
            jnp.swapaxes(flat, 0, 1).reshape(N_EXPERTS, bsz * cap))


def ec_moe(h, logits, w_gate, w_up, w_down, layer, segments):
    gs, flats = zip(*(route(logits, *seg) for seg in segments))
    flat, g = lax.sort_key_val(jnp.concatenate(flats, axis=1), jnp.concatenate(gs, axis=1), dimension=1)
    return expert_ffn(h[flat], w_gate, w_up, w_down, layer, g[..., None]), flat


def moe_plan(flat, n_rows):
    ne, r = flat.shape
    nwin = r // MOE_WIN
    n_tiles = n_rows // MOE_TT
    bounds = jnp.arange(n_tiles + 1, dtype=flat.dtype) * MOE_TT
    p = jnp.sum(flat[None, :, :] < bounds[:, None, None], axis=2, dtype=jnp.int32)
    p0, p1 = p[:-1], p[1:]
    w0 = p0 // MOE_WIN
    nw = jnp.where(p1 > p0, (p1 - 1) // MOE_WIN - w0 + 1, 0)
    off = jnp.cumsum(nw, axis=1)
    q = jnp.arange(MOE_MAXQ, dtype=jnp.int32)
    e_of_q = jnp.minimum(jnp.sum(q[None, :, None] >= off[:, None, :], axis=2, dtype=jnp.int32), ne - 1)
    pick = e_of_q[:, :, None] == jnp.arange(ne, dtype=jnp.int32)
    w = q[None, :] + jnp.sum(jnp.where(pick, (w0 - (off - nw))[:, None, :], 0), axis=2)
    items = jnp.clip(e_of_q * nwin + w, 0, ne * nwin - 1)
    return items.astype(jnp.int32), off[:, -1].astype(jnp.int32)


def _combine_kernel(items_ref, count_ref, tok_ref, ye_ref, x_ref, g_ref, gt_ref, *rest, nxt, skip):
    if nxt:
        gn_ref, sh_ref, sc_ref, o_ref, u_ref, stage, sem, acc_ref = rest
    else:
        o_ref, stage, sem, acc_ref = rest
    i = pl.program_id(0)
    t = i + skip
    n = count_ref[t]
    n_groups = (n + MOE_GROUP - 1) // MOE_GROUP

    @pl.when(i == 0)
    def _():
        stage[...] = jnp.zeros_like(stage)

    def window_copy(g, buf, k):
        item = items_ref[t, jnp.minimum(MOE_GROUP * g + k, MOE_MAXQ - 1)]
        return pltpu.make_async_copy(ye_ref.at[pl.ds(item * MOE_WIN, MOE_WIN)],
                                     stage.at[buf, pl.ds(k * MOE_WIN, MOE_WIN)], sem.at[buf])

    def for_group(g, buf, wait):
        for k in range(MOE_GROUP):
            @pl.when(MOE_GROUP * g + k < n)
            def _():
                cp = window_copy(g, buf, k)
                cp.wait() if wait else cp.start()

    acc_ref[...] = jnp.zeros_like(acc_ref)

    for j in range(MOE_NBUF - 1):
        @pl.when(j < n_groups)
        def _():
            for_group(j, j, False)

    row = t * MOE_TT + lax.broadcasted_iota(jnp.int32, (MOE_TT, MOE_GROUP * MOE_WIN), 0)

    def body(g, carry):
        buf = g % MOE_NBUF
        ahead = g + MOE_NBUF - 1

        @pl.when(ahead < n_groups)
        def _():
            for_group(ahead, ahead % MOE_NBUF, False)

        for_group(g, buf, True)
        toks = []
        for k in range(MOE_GROUP):
            qk = MOE_GROUP * g + k
            item = items_ref[t, jnp.minimum(qk, MOE_MAXQ - 1)]
            toks.append(jnp.where(qk < n, tok_ref[pl.ds(item, 1), :], -1))
        onehot = jnp.where(row == jnp.concatenate(toks, axis=1), 1.0, 0.0).astype(jnp.bfloat16)
        acc_ref[...] += jnp.dot(onehot, stage[buf], preferred_element_type=jnp.float32)
        return carry

    lax.fori_loop(0, n_groups, body, 0)

    y = acc_ref[...]
    y = y * lax.rsqrt(jnp.mean(y * y, axis=-1, keepdims=True) + EPS) * g_ref[...]
    x = x_ref[...] + gt_ref[...] * y
    if nxt:
        u = x * lax.rsqrt(jnp.mean(x * x, axis=-1, keepdims=True) + EPS) * gn_ref[...]
        u_ref[...] = (u * (1.0 + sc_ref[...]) + sh_ref[...]).astype(u_ref.dtype)
    o_ref[...] = x


def moe_combine(x, ye, flat, g, gt, n_ctx_rows, n_lat_rows, nxt=None):
    rows, d = x.shape
    ne, r, _ = ye.shape
    items, count = moe_plan(flat, rows)
    n_ctx_tiles = 2 * n_ctx_rows // MOE_TT
    idx = functools.partial(_mod_index, n_ctx_tiles=n_ctx_tiles, tiles_per_batch=n_lat_rows // MOE_TT)
    skip = 0 if nxt else n_ctx_tiles
    row = pl.BlockSpec((MOE_TT, d), lambda i, *_: (i + skip, 0))
    vec = pl.BlockSpec((1, d), lambda i, *_: (0, 0))
    mod = pl.BlockSpec((None, 1, d), lambda i, *_: (idx(i + skip), 0, 0))
    out_row = pl.BlockSpec((MOE_TT, d), lambda i, *_: (i, 0))
    n_out = rows - skip * MOE_TT
    in_specs = [pl.BlockSpec((ne * r // MOE_WIN, MOE_WIN), lambda i, *_: (0, 0)),
                pl.BlockSpec(memory_space=pl.ANY), row, vec, mod]
    args = [flat.reshape(ne * r // MOE_WIN, MOE_WIN), ye.reshape(ne * r, d), x, g.reshape(1, d), gt]
    out_specs, out_shape = out_row, jax.ShapeDtypeStruct((n_out, d), jnp.float32)
    if nxt:
        in_specs += [vec, mod, mod]
        args += [nxt[0].reshape(1, d), nxt[1], nxt[2]]
        out_specs = [out_row, out_row]
        out_shape = [out_shape, jax.ShapeDtypeStruct((n_out, d), jnp.bfloat16)]
    return pl.pallas_call(
        functools.partial(_combine_kernel, nxt=bool(nxt), skip=skip),
        grid_spec=pltpu.PrefetchScalarGridSpec(
            num_scalar_prefetch=2,
            grid=(n_out // MOE_TT,),
            in_specs=in_specs, out_specs=out_specs,
            scratch_shapes=[pltpu.VMEM((MOE_NBUF, MOE_GROUP * MOE_WIN, d), jnp.bfloat16),
                            pltpu.SemaphoreType.DMA((MOE_NBUF,)),
                            pltpu.VMEM((MOE_TT, d), jnp.float32)]),
        out_shape=out_shape,
        compiler_params=_cparams("arbitrary"),
        name="moe_combine",
    )(items, count, *args)


def kernel(x, c, ctx, c_ctx, w_mod, b_mod, norm_g, w_in, hy_conv_w, hy_conv_b, hy_ff1, hy_ff1_b, hy_freq,
           hy_ff2, hy_ff2_b, hy_ff3, hy_bias, lru_conv_w, lru_conv_b, lru_wa, lru_ba, lru_wi, lru_bi,
           lru_lambda, hg_lb_logits, hg_norm_g, w_branch, w_out, w_router, w_gate, w_up, w_down):
    bsz, n_lat, d = x.shape
    n_ctx = ctx.shape[1]
    assert bsz == 2 and d == D_MODEL and n_ctx == SEQ_T and 2 * n_lat == FFT_L and n_lat % ROW_T == 0
    gam = jax.nn.softmax(hg_lb_logits.astype(jnp.float32), axis=0)
    lb_all = jnp.maximum(jnp.cumsum(gam, axis=0) - gam[:1], 0.0)
    mats = fft_matrices()
    cvec = jnp.zeros((SUBLANES, d), jnp.float32).at[:bsz].set(jax.nn.silu(c)).at[bsz].set(jax.nn.silu(c_ctx))
    n_ctx_all = bsz * n_ctx
    xa = (ctx.reshape(n_ctx_all, d), x.reshape(bsz * n_lat, d))
    w_branch_b, w_out_b = w_branch.astype(jnp.bfloat16), w_out.astype(jnp.bfloat16)
    mods, filters = [], []
    for l in range(DEPTH):
        mod = mm(cvec, w_mod, layer=l, name="adaln")[:bsz + 1] + b_mod[l]
        mods.append([mod.reshape(bsz + 1, 1, 6, d)[:, :, k] for k in range(6)])
        ffp = (hy_ff1[l], hy_ff1_b[l], hy_freq[l], hy_ff2[l], hy_ff2_b[l])
        taps_l, norm_l = hyena_filter_taps(hyena_features(n_lat, *ffp), hy_ff3[l])
        taps_c, norm_c = hyena_filter_taps(hyena_features(n_ctx, *ffp), hy_ff3[l])
        filters.append((hyena_spectra(taps_l, 1.0 / norm_l, mats), taps_c, 1.0 / norm_c))
    u = prenorm(xa, norm_g[0, 0], mods[0][0], mods[0][1], n_ctx, n_lat)
    for l in range(DEPTH):
        last = l == DEPTH - 1
        p = {'lru_conv_w': lru_conv_w[l], 'lru_conv_b': lru_conv_b[l], 'lru_wa': lru_wa[l], 'lru_ba': lru_ba[l],
             'lru_wi': lru_wi[l], 'lru_bi': lru_bi[l], 'lru_lambda': lru_lambda[l]}
        sh1, sc1, gt1, sh2, sc2, gt2 = mods[l]

        z = mm(u, w_in, jnp.bfloat16, tm=IN_PROJ_TM, tn=1024, layer=l, name="in_proj")

        y_lru = lru_branch(z, p, n_ctx, n_lat)
        y_hg = hgrn_branch(z, lb_all[l], hg_norm_g[l], n_ctx, n_lat)

        spectra, taps_c, inv_norm_c = filters[l]
        xc_l = hy_conv_latent(z, hy_conv_w[l], hy_conv_b[l], n_ctx_all, n_lat)
        y_hy_l = hyena_latent(xc_l, spectra, hy_bias[l], mats)
        xc_c = hy_conv(z, hy_conv_w[l], hy_conv_b[l], 0, n_ctx, True)
        y_hy_c = hyena_context(xc_c, taps_c, inv_norm_c, hy_bias[l])

        acc = merge_branches(z, y_hy_c.reshape(n_ctx_all, MIX_W), y_hy_l.reshape(bsz * n_lat, MIX_W), y_lru, y_hg,
                             w_branch_b, l)
        xa, h, logits = out_proj(acc, w_out_b, l, xa, norm_g[l, 1], gt1, norm_g[l, 2], sh2, sc2,
                                 w_router[l], n_ctx, n_lat)
        segments = [(n_ctx_all, bsz, n_lat)] if last else [(n_ctx_all, bsz, n_lat), (0, bsz, n_ctx)]
        ye, flat = ec_moe(h, logits, w_gate, w_up, w_down, l, segments)
        if last:
            xa = moe_combine(xa, ye, flat, norm_g[l, 3], gt2, n_ctx, n_lat)
        else:
            nxt = (norm_g[l + 1, 0], mods[l + 1][0], mods[l + 1][1])
            xa, u = moe_combine(xa, ye, flat, norm_g[l, 3], gt2, n_ctx, n_lat, nxt)
    return xa.reshape(bsz, n_lat, d)
```

```python
import functools
import math

import numpy as np
import jax
import jax.numpy as jnp
from jax import lax
from jax.experimental import pallas as pl
from jax.experimental.pallas import tpu as pltpu

D_MODEL = 2048
DEPTH = 2
GRID_W = 64
MIX_W = D_MODEL // 2
N_BRANCH = 3
HY_BANDS = 16
HY_TARGET = 1e-2
HY_FAST_PCT = 0.3
HY_SLOW_PCT = 1.5
LRU_BLOCKS = 8
LRU_BS = MIX_W // LRU_BLOCKS
LRU_C = 8.0
HG_HEADS = 8
HG_DK = MIX_W // HG_HEADS
N_EXPERTS = 16
EC_CAPACITY = 2
EPS = 1e-6
TINY = 1e-30
LOG2E = 1.4426950408889634
HY_COL0 = 0
LRU_COL0 = 3 * MIX_W
HG_COL0 = 5 * MIX_W
GATE_COL0 = 10 * MIX_W
IN_COLS = GATE_COL0 + N_BRANCH * D_MODEL

SUBLANES = 8
LANES = 128
SEQ_T = 256
HG_C = 128
HG_LEVELS = 7
HG_HB = 8
ROW_T = 512
IN_PROJ_TM = 2816
IN_PROJ_TN = 512
FFT_R = 128
FFT_L = FFT_R * FFT_R
FFT_KB = 8
FFT_SB = 16
HY_PARTS = 4
ROUTER_PAD = LANES
FFN_TF = 256
FFN_TD = 512
MOE_TT = 512
MOE_WIN = 32
MOE_GROUP = 16
MOE_NBUF = 3
MOE_MAXQ = N_EXPERTS * (MOE_TT // MOE_WIN + 1)
VMEM_LIMIT_BYTES = 48 * 1024 * 1024
FFT_OUT_VMEM_BYTES = 56 * 1024 * 1024


def _cparams(*sem, vmem=VMEM_LIMIT_BYTES):
    return pltpu.CompilerParams(dimension_semantics=sem, vmem_limit_bytes=vmem)


def _pick_tile(n, pref):
    for t in (pref, 1024, 512, 256, 128):
        if t <= n and n % t == 0:
            return t
    return n


def _row_block(b, j, n_ctx, n_lat, reverse):
    if reverse:
        kc, kl = n_ctx - 1 - j, n_ctx + n_lat - 1 - j
    else:
        kc, kl = j, j - n_ctx
    return jnp.where(j < n_ctx, b * n_ctx + kc, 2 * n_ctx + b * n_lat + kl)


def _mm_kernel(a_ref, b_ref, o_ref):
    o_ref[...] = jnp.dot(a_ref[...].astype(jnp.bfloat16), b_ref[...].astype(jnp.bfloat16),
                         preferred_element_type=jnp.float32).astype(o_ref.dtype)


def mm(a, b, out_dtype=jnp.float32, tm=512, tn=1024, layer=None, name="mm"):
    m, k = a.shape
    n = b.shape[-1]
    tm = _pick_tile(m, tm)
    tn = _pick_tile(n, tn)
    if layer is None:
        bspec = pl.BlockSpec((k, tn), lambda i, j: (0, j))
    else:
        bspec = pl.BlockSpec((None, k, tn), lambda i, j: (layer, 0, j))
    return pl.pallas_call(
        _mm_kernel,
        grid=(m // tm, n // tn),
        in_specs=[pl.BlockSpec((tm, k), lambda i, j: (i, 0)), bspec],
        out_specs=pl.BlockSpec((tm, tn), lambda i, j: (i, j)),
        out_shape=jax.ShapeDtypeStruct((m, n), out_dtype),
        compiler_params=_cparams("parallel", "arbitrary"),
        name=name,
    )(a, b)


def _mod_index(i, n_ctx_tiles, tiles_per_batch):
    return jnp.where(i < n_ctx_tiles, 2, (i - n_ctx_tiles) // tiles_per_batch)


def _split_row_specs(tile, width, n_ctx_tiles):
    return (pl.BlockSpec((tile, width), lambda i: (jnp.minimum(i, n_ctx_tiles - 1), 0)),
            pl.BlockSpec((tile, width), lambda i: (jnp.maximum(i - n_ctx_tiles, 0), 0)))


def _pick_rows(ctx_ref, lat_ref, n_ctx_tiles):
    return jnp.where(pl.program_id(0) < n_ctx_tiles, ctx_ref[...], lat_ref[...])


def _row_operand(x, tile):
    if isinstance(x, tuple):
        d = x[0].shape[1]
        return list(_split_row_specs(tile, d, x[0].shape[0] // tile)), list(x), x[0].shape[0] + x[1].shape[0], d
    return [pl.BlockSpec((tile, x.shape[1]), lambda i: (i, 0))], [x], x.shape[0], x.shape[1]


def _prenorm_kernel(*refs, n_ctx_tiles):
    *x_refs, g_ref, sh_ref, sc_ref, u_ref = refs
    x = _pick_rows(*x_refs, n_ctx_tiles) if len(x_refs) == 2 else x_refs[0][...]
    y = x * lax.rsqrt(jnp.mean(x * x, axis=-1, keepdims=True) + EPS) * g_ref[...]
    u_ref[...] = (y * (1.0 + sc_ref[...]) + sh_ref[...]).astype(u_ref.dtype)


def prenorm(x, g, sh, sc, n_ctx_rows, n_lat_rows):
    xspecs, xs, rows, d = _row_operand(x, ROW_T)
    n_ctx_tiles = 2 * n_ctx_rows // ROW_T
    idx = functools.partial(_mod_index, n_ctx_tiles=n_ctx_tiles, tiles_per_batch=n_lat_rows // ROW_T)
    vec = pl.BlockSpec((None, 1, d), lambda i: (idx(i), 0, 0))
    return pl.pallas_call(
        functools.partial(_prenorm_kernel, n_ctx_tiles=n_ctx_tiles),
        grid=(rows // ROW_T,),
        in_specs=[*xspecs, pl.BlockSpec((1, d), lambda i: (0, 0)), vec, vec],
        out_specs=pl.BlockSpec((ROW_T, d), lambda i: (i, 0)),
        out_shape=jax.ShapeDtypeStruct((rows, d), jnp.bfloat16),
        compiler_params=_cparams("parallel"),
        name="prenorm",
    )(*xs, g.reshape(1, d), sh, sc)


def _gelu_tanh(x):
    return 0.5 * x * (1.0 + jnp.tanh(math.sqrt(2.0 / math.pi) * (x + 0.044715 * (x * x * x))))


def _conv_pos(is_ctx, shape):
    t = lax.broadcasted_iota(jnp.int32, shape, 0)
    pos = jnp.where(is_ctx, t, t & (GRID_W - 1))
    last = jnp.where(is_ctx, SEQ_T - 1, GRID_W - 1)
    return pos, last


def _lru_pass_kernel(x_ref, cw_ref, cb_ref, w_ref, gb_ref, lam_ref, *rest, reverse, final):
    if final:
        lg_ref, hprev_ref, out_ref, h_ref, a_ref, b_ref = rest
    else:
        out_ref, h_ref, a_ref, b_ref = rest
    j = pl.program_id(1)

    @pl.when(j == 0)
    def _():
        h_ref[...] = jnp.zeros_like(h_ref)

    x = x_ref[...].astype(jnp.float32)
    pos, last = _conv_pos(j == 0, x.shape)
    cw = cw_ref[...]
    xc = cb_ref[...] + cw[2:3] * x
    xc = xc + cw[0:1] * jnp.where(pos >= 2, pltpu.roll(x, 2, 0), 0.0)
    xc = xc + cw[1:2] * jnp.where(pos >= 1, pltpu.roll(x, 1, 0), 0.0)
    xc = xc + cw[3:4] * jnp.where(pos < last, pltpu.roll(x, SEQ_T - 1, 0), 0.0)

    xb = xc.astype(jnp.bfloat16)
    lam = lam_ref[...]
    sp = jnp.maximum(-lam, 0.0) + jnp.log(1.0 + jnp.exp(-jnp.abs(lam)))
    n_tiles = SEQ_T // SUBLANES
    row = lax.broadcasted_iota(jnp.int32, (1, SUBLANES, LRU_BS), 1)
    for n in range(LRU_BLOCKS):
        sl = slice(n * LRU_BS, (n + 1) * LRU_BS)
        pre = jnp.dot(xb[:, sl], w_ref[n], preferred_element_type=jnp.float32) + gb_ref[n]
        gate = 1.0 / (1.0 + jnp.exp(-pre))
        r, i = gate[:, :LRU_BS], gate[:, LRU_BS:]
        a = jnp.exp(-LRU_C * r * sp[:, sl])
        om = 1.0 - a * a
        b = om * lax.rsqrt(jnp.maximum(om, TINY)) * (i * xc[:, sl])
        a = a.reshape(n_tiles, SUBLANES, LRU_BS)
        b = b.reshape(n_tiles, SUBLANES, LRU_BS)
        for d in (1, 2, 4):
            if reverse:
                m = row < SUBLANES - d
                a_sh, b_sh = pltpu.roll(a, SUBLANES - d, 1), pltpu.roll(b, SUBLANES - d, 1)
            else:
                m = row >= d
                a_sh, b_sh = pltpu.roll(a, d, 1), pltpu.roll(b, d, 1)
            b = jnp.where(m, a * b_sh + b, b)
            a = jnp.where(m, a * a_sh, a)
        a_ref[:, sl] = a.reshape(SEQ_T, LRU_BS)
        b_ref[:, sl] = b.reshape(SEQ_T, LRU_BS)

    h = h_ref[...]
    for k in range(n_tiles):
        kk = n_tiles - 1 - k if reverse else k
        rs = slice(kk * SUBLANES, (kk + 1) * SUBLANES)
        ht = a_ref[rs, :] * h + b_ref[rs, :]
        h = ht[0:1, :] if reverse else ht[SUBLANES - 1:SUBLANES, :]
        if final:
            ht = (ht + hprev_ref[rs, :]) * _gelu_tanh(lg_ref[rs, :].astype(jnp.float32))
        out_ref[rs, :] = ht.astype(out_ref.dtype)
    h_ref[...] = h


def lru_pass(z, conv_w, conv_b, wa, ba, wi, bi, lam, n_ctx_rows, n_lat_rows, reverse, h_prev=None):
    rows = z.shape[0]
    n_lat = n_lat_rows // SEQ_T
    cb = LRU_COL0 // MIX_W
    final = h_prev is not None
    rb = functools.partial(_row_block, n_ctx=n_ctx_rows // SEQ_T, n_lat=n_lat, reverse=reverse)

    def zspec(k):
        return pl.BlockSpec((SEQ_T, MIX_W), lambda b, j: (rb(b, j), cb + k))

    row_spec = pl.BlockSpec((SEQ_T, MIX_W), lambda b, j: (rb(b, j), 0))

    def const_spec(shape):
        return pl.BlockSpec(shape, lambda b, j: (0,) * len(shape))

    w2 = jnp.concatenate([wa, wi], axis=-1).astype(jnp.bfloat16)
    gb = jnp.concatenate([ba.reshape(LRU_BLOCKS, 1, LRU_BS), bi.reshape(LRU_BLOCKS, 1, LRU_BS)], axis=-1)
    in_specs = [zspec(0), const_spec((4, MIX_W)), const_spec((1, MIX_W)),
                const_spec((LRU_BLOCKS, LRU_BS, 2 * LRU_BS)), const_spec((LRU_BLOCKS, 1, 2 * LRU_BS)),
                const_spec((1, MIX_W))]
    args = [z, conv_w, conv_b.reshape(1, MIX_W), w2, gb, lam.reshape(1, MIX_W)]
    if final:
        in_specs += [zspec(1), row_spec]
        args += [z, h_prev]
    return pl.pallas_call(
        functools.partial(_lru_pass_kernel, reverse=reverse, final=final),
        grid=(2, n_ctx_rows // SEQ_T + n_lat),
        in_specs=in_specs,
        out_specs=row_spec,
        out_shape=jax.ShapeDtypeStruct((rows, MIX_W), jnp.bfloat16 if final else jnp.float32),
        scratch_shapes=[pltpu.VMEM((1, MIX_W), jnp.float32),
                        pltpu.VMEM((SEQ_T, MIX_W), jnp.float32),
                        pltpu.VMEM((SEQ_T, MIX_W), jnp.float32)],
        compiler_params=_cparams("parallel", "arbitrary"),
        name="lru_bwd" if reverse else "lru_fwd",
    )(*args)


def lru_branch(z, p, n_ctx_rows, n_lat_rows):
    def one(d, h_prev):
        return lru_pass(z, p['lru_conv_w'], p['lru_conv_b'], p['lru_wa'][d], p['lru_ba'][d], p['lru_wi'][d],
                        p['lru_bi'][d], p['lru_lambda'][d], n_ctx_rows, n_lat_rows, d == 1, h_prev)
    return one(1, one(0, None))


def _hg_level_matrix(reverse):
    t = np.arange(HG_C)[:, None]
    s = np.arange(HG_C)[None, :]
    x = t ^ s
    lv = np.where(x > 0, np.floor(np.log2(np.maximum(x, 1))).astype(np.int32), -1)
    lv = np.where(s < t, lv, -1).astype(np.int32)
    return jnp.asarray(lv.T if reverse else lv)


def _hg_pass_kernel(lv_ref, lb_ref, q_ref, f_ref, v_ref, *rest, reverse, final):
    if final:
        og_ref, oprev_ref, g_ref, out_ref, st_ref = rest
    else:
        out_ref, st_ref = rest

    @pl.when(pl.program_id(1) == 0)
    def _():
        st_ref[...] = jnp.zeros_like(st_ref)

    lv = lv_ref[...]
    at_level = [lv == lvl for lvl in range(HG_LEVELS)]
    n_tiles = HG_C // SUBLANES
    wb = HG_HB * HG_DK
    row = lax.broadcasted_iota(jnp.int32, (1, SUBLANES, wb), 1)
    for hb in range(HG_HEADS // HG_HB):
        sl = slice(hb * wb, (hb + 1) * wb)
        heads = [(hb * HG_HB + j, slice(j * HG_DK, (j + 1) * HG_DK)) for j in range(HG_HB)]
        x = f_ref[:, sl].astype(jnp.float32)
        lb = lb_ref[:, sl]
        e = jnp.exp(-jnp.abs(x))
        inv = 1.0 / (1.0 + e)
        pos = x >= 0
        sig = jnp.where(pos, inv, e * inv)
        sigm = jnp.where(pos, e * inv, inv)
        lsig = jnp.minimum(x, 0.0) - jnp.log(1.0 + e)
        lf = jnp.where(lb > 0, jnp.log(lb + (1.0 - lb) * sig), lsig)
        kk = (1.0 - lb) * sigm
        q = q_ref[:, sl].astype(jnp.float32)
        v = v_ref[:, sl]

        c = (lf * LOG2E).reshape(n_tiles, SUBLANES, wb)
        bt = c
        att = [None] * HG_HB
        for lvl in range(HG_LEVELS):
            m = 1 << lvl
            qm = (q * jnp.exp2(c).reshape(HG_C, wb)).astype(jnp.bfloat16)
            km = (kk * jnp.exp2(bt - c).reshape(HG_C, wb)).astype(jnp.bfloat16)
            for j, (_, hs) in enumerate(heads):
                d = lax.dot_general(qm[:, hs], km[:, hs], (((1,), (1,)), ((), ())),
                                    preferred_element_type=jnp.float32)
                att[j] = jnp.where(at_level[lvl], d, 0.0 if lvl == 0 else att[j])
            if m < SUBLANES:
                upper = (row & m) != 0
                down = pltpu.roll(bt, m, 1)
                up = pltpu.roll(bt, SUBLANES - m, 1)
                if reverse:
                    c = c + jnp.where(upper, 0.0, up)
                else:
                    c = c + jnp.where(upper, down, 0.0)
                bt = bt + jnp.where(upper, down, up)
            else:
                k = m // SUBLANES
                pair = (n_tiles // (2 * k), 2, k, SUBLANES, wb)
                c5, b5 = c.reshape(pair), bt.reshape(pair)
                tot2 = b5[:, 0] + b5[:, 1]
                if reverse:
                    c = jnp.stack([c5[:, 0] + b5[:, 1], c5[:, 1]], axis=1)
                else:
                    c = jnp.stack([c5[:, 0], c5[:, 1] + b5[:, 0]], axis=1)
                c = c.reshape(n_tiles, SUBLANES, wb)
                bt = jnp.stack([tot2, tot2], axis=1).reshape(n_tiles, SUBLANES, wb)

        qm = (q * jnp.exp2(c).reshape(HG_C, wb)).astype(jnp.bfloat16)
        km = (kk * jnp.exp2(bt - c).reshape(HG_C, wb)).astype(jnp.bfloat16)
        tot = jnp.exp2(bt[0, 0:1, :])
        diag = q * kk
        for j, (h, hs) in enumerate(heads):
            osl = slice(h * HG_DK, (h + 1) * HG_DK)
            st = st_ref[h]
            vh = v[:, hs]
            o = lax.dot_general(qm[:, hs], st.astype(jnp.bfloat16), (((1,), (1,)), ((), ())),
                                preferred_element_type=jnp.float32)
            o = o + jnp.dot(att[j].astype(jnp.bfloat16), vh, preferred_element_type=jnp.float32)
            o = o + jnp.sum(diag[:, hs], axis=-1, keepdims=True) * vh.astype(jnp.float32)
            st_ref[h] = st * tot[:, hs] + lax.dot_general(vh, km[:, hs], (((0,), (0,)), ((), ())),
                                                          preferred_element_type=jnp.float32)
            if final:
                o = o + oprev_ref[:, osl]
                y = o * lax.rsqrt(jnp.mean(o * o, axis=-1, keepdims=True) + EPS) * g_ref[:, osl]
                og = og_ref[:, osl].astype(jnp.float32)
                out_ref[:, osl] = (y * (og / (1.0 + jnp.exp(-og)))).astype(out_ref.dtype)
            else:
                out_ref[:, osl] = o


def hgrn_pass(z, lb, n_ctx_rows, n_lat_rows, reverse, o_prev=None, norm_g=None):
    rows = z.shape[0]
    n_ctx, n_lat = n_ctx_rows // HG_C, n_lat_rows // HG_C
    cb = HG_COL0 // MIX_W
    final = o_prev is not None
    rb = functools.partial(_row_block, n_ctx=n_ctx, n_lat=n_lat, reverse=reverse)

    def zspec(k):
        return pl.BlockSpec((HG_C, MIX_W), lambda b, j: (rb(b, j), cb + k))

    row_spec = pl.BlockSpec((HG_C, MIX_W), lambda b, j: (rb(b, j), 0))
    vec_spec = pl.BlockSpec((1, MIX_W), lambda b, j: (0, 0))
    in_specs = [pl.BlockSpec((HG_C, HG_C), lambda b, j: (0, 0)), vec_spec,
                zspec(0), zspec(2 if reverse else 1), zspec(3)]
    args = [_hg_level_matrix(reverse), lb.reshape(1, MIX_W), z, z, z]
    if final:
        in_specs += [zspec(4), row_spec, vec_spec]
        args += [z, o_prev, norm_g.reshape(1, MIX_W)]
    return pl.pallas_call(
        functools.partial(_hg_pass_kernel, reverse=reverse, final=final),
        grid=(2, n_ctx + n_lat),
        in_specs=in_specs,
        out_specs=row_spec,
        out_shape=jax.ShapeDtypeStruct((rows, MIX_W), jnp.bfloat16 if final else jnp.float32),
        scratch_shapes=[pltpu.VMEM((HG_HEADS, HG_DK, HG_DK), jnp.float32)],
        compiler_params=_cparams("parallel", "arbitrary"),
        name="hgrn_bwd" if reverse else "hgrn_fwd",
    )(*args)


def hgrn_branch(z, lb, norm_g, n_ctx_rows, n_lat_rows):
    o_f = hgrn_pass(z, lb, n_ctx_rows, n_lat_rows, False)
    return hgrn_pass(z, lb, n_ctx_rows, n_lat_rows, True, o_f, norm_g)


def _hy_conv_kernel(x_ref, w_ref, b_ref, out_ref, *, is_ctx):
    x = x_ref[...].astype(jnp.float32)
    t = lax.broadcasted_iota(jnp.int32, x.shape, 0)
    pos = t if is_ctx else t & (GRID_W - 1)
    last = SEQ_T - 1 if is_ctx else GRID_W - 1
    w = w_ref[...]
    y = b_ref[...] + w[1:2] * x
    y = y + w[0:1] * jnp.where(pos >= 1, pltpu.roll(x, 1, 0), 0.0)
    y = y + w[2:3] * jnp.where(pos < last, pltpu.roll(x, SEQ_T - 1, 0), 0.0)
    for g in range(3):
        out_ref[g] = y[:, g * MIX_W:(g + 1) * MIX_W].astype(out_ref.dtype)


def hy_conv(z, conv_w, conv_b, block0, n_rows, is_ctx):
    nb = n_rows // SEQ_T
    return pl.pallas_call(
        functools.partial(_hy_conv_kernel, is_ctx=is_ctx),
        grid=(2, nb),
        in_specs=[pl.BlockSpec((SEQ_T, 3 * MIX_W), lambda b, j: (block0 + b * nb + j, 0)),
                  pl.BlockSpec((3, 3 * MIX_W), lambda b, j: (0, 0)),
                  pl.BlockSpec((1, 3 * MIX_W), lambda b, j: (0, 0))],
        out_specs=pl.BlockSpec((3, None, SEQ_T, MIX_W), lambda b, j: (0, b, j, 0)),
        out_shape=jax.ShapeDtypeStruct((3, 2, n_rows, MIX_W), jnp.bfloat16),
        compiler_params=_cparams("parallel", "arbitrary"),
        name="hy_conv_ctx" if is_ctx else "hy_conv_lat",
    )(z, conv_w, conv_b.reshape(1, 3 * MIX_W))


def _hy_conv_lat_kernel(*refs):
    x_refs, (w_ref, b_ref, out_ref) = refs[:HY_PARTS], refs[HY_PARTS:]
    x = jnp.concatenate([r[...] for r in x_refs], axis=0).astype(jnp.float32)
    n = x.shape[0]
    pos = lax.broadcasted_iota(jnp.int32, x.shape, 0) & (GRID_W - 1)
    w = w_ref[...]
    y = b_ref[...] + w[1:2] * x
    y = y + w[0:1] * jnp.where(pos >= 1, pltpu.roll(x, 1, 0), 0.0)
    y = y + w[2:3] * jnp.where(pos < GRID_W - 1, pltpu.roll(x, n - 1, 0), 0.0)
    y = y.astype(out_ref.dtype).reshape(n // FFT_R, FFT_R, MIX_W)
    out_ref[...] = jnp.swapaxes(y, 0, 1)


def hy_conv_latent(z, conv_w, conv_b, row0, n_rows):
    part = FFT_SB * FFT_R // HY_PARTS
    nb = n_rows // (FFT_SB * FFT_R)

    def xspec(k):
        return pl.BlockSpec((part, MIX_W), lambda g, b, j: ((row0 + b * n_rows) // part + j * HY_PARTS + k, g))

    return pl.pallas_call(
        _hy_conv_lat_kernel,
        grid=(3, 2, nb),
        in_specs=[xspec(k) for k in range(HY_PARTS)] + [
            pl.BlockSpec((3, MIX_W), lambda g, b, j: (0, g)), pl.BlockSpec((1, MIX_W), lambda g, b, j: (0, g))],
        out_specs=pl.BlockSpec((None, None, FFT_R, FFT_SB, MIX_W), lambda g, b, j: (g, b, 0, j, 0)),
        out_shape=jax.ShapeDtypeStruct((3, 2, FFT_R, n_rows // FFT_R, MIX_W), jnp.bfloat16),
        compiler_params=_cparams("parallel", "parallel", "arbitrary"),
        name="hy_conv_lat",
    )(*([z] * HY_PARTS), conv_w, conv_b.reshape(1, 3 * MIX_W))


def _cis(num, den):
    ang = (2.0 * math.pi / den) * (num % den).astype(jnp.float32)
    return jnp.cos(ang), jnp.sin(ang)


def fft_matrices():
    r = FFT_R
    i = jnp.arange(r, dtype=jnp.int32)
    c, s = _cis(i[:, None] * i[None, :], r)
    half = r // 2
    fa_data = jnp.concatenate([jnp.concatenate([c[:, :half], s[:, :half]], axis=1),
                               jnp.concatenate([-s[:, :half], c[:, :half]], axis=1)], axis=0)
    fa_taps = jnp.concatenate([c, -s], axis=0)
    k1, k2, s2 = i[:, None, None], i[None, :, None], i[None, None, :]
    gc, gs = _cis(s2 * (r * k2 + k1), FFT_L)
    gr, gi = gc, -gs
    m1 = jnp.concatenate([jnp.concatenate([gr, -gi], axis=2), jnp.concatenate([gi, gr], axis=2)], axis=1)
    er, ei = c[:half] / FFT_L, s[:half] / FFT_L
    e2 = jnp.concatenate([jnp.concatenate([er, -ei], axis=1), jnp.concatenate([ei, er], axis=1)], axis=0)
    bf = jnp.bfloat16
    return dict(fa_data=fa_data.astype(bf), fa_taps=fa_taps.astype(bf), m1=m1.astype(bf), e2=e2.astype(bf))


def _fft_a_kernel(f_ref, x_ref, o_ref, *, two):
    f = f_ref[...]
    res = []
    for s in range(FFT_SB):
        x = jnp.concatenate([x_ref[0, s], x_ref[1, s]], axis=0) if two else x_ref[s]
        res.append(jnp.dot(f, x, preferred_element_type=jnp.float32).astype(o_ref.dtype))
    o_ref[...] = jnp.swapaxes(jnp.stack(res), 0, 1).reshape(o_ref.shape)


def fft_stage_a(fmat, x, g):
    two = x.ndim == 5
    xspec = (pl.BlockSpec((None, 2, FFT_SB, FFT_R // 2, MIX_W), lambda j: (g, 0, j, 0, 0)) if two
             else pl.BlockSpec((None, FFT_SB, FFT_R, MIX_W), lambda j: (g, j, 0, 0)))
    return pl.pallas_call(
        functools.partial(_fft_a_kernel, two=two),
        grid=(FFT_R // FFT_SB,),
        in_specs=[pl.BlockSpec((2 * FFT_R, FFT_R), lambda j: (0, 0)), xspec],
        out_specs=pl.BlockSpec((2, FFT_R, FFT_SB, MIX_W), lambda j: (0, 0, j, 0)),
        out_shape=jax.ShapeDtypeStruct((2, FFT_R, FFT_R, MIX_W), jnp.bfloat16),
        compiler_params=_cparams("parallel"),
        name="fft_stage_a",
    )(fmat, x)


def _fft_mid_kernel(m1_ref, a_ref, *rest, conv):
    if conv:
        h_ref, o_ref = rest
    else:
        s_ref, o_ref = rest
    r = FFT_R
    for k in range(FFT_KB):
        a = a_ref[:, k].reshape(2 * r, MIX_W)
        x = jnp.dot(m1_ref[k], a, preferred_element_type=jnp.float32)
        if conv:
            h = h_ref[k].astype(jnp.float32)
            xr, xi, hr, hi = x[:r], x[r:], h[:r], h[r:]
            zc = jnp.concatenate([xr * hr - xi * hi, xr * hi + xi * hr], axis=0).astype(jnp.bfloat16)
            p = lax.dot_general(m1_ref[k], zc, (((0,), (0,)), ((), ())), preferred_element_type=jnp.float32)
            o_ref[:, k] = p.reshape(2, r, MIX_W).astype(o_ref.dtype)
        else:
            o_ref[k] = (x * s_ref[...]).astype(o_ref.dtype)


def fft_mid(m1, a, h=None, scale=None):
    conv = h is not None
    r = FFT_R
    mspec = pl.BlockSpec((FFT_KB, 2 * r, 2 * r), lambda j: (j, 0, 0))
    aspec = pl.BlockSpec((2, FFT_KB, r, MIX_W), lambda j: (0, j, 0, 0))
    hspec = pl.BlockSpec((FFT_KB, 2 * r, MIX_W), lambda j: (j, 0, 0))
    if conv:
        in_specs, args = [mspec, aspec, hspec], (m1, a, h)
        out_specs, out_shape = aspec, jax.ShapeDtypeStruct((2, r, r, MIX_W), jnp.bfloat16)
    else:
        in_specs, args = [mspec, aspec, pl.BlockSpec((1, MIX_W), lambda j: (0, 0))], (m1, a, scale)
        out_specs, out_shape = hspec, jax.ShapeDtypeStruct((r, 2 * r, MIX_W), jnp.bfloat16)
    return pl.pallas_call(
        functools.partial(_fft_mid_kernel, conv=conv),
        grid=(r // FFT_KB,),
        in_specs=in_specs, out_specs=out_specs, out_shape=out_shape,
        compiler_params=_cparams("parallel"),
        name="fft_mid_conv" if conv else "fft_mid_filter",
    )(*args)


def _fft_out_kernel(e_ref, p_ref, mul_ref, add_ref, bias_ref, o_ref, *, time_order):
    e = e_ref[...]
    half = FFT_R // 2
    bias = bias_ref[...]
    p = jnp.swapaxes(p_ref[...].reshape(2 * FFT_R, FFT_SB, MIX_W), 0, 1)
    res = [[], []]
    for s in range(FFT_SB):
        y = jnp.dot(e, p[s], preferred_element_type=jnp.float32)
        for b in range(2):
            yb = y[b * half:(b + 1) * half] + add_ref[b, s].astype(jnp.float32) * bias
            res[b].append((mul_ref[b, s].astype(jnp.float32) * yb).astype(o_ref.dtype))
    for b in range(2):
        out = jnp.stack(res[b])
        o_ref[b] = jnp.swapaxes(out, 0, 1) if time_order else out


def fft_out(e2, p, mul, gm, add, ga, bias, time_order):
    half = FFT_R // 2

    def bspec(g):
        return pl.BlockSpec((None, 2, FFT_SB, half, MIX_W), lambda j: (g, 0, j, 0, 0))

    if time_order:
        out_spec = pl.BlockSpec((None, 2, half, FFT_SB, MIX_W), lambda j: (0, 0, 0, j, 0))
        out_shape = jax.ShapeDtypeStruct((1, 2, half, FFT_R, MIX_W), jnp.bfloat16)
    else:
        out_spec, out_shape = bspec(0), jax.ShapeDtypeStruct((1, 2, FFT_R, half, MIX_W), jnp.bfloat16)
    return pl.pallas_call(
        functools.partial(_fft_out_kernel, time_order=time_order),
        grid=(FFT_R // FFT_SB,),
        in_specs=[pl.BlockSpec((FFT_R, 2 * FFT_R), lambda j: (0, 0)),
                  pl.BlockSpec((2, FFT_R, FFT_SB, MIX_W), lambda j: (0, 0, j, 0)), bspec(gm), bspec(ga),
                  pl.BlockSpec((1, MIX_W), lambda j: (0, 0))],
        out_specs=out_spec,
        out_shape=out_shape,
        compiler_params=_cparams("parallel", vmem=FFT_OUT_VMEM_BYTES),
        name="fft_out",
    )(e2, p, mul, add, bias)


def hyena_spectra(taps, inv_norm, mats):
    return [fft_mid(mats['m1'], fft_stage_a(mats['fa_taps'], taps, order), scale=inv_norm[order])
            for order in range(2)]


def hyena_latent(xs, spectra, bias, mats):
    u, gu = xs, 2
    for order in range(2):
        p = fft_mid(mats['m1'], fft_stage_a(mats['fa_data'], u, gu), h=spectra[order])
        u, gu = fft_out(mats['e2'], p, xs, order, u, gu, bias[order].reshape(1, MIX_W), order == 1), 0
    return u.reshape(2, FFT_L // 2, MIX_W)


def hyena_context(xc, taps, inv_norm, bias):
    n = xc.shape[2]
    length = 2 * n
    k = jnp.arange(length, dtype=jnp.int32)
    c, s = _cis(k[:, None] * k[None, :], length)
    fr, fi = c[:, :n], -s[:, :n]
    m_fwd = jnp.concatenate([jnp.concatenate([fr, -fi], axis=1), jnp.concatenate([fi, fr], axis=1)], axis=0)
    m_taps = jnp.concatenate([c, -s], axis=0)
    er, ei = c[:n] / length, s[:n] / length
    m_inv = jnp.concatenate([jnp.concatenate([er, -ei], axis=1), jnp.concatenate([ei, er], axis=1)], axis=0)
    x1, x2, u = (xc[g].astype(jnp.float32) for g in range(3))
    for order, gate in enumerate((x1, x2)):
        h = mm(m_taps, taps[order], name="ctx_dft_taps") * inv_norm[order]
        w = mm(m_fwd, u.reshape(length, MIX_W), name="ctx_dft_fwd")
        wr, wi, hr, hi = w[:length], w[length:], h[:length], h[length:]
        zc = jnp.concatenate([wr * hr - wi * hi, wr * hi + wi * hr], axis=0)
        y = mm(m_inv, zc, name="ctx_dft_inv").reshape(2, n, MIX_W)
        u = gate * (y + u * bias[order])
    return u.astype(jnp.bfloat16)


def hyena_features(n, ff1, ff1_b, freq, ff2, ff2_b):
    k = jnp.arange(n, dtype=jnp.int32)
    pos = jnp.stack([k, (n - k) % n]).astype(jnp.float32)
    t = pos / (n - 1)
    fw = (2.0 * math.pi * pos / n)[..., None] * jnp.linspace(1e-4, HY_BANDS - 1, HY_BANDS, dtype=jnp.float32)
    feats = jnp.concatenate([t[..., None], jnp.cos(fw), -jnp.sin(fw)], axis=-1)
    h = jnp.sin(freq * (feats @ ff1 + ff1_b))
    return jnp.sin(freq * (h @ ff2 + ff2_b))


def _filter_kernel(feat_ref, w_ref, delta_ref, taps_ref, sum_ref, *, n, tm, split):
    dr = pl.program_id(1)
    i = pl.program_id(2)
    row = i * tm + lax.broadcasted_iota(jnp.int32, (tm, MIX_W), 0)
    lag = jnp.where(dr == 0, row, jnp.where(row == 0, 0, n - row))
    t = lag.astype(jnp.float32) / (n - 1)
    h = jnp.dot(feat_ref[...].astype(jnp.bfloat16), w_ref[...].astype(jnp.bfloat16),
                preferred_element_type=jnp.float32)
    h = h * jnp.exp(-t * delta_ref[...])

    @pl.when((dr == 0) & (i == 0))
    def _():
        sum_ref[...] = jnp.zeros_like(sum_ref)

    sum_ref[...] += jnp.sum(jnp.abs(h), axis=0, keepdims=True)
    taps = jnp.where((dr == 0) | (row != 0), h, 0.0).astype(taps_ref.dtype)
    if split:
        taps = jnp.swapaxes(taps.reshape(tm // FFT_R, FFT_R, MIX_W), 0, 1)
    taps_ref[...] = taps


def hyena_filter_taps(feats, ff3):
    _, n, nf = feats.shape
    split = 2 * n == FFT_L
    tm = FFT_SB * FFT_R if split else n
    nb = n // tm
    w = ff3.reshape(nf, 2, 2, MIX_W).transpose(1, 2, 0, 3)
    deltas = jnp.abs(jnp.linspace(math.log(HY_TARGET) / HY_SLOW_PCT, math.log(HY_TARGET) / HY_FAST_PCT,
                                  MIX_W, dtype=jnp.float32)).reshape(1, MIX_W)
    if split:
        tspec = pl.BlockSpec((None, FFT_R, FFT_SB, MIX_W), lambda o, dr, i: (o, 0, dr * nb + i, 0))
        tshape = jax.ShapeDtypeStruct((2, FFT_R, FFT_R, MIX_W), jnp.bfloat16)
    else:
        tspec = pl.BlockSpec((None, tm, MIX_W), lambda o, dr, i: (o, dr * nb + i, 0))
        tshape = jax.ShapeDtypeStruct((2, 2 * n, MIX_W), jnp.bfloat16)
    return pl.pallas_call(
        functools.partial(_filter_kernel, n=n, tm=tm, split=split),
        grid=(2, 2, nb),
        in_specs=[pl.BlockSpec((None, tm, nf), lambda o, dr, i: (dr, i, 0)),
                  pl.BlockSpec((None, None, nf, MIX_W), lambda o, dr, i: (o, dr, 0, 0)),
                  pl.BlockSpec((1, MIX_W), lambda o, dr, i: (0, 0))],
        out_specs=[tspec, pl.BlockSpec((None, 1, MIX_W), lambda o, dr, i: (o, 0, 0))],
        out_shape=[tshape, jax.ShapeDtypeStruct((2, 1, MIX_W), jnp.float32)],
        compiler_params=_cparams("arbitrary", "arbitrary", "arbitrary"),
        name="hyena_filter",
    )(feats, w, deltas)


def _merge_kernel(yhc_ref, yhl_ref, yl_ref, yg_ref, w_ref, g0_ref, g1_ref, g2_ref, o_ref, *, n_ctx_tiles):
    ys = (_pick_rows(yhc_ref, yhl_ref, n_ctx_tiles), yl_ref[...], yg_ref[...])
    acc = None
    for j, (y, g_ref) in enumerate(zip(ys, (g0_ref, g1_ref, g2_ref))):
        gate = 1.0 / (1.0 + jnp.exp(-g_ref[...].astype(jnp.float32)))
        t = gate * jnp.dot(y, w_ref[j], preferred_element_type=jnp.float32)
        acc = t if acc is None else acc + t
    o_ref[...] = acc.astype(o_ref.dtype)


def merge_branches(z, y_hy_ctx, y_hy_lat, y_lru, y_hg, w_branch, layer):
    rows = z.shape[0]
    tm = ROW_T // 2
    n_ctx_tiles = y_hy_ctx.shape[0] // tm
    yspec = pl.BlockSpec((tm, MIX_W), lambda i: (i, 0))

    def gspec(k):
        return pl.BlockSpec((tm, D_MODEL), lambda i: (i, GATE_COL0 // D_MODEL + k))

    return pl.pallas_call(
        functools.partial(_merge_kernel, n_ctx_tiles=n_ctx_tiles),
        grid=(rows // tm,),
        in_specs=[*_split_row_specs(tm, MIX_W, n_ctx_tiles), yspec, yspec,
                  pl.BlockSpec((None, N_BRANCH, MIX_W, D_MODEL), lambda i: (layer, 0, 0, 0)),
                  gspec(0), gspec(1), gspec(2)],
        out_specs=pl.BlockSpec((tm, D_MODEL), lambda i: (i, 0)),
        out_shape=jax.ShapeDtypeStruct((rows, D_MODEL), jnp.bfloat16),
        compiler_params=_cparams("parallel"),
        name="merge",
    )(y_hy_ctx, y_hy_lat, y_lru, y_hg, w_branch, z, z, z)


def _out_proj_kernel(a_ref, w_ref, *refs, n_ctx_tiles):
    *x_refs, g1_ref, gt_ref, g2_ref, sh_ref, sc_ref, wr_ref, xo_ref, h_ref, lg_ref = refs
    x = _pick_rows(*x_refs, n_ctx_tiles) if len(x_refs) == 2 else x_refs[0][...]
    y = jnp.dot(a_ref[...], w_ref[...], preferred_element_type=jnp.float32)
    y = y * lax.rsqrt(jnp.mean(y * y, axis=-1, keepdims=True) + EPS) * g1_ref[...]
    x = x + gt_ref[...] * y
    xo_ref[...] = x
    h = x * lax.rsqrt(jnp.mean(x * x, axis=-1, keepdims=True) + EPS) * g2_ref[...]
    h = (h * (1.0 + sc_ref[...]) + sh_ref[...]).astype(jnp.bfloat16)
    h_ref[...] = h
    lg_ref[...] = jnp.dot(h, wr_ref[...], preferred_element_type=jnp.float32)


def out_proj(acc, w_out, layer, x, g1, gt, g2, sh, sc, w_router, n_ctx_rows, n_lat_rows):
    tm = ROW_T // 2
    xspecs, xs, rows, d = _row_operand(x, tm)
    n_ctx_tiles = 2 * n_ctx_rows // tm
    idx = functools.partial(_mod_index, n_ctx_tiles=n_ctx_tiles, tiles_per_batch=n_lat_rows // tm)
    row = pl.BlockSpec((tm, d), lambda i: (i, 0))
    vec = pl.BlockSpec((1, d), lambda i: (0, 0))
    mod = pl.BlockSpec((None, 1, d), lambda i: (idx(i), 0, 0))
    wr = jnp.zeros((d, ROUTER_PAD), jnp.bfloat16).at[:, :N_EXPERTS].set(w_router.astype(jnp.bfloat16))
    return pl.pallas_call(
        functools.partial(_out_proj_kernel, n_ctx_tiles=n_ctx_tiles),
        grid=(rows // tm,),
        in_specs=[row, pl.BlockSpec((None, d, d), lambda i: (layer, 0, 0)), *xspecs, vec, mod, vec, mod, mod,
                  pl.BlockSpec((d, ROUTER_PAD), lambda i: (0, 0))],
        out_specs=[row, row, pl.BlockSpec((tm, ROUTER_PAD), lambda i: (i, 0))],
        out_shape=[jax.ShapeDtypeStruct((rows, d), jnp.float32), jax.ShapeDtypeStruct((rows, d), jnp.bfloat16),
                   jax.ShapeDtypeStruct((rows, ROUTER_PAD), jnp.float32)],
        compiler_params=_cparams("parallel"),
        name="out_proj",
    )(acc, w_out, *xs, g1.reshape(1, d), gt, g2.reshape(1, d), sh, sc, wr)


def _ffn_up_kernel(x_ref, wg_ref, wu_ref, h_ref):
    x = x_ref[...]
    hg = jnp.dot(x, wg_ref[...].astype(jnp.bfloat16), preferred_element_type=jnp.float32)
    hu = jnp.dot(x, wu_ref[...].astype(jnp.bfloat16), preferred_element_type=jnp.float32)
    h_ref[...] = (hg / (1.0 + jnp.exp(-hg)) * hu).astype(h_ref.dtype)


def _ffn_down_kernel(h_ref, wd_ref, g_ref, o_ref):
    y = jnp.dot(h_ref[...], wd_ref[...].astype(jnp.bfloat16), preferred_element_type=jnp.float32)
    o_ref[...] = (y * g_ref[...]).astype(o_ref.dtype)


def expert_ffn(xe, w_gate, w_up, w_down, layer, g):
    ne, r, d = xe.shape
    dff = w_gate.shape[-1]
    hid = pl.pallas_call(
        _ffn_up_kernel,
        grid=(ne, dff // FFN_TF),
        in_specs=[pl.BlockSpec((None, r, d), lambda e, f: (e, 0, 0)),
                  pl.BlockSpec((None, None, d, FFN_TF), lambda e, f: (layer, e, 0, f)),
                  pl.BlockSpec((None, None, d, FFN_TF), lambda e, f: (layer, e, 0, f))],
        out_specs=pl.BlockSpec((None, r, FFN_TF), lambda e, f: (e, 0, f)),
        out_shape=jax.ShapeDtypeStruct((ne, r, dff), jnp.bfloat16),
        compiler_params=_cparams("parallel", "arbitrary"),
        name="ffn_up",
    )(xe, w_gate, w_up)
    return pl.pallas_call(
        _ffn_down_kernel,
        grid=(ne, d // FFN_TD),
        in_specs=[pl.BlockSpec((None, r, dff), lambda e, j: (e, 0, 0)),
                  pl.BlockSpec((None, None, dff, FFN_TD), lambda e, j: (layer, e, 0, j)),
                  pl.BlockSpec((None, r, 1), lambda e, j: (e, 0, 0))],
        out_specs=pl.BlockSpec((None, r, FFN_TD), lambda e, j: (e, 0, j)),
        out_shape=jax.ShapeDtypeStruct((ne, r, d), jnp.bfloat16),
        compiler_params=_cparams("parallel", "arbitrary"),
        name="ffn_down",
    )(hid, w_down, g)


def route(logits, row0, bsz, n):
    cap = EC_CAPACITY * n // N_EXPERTS
    aff = jax.nn.softmax(logits[row0:row0 + bsz * n, :N_EXPERTS].reshape(bsz, n, N_EXPERTS), axis=-1)
    g, idx = lax.top_k(jnp.swapaxes(aff, 1, 2), cap)
    flat = idx + (row0 + jnp.arange(bsz, dtype=idx.dtype) * n)[:, None, None]
    return (jnp.swapaxes(g, 0, 1).reshape(N_EXPERTS, bsz * cap),
            jnp.swapaxes(flat, 0, 1).reshape(N_EXPERTS, bsz * cap))


def ec_moe(h, logits, w_gate, w_up, w_down, layer, segments):
    gs, flats = zip(*(route(logits, *seg) for seg in segments))
    flat, g = lax.sort_key_val(jnp.concatenate(flats, axis=1), jnp.concatenate(gs, axis=1), dimension=1)
    return expert_ffn(h[flat], w_gate, w_up, w_down, layer, g[..., None]), flat


def moe_plan(flat, n_rows):
    ne, r = flat.shape
    nwin = r // MOE_WIN
    n_tiles = n_rows // MOE_TT
    bounds = jnp.arange(n_tiles + 1, dtype=flat.dtype) * MOE_TT
    p = jnp.sum(flat[None, :, :] < bounds[:, None, None], axis=2, dtype=jnp.int32)
    p0, p1 = p[:-1], p[1:]
    w0 = p0 // MOE_WIN
    nw = jnp.where(p1 > p0, (p1 - 1) // MOE_WIN - w0 + 1, 0)
    off = jnp.cumsum(nw, axis=1)
    q = jnp.arange(MOE_MAXQ, dtype=jnp.int32)
    e_of_q = jnp.minimum(jnp.sum(q[None, :, None] >= off[:, None, :], axis=2, dtype=jnp.int32), ne - 1)
    pick = e_of_q[:, :, None] == jnp.arange(ne, dtype=jnp.int32)
    w = q[None, :] + jnp.sum(jnp.where(pick, (w0 - (off - nw))[:, None, :], 0), axis=2)
    items = jnp.clip(e_of_q * nwin + w, 0, ne * nwin - 1)
    return items.astype(jnp.int32), off[:, -1].astype(jnp.int32)


def _combine_kernel(items_ref, count_ref, tok_ref, ye_ref, x_ref, g_ref, gt_ref, *rest, nxt, skip):
    if nxt:
        gn_ref, sh_ref, sc_ref, o_ref, u_ref, stage, sem, acc_ref = rest
    else:
        o_ref, stage, sem, acc_ref = rest
    i = pl.program_id(0)
    t = i + skip
    n = count_ref[t]
    n_groups = (n + MOE_GROUP - 1) // MOE_GROUP

    @pl.when(i == 0)
    def _():
        stage[...] = jnp.zeros_like(stage)

    def window_copy(g, buf, k):
        item = items_ref[t, jnp.minimum(MOE_GROUP * g + k, MOE_MAXQ - 1)]
        return pltpu.make_async_copy(ye_ref.at[pl.ds(item * MOE_WIN, MOE_WIN)],
                                     stage.at[buf, pl.ds(k * MOE_WIN, MOE_WIN)], sem.at[buf])

    def for_group(g, buf, wait):
        for k in range(MOE_GROUP):
            @pl.when(MOE_GROUP * g + k < n)
            def _():
                cp = window_copy(g, buf, k)
                cp.wait() if wait else cp.start()

    acc_ref[...] = jnp.zeros_like(acc_ref)

    for j in range(MOE_NBUF - 1):
        @pl.when(j < n_groups)
        def _():
            for_group(j, j, False)

    row = t * MOE_TT + lax.broadcasted_iota(jnp.int32, (MOE_TT, MOE_GROUP * MOE_WIN), 0)

    def body(g, carry):
        buf = g % MOE_NBUF
        ahead = g + MOE_NBUF - 1

        @pl.when(ahead < n_groups)
        def _():
            for_group(ahead, ahead % MOE_NBUF, False)

        for_group(g, buf, True)
        toks = []
        for k in range(MOE_GROUP):
            qk = MOE_GROUP * g + k
            item = items_ref[t, jnp.minimum(qk, MOE_MAXQ - 1)]
            toks.append(jnp.where(qk < n, tok_ref[pl.ds(item, 1), :], -1))
        onehot = jnp.where(row == jnp.concatenate(toks, axis=1), 1.0, 0.0).astype(jnp.bfloat16)
        acc_ref[...] += jnp.dot(onehot, stage[buf], preferred_element_type=jnp.float32)
        return carry

    lax.fori_loop(0, n_groups, body, 0)

    y = acc_ref[...]
    y = y * lax.rsqrt(jnp.mean(y * y, axis=-1, keepdims=True) + EPS) * g_ref[...]
    x = x_ref[...] + gt_ref[...] * y
    if nxt:
        u = x * lax.rsqrt(jnp.mean(x * x, axis=-1, keepdims=True) + EPS) * gn_ref[...]
        u_ref[...] = (u * (1.0 + sc_ref[...]) + sh_ref[...]).astype(u_ref.dtype)
    o_ref[...] = x


def moe_combine(x, ye, flat, g, gt, n_ctx_rows, n_lat_rows, nxt=None):
    rows, d = x.shape
    ne, r, _ = ye.shape
    items, count = moe_plan(flat, rows)
    n_ctx_tiles = 2 * n_ctx_rows // MOE_TT
    idx = functools.partial(_mod_index, n_ctx_tiles=n_ctx_tiles, tiles_per_batch=n_lat_rows // MOE_TT)
    skip = 0 if nxt else n_ctx_tiles
    row = pl.BlockSpec((MOE_TT, d), lambda i, *_: (i + skip, 0))
    vec = pl.BlockSpec((1, d), lambda i, *_: (0, 0))
    mod = pl.BlockSpec((None, 1, d), lambda i, *_: (idx(i + skip), 0, 0))
    out_row = pl.BlockSpec((MOE_TT, d), lambda i, *_: (i, 0))
    n_out = rows - skip * MOE_TT
    in_specs = [pl.BlockSpec((ne * r // MOE_WIN, MOE_WIN), lambda i, *_: (0, 0)),
                pl.BlockSpec(memory_space=pl.ANY), row, vec, mod]
    args = [flat.reshape(ne * r // MOE_WIN, MOE_WIN), ye.reshape(ne * r, d), x, g.reshape(1, d), gt]
    out_specs, out_shape = out_row, jax.ShapeDtypeStruct((n_out, d), jnp.float32)
    if nxt:
        in_specs += [vec, mod, mod]
        args += [nxt[0].reshape(1, d), nxt[1], nxt[2]]
        out_specs = [out_row, out_row]
        out_shape = [out_shape, jax.ShapeDtypeStruct((n_out, d), jnp.bfloat16)]
    return pl.pallas_call(
        functools.partial(_combine_kernel, nxt=bool(nxt), skip=skip),
        grid_spec=pltpu.PrefetchScalarGridSpec(
            num_scalar_prefetch=2,
            grid=(n_out // MOE_TT,),
            in_specs=in_specs, out_specs=out_specs,
            scratch_shapes=[pltpu.VMEM((MOE_NBUF, MOE_GROUP * MOE_WIN, d), jnp.bfloat16),
                            pltpu.SemaphoreType.DMA((MOE_NBUF,)),
                            pltpu.VMEM((MOE_TT, d), jnp.float32)]),
        out_shape=out_shape,
        compiler_params=_cparams("arbitrary"),
        name="moe_combine",
    )(items, count, *args)


def kernel(x, c, ctx, c_ctx, w_mod, b_mod, norm_g, w_in, hy_conv_w, hy_conv_b, hy_ff1, hy_ff1_b, hy_freq,
           hy_ff2, hy_ff2_b, hy_ff3, hy_bias, lru_conv_w, lru_conv_b, lru_wa, lru_ba, lru_wi, lru_bi,
           lru_lambda, hg_lb_logits, hg_norm_g, w_branch, w_out, w_router, w_gate, w_up, w_down):
    bsz, n_lat, d = x.shape
    n_ctx = ctx.shape[1]
    assert bsz == 2 and d == D_MODEL and n_ctx == SEQ_T and 2 * n_lat == FFT_L and n_lat % ROW_T == 0
    gam = jax.nn.softmax(hg_lb_logits.astype(jnp.float32), axis=0)
    lb_all = jnp.maximum(jnp.cumsum(gam, axis=0) - gam[:1], 0.0)
    mats = fft_matrices()
    cvec = jnp.zeros((SUBLANES, d), jnp.float32).at[:bsz].set(jax.nn.silu(c)).at[bsz].set(jax.nn.silu(c_ctx))
    n_ctx_all = bsz * n_ctx
    xa = (ctx.reshape(n_ctx_all, d), x.reshape(bsz * n_lat, d))
    w_branch_b, w_out_b = w_branch.astype(jnp.bfloat16), w_out.astype(jnp.bfloat16)
    mods, filters = [], []
    for l in range(DEPTH):
        mod = mm(cvec, w_mod, layer=l, name="adaln")[:bsz + 1] + b_mod[l]
        mods.append([mod.reshape(bsz + 1, 1, 6, d)[:, :, k] for k in range(6)])
        ffp = (hy_ff1[l], hy_ff1_b[l], hy_freq[l], hy_ff2[l], hy_ff2_b[l])
        taps_l, norm_l = hyena_filter_taps(hyena_features(n_lat, *ffp), hy_ff3[l])
        taps_c, norm_c = hyena_filter_taps(hyena_features(n_ctx, *ffp), hy_ff3[l])
        filters.append((hyena_spectra(taps_l, 1.0 / norm_l, mats), taps_c, 1.0 / norm_c))
    u = prenorm(xa, norm_g[0, 0], mods[0][0], mods[0][1], n_ctx, n_lat)
    for l in range(DEPTH):
        last = l == DEPTH - 1
        p = {'lru_conv_w': lru_conv_w[l], 'lru_conv_b': lru_conv_b[l], 'lru_wa': lru_wa[l], 'lru_ba': lru_ba[l],
             'lru_wi': lru_wi[l], 'lru_bi': lru_bi[l], 'lru_lambda': lru_lambda[l]}
        sh1, sc1, gt1, sh2, sc2, gt2 = mods[l]

        z = mm(u, w_in, jnp.bfloat16, tm=IN_PROJ_TM, tn=IN_PROJ_TN, layer=l, name="in_proj")

        y_lru = lru_branch(z, p, n_ctx, n_lat)
        y_hg = hgrn_branch(z, lb_all[l], hg_norm_g[l], n_ctx, n_lat)

        spectra, taps_c, inv_norm_c = filters[l]
        xc_l = hy_conv_latent(z, hy_conv_w[l], hy_conv_b[l], n_ctx_all, n_lat)
        y_hy_l = hyena_latent(xc_l, spectra, hy_bias[l], mats)
        xc_c = hy_conv(z, hy_conv_w[l], hy_conv_b[l], 0, n_ctx, True)
        y_hy_c = hyena_context(xc_c, taps_c, inv_norm_c, hy_bias[l])

        acc = merge_branches(z, y_hy_c.reshape(n_ctx_all, MIX_W), y_hy_l.reshape(bsz * n_lat, MIX_W), y_lru, y_hg,
                             w_branch_b, l)
        xa, h, logits = out_proj(acc, w_out_b, l, xa, norm_g[l, 1], gt1, norm_g[l, 2], sh2, sc2,
                                 w_router[l], n_ctx, n_lat)
        segments = [(n_ctx_all, bsz, n_lat)] if last else [(n_ctx_all, bsz, n_lat), (0, bsz, n_ctx)]
        ye, flat = ec_moe(h, logits, w_gate, w_up, w_down, l, segments)
        if last:
            xa = moe_combine(xa, ye, flat, norm_g[l, 3], gt2, n_ctx, n_lat)
        else:
            nxt = (norm_g[l + 1, 0], mods[l + 1][0], mods[l + 1][1])
            xa, u = moe_combine(xa, ye, flat, norm_g[l, 3], gt2, n_ctx, n_lat, nxt)
    return xa.reshape(bsz, n_lat, d)
```

```python
import functools
import math

import numpy as np
import jax
import jax.numpy as jnp
from jax import lax
from jax.experimental import pallas as pl
from jax.experimental.pallas import tpu as pltpu

D_MODEL = 2048
DEPTH = 2
GRID_W = 64
MIX_W = D_MODEL // 2
N_BRANCH = 3
HY_BANDS = 16
HY_TARGET = 1e-2
HY_FAST_PCT = 0.3
HY_SLOW_PCT = 1.5
LRU_BLOCKS = 8
LRU_BS = MIX_W // LRU_BLOCKS
LRU_C = 8.0
HG_HEADS = 8
HG_DK = MIX_W // HG_HEADS
N_EXPERTS = 16
EC_CAPACITY = 2
EPS = 1e-6
TINY = 1e-30
LOG2E = 1.4426950408889634
HY_COL0 = 0
LRU_COL0 = 3 * MIX_W
HG_COL0 = 5 * MIX_W
GATE_COL0 = 10 * MIX_W
IN_COLS = GATE_COL0 + N_BRANCH * D_MODEL

SUBLANES = 8
LANES = 128
SEQ_T = 256
HG_C = 128
HG_LEVELS = 7
HG_HB = 8
ROW_T = 512
IN_PROJ_TM = 2816
IN_PROJ_TN = 512
FFT_R = 128
FFT_L = FFT_R * FFT_R
FFT_KB = 8
FFT_SB = 16
HY_PARTS = 4
ROUTER_PAD = LANES
FFN_TF = 256
FFN_TD = 512
MOE_TT = 512
MOE_WIN = 32
MOE_GROUP = 16
MOE_NBUF = 3
MOE_MAXQ = N_EXPERTS * (MOE_TT // MOE_WIN + 1)
VMEM_LIMIT_BYTES = 48 * 1024 * 1024
FFT_OUT_VMEM_BYTES = 56 * 1024 * 1024


def _cparams(*sem, vmem=VMEM_LIMIT_BYTES):
    return pltpu.CompilerParams(dimension_semantics=sem, vmem_limit_bytes=vmem)


def _pick_tile(n, pref):
    for t in (pref, 1024, 512, 256, 128):
        if t <= n and n % t == 0:
            return t
    return n


def _row_block(b, j, n_ctx, n_lat, reverse):
    if reverse:
        kc, kl = n_ctx - 1 - j, n_ctx + n_lat - 1 - j
    else:
        kc, kl = j, j - n_ctx
    return jnp.where(j < n_ctx, b * n_ctx + kc, 2 * n_ctx + b * n_lat + kl)


def _mm_kernel(a_ref, b_ref, o_ref):
    o_ref[...] = jnp.dot(a_ref[...].astype(jnp.bfloat16), b_ref[...].astype(jnp.bfloat16),
                         preferred_element_type=jnp.float32).astype(o_ref.dtype)


def mm(a, b, out_dtype=jnp.float32, tm=512, tn=1024, layer=None, name="mm"):
    m, k = a.shape
    n = b.shape[-1]
    tm = _pick_tile(m, tm)
    tn = _pick_tile(n, tn)
    if layer is None:
        bspec = pl.BlockSpec((k, tn), lambda i, j: (0, j))
    else:
        bspec = pl.BlockSpec((None, k, tn), lambda i, j: (layer, 0, j))
    return pl.pallas_call(
        _mm_kernel,
        grid=(m // tm, n // tn),
        in_specs=[pl.BlockSpec((tm, k), lambda i, j: (i, 0)), bspec],
        out_specs=pl.BlockSpec((tm, tn), lambda i, j: (i, j)),
        out_shape=jax.ShapeDtypeStruct((m, n), out_dtype),
        compiler_params=_cparams("parallel", "arbitrary"),
        name=name,
    )(a, b)


def _mod_index(i, n_ctx_tiles, tiles_per_batch):
    return jnp.where(i < n_ctx_tiles, 2, (i - n_ctx_tiles) // tiles_per_batch)


def _split_row_specs(tile, width, n_ctx_tiles):
    return (pl.BlockSpec((tile, width), lambda i: (jnp.minimum(i, n_ctx_tiles - 1), 0)),
            pl.BlockSpec((tile, width), lambda i: (jnp.maximum(i - n_ctx_tiles, 0), 0)))


def _pick_rows(ctx_ref, lat_ref, n_ctx_tiles):
    return jnp.where(pl.program_id(0) < n_ctx_tiles, ctx_ref[...], lat_ref[...])


def _row_operand(x, tile):
    if isinstance(x, tuple):
        d = x[0].shape[1]
        return list(_split_row_specs(tile, d, x[0].shape[0] // tile)), list(x), x[0].shape[0] + x[1].shape[0], d
    return [pl.BlockSpec((tile, x.shape[1]), lambda i: (i, 0))], [x], x.shape[0], x.shape[1]


def _prenorm_kernel(*refs, n_ctx_tiles):
    *x_refs, g_ref, sh_ref, sc_ref, u_ref = refs
    x = _pick_rows(*x_refs, n_ctx_tiles) if len(x_refs) == 2 else x_refs[0][...]
    y = x * lax.rsqrt(jnp.mean(x * x, axis=-1, keepdims=True) + EPS) * g_ref[...]
    u_ref[...] = (y * (1.0 + sc_ref[...]) + sh_ref[...]).astype(u_ref.dtype)


def prenorm(x, g, sh, sc, n_ctx_rows, n_lat_rows):
    xspecs, xs, rows, d = _row_operand(x, ROW_T)
    n_ctx_tiles = 2 * n_ctx_rows // ROW_T
    idx = functools.partial(_mod_index, n_ctx_tiles=n_ctx_tiles, tiles_per_batch=n_lat_rows // ROW_T)
    vec = pl.BlockSpec((None, 1, d), lambda i: (idx(i), 0, 0))
    return pl.pallas_call(
        functools.partial(_prenorm_kernel, n_ctx_tiles=n_ctx_tiles),
        grid=(rows // ROW_T,),
        in_specs=[*xspecs, pl.BlockSpec((1, d), lambda i: (0, 0)), vec, vec],
        out_specs=pl.BlockSpec((ROW_T, d), lambda i: (i, 0)),
        out_shape=jax.ShapeDtypeStruct((rows, d), jnp.bfloat16),
        compiler_params=_cparams("parallel"),
        name="prenorm",
    )(*xs, g.reshape(1, d), sh, sc)


def _gelu_tanh(x):
    return 0.5 * x * (1.0 + jnp.tanh(math.sqrt(2.0 / math.pi) * (x + 0.044715 * (x * x * x))))


def _conv_pos(is_ctx, shape):
    t = lax.broadcasted_iota(jnp.int32, shape, 0)
    pos = jnp.where(is_ctx, t, t & (GRID_W - 1))
    last = jnp.where(is_ctx, SEQ_T - 1, GRID_W - 1)
    return pos, last


def _lru_pass_kernel(x_ref, cw_ref, cb_ref, w_ref, gb_ref, lam_ref, *rest, reverse, final):
    if final:
        lg_ref, hprev_ref, out_ref, h_ref, a_ref, b_ref = rest
    else:
        out_ref, h_ref, a_ref, b_ref = rest
    j = pl.program_id(1)

    @pl.when(j == 0)
    def _():
        h_ref[...] = jnp.zeros_like(h_ref)

    x = x_ref[...].astype(jnp.float32)
    pos, last = _conv_pos(j == 0, x.shape)
    cw = cw_ref[...]
    xc = cb_ref[...] + cw[2:3] * x
    xc = xc + cw[0:1] * jnp.where(pos >= 2, pltpu.roll(x, 2, 0), 0.0)
    xc = xc + cw[1:2] * jnp.where(pos >= 1, pltpu.roll(x, 1, 0), 0.0)
    xc = xc + cw[3:4] * jnp.where(pos < last, pltpu.roll(x, SEQ_T - 1, 0), 0.0)

    xb = xc.astype(jnp.bfloat16)
    lam = lam_ref[...]
    sp = jnp.maximum(-lam, 0.0) + jnp.log(1.0 + jnp.exp(-jnp.abs(lam)))
    n_tiles = SEQ_T // SUBLANES
    row = lax.broadcasted_iota(jnp.int32, (1, SUBLANES, LRU_BS), 1)
    for n in range(LRU_BLOCKS):
        sl = slice(n * LRU_BS, (n + 1) * LRU_BS)
        pre = jnp.dot(xb[:, sl], w_ref[n], preferred_element_type=jnp.float32) + gb_ref[n]
        gate = 1.0 / (1.0 + jnp.exp(-pre))
        r, i = gate[:, :LRU_BS], gate[:, LRU_BS:]
        a = jnp.exp(-LRU_C * r * sp[:, sl])
        om = 1.0 - a * a
        b = om * lax.rsqrt(jnp.maximum(om, TINY)) * (i * xc[:, sl])
        a = a.reshape(n_tiles, SUBLANES, LRU_BS)
        b = b.reshape(n_tiles, SUBLANES, LRU_BS)
        for d in (1, 2, 4):
            if reverse:
                m = row < SUBLANES - d
                a_sh, b_sh = pltpu.roll(a, SUBLANES - d, 1), pltpu.roll(b, SUBLANES - d, 1)
            else:
                m = row >= d
                a_sh, b_sh = pltpu.roll(a, d, 1), pltpu.roll(b, d, 1)
            b = jnp.where(m, a * b_sh + b, b)
            a = jnp.where(m, a * a_sh, a)
        a_ref[:, sl] = a.reshape(SEQ_T, LRU_BS)
        b_ref[:, sl] = b.reshape(SEQ_T, LRU_BS)

    h = h_ref[...]
    for k in range(n_tiles):
        kk = n_tiles - 1 - k if reverse else k
        rs = slice(kk * SUBLANES, (kk + 1) * SUBLANES)
        ht = a_ref[rs, :] * h + b_ref[rs, :]
        h = ht[0:1, :] if reverse else ht[SUBLANES - 1:SUBLANES, :]
        if final:
            ht = (ht + hprev_ref[rs, :]) * _gelu_tanh(lg_ref[rs, :].astype(jnp.float32))
        out_ref[rs, :] = ht.astype(out_ref.dtype)
    h_ref[...] = h


def lru_pass(z, conv_w, conv_b, wa, ba, wi, bi, lam, n_ctx_rows, n_lat_rows, reverse, h_prev=None):
    rows = z.shape[0]
    n_lat = n_lat_rows // SEQ_T
    cb = LRU_COL0 // MIX_W
    final = h_prev is not None
    rb = functools.partial(_row_block, n_ctx=n_ctx_rows // SEQ_T, n_lat=n_lat, reverse=reverse)

    def zspec(k):
        return pl.BlockSpec((SEQ_T, MIX_W), lambda b, j: (rb(b, j), cb + k))

    row_spec = pl.BlockSpec((SEQ_T, MIX_W), lambda b, j: (rb(b, j), 0))

    def const_spec(shape):
        return pl.BlockSpec(shape, lambda b, j: (0,) * len(shape))

    w2 = jnp.concatenate([wa, wi], axis=-1).astype(jnp.bfloat16)
    gb = jnp.concatenate([ba.reshape(LRU_BLOCKS, 1, LRU_BS), bi.reshape(LRU_BLOCKS, 1, LRU_BS)], axis=-1)
    in_specs = [zspec(0), const_spec((4, MIX_W)), const_spec((1, MIX_W)),
                const_spec((LRU_BLOCKS, LRU_BS, 2 * LRU_BS)), const_spec((LRU_BLOCKS, 1, 2 * LRU_BS)),
                const_spec((1, MIX_W))]
    args = [z, conv_w, conv_b.reshape(1, MIX_W), w2, gb, lam.reshape(1, MIX_W)]
    if final:
        in_specs += [zspec(1), row_spec]
        args += [z, h_prev]
    return pl.pallas_call(
        functools.partial(_lru_pass_kernel, reverse=reverse, final=final),
        grid=(2, n_ctx_rows // SEQ_T + n_lat),
        in_specs=in_specs,
        out_specs=row_spec,
        out_shape=jax.ShapeDtypeStruct((rows, MIX_W), jnp.bfloat16 if final else jnp.float32),
        scratch_shapes=[pltpu.VMEM((1, MIX_W), jnp.float32),
                        pltpu.VMEM((SEQ_T, MIX_W), jnp.float32),
                        pltpu.VMEM((SEQ_T, MIX_W), jnp.float32)],
        compiler_params=_cparams("parallel", "arbitrary"),
        name="lru_bwd" if reverse else "lru_fwd",
    )(*args)


def lru_branch(z, p, n_ctx_rows, n_lat_rows):
    def one(d, h_prev):
        return lru_pass(z, p['lru_conv_w'], p['lru_conv_b'], p['lru_wa'][d], p['lru_ba'][d], p['lru_wi'][d],
                        p['lru_bi'][d], p['lru_lambda'][d], n_ctx_rows, n_lat_rows, d == 1, h_prev)
    return one(1, one(0, None))


def _hg_level_matrix(reverse):
    t = np.arange(HG_C)[:, None]
    s = np.arange(HG_C)[None, :]
    x = t ^ s
    lv = np.where(x > 0, np.floor(np.log2(np.maximum(x, 1))).astype(np.int32), -1)
    lv = np.where(s < t, lv, -1).astype(np.int32)
    return jnp.asarray(lv.T if reverse else lv)


def _hg_pass_kernel(lv_ref, lb_ref, q_ref, f_ref, v_ref, *rest, reverse, final):
    if final:
        og_ref, oprev_ref, g_ref, out_ref, st_ref = rest
    else:
        out_ref, st_ref = rest

    @pl.when(pl.program_id(1) == 0)
    def _():
        st_ref[...] = jnp.zeros_like(st_ref)

    lv = lv_ref[...]
    at_level = [lv == lvl for lvl in range(HG_LEVELS)]
    n_tiles = HG_C // SUBLANES
    wb = HG_HB * HG_DK
    row = lax.broadcasted_iota(jnp.int32, (1, SUBLANES, wb), 1)
    for hb in range(HG_HEADS // HG_HB):
        sl = slice(hb * wb, (hb + 1) * wb)
        heads = [(hb * HG_HB + j, slice(j * HG_DK, (j + 1) * HG_DK)) for j in range(HG_HB)]
        x = f_ref[:, sl].astype(jnp.float32)
        lb = lb_ref[:, sl]
        e = jnp.exp(-jnp.abs(x))
        inv = 1.0 / (1.0 + e)
        pos = x >= 0
        sig = jnp.where(pos, inv, e * inv)
        sigm = jnp.where(pos, e * inv, inv)
        lsig = jnp.minimum(x, 0.0) - jnp.log(1.0 + e)
        lf = jnp.where(lb > 0, jnp.log(lb + (1.0 - lb) * sig), lsig)
        kk = (1.0 - lb) * sigm
        q = q_ref[:, sl].astype(jnp.float32)
        v = v_ref[:, sl]

        c = (lf * LOG2E).reshape(n_tiles, SUBLANES, wb)
        bt = c
        att = [None] * HG_HB
        for lvl in range(HG_LEVELS):
            m = 1 << lvl
            qm = (q * jnp.exp2(c).reshape(HG_C, wb)).astype(jnp.bfloat16)
            km = (kk * jnp.exp2(bt - c).reshape(HG_C, wb)).astype(jnp.bfloat16)
            for j, (_, hs) in enumerate(heads):
                d = lax.dot_general(qm[:, hs], km[:, hs], (((1,), (1,)), ((), ())),
                                    preferred_element_type=jnp.float32)
                att[j] = jnp.where(at_level[lvl], d, 0.0 if lvl == 0 else att[j])
            if m < SUBLANES:
                upper = (row & m) != 0
                down = pltpu.roll(bt, m, 1)
                up = pltpu.roll(bt, SUBLANES - m, 1)
                if reverse:
                    c = c + jnp.where(upper, 0.0, up)
                else:
                    c = c + jnp.where(upper, down, 0.0)
                bt = bt + jnp.where(upper, down, up)
            else:
                k = m // SUBLANES
                pair = (n_tiles // (2 * k), 2, k, SUBLANES, wb)
                c5, b5 = c.reshape(pair), bt.reshape(pair)
                tot2 = b5[:, 0] + b5[:, 1]
                if reverse:
                    c = jnp.stack([c5[:, 0] + b5[:, 1], c5[:, 1]], axis=1)
                else:
                    c = jnp.stack([c5[:, 0], c5[:, 1] + b5[:, 0]], axis=1)
                c = c.reshape(n_tiles, SUBLANES, wb)
                bt = jnp.stack([tot2, tot2], axis=1).reshape(n_tiles, SUBLANES, wb)

        qm = (q * jnp.exp2(c).reshape(HG_C, wb)).astype(jnp.bfloat16)
        km = (kk * jnp.exp2(bt - c).reshape(HG_C, wb)).astype(jnp.bfloat16)
        tot = jnp.exp2(bt[0, 0:1, :])
        diag = q * kk
        for j, (h, hs) in enumerate(heads):
            osl = slice(h * HG_DK, (h + 1) * HG_DK)
            st = st_ref[h]
            vh = v[:, hs]
            o = lax.dot_general(qm[:, hs], st.astype(jnp.bfloat16), (((1,), (1,)), ((), ())),
                                preferred_element_type=jnp.float32)
            o = o + jnp.dot(att[j].astype(jnp.bfloat16), vh, preferred_element_type=jnp.float32)
            o = o + jnp.sum(diag[:, hs], axis=-1, keepdims=True) * vh.astype(jnp.float32)
            st_ref[h] = st * tot[:, hs] + lax.dot_general(vh, km[:, hs], (((0,), (0,)), ((), ())),
                                                          preferred_element_type=jnp.float32)
            if final:
                o = o + oprev_ref[:, osl]
                y = o * lax.rsqrt(jnp.mean(o * o, axis=-1, keepdims=True) + EPS) * g_ref[:, osl]
                og = og_ref[:, osl].astype(jnp.float32)
                out_ref[:, osl] = (y * (og / (1.0 + jnp.exp(-og)))).astype(out_ref.dtype)
            else:
                out_ref[:, osl] = o


def hgrn_pass(z, lb, n_ctx_rows, n_lat_rows, reverse, o_prev=None, norm_g=None):
    rows = z.shape[0]
    n_ctx, n_lat = n_ctx_rows // HG_C, n_lat_rows // HG_C
    cb = HG_COL0 // MIX_W
    final = o_prev is not None
    rb = functools.partial(_row_block, n_ctx=n_ctx, n_lat=n_lat, reverse=reverse)

    def zspec(k):
        return pl.BlockSpec((HG_C, MIX_W), lambda b, j: (rb(b, j), cb + k))

    row_spec = pl.BlockSpec((HG_C, MIX_W), lambda b, j: (rb(b, j), 0))
    vec_spec = pl.BlockSpec((1, MIX_W), lambda b, j: (0, 0))
    in_specs = [pl.BlockSpec((HG_C, HG_C), lambda b, j: (0, 0)), vec_spec,
                zspec(0), zspec(2 if reverse else 1), zspec(3)]
    args = [_hg_level_matrix(reverse), lb.reshape(1, MIX_W), z, z, z]
    if final:
        in_specs += [zspec(4), row_spec, vec_spec]
        args += [z, o_prev, norm_g.reshape(1, MIX_W)]
    return pl.pallas_call(
        functools.partial(_hg_pass_kernel, reverse=reverse, final=final),
        grid=(2, n_ctx + n_lat),
        in_specs=in_specs,
        out_specs=row_spec,
        out_shape=jax.ShapeDtypeStruct((rows, MIX_W), jnp.bfloat16 if final else jnp.float32),
        scratch_shapes=[pltpu.VMEM((HG_HEADS, HG_DK, HG_DK), jnp.float32)],
        compiler_params=_cparams("parallel", "arbitrary"),
        name="hgrn_bwd" if reverse else "hgrn_fwd",
    )(*args)


def hgrn_branch(z, lb, norm_g, n_ctx_rows, n_lat_rows):
    o_f = hgrn_pass(z, lb, n_ctx_rows, n_lat_rows, False)
    return hgrn_pass(z, lb, n_ctx_rows, n_lat_rows, True, o_f, norm_g)


def _hy_conv_kernel(x_ref, w_ref, b_ref, out_ref, *, is_ctx):
    x = x_ref[...].astype(jnp.float32)
    t = lax.broadcasted_iota(jnp.int32, x.shape, 0)
    pos = t if is_ctx else t & (GRID_W - 1)
    last = SEQ_T - 1 if is_ctx else GRID_W - 1
    w = w_ref[...]
    y = b_ref[...] + w[1:2] * x
    y = y + w[0:1] * jnp.where(pos >= 1, pltpu.roll(x, 1, 0), 0.0)
    y = y + w[2:3] * jnp.where(pos < last, pltpu.roll(x, SEQ_T - 1, 0), 0.0)
    for g in range(3):
        out_ref[g] = y[:, g * MIX_W:(g + 1) * MIX_W].astype(out_ref.dtype)


def hy_conv(z, conv_w, conv_b, block0, n_rows, is_ctx):
    nb = n_rows // SEQ_T
    return pl.pallas_call(
        functools.partial(_hy_conv_kernel, is_ctx=is_ctx),
        grid=(2, nb),
        in_specs=[pl.BlockSpec((SEQ_T, 3 * MIX_W), lambda b, j: (block0 + b * nb + j, 0)),
                  pl.BlockSpec((3, 3 * MIX_W), lambda b, j: (0, 0)),
                  pl.BlockSpec((1, 3 * MIX_W), lambda b, j: (0, 0))],
        out_specs=pl.BlockSpec((3, None, SEQ_T, MIX_W), lambda b, j: (0, b, j, 0)),
        out_shape=jax.ShapeDtypeStruct((3, 2, n_rows, MIX_W), jnp.bfloat16),
        compiler_params=_cparams("parallel", "arbitrary"),
        name="hy_conv_ctx" if is_ctx else "hy_conv_lat",
    )(z, conv_w, conv_b.reshape(1, 3 * MIX_W))


def _hy_conv_lat_kernel(*refs):
    x_refs, (w_ref, b_ref, out_ref) = refs[:HY_PARTS], refs[HY_PARTS:]
    x = jnp.concatenate([r[...] for r in x_refs], axis=0).astype(jnp.float32)
    n = x.shape[0]
    pos = lax.broadcasted_iota(jnp.int32, x.shape, 0) & (GRID_W - 1)
    w = w_ref[...]
    y = b_ref[...] + w[1:2] * x
    y = y + w[0:1] * jnp.where(pos >= 1, pltpu.roll(x, 1, 0), 0.0)
    y = y + w[2:3] * jnp.where(pos < GRID_W - 1, pltpu.roll(x, n - 1, 0), 0.0)
    y = y.astype(out_ref.dtype).reshape(n // FFT_R, FFT_R, MIX_W)
    out_ref[...] = jnp.swapaxes(y, 0, 1)


def hy_conv_latent(z, conv_w, conv_b, row0, n_rows):
    part = FFT_SB * FFT_R // HY_PARTS
    nb = n_rows // (FFT_SB * FFT_R)

    def xspec(k):
        return pl.BlockSpec((part, MIX_W), lambda g, b, j: ((row0 + b * n_rows) // part + j * HY_PARTS + k, g))

    return pl.pallas_call(
        _hy_conv_lat_kernel,
        grid=(3, 2, nb),
        in_specs=[xspec(k) for k in range(HY_PARTS)] + [
            pl.BlockSpec((3, MIX_W), lambda g, b, j: (0, g)), pl.BlockSpec((1, MIX_W), lambda g, b, j: (0, g))],
        out_specs=pl.BlockSpec((None, None, FFT_R, FFT_SB, MIX_W), lambda g, b, j: (g, b, 0, j, 0)),
        out_shape=jax.ShapeDtypeStruct((3, 2, FFT_R, n_rows // FFT_R, MIX_W), jnp.bfloat16),
        compiler_params=_cparams("parallel", "parallel", "arbitrary"),
        name="hy_conv_lat",
    )(*([z] * HY_PARTS), conv_w, conv_b.reshape(1, 3 * MIX_W))


def _cis(num, den):
    ang = (2.0 * math.pi / den) * (num % den).astype(jnp.float32)
    return jnp.cos(ang), jnp.sin(ang)


def fft_matrices():
    r = FFT_R
    i = jnp.arange(r, dtype=jnp.int32)
    c, s = _cis(i[:, None] * i[None, :], r)
    half = r // 2
    fa_data = jnp.concatenate([jnp.concatenate([c[:, :half], s[:, :half]], axis=1),
                               jnp.concatenate([-s[:, :half], c[:, :half]], axis=1)], axis=0)
    fa_taps = jnp.concatenate([c, -s], axis=0)
    ct, st = _cis(i[:, None] * i[None, :], FFT_L)
    gc = c[None] * ct[:, None, :] - s[None] * st[:, None, :]
    gs = s[None] * ct[:, None, :] + c[None] * st[:, None, :]
    gr, gi = gc, -gs
    m1 = jnp.concatenate([jnp.concatenate([gr, -gi], axis=2), jnp.concatenate([gi, gr], axis=2)], axis=1)
    er, ei = c[:half] / FFT_L, s[:half] / FFT_L
    e2 = jnp.concatenate([jnp.concatenate([er, -ei], axis=1), jnp.concatenate([ei, er], axis=1)], axis=0)
    bf = jnp.bfloat16
    return dict(fa_data=fa_data.astype(bf), fa_taps=fa_taps.astype(bf), m1=m1.astype(bf), e2=e2.astype(bf))


def _fft_a_kernel(f_ref, x_ref, o_ref, *, two):
    f = f_ref[...]
    res = []
    for s in range(FFT_SB):
        x = jnp.concatenate([x_ref[0, s], x_ref[1, s]], axis=0) if two else x_ref[s]
        res.append(jnp.dot(f, x, preferred_element_type=jnp.float32).astype(o_ref.dtype))
    o_ref[...] = jnp.swapaxes(jnp.stack(res), 0, 1).reshape(o_ref.shape)


def fft_stage_a(fmat, x, g):
    two = x.ndim == 5
    xspec = (pl.BlockSpec((None, 2, FFT_SB, FFT_R // 2, MIX_W), lambda j: (g, 0, j, 0, 0)) if two
             else pl.BlockSpec((None, FFT_SB, FFT_R, MIX_W), lambda j: (g, j, 0, 0)))
    return pl.pallas_call(
        functools.partial(_fft_a_kernel, two=two),
        grid=(FFT_R // FFT_SB,),
        in_specs=[pl.BlockSpec((2 * FFT_R, FFT_R), lambda j: (0, 0)), xspec],
        out_specs=pl.BlockSpec((2, FFT_R, FFT_SB, MIX_W), lambda j: (0, 0, j, 0)),
        out_shape=jax.ShapeDtypeStruct((2, FFT_R, FFT_R, MIX_W), jnp.bfloat16),
        compiler_params=_cparams("parallel"),
        name="fft_stage_a",
    )(fmat, x)


def _fft_mid_kernel(m1_ref, a_ref, *rest, conv):
    if conv:
        h_ref, o_ref = rest
    else:
        s_ref, o_ref = rest
    r = FFT_R
    for k in range(FFT_KB):
        a = a_ref[:, k].reshape(2 * r, MIX_W)
        x = jnp.dot(m1_ref[k], a, preferred_element_type=jnp.float32)
        if conv:
            h = h_ref[k].astype(jnp.float32)
            xr, xi, hr, hi = x[:r], x[r:], h[:r], h[r:]
            zc = jnp.concatenate([xr * hr - xi * hi, xr * hi + xi * hr], axis=0).astype(jnp.bfloat16)
            p = lax.dot_general(m1_ref[k], zc, (((0,), (0,)), ((), ())), preferred_element_type=jnp.float32)
            o_ref[:, k] = p.reshape(2, r, MIX_W).astype(o_ref.dtype)
        else:
            o_ref[k] = (x * s_ref[...]).astype(o_ref.dtype)


def fft_mid(m1, a, h=None, scale=None):
    conv = h is not None
    r = FFT_R
    mspec = pl.BlockSpec((FFT_KB, 2 * r, 2 * r), lambda j: (j, 0, 0))
    aspec = pl.BlockSpec((2, FFT_KB, r, MIX_W), lambda j: (0, j, 0, 0))
    hspec = pl.BlockSpec((FFT_KB, 2 * r, MIX_W), lambda j: (j, 0, 0))
    if conv:
        in_specs, args = [mspec, aspec, hspec], (m1, a, h)
        out_specs, out_shape = aspec, jax.ShapeDtypeStruct((2, r, r, MIX_W), jnp.bfloat16)
    else:
        in_specs, args = [mspec, aspec, pl.BlockSpec((1, MIX_W), lambda j: (0, 0))], (m1, a, scale)
        out_specs, out_shape = hspec, jax.ShapeDtypeStruct((r, 2 * r, MIX_W), jnp.bfloat16)
    return pl.pallas_call(
        functools.partial(_fft_mid_kernel, conv=conv),
        grid=(r // FFT_KB,),
        in_specs=in_specs, out_specs=out_specs, out_shape=out_shape,
        compiler_params=_cparams("parallel"),
        name="fft_mid_conv" if conv else "fft_mid_filter",
    )(*args)


def _fft_out_kernel(e_ref, p_ref, mul_ref, add_ref, bias_ref, o_ref, *, time_order):
    e = e_ref[...]
    half = FFT_R // 2
    bias = bias_ref[...]
    p = jnp.swapaxes(p_ref[...].reshape(2 * FFT_R, FFT_SB, MIX_W), 0, 1)
    res = [[], []]
    for s in range(FFT_SB):
        y = jnp.dot(e, p[s], preferred_element_type=jnp.float32)
        for b in range(2):
            yb = y[b * half:(b + 1) * half] + add_ref[b, s].astype(jnp.float32) * bias
            res[b].append((mul_ref[b, s].astype(jnp.float32) * yb).astype(o_ref.dtype))
    for b in range(2):
        out = jnp.stack(res[b])
        o_ref[b] = jnp.swapaxes(out, 0, 1) if time_order else out


def fft_out(e2, p, mul, gm, add, ga, bias, time_order):
    half = FFT_R // 2

    def bspec(g):
        return pl.BlockSpec((None, 2, FFT_SB, half, MIX_W), lambda j: (g, 0, j, 0, 0))

    if time_order:
        out_spec = pl.BlockSpec((None, 2, half, FFT_SB, MIX_W), lambda j: (0, 0, 0, j, 0))
        out_shape = jax.ShapeDtypeStruct((1, 2, half, FFT_R, MIX_W), jnp.bfloat16)
    else:
        out_spec, out_shape = bspec(0), jax.ShapeDtypeStruct((1, 2, FFT_R, half, MIX_W), jnp.bfloat16)
    return pl.pallas_call(
        functools.partial(_fft_out_kernel, time_order=time_order),
        grid=(FFT_R // FFT_SB,),
        in_specs=[pl.BlockSpec((FFT_R, 2 * FFT_R), lambda j: (0, 0)),
                  pl.BlockSpec((2, FFT_R, FFT_SB, MIX_W), lambda j: (0, 0, j, 0)), bspec(gm), bspec(ga),
                  pl.BlockSpec((1, MIX_W), lambda j: (0, 0))],
        out_specs=out_spec,
        out_shape=out_shape,
        compiler_params=_cparams("parallel", vmem=FFT_OUT_VMEM_BYTES),
        name="fft_out",
    )(e2, p, mul, add, bias)


def hyena_spectra(taps, inv_norm, mats):
    return [fft_mid(mats['m1'], fft_stage_a(mats['fa_taps'], taps, order), scale=inv_norm[order])
            for order in range(2)]


def hyena_latent(xs, spectra, bias, mats):
    u, gu = xs, 2
    for order in range(2):
        p = fft_mid(mats['m1'], fft_stage_a(mats['fa_data'], u, gu), h=spectra[order])
        u, gu = fft_out(mats['e2'], p, xs, order, u, gu, bias[order].reshape(1, MIX_W), order == 1), 0
    return u.reshape(2, FFT_L // 2, MIX_W)


def hyena_context(xc, taps, inv_norm, bias):
    n = xc.shape[2]
    length = 2 * n
    k = jnp.arange(length, dtype=jnp.int32)
    c, s = _cis(k[:, None] * k[None, :], length)
    fr, fi = c[:, :n], -s[:, :n]
    m_fwd = jnp.concatenate([jnp.concatenate([fr, -fi], axis=1), jnp.concatenate([fi, fr], axis=1)], axis=0)
    m_taps = jnp.concatenate([c, -s], axis=0)
    er, ei = c[:n] / length, s[:n] / length
    m_inv = jnp.concatenate([jnp.concatenate([er, -ei], axis=1), jnp.concatenate([ei, er], axis=1)], axis=0)
    x1, x2, u = (xc[g].astype(jnp.float32) for g in range(3))
    for order, gate in enumerate((x1, x2)):
        h = mm(m_taps, taps[order], name="ctx_dft_taps") * inv_norm[order]
        w = mm(m_fwd, u.reshape(length, MIX_W), name="ctx_dft_fwd")
        wr, wi, hr, hi = w[:length], w[length:], h[:length], h[length:]
        zc = jnp.concatenate([wr * hr - wi * hi, wr * hi + wi * hr], axis=0)
        y = mm(m_inv, zc, name="ctx_dft_inv").reshape(2, n, MIX_W)
        u = gate * (y + u * bias[order])
    return u.astype(jnp.bfloat16)


def hyena_features(n, ff1, ff1_b, freq, ff2, ff2_b):
    k = jnp.arange(n, dtype=jnp.int32)
    pos = jnp.stack([k, (n - k) % n]).astype(jnp.float32)
    t = pos / (n - 1)
    fw = (2.0 * math.pi * pos / n)[..., None] * jnp.linspace(1e-4, HY_BANDS - 1, HY_BANDS, dtype=jnp.float32)
    feats = jnp.concatenate([t[..., None], jnp.cos(fw), -jnp.sin(fw)], axis=-1)
    h = jnp.sin(freq * (feats @ ff1 + ff1_b))
    return jnp.sin(freq * (h @ ff2 + ff2_b))


def _filter_kernel(feat_ref, w_ref, delta_ref, taps_ref, sum_ref, *, n, tm, split):
    dr = pl.program_id(1)
    i = pl.program_id(2)
    row = i * tm + lax.broadcasted_iota(jnp.int32, (tm, MIX_W), 0)
    lag = jnp.where(dr == 0, row, jnp.where(row == 0, 0, n - row))
    t = lag.astype(jnp.float32) / (n - 1)
    h = jnp.dot(feat_ref[...].astype(jnp.bfloat16), w_ref[...].astype(jnp.bfloat16),
                preferred_element_type=jnp.float32)
    h = h * jnp.exp(-t * delta_ref[...])

    @pl.when((dr == 0) & (i == 0))
    def _():
        sum_ref[...] = jnp.zeros_like(sum_ref)

    sum_ref[...] += jnp.sum(jnp.abs(h), axis=0, keepdims=True)
    taps = jnp.where((dr == 0) | (row != 0), h, 0.0).astype(taps_ref.dtype)
    if split:
        taps = jnp.swapaxes(taps.reshape(tm // FFT_R, FFT_R, MIX_W), 0, 1)
    taps_ref[...] = taps


def hyena_filter_taps(feats, ff3):
    _, n, nf = feats.shape
    split = 2 * n == FFT_L
    tm = FFT_SB * FFT_R if split else n
    nb = n // tm
    w = ff3.reshape(nf, 2, 2, MIX_W).transpose(1, 2, 0, 3)
    deltas = jnp.abs(jnp.linspace(math.log(HY_TARGET) / HY_SLOW_PCT, math.log(HY_TARGET) / HY_FAST_PCT,
                                  MIX_W, dtype=jnp.float32)).reshape(1, MIX_W)
    if split:
        tspec = pl.BlockSpec((None, FFT_R, FFT_SB, MIX_W), lambda o, dr, i: (o, 0, dr * nb + i, 0))
        tshape = jax.ShapeDtypeStruct((2, FFT_R, FFT_R, MIX_W), jnp.bfloat16)
    else:
        tspec = pl.BlockSpec((None, tm, MIX_W), lambda o, dr, i: (o, dr * nb + i, 0))
        tshape = jax.ShapeDtypeStruct((2, 2 * n, MIX_W), jnp.bfloat16)
    return pl.pallas_call(
        functools.partial(_filter_kernel, n=n, tm=tm, split=split),
        grid=(2, 2, nb),
        in_specs=[pl.BlockSpec((None, tm, nf), lambda o, dr, i: (dr, i, 0)),
                  pl.BlockSpec((None, None, nf, MIX_W), lambda o, dr, i: (o, dr, 0, 0)),
                  pl.BlockSpec((1, MIX_W), lambda o, dr, i: (0, 0))],
        out_specs=[tspec, pl.BlockSpec((None, 1, MIX_W), lambda o, dr, i: (o, 0, 0))],
        out_shape=[tshape, jax.ShapeDtypeStruct((2, 1, MIX_W), jnp.float32)],
        compiler_params=_cparams("arbitrary", "arbitrary", "arbitrary"),
        name="hyena_filter",
    )(feats, w, deltas)


def _merge_kernel(yhc_ref, yhl_ref, yl_ref, yg_ref, w_ref, g0_ref, g1_ref, g2_ref, o_ref, *, n_ctx_tiles):
    ys = (_pick_rows(yhc_ref, yhl_ref, n_ctx_tiles), yl_ref[...], yg_ref[...])
    acc = None
    for j, (y, g_ref) in enumerate(zip(ys, (g0_ref, g1_ref, g2_ref))):
        gate = 1.0 / (1.0 + jnp.exp(-g_ref[...].astype(jnp.float32)))
        t = gate * jnp.dot(y, w_ref[j], preferred_element_type=jnp.float32)
        acc = t if acc is None else acc + t
    o_ref[...] = acc.astype(o_ref.dtype)


def merge_branches(z, y_hy_ctx, y_hy_lat, y_lru, y_hg, w_branch, layer):
    rows = z.shape[0]
    tm = ROW_T // 2
    n_ctx_tiles = y_hy_ctx.shape[0] // tm
    yspec = pl.BlockSpec((tm, MIX_W), lambda i: (i, 0))

    def gspec(k):
        return pl.BlockSpec((tm, D_MODEL), lambda i: (i, GATE_COL0 // D_MODEL + k))

    return pl.pallas_call(
        functools.partial(_merge_kernel, n_ctx_tiles=n_ctx_tiles),
        grid=(rows // tm,),
        in_specs=[*_split_row_specs(tm, MIX_W, n_ctx_tiles), yspec, yspec,
                  pl.BlockSpec((None, N_BRANCH, MIX_W, D_MODEL), lambda i: (layer, 0, 0, 0)),
                  gspec(0), gspec(1), gspec(2)],
        out_specs=pl.BlockSpec((tm, D_MODEL), lambda i: (i, 0)),
        out_shape=jax.ShapeDtypeStruct((rows, D_MODEL), jnp.bfloat16),
        compiler_params=_cparams("parallel"),
        name="merge",
    )(y_hy_ctx, y_hy_lat, y_lru, y_hg, w_branch, z, z, z)


def _out_proj_kernel(a_ref, w_ref, *refs, n_ctx_tiles):
    *x_refs, g1_ref, gt_ref, g2_ref, sh_ref, sc_ref, wr_ref, xo_ref, h_ref, lg_ref = refs
    x = _pick_rows(*x_refs, n_ctx_tiles) if len(x_refs) == 2 else x_refs[0][...]
    y = jnp.dot(a_ref[...], w_ref[...], preferred_element_type=jnp.float32)
    y = y * lax.rsqrt(jnp.mean(y * y, axis=-1, keepdims=True) + EPS) * g1_ref[...]
    x = x + gt_ref[...] * y
    xo_ref[...] = x
    h = x * lax.rsqrt(jnp.mean(x * x, axis=-1, keepdims=True) + EPS) * g2_ref[...]
    h = (h * (1.0 + sc_ref[...]) + sh_ref[...]).astype(jnp.bfloat16)
    h_ref[...] = h
    lg_ref[...] = jnp.dot(h, wr_ref[...], preferred_element_type=jnp.float32)


def out_proj(acc, w_out, layer, x, g1, gt, g2, sh, sc, w_router, n_ctx_rows, n_lat_rows):
    tm = ROW_T // 2
    xspecs, xs, rows, d = _row_operand(x, tm)
    n_ctx_tiles = 2 * n_ctx_rows // tm
    idx = functools.partial(_mod_index, n_ctx_tiles=n_ctx_tiles, tiles_per_batch=n_lat_rows // tm)
    row = pl.BlockSpec((tm, d), lambda i: (i, 0))
    vec = pl.BlockSpec((1, d), lambda i: (0, 0))
    mod = pl.BlockSpec((None, 1, d), lambda i: (idx(i), 0, 0))
    wr = jnp.zeros((d, ROUTER_PAD), jnp.bfloat16).at[:, :N_EXPERTS].set(w_router.astype(jnp.bfloat16))
    return pl.pallas_call(
        functools.partial(_out_proj_kernel, n_ctx_tiles=n_ctx_tiles),
        grid=(rows // tm,),
        in_specs=[row, pl.BlockSpec((None, d, d), lambda i: (layer, 0, 0)), *xspecs, vec, mod, vec, mod, mod,
                  pl.BlockSpec((d, ROUTER_PAD), lambda i: (0, 0))],
        out_specs=[row, row, pl.BlockSpec((tm, ROUTER_PAD), lambda i: (i, 0))],
        out_shape=[jax.ShapeDtypeStruct((rows, d), jnp.float32), jax.ShapeDtypeStruct((rows, d), jnp.bfloat16),
                   jax.ShapeDtypeStruct((rows, ROUTER_PAD), jnp.float32)],
        compiler_params=_cparams("parallel"),
        name="out_proj",
    )(acc, w_out, *xs, g1.reshape(1, d), gt, g2.reshape(1, d), sh, sc, wr)


def _ffn_up_kernel(x_ref, wg_ref, wu_ref, h_ref):
    x = x_ref[...]
    hg = jnp.dot(x, wg_ref[...].astype(jnp.bfloat16), preferred_element_type=jnp.float32)
    hu = jnp.dot(x, wu_ref[...].astype(jnp.bfloat16), preferred_element_type=jnp.float32)
    h_ref[...] = (hg / (1.0 + jnp.exp(-hg)) * hu).astype(h_ref.dtype)


def _ffn_down_kernel(h_ref, wd_ref, g_ref, o_ref):
    y = jnp.dot(h_ref[...], wd_ref[...].astype(jnp.bfloat16), preferred_element_type=jnp.float32)
    o_ref[...] = (y * g_ref[...]).astype(o_ref.dtype)


def expert_ffn(xe, w_gate, w_up, w_down, layer, g):
    ne, r, d = xe.shape
    dff = w_gate.shape[-1]
    hid = pl.pallas_call(
        _ffn_up_kernel,
        grid=(ne, dff // FFN_TF),
        in_specs=[pl.BlockSpec((None, r, d), lambda e, f: (e, 0, 0)),
                  pl.BlockSpec((None, None, d, FFN_TF), lambda e, f: (layer, e, 0, f)),
                  pl.BlockSpec((None, None, d, FFN_TF), lambda e, f: (layer, e, 0, f))],
        out_specs=pl.BlockSpec((None, r, FFN_TF), lambda e, f: (e, 0, f)),
        out_shape=jax.ShapeDtypeStruct((ne, r, dff), jnp.bfloat16),
        compiler_params=_cparams("parallel", "arbitrary"),
        name="ffn_up",
    )(xe, w_gate, w_up)
    return pl.pallas_call(
        _ffn_down_kernel,
        grid=(ne, d // FFN_TD),
        in_specs=[pl.BlockSpec((None, r, dff), lambda e, j: (e, 0, 0)),
                  pl.BlockSpec((None, None, dff, FFN_TD), lambda e, j: (layer, e, 0, j)),
                  pl.BlockSpec((None, r, 1), lambda e, j: (e, 0, 0))],
        out_specs=pl.BlockSpec((None, r, FFN_TD), lambda e, j: (e, 0, j)),
        out_shape=jax.ShapeDtypeStruct((ne, r, d), jnp.bfloat16),
        compiler_params=_cparams("parallel", "arbitrary"),
        name="ffn_down",
    )(hid, w_down, g)


def route(logits, row0, bsz, n):
    cap = EC_CAPACITY * n // N_EXPERTS
    aff = jax.nn.softmax(logits[row0:row0 + bsz * n, :N_EXPERTS].reshape(bsz, n, N_EXPERTS), axis=-1)
    g, idx = lax.top_k(jnp.swapaxes(aff, 1, 2), cap)
    flat = idx + (row0 + jnp.arange(bsz, dtype=idx.dtype) * n)[:, None, None]
    return (jnp.swapaxes(g, 0, 1).reshape(N_EXPERTS, bsz * cap),
            jnp.swapaxes(flat, 0, 1).reshape(N_EXPERTS, bsz * cap))


def ec_moe(h, logits, w_gate, w_up, w_down, layer, segments):
    gs, flats = zip(*(route(logits, *seg) for seg in segments))
    flat, g = lax.sort_key_val(jnp.concatenate(flats, axis=1), jnp.concatenate(gs, axis=1), dimension=1)
    return expert_ffn(h[flat], w_gate, w_up, w_down, layer, g[..., None]), flat


def moe_plan(flat, n_rows):
    ne, r = flat.shape
    nwin = r // MOE_WIN
    n_tiles = n_rows // MOE_TT
    bounds = jnp.arange(n_tiles + 1, dtype=flat.dtype) * MOE_TT
    p = jnp.sum(flat[None, :, :] < bounds[:, None, None], axis=2, dtype=jnp.int32)
    p0, p1 = p[:-1], p[1:]
    w0 = p0 // MOE_WIN
    nw = jnp.where(p1 > p0, (p1 - 1) // MOE_WIN - w0 + 1, 0)
    off = jnp.cumsum(nw, axis=1)
    q = jnp.arange(MOE_MAXQ, dtype=jnp.int32)
    e_of_q = jnp.minimum(jnp.sum(q[None, :, None] >= off[:, None, :], axis=2, dtype=jnp.int32), ne - 1)
    pick = e_of_q[:, :, None] == jnp.arange(ne, dtype=jnp.int32)
    w = q[None, :] + jnp.sum(jnp.where(pick, (w0 - (off - nw))[:, None, :], 0), axis=2)
    items = jnp.clip(e_of_q * nwin + w, 0, ne * nwin - 1)
    return items.astype(jnp.int32), off[:, -1].astype(jnp.int32)


def _combine_kernel(items_ref, count_ref, tok_ref, ye_ref, x_ref, g_ref, gt_ref, *rest, nxt, skip):
    if nxt:
        gn_ref, sh_ref, sc_ref, o_ref, u_ref, stage, sem, acc_ref = rest
    else:
        o_ref, stage, sem, acc_ref = rest
    i = pl.program_id(0)
    t = i + skip
    n = count_ref[t]
    n_groups = (n + MOE_GROUP - 1) // MOE_GROUP

    @pl.when(i == 0)
    def _():
        stage[...] = jnp.zeros_like(stage)

    def window_copy(tile, g, buf, k):
        item = items_ref[tile, jnp.minimum(MOE_GROUP * g + k, MOE_MAXQ - 1)]
        return pltpu.make_async_copy(ye_ref.at[pl.ds(item * MOE_WIN, MOE_WIN)],
                                     stage.at[buf, pl.ds(k * MOE_WIN, MOE_WIN)], sem.at[buf])

    def for_group(tile, g, buf, wait):
        for k in range(MOE_GROUP):
            @pl.when(MOE_GROUP * g + k < count_ref[tile])
            def _():
                cp = window_copy(tile, g, buf, k)
                cp.wait() if wait else cp.start()

    def start_first_groups(tile):
        for j in range(MOE_NBUF - 1):
            for_group(tile, j, j, False)

    acc_ref[...] = jnp.zeros_like(acc_ref)

    @pl.when(i == 0)
    def _():
        start_first_groups(t)

    row = t * MOE_TT + lax.broadcasted_iota(jnp.int32, (MOE_TT, MOE_GROUP * MOE_WIN), 0)

    def body(g, carry):
        buf = g % MOE_NBUF
        ahead = g + MOE_NBUF - 1

        @pl.when(ahead < n_groups)
        def _():
            for_group(t, ahead, ahead % MOE_NBUF, False)

        for_group(t, g, buf, True)
        toks = []
        for k in range(MOE_GROUP):
            qk = MOE_GROUP * g + k
            item = items_ref[t, jnp.minimum(qk, MOE_MAXQ - 1)]
            toks.append(jnp.where(qk < n, tok_ref[pl.ds(item, 1), :], -1))
        onehot = jnp.where(row == jnp.concatenate(toks, axis=1), 1.0, 0.0).astype(jnp.bfloat16)
        acc_ref[...] += jnp.dot(onehot, stage[buf], preferred_element_type=jnp.float32)
        return carry

    lax.fori_loop(0, n_groups, body, 0)

    @pl.when(i + 1 < pl.num_programs(0))
    def _():
        start_first_groups(t + 1)

    y = acc_ref[...]
    y = y * lax.rsqrt(jnp.mean(y * y, axis=-1, keepdims=True) + EPS) * g_ref[...]
    x = x_ref[...] + gt_ref[...] * y
    if nxt:
        u = x * lax.rsqrt(jnp.mean(x * x, axis=-1, keepdims=True) + EPS) * gn_ref[...]
        u_ref[...] = (u * (1.0 + sc_ref[...]) + sh_ref[...]).astype(u_ref.dtype)
    o_ref[...] = x


def moe_combine(x, ye, flat, g, gt, n_ctx_rows, n_lat_rows, nxt=None):
    rows, d = x.shape
    ne, r, _ = ye.shape
    items, count = moe_plan(flat, rows)
    n_ctx_tiles = 2 * n_ctx_rows // MOE_TT
    idx = functools.partial(_mod_index, n_ctx_tiles=n_ctx_tiles, tiles_per_batch=n_lat_rows // MOE_TT)
    skip = 0 if nxt else n_ctx_tiles
    row = pl.BlockSpec((MOE_TT, d), lambda i, *_: (i + skip, 0))
    vec = pl.BlockSpec((1, d), lambda i, *_: (0, 0))
    mod = pl.BlockSpec((None, 1, d), lambda i, *_: (idx(i + skip), 0, 0))
    out_row = pl.BlockSpec((MOE_TT, d), lambda i, *_: (i, 0))
    n_out = rows - skip * MOE_TT
    in_specs = [pl.BlockSpec((ne * r // MOE_WIN, MOE_WIN), lambda i, *_: (0, 0)),
                pl.BlockSpec(memory_space=pl.ANY), row, vec, mod]
    args = [flat.reshape(ne * r // MOE_WIN, MOE_WIN), ye.reshape(ne * r, d), x, g.reshape(1, d), gt]
    out_specs, out_shape = out_row, jax.ShapeDtypeStruct((n_out, d), jnp.float32)
    if nxt:
        in_specs += [vec, mod, mod]
        args += [nxt[0].reshape(1, d), nxt[1], nxt[2]]
        out_specs = [out_row, out_row]
        out_shape = [out_shape, jax.ShapeDtypeStruct((n_out, d), jnp.bfloat16)]
    return pl.pallas_call(
        functools.partial(_combine_kernel, nxt=bool(nxt), skip=skip),
        grid_spec=pltpu.PrefetchScalarGridSpec(
            num_scalar_prefetch=2,
            grid=(n_out // MOE_TT,),
            in_specs=in_specs, out_specs=out_specs,
            scratch_shapes=[pltpu.VMEM((MOE_NBUF, MOE_GROUP * MOE_WIN, d), jnp.bfloat16),
                            pltpu.SemaphoreType.DMA((MOE_NBUF,)),
                            pltpu.VMEM((MOE_TT, d), jnp.float32)]),
        out_shape=out_shape,
        compiler_params=_cparams("arbitrary"),
        name="moe_combine",
    )(items, count, *args)


def kernel(x, c, ctx, c_ctx, w_mod, b_mod, norm_g, w_in, hy_conv_w, hy_conv_b, hy_ff1, hy_ff1_b, hy_freq,
           hy_ff2, hy_ff2_b, hy_ff3, hy_bias, lru_conv_w, lru_conv_b, lru_wa, lru_ba, lru_wi, lru_bi,
           lru_lambda, hg_lb_logits, hg_norm_g, w_branch, w_out, w_router, w_gate, w_up, w_down):
    bsz, n_lat, d = x.shape
    n_ctx = ctx.shape[1]
    assert bsz == 2 and d == D_MODEL and n_ctx == SEQ_T and 2 * n_lat == FFT_L and n_lat % ROW_T == 0
    gam = jax.nn.softmax(hg_lb_logits.astype(jnp.float32), axis=0)
    lb_all = jnp.maximum(jnp.cumsum(gam, axis=0) - gam[:1], 0.0)
    mats = fft_matrices()
    cvec = jnp.zeros((SUBLANES, d), jnp.float32).at[:bsz].set(jax.nn.silu(c)).at[bsz].set(jax.nn.silu(c_ctx))
    n_ctx_all = bsz * n_ctx
    xa = (ctx.reshape(n_ctx_all, d), x.reshape(bsz * n_lat, d))
    w_branch_b, w_out_b = w_branch.astype(jnp.bfloat16), w_out.astype(jnp.bfloat16)
    mods, filters = [], []
    for l in range(DEPTH):
        mod = mm(cvec, w_mod, layer=l, name="adaln")[:bsz + 1] + b_mod[l]
        mods.append([mod.reshape(bsz + 1, 1, 6, d)[:, :, k] for k in range(6)])
        ffp = (hy_ff1[l], hy_ff1_b[l], hy_freq[l], hy_ff2[l], hy_ff2_b[l])
        taps_l, norm_l = hyena_filter_taps(hyena_features(n_lat, *ffp), hy_ff3[l])
        taps_c, norm_c = hyena_filter_taps(hyena_features(n_ctx, *ffp), hy_ff3[l])
        filters.append((hyena_spectra(taps_l, 1.0 / norm_l, mats), taps_c, 1.0 / norm_c))
    u = prenorm(xa, norm_g[0, 0], mods[0][0], mods[0][1], n_ctx, n_lat)
    for l in range(DEPTH):
        last = l == DEPTH - 1
        p = {'lru_conv_w': lru_conv_w[l], 'lru_conv_b': lru_conv_b[l], 'lru_wa': lru_wa[l], 'lru_ba': lru_ba[l],
             'lru_wi': lru_wi[l], 'lru_bi': lru_bi[l], 'lru_lambda': lru_lambda[l]}
        sh1, sc1, gt1, sh2, sc2, gt2 = mods[l]

        z = mm(u, w_in, jnp.bfloat16, tm=IN_PROJ_TM, tn=IN_PROJ_TN, layer=l, name="in_proj")

        y_lru = lru_branch(z, p, n_ctx, n_lat)
        y_hg = hgrn_branch(z, lb_all[l], hg_norm_g[l], n_ctx, n_lat)

        spectra, taps_c, inv_norm_c = filters[l]
        xc_l = hy_conv_latent(z, hy_conv_w[l], hy_conv_b[l], n_ctx_all, n_lat)
        y_hy_l = hyena_latent(xc_l, spectra, hy_bias[l], mats)
        xc_c = hy_conv(z, hy_conv_w[l], hy_conv_b[l], 0, n_ctx, True)
        y_hy_c = hyena_context(xc_c, taps_c, inv_norm_c, hy_bias[l])

        acc = merge_branches(z, y_hy_c.reshape(n_ctx_all, MIX_W), y_hy_l.reshape(bsz * n_lat, MIX_W), y_lru, y_hg,
                             w_branch_b, l)
        xa, h, logits = out_proj(acc, w_out_b, l, xa, norm_g[l, 1], gt1, norm_g[l, 2], sh2, sc2,
                                 w_router[l], n_ctx, n_lat)
        segments = [(n_ctx_all, bsz, n_lat)] if last else [(n_ctx_all, bsz, n_lat), (0, bsz, n_ctx)]
        ye, flat = ec_moe(h, logits, w_gate, w_up, w_down, l, segments)
        if last:
            xa = moe_combine(xa, ye, flat, norm_g[l, 3], gt2, n_ctx, n_lat)
        else:
            nxt = (norm_g[l + 1, 0], mods[l + 1][0], mods[l + 1][1])
            xa, u = moe_combine(xa, ye, flat, norm_g[l, 3], gt2, n_ctx, n_lat, nxt)
    return xa.reshape(bsz, n_lat, d)
```

```python
import functools
import math

import numpy as np
import jax
import jax.numpy as jnp
from jax import lax
from jax.experimental import pallas as pl
from jax.experimental.pallas import tpu as pltpu

D_MODEL = 2048
DEPTH = 2
GRID_W = 64
MIX_W = D_MODEL // 2
N_BRANCH = 3
HY_BANDS = 16
HY_TARGET = 1e-2
HY_FAST_PCT = 0.3
HY_SLOW_PCT = 1.5
LRU_BLOCKS = 8
LRU_BS = MIX_W // LRU_BLOCKS
LRU_C = 8.0
HG_HEADS = 8
HG_DK = MIX_W // HG_HEADS
N_EXPERTS = 16
EC_CAPACITY = 2
EPS = 1e-6
TINY = 1e-30
LOG2E = 1.4426950408889634
HY_COL0 = 0
LRU_COL0 = 3 * MIX_W
HG_COL0 = 5 * MIX_W
GATE_COL0 = 10 * MIX_W
IN_COLS = GATE_COL0 + N_BRANCH * D_MODEL

SUBLANES = 8
LANES = 128
SEQ_T = 256
HG_C = 128
HG_LEVELS = 7
HG_HB = 8
ROW_T = 512
IN_PROJ_TM = 2816
IN_PROJ_TN = 512
FFT_R = 128
FFT_L = FFT_R * FFT_R
FFT_KB = 8
FFT_SB = 16
HY_PARTS = 4
ROUTER_PAD = LANES
FFN_TF = 256
FFN_TD = 512
MOE_TT = 512
MOE_WIN = 32
MOE_GROUP = 16
MOE_NBUF = 3
MOE_MAXQ = N_EXPERTS * (MOE_TT // MOE_WIN + 1)
VMEM_LIMIT_BYTES = 48 * 1024 * 1024
FFT_OUT_VMEM_BYTES = 56 * 1024 * 1024
MERGE_OUT_VMEM_BYTES = 56 * 1024 * 1024


def _cparams(*sem, vmem=VMEM_LIMIT_BYTES):
    return pltpu.CompilerParams(dimension_semantics=sem, vmem_limit_bytes=vmem)


def _pick_tile(n, pref):
    for t in (pref, 1024, 512, 256, 128):
        if t <= n and n % t == 0:
            return t
    return n


def _row_block(b, j, n_ctx, n_lat, reverse):
    if reverse:
        kc, kl = n_ctx - 1 - j, n_ctx + n_lat - 1 - j
    else:
        kc, kl = j, j - n_ctx
    return jnp.where(j < n_ctx, b * n_ctx + kc, 2 * n_ctx + b * n_lat + kl)


def _mm_kernel(a_ref, b_ref, o_ref):
    o_ref[...] = jnp.dot(a_ref[...].astype(jnp.bfloat16), b_ref[...].astype(jnp.bfloat16),
                         preferred_element_type=jnp.float32).astype(o_ref.dtype)


def mm(a, b, out_dtype=jnp.float32, tm=512, tn=1024, layer=None, name="mm"):
    m, k = a.shape
    n = b.shape[-1]
    tm = _pick_tile(m, tm)
    tn = _pick_tile(n, tn)
    if layer is None:
        bspec = pl.BlockSpec((k, tn), lambda i, j: (0, j))
    else:
        bspec = pl.BlockSpec((None, k, tn), lambda i, j: (layer, 0, j))
    return pl.pallas_call(
        _mm_kernel,
        grid=(m // tm, n // tn),
        in_specs=[pl.BlockSpec((tm, k), lambda i, j: (i, 0)), bspec],
        out_specs=pl.BlockSpec((tm, tn), lambda i, j: (i, j)),
        out_shape=jax.ShapeDtypeStruct((m, n), out_dtype),
        compiler_params=_cparams("parallel", "arbitrary"),
        name=name,
    )(a, b)


def _mod_index(i, n_ctx_tiles, tiles_per_batch):
    return jnp.where(i < n_ctx_tiles, 2, (i - n_ctx_tiles) // tiles_per_batch)


def _split_row_specs(tile, width, n_ctx_tiles):
    return (pl.BlockSpec((tile, width), lambda i: (jnp.minimum(i, n_ctx_tiles - 1), 0)),
            pl.BlockSpec((tile, width), lambda i: (jnp.maximum(i - n_ctx_tiles, 0), 0)))


def _pick_rows(ctx_ref, lat_ref, n_ctx_tiles):
    return jnp.where(pl.program_id(0) < n_ctx_tiles, ctx_ref[...], lat_ref[...])


def _row_operand(x, tile):
    if isinstance(x, tuple):
        d = x[0].shape[1]
        return list(_split_row_specs(tile, d, x[0].shape[0] // tile)), list(x), x[0].shape[0] + x[1].shape[0], d
    return [pl.BlockSpec((tile, x.shape[1]), lambda i: (i, 0))], [x], x.shape[0], x.shape[1]


def _prenorm_kernel(*refs, n_ctx_tiles):
    *x_refs, g_ref, sh_ref, sc_ref, u_ref = refs
    x = _pick_rows(*x_refs, n_ctx_tiles) if len(x_refs) == 2 else x_refs[0][...]
    y = x * lax.rsqrt(jnp.mean(x * x, axis=-1, keepdims=True) + EPS) * g_ref[...]
    u_ref[...] = (y * (1.0 + sc_ref[...]) + sh_ref[...]).astype(u_ref.dtype)


def prenorm(x, g, sh, sc, n_ctx_rows, n_lat_rows):
    xspecs, xs, rows, d = _row_operand(x, ROW_T)
    n_ctx_tiles = 2 * n_ctx_rows // ROW_T
    idx = functools.partial(_mod_index, n_ctx_tiles=n_ctx_tiles, tiles_per_batch=n_lat_rows // ROW_T)
    vec = pl.BlockSpec((None, 1, d), lambda i: (idx(i), 0, 0))
    return pl.pallas_call(
        functools.partial(_prenorm_kernel, n_ctx_tiles=n_ctx_tiles),
        grid=(rows // ROW_T,),
        in_specs=[*xspecs, pl.BlockSpec((1, d), lambda i: (0, 0)), vec, vec],
        out_specs=pl.BlockSpec((ROW_T, d), lambda i: (i, 0)),
        out_shape=jax.ShapeDtypeStruct((rows, d), jnp.bfloat16),
        compiler_params=_cparams("parallel"),
        name="prenorm",
    )(*xs, g.reshape(1, d), sh, sc)


def _gelu_tanh(x):
    return 0.5 * x * (1.0 + jnp.tanh(math.sqrt(2.0 / math.pi) * (x + 0.044715 * (x * x * x))))


def _conv_pos(is_ctx, shape):
    t = lax.broadcasted_iota(jnp.int32, shape, 0)
    pos = jnp.where(is_ctx, t, t & (GRID_W - 1))
    last = jnp.where(is_ctx, SEQ_T - 1, GRID_W - 1)
    return pos, last


def _lru_pass_kernel(x_ref, cw_ref, cb_ref, w_ref, gb_ref, lam_ref, *rest, reverse, final):
    if final:
        lg_ref, hprev_ref, out_ref, h_ref, a_ref, b_ref = rest
    else:
        out_ref, h_ref, a_ref, b_ref = rest
    j = pl.program_id(1)

    @pl.when(j == 0)
    def _():
        h_ref[...] = jnp.zeros_like(h_ref)

    x = x_ref[...].astype(jnp.float32)
    pos, last = _conv_pos(j == 0, x.shape)
    cw = cw_ref[...]
    xc = cb_ref[...] + cw[2:3] * x
    xc = xc + cw[0:1] * jnp.where(pos >= 2, pltpu.roll(x, 2, 0), 0.0)
    xc = xc + cw[1:2] * jnp.where(pos >= 1, pltpu.roll(x, 1, 0), 0.0)
    xc = xc + cw[3:4] * jnp.where(pos < last, pltpu.roll(x, SEQ_T - 1, 0), 0.0)

    xb = xc.astype(jnp.bfloat16)
    lam = lam_ref[...]
    sp = jnp.maximum(-lam, 0.0) + jnp.log(1.0 + jnp.exp(-jnp.abs(lam)))
    n_tiles = SEQ_T // SUBLANES
    row = lax.broadcasted_iota(jnp.int32, (1, SUBLANES, LRU_BS), 1)
    for n in range(LRU_BLOCKS):
        sl = slice(n * LRU_BS, (n + 1) * LRU_BS)
        pre = jnp.dot(xb[:, sl], w_ref[n], preferred_element_type=jnp.float32) + gb_ref[n]
        gate = 1.0 / (1.0 + jnp.exp(-pre))
        r, i = gate[:, :LRU_BS], gate[:, LRU_BS:]
        a = jnp.exp(-LRU_C * r * sp[:, sl])
        om = 1.0 - a * a
        b = om * lax.rsqrt(jnp.maximum(om, TINY)) * (i * xc[:, sl])
        a = a.reshape(n_tiles, SUBLANES, LRU_BS)
        b = b.reshape(n_tiles, SUBLANES, LRU_BS)
        for d in (1, 2, 4):
            if reverse:
                m = row < SUBLANES - d
                a_sh, b_sh = pltpu.roll(a, SUBLANES - d, 1), pltpu.roll(b, SUBLANES - d, 1)
            else:
                m = row >= d
                a_sh, b_sh = pltpu.roll(a, d, 1), pltpu.roll(b, d, 1)
            b = jnp.where(m, a * b_sh + b, b)
            a = jnp.where(m, a * a_sh, a)
        a_ref[:, sl] = a.reshape(SEQ_T, LRU_BS)
        b_ref[:, sl] = b.reshape(SEQ_T, LRU_BS)

    h = h_ref[...]
    for k in range(n_tiles):
        kk = n_tiles - 1 - k if reverse else k
        rs = slice(kk * SUBLANES, (kk + 1) * SUBLANES)
        ht = a_ref[rs, :] * h + b_ref[rs, :]
        h = ht[0:1, :] if reverse else ht[SUBLANES - 1:SUBLANES, :]
        if final:
            ht = (ht + hprev_ref[rs, :]) * _gelu_tanh(lg_ref[rs, :].astype(jnp.float32))
        out_ref[rs, :] = ht.astype(out_ref.dtype)
    h_ref[...] = h


def lru_pass(z, conv_w, conv_b, wa, ba, wi, bi, lam, n_ctx_rows, n_lat_rows, reverse, h_prev=None):
    rows = z.shape[0]
    n_lat = n_lat_rows // SEQ_T
    cb = LRU_COL0 // MIX_W
    final = h_prev is not None
    rb = functools.partial(_row_block, n_ctx=n_ctx_rows // SEQ_T, n_lat=n_lat, reverse=reverse)

    def zspec(k):
        return pl.BlockSpec((SEQ_T, MIX_W), lambda b, j: (rb(b, j), cb + k))

    row_spec = pl.BlockSpec((SEQ_T, MIX_W), lambda b, j: (rb(b, j), 0))

    def const_spec(shape):
        return pl.BlockSpec(shape, lambda b, j: (0,) * len(shape))

    w2 = jnp.concatenate([wa, wi], axis=-1).astype(jnp.bfloat16)
    gb = jnp.concatenate([ba.reshape(LRU_BLOCKS, 1, LRU_BS), bi.reshape(LRU_BLOCKS, 1, LRU_BS)], axis=-1)
    in_specs = [zspec(0), const_spec((4, MIX_W)), const_spec((1, MIX_W)),
                const_spec((LRU_BLOCKS, LRU_BS, 2 * LRU_BS)), const_spec((LRU_BLOCKS, 1, 2 * LRU_BS)),
                const_spec((1, MIX_W))]
    args = [z, conv_w, conv_b.reshape(1, MIX_W), w2, gb, lam.reshape(1, MIX_W)]
    if final:
        in_specs += [zspec(1), row_spec]
        args += [z, h_prev]
    return pl.pallas_call(
        functools.partial(_lru_pass_kernel, reverse=reverse, final=final),
        grid=(2, n_ctx_rows // SEQ_T + n_lat),
        in_specs=in_specs,
        out_specs=row_spec,
        out_shape=jax.ShapeDtypeStruct((rows, MIX_W), jnp.bfloat16 if final else jnp.float32),
        scratch_shapes=[pltpu.VMEM((1, MIX_W), jnp.float32),
                        pltpu.VMEM((SEQ_T, MIX_W), jnp.float32),
                        pltpu.VMEM((SEQ_T, MIX_W), jnp.float32)],
        compiler_params=_cparams("parallel", "arbitrary"),
        name="lru_bwd" if reverse else "lru_fwd",
    )(*args)


def lru_branch(z, p, n_ctx_rows, n_lat_rows):
    def one(d, h_prev):
        return lru_pass(z, p['lru_conv_w'], p['lru_conv_b'], p['lru_wa'][d], p['lru_ba'][d], p['lru_wi'][d],
                        p['lru_bi'][d], p['lru_lambda'][d], n_ctx_rows, n_lat_rows, d == 1, h_prev)
    return one(1, one(0, None))


def _hg_level_matrix(reverse):
    t = np.arange(HG_C)[:, None]
    s = np.arange(HG_C)[None, :]
    x = t ^ s
    lv = np.where(x > 0, np.floor(np.log2(np.maximum(x, 1))).astype(np.int32), -1)
    lv = np.where(s < t, lv, -1).astype(np.int32)
    return jnp.asarray(lv.T if reverse else lv)


def _hg_pass_kernel(lv_ref, lb_ref, q_ref, f_ref, v_ref, *rest, reverse, final):
    if final:
        og_ref, oprev_ref, g_ref, out_ref, st_ref = rest
    else:
        out_ref, st_ref = rest

    @pl.when(pl.program_id(1) == 0)
    def _():
        st_ref[...] = jnp.zeros_like(st_ref)

    lv = lv_ref[...]
    at_level = [lv == lvl for lvl in range(HG_LEVELS)]
    n_tiles = HG_C // SUBLANES
    wb = HG_HB * HG_DK
    row = lax.broadcasted_iota(jnp.int32, (1, SUBLANES, wb), 1)
    for hb in range(HG_HEADS // HG_HB):
        sl = slice(hb * wb, (hb + 1) * wb)
        heads = [(hb * HG_HB + j, slice(j * HG_DK, (j + 1) * HG_DK)) for j in range(HG_HB)]
        x = f_ref[:, sl].astype(jnp.float32)
        lb = lb_ref[:, sl]
        e = jnp.exp(-jnp.abs(x))
        inv = 1.0 / (1.0 + e)
        pos = x >= 0
        sig = jnp.where(pos, inv, e * inv)
        sigm = jnp.where(pos, e * inv, inv)
        lsig = jnp.minimum(x, 0.0) - jnp.log(1.0 + e)
        lf = jnp.where(lb > 0, jnp.log(lb + (1.0 - lb) * sig), lsig)
        kk = (1.0 - lb) * sigm
        q = q_ref[:, sl].astype(jnp.float32)
        v = v_ref[:, sl]

        c = (lf * LOG2E).reshape(n_tiles, SUBLANES, wb)
        bt = c
        att = [None] * HG_HB
        for lvl in range(HG_LEVELS):
            m = 1 << lvl
            qm = (q * jnp.exp2(c).reshape(HG_C, wb)).astype(jnp.bfloat16)
            km = (kk * jnp.exp2(bt - c).reshape(HG_C, wb)).astype(jnp.bfloat16)
            for j, (_, hs) in enumerate(heads):
                d = lax.dot_general(qm[:, hs], km[:, hs], (((1,), (1,)), ((), ())),
                                    preferred_element_type=jnp.float32)
                att[j] = jnp.where(at_level[lvl], d, 0.0 if lvl == 0 else att[j])
            if m < SUBLANES:
                upper = (row & m) != 0
                down = pltpu.roll(bt, m, 1)
                up = pltpu.roll(bt, SUBLANES - m, 1)
                if reverse:
                    c = c + jnp.where(upper, 0.0, up)
                else:
                    c = c + jnp.where(upper, down, 0.0)
                bt = bt + jnp.where(upper, down, up)
            else:
                k = m // SUBLANES
                pair = (n_tiles // (2 * k), 2, k, SUBLANES, wb)
                c5, b5 = c.reshape(pair), bt.reshape(pair)
                tot2 = b5[:, 0] + b5[:, 1]
                if reverse:
                    c = jnp.stack([c5[:, 0] + b5[:, 1], c5[:, 1]], axis=1)
                else:
                    c = jnp.stack([c5[:, 0], c5[:, 1] + b5[:, 0]], axis=1)
                c = c.reshape(n_tiles, SUBLANES, wb)
                bt = jnp.stack([tot2, tot2], axis=1).reshape(n_tiles, SUBLANES, wb)

        qm = (q * jnp.exp2(c).reshape(HG_C, wb)).astype(jnp.bfloat16)
        km = (kk * jnp.exp2(bt - c).reshape(HG_C, wb)).astype(jnp.bfloat16)
        tot = jnp.exp2(bt[0, 0:1, :])
        diag = q * kk
        for j, (h, hs) in enumerate(heads):
            osl = slice(h * HG_DK, (h + 1) * HG_DK)
            st = st_ref[h]
            vh = v[:, hs]
            o = lax.dot_general(qm[:, hs], st.astype(jnp.bfloat16), (((1,), (1,)), ((), ())),
                                preferred_element_type=jnp.float32)
            o = o + jnp.dot(att[j].astype(jnp.bfloat16), vh, preferred_element_type=jnp.float32)
            o = o + jnp.sum(diag[:, hs], axis=-1, keepdims=True) * vh.astype(jnp.float32)
            st_ref[h] = st * tot[:, hs] + lax.dot_general(vh, km[:, hs], (((0,), (0,)), ((), ())),
                                                          preferred_element_type=jnp.float32)
            if final:
                o = o + oprev_ref[:, osl]
                y = o * lax.rsqrt(jnp.mean(o * o, axis=-1, keepdims=True) + EPS) * g_ref[:, osl]
                og = og_ref[:, osl].astype(jnp.float32)
                out_ref[:, osl] = (y * (og / (1.0 + jnp.exp(-og)))).astype(out_ref.dtype)
            else:
                out_ref[:, osl] = o


def hgrn_pass(z, lb, n_ctx_rows, n_lat_rows, reverse, o_prev=None, norm_g=None):
    rows = z.shape[0]
    n_ctx, n_lat = n_ctx_rows // HG_C, n_lat_rows // HG_C
    cb = HG_COL0 // MIX_W
    final = o_prev is not None
    rb = functools.partial(_row_block, n_ctx=n_ctx, n_lat=n_lat, reverse=reverse)

    def zspec(k):
        return pl.BlockSpec((HG_C, MIX_W), lambda b, j: (rb(b, j), cb + k))

    row_spec = pl.BlockSpec((HG_C, MIX_W), lambda b, j: (rb(b, j), 0))
    vec_spec = pl.BlockSpec((1, MIX_W), lambda b, j: (0, 0))
    in_specs = [pl.BlockSpec((HG_C, HG_C), lambda b, j: (0, 0)), vec_spec,
                zspec(0), zspec(2 if reverse else 1), zspec(3)]
    args = [_hg_level_matrix(reverse), lb.reshape(1, MIX_W), z, z, z]
    if final:
        in_specs += [zspec(4), row_spec, vec_spec]
        args += [z, o_prev, norm_g.reshape(1, MIX_W)]
    return pl.pallas_call(
        functools.partial(_hg_pass_kernel, reverse=reverse, final=final),
        grid=(2, n_ctx + n_lat),
        in_specs=in_specs,
        out_specs=row_spec,
        out_shape=jax.ShapeDtypeStruct((rows, MIX_W), jnp.bfloat16 if final else jnp.float32),
        scratch_shapes=[pltpu.VMEM((HG_HEADS, HG_DK, HG_DK), jnp.float32)],
        compiler_params=_cparams("parallel", "arbitrary"),
        name="hgrn_bwd" if reverse else "hgrn_fwd",
    )(*args)


def hgrn_branch(z, lb, norm_g, n_ctx_rows, n_lat_rows):
    o_f = hgrn_pass(z, lb, n_ctx_rows, n_lat_rows, False)
    return hgrn_pass(z, lb, n_ctx_rows, n_lat_rows, True, o_f, norm_g)


def _hy_conv_kernel(x_ref, w_ref, b_ref, out_ref, *, is_ctx):
    x = x_ref[...].astype(jnp.float32)
    t = lax.broadcasted_iota(jnp.int32, x.shape, 0)
    pos = t if is_ctx else t & (GRID_W - 1)
    last = SEQ_T - 1 if is_ctx else GRID_W - 1
    w = w_ref[...]
    y = b_ref[...] + w[1:2] * x
    y = y + w[0:1] * jnp.where(pos >= 1, pltpu.roll(x, 1, 0), 0.0)
    y = y + w[2:3] * jnp.where(pos < last, pltpu.roll(x, SEQ_T - 1, 0), 0.0)
    for g in range(3):
        out_ref[g] = y[:, g * MIX_W:(g + 1) * MIX_W].astype(out_ref.dtype)


def hy_conv(z, conv_w, conv_b, block0, n_rows, is_ctx):
    nb = n_rows // SEQ_T
    return pl.pallas_call(
        functools.partial(_hy_conv_kernel, is_ctx=is_ctx),
        grid=(2, nb),
        in_specs=[pl.BlockSpec((SEQ_T, 3 * MIX_W), lambda b, j: (block0 + b * nb + j, 0)),
                  pl.BlockSpec((3, 3 * MIX_W), lambda b, j: (0, 0)),
                  pl.BlockSpec((1, 3 * MIX_W), lambda b, j: (0, 0))],
        out_specs=pl.BlockSpec((3, None, SEQ_T, MIX_W), lambda b, j: (0, b, j, 0)),
        out_shape=jax.ShapeDtypeStruct((3, 2, n_rows, MIX_W), jnp.bfloat16),
        compiler_params=_cparams("parallel", "arbitrary"),
        name="hy_conv_ctx" if is_ctx else "hy_conv_lat",
    )(z, conv_w, conv_b.reshape(1, 3 * MIX_W))


def _hy_conv_lat_kernel(*refs):
    x_refs, (w_ref, b_ref, out_ref) = refs[:HY_PARTS], refs[HY_PARTS:]
    x = jnp.concatenate([r[...] for r in x_refs], axis=0).astype(jnp.float32)
    n = x.shape[0]
    pos = lax.broadcasted_iota(jnp.int32, x.shape, 0) & (GRID_W - 1)
    w = w_ref[...]
    y = b_ref[...] + w[1:2] * x
    y = y + w[0:1] * jnp.where(pos >= 1, pltpu.roll(x, 1, 0), 0.0)
    y = y + w[2:3] * jnp.where(pos < GRID_W - 1, pltpu.roll(x, n - 1, 0), 0.0)
    y = y.astype(out_ref.dtype).reshape(n // FFT_R, FFT_R, MIX_W)
    out_ref[...] = jnp.swapaxes(y, 0, 1)


def hy_conv_latent(z, conv_w, conv_b, row0, n_rows):
    part = FFT_SB * FFT_R // HY_PARTS
    nb = n_rows // (FFT_SB * FFT_R)

    def xspec(k):
        return pl.BlockSpec((part, MIX_W), lambda g, b, j: ((row0 + b * n_rows) // part + j * HY_PARTS + k, g))

    return pl.pallas_call(
        _hy_conv_lat_kernel,
        grid=(3, 2, nb),
        in_specs=[xspec(k) for k in range(HY_PARTS)] + [
            pl.BlockSpec((3, MIX_W), lambda g, b, j: (0, g)), pl.BlockSpec((1, MIX_W), lambda g, b, j: (0, g))],
        out_specs=pl.BlockSpec((None, None, FFT_R, FFT_SB, MIX_W), lambda g, b, j: (g, b, 0, j, 0)),
        out_shape=jax.ShapeDtypeStruct((3, 2, FFT_R, n_rows // FFT_R, MIX_W), jnp.bfloat16),
        compiler_params=_cparams("parallel", "parallel", "arbitrary"),
        name="hy_conv_lat",
    )(*([z] * HY_PARTS), conv_w, conv_b.reshape(1, 3 * MIX_W))


def _cis(num, den):
    ang = (2.0 * math.pi / den) * (num % den).astype(jnp.float32)
    return jnp.cos(ang), jnp.sin(ang)


def fft_matrices():
    r = FFT_R
    i = jnp.arange(r, dtype=jnp.int32)
    c, s = _cis(i[:, None] * i[None, :], r)
    half = r // 2
    fa_data = jnp.concatenate([jnp.concatenate([c[:, :half], s[:, :half]], axis=1),
                               jnp.concatenate([-s[:, :half], c[:, :half]], axis=1)], axis=0)
    fa_taps = jnp.concatenate([c, -s], axis=0)
    ct, st = _cis(i[:, None] * i[None, :], FFT_L)
    gc = c[None] * ct[:, None, :] - s[None] * st[:, None, :]
    gs = s[None] * ct[:, None, :] + c[None] * st[:, None, :]
    gr, gi = gc, -gs
    m1 = jnp.concatenate([jnp.concatenate([gr, -gi], axis=2), jnp.concatenate([gi, gr], axis=2)], axis=1)
    er, ei = c[:half] / FFT_L, s[:half] / FFT_L
    e2 = jnp.concatenate([jnp.concatenate([er, -ei], axis=1), jnp.concatenate([ei, er], axis=1)], axis=0)
    bf = jnp.bfloat16
    return dict(fa_data=fa_data.astype(bf), fa_taps=fa_taps.astype(bf), m1=m1.astype(bf), e2=e2.astype(bf))


def _fft_a_kernel(f_ref, x_ref, o_ref, *, two):
    f = f_ref[...]
    res = []
    for s in range(FFT_SB):
        x = jnp.concatenate([x_ref[0, s], x_ref[1, s]], axis=0) if two else x_ref[s]
        res.append(jnp.dot(f, x, preferred_element_type=jnp.float32).astype(o_ref.dtype))
    o_ref[...] = jnp.swapaxes(jnp.stack(res), 0, 1).reshape(o_ref.shape)


def fft_stage_a(fmat, x, g):
    two = x.ndim == 5
    xspec = (pl.BlockSpec((None, 2, FFT_SB, FFT_R // 2, MIX_W), lambda j: (g, 0, j, 0, 0)) if two
             else pl.BlockSpec((None, FFT_SB, FFT_R, MIX_W), lambda j: (g, j, 0, 0)))
    return pl.pallas_call(
        functools.partial(_fft_a_kernel, two=two),
        grid=(FFT_R // FFT_SB,),
        in_specs=[pl.BlockSpec((2 * FFT_R, FFT_R), lambda j: (0, 0)), xspec],
        out_specs=pl.BlockSpec((2, FFT_R, FFT_SB, MIX_W), lambda j: (0, 0, j, 0)),
        out_shape=jax.ShapeDtypeStruct((2, FFT_R, FFT_R, MIX_W), jnp.bfloat16),
        compiler_params=_cparams("parallel"),
        name="fft_stage_a",
    )(fmat, x)


def _fft_mid_kernel(m1_ref, a_ref, *rest, conv):
    if conv:
        h_ref, o_ref = rest
    else:
        s_ref, o_ref = rest
    r = FFT_R
    for k in range(FFT_KB):
        a = a_ref[:, k].reshape(2 * r, MIX_W)
        x = jnp.dot(m1_ref[k], a, preferred_element_type=jnp.float32)
        if conv:
            h = h_ref[k].astype(jnp.float32)
            xr, xi, hr, hi = x[:r], x[r:], h[:r], h[r:]
            zc = jnp.concatenate([xr * hr - xi * hi, xr * hi + xi * hr], axis=0).astype(jnp.bfloat16)
            p = lax.dot_general(m1_ref[k], zc, (((0,), (0,)), ((), ())), preferred_element_type=jnp.float32)
            o_ref[:, k] = p.reshape(2, r, MIX_W).astype(o_ref.dtype)
        else:
            o_ref[k] = (x * s_ref[...]).astype(o_ref.dtype)


def fft_mid(m1, a, h=None, scale=None):
    conv = h is not None
    r = FFT_R
    mspec = pl.BlockSpec((FFT_KB, 2 * r, 2 * r), lambda j: (j, 0, 0))
    aspec = pl.BlockSpec((2, FFT_KB, r, MIX_W), lambda j: (0, j, 0, 0))
    hspec = pl.BlockSpec((FFT_KB, 2 * r, MIX_W), lambda j: (j, 0, 0))
    if conv:
        in_specs, args = [mspec, aspec, hspec], (m1, a, h)
        out_specs, out_shape = aspec, jax.ShapeDtypeStruct((2, r, r, MIX_W), jnp.bfloat16)
    else:
        in_specs, args = [mspec, aspec, pl.BlockSpec((1, MIX_W), lambda j: (0, 0))], (m1, a, scale)
        out_specs, out_shape = hspec, jax.ShapeDtypeStruct((r, 2 * r, MIX_W), jnp.bfloat16)
    return pl.pallas_call(
        functools.partial(_fft_mid_kernel, conv=conv),
        grid=(r // FFT_KB,),
        in_specs=in_specs, out_specs=out_specs, out_shape=out_shape,
        compiler_params=_cparams("parallel"),
        name="fft_mid_conv" if conv else "fft_mid_filter",
    )(*args)


def _fft_out_kernel(e_ref, p_ref, mul_ref, add_ref, bias_ref, o_ref, *, time_order):
    e = e_ref[...]
    half = FFT_R // 2
    bias = bias_ref[...]
    p = jnp.swapaxes(p_ref[...].reshape(2 * FFT_R, FFT_SB, MIX_W), 0, 1)
    res = [[], []]
    for s in range(FFT_SB):
        y = jnp.dot(e, p[s], preferred_element_type=jnp.float32)
        for b in range(2):
            yb = y[b * half:(b + 1) * half] + add_ref[b, s].astype(jnp.float32) * bias
            res[b].append((mul_ref[b, s].astype(jnp.float32) * yb).astype(o_ref.dtype))
    for b in range(2):
        out = jnp.stack(res[b])
        o_ref[b] = jnp.swapaxes(out, 0, 1) if time_order else out


def fft_out(e2, p, mul, gm, add, ga, bias, time_order):
    half = FFT_R // 2

    def bspec(g):
        return pl.BlockSpec((None, 2, FFT_SB, half, MIX_W), lambda j: (g, 0, j, 0, 0))

    if time_order:
        out_spec = pl.BlockSpec((None, 2, half, FFT_SB, MIX_W), lambda j: (0, 0, 0, j, 0))
        out_shape = jax.ShapeDtypeStruct((1, 2, half, FFT_R, MIX_W), jnp.bfloat16)
    else:
        out_spec, out_shape = bspec(0), jax.ShapeDtypeStruct((1, 2, FFT_R, half, MIX_W), jnp.bfloat16)
    return pl.pallas_call(
        functools.partial(_fft_out_kernel, time_order=time_order),
        grid=(FFT_R // FFT_SB,),
        in_specs=[pl.BlockSpec((FFT_R, 2 * FFT_R), lambda j: (0, 0)),
                  pl.BlockSpec((2, FFT_R, FFT_SB, MIX_W), lambda j: (0, 0, j, 0)), bspec(gm), bspec(ga),
                  pl.BlockSpec((1, MIX_W), lambda j: (0, 0))],
        out_specs=out_spec,
        out_shape=out_shape,
        compiler_params=_cparams("parallel", vmem=FFT_OUT_VMEM_BYTES),
        name="fft_out",
    )(e2, p, mul, add, bias)


def hyena_spectra(taps, inv_norm, mats):
    return [fft_mid(mats['m1'], fft_stage_a(mats['fa_taps'], taps, order), scale=inv_norm[order])
            for order in range(2)]


def hyena_latent(xs, spectra, bias, mats):
    u, gu = xs, 2
    for order in range(2):
        p = fft_mid(mats['m1'], fft_stage_a(mats['fa_data'], u, gu), h=spectra[order])
        u, gu = fft_out(mats['e2'], p, xs, order, u, gu, bias[order].reshape(1, MIX_W), order == 1), 0
    return u.reshape(2, FFT_L // 2, MIX_W)


def hyena_context(xc, taps, inv_norm, bias):
    n = xc.shape[2]
    length = 2 * n
    k = jnp.arange(length, dtype=jnp.int32)
    c, s = _cis(k[:, None] * k[None, :], length)
    fr, fi = c[:, :n], -s[:, :n]
    m_fwd = jnp.concatenate([jnp.concatenate([fr, -fi], axis=1), jnp.concatenate([fi, fr], axis=1)], axis=0)
    m_taps = jnp.concatenate([c, -s], axis=0)
    er, ei = c[:n] / length, s[:n] / length
    m_inv = jnp.concatenate([jnp.concatenate([er, -ei], axis=1), jnp.concatenate([ei, er], axis=1)], axis=0)
    x1, x2, u = (xc[g].astype(jnp.float32) for g in range(3))
    for order, gate in enumerate((x1, x2)):
        h = mm(m_taps, taps[order], name="ctx_dft_taps") * inv_norm[order]
        w = mm(m_fwd, u.reshape(length, MIX_W), name="ctx_dft_fwd")
        wr, wi, hr, hi = w[:length], w[length:], h[:length], h[length:]
        zc = jnp.concatenate([wr * hr - wi * hi, wr * hi + wi * hr], axis=0)
        y = mm(m_inv, zc, name="ctx_dft_inv").reshape(2, n, MIX_W)
        u = gate * (y + u * bias[order])
    return u.astype(jnp.bfloat16)


def hyena_features(n, ff1, ff1_b, freq, ff2, ff2_b):
    k = jnp.arange(n, dtype=jnp.int32)
    pos = jnp.stack([k, (n - k) % n]).astype(jnp.float32)
    t = pos / (n - 1)
    fw = (2.0 * math.pi * pos / n)[..., None] * jnp.linspace(1e-4, HY_BANDS - 1, HY_BANDS, dtype=jnp.float32)
    feats = jnp.concatenate([t[..., None], jnp.cos(fw), -jnp.sin(fw)], axis=-1)
    h = jnp.sin(freq * (feats @ ff1 + ff1_b))
    return jnp.sin(freq * (h @ ff2 + ff2_b))


def _filter_kernel(feat_ref, w_ref, delta_ref, taps_ref, sum_ref, *, n, tm, split):
    dr = pl.program_id(1)
    i = pl.program_id(2)
    row = i * tm + lax.broadcasted_iota(jnp.int32, (tm, MIX_W), 0)
    lag = jnp.where(dr == 0, row, jnp.where(row == 0, 0, n - row))
    t = lag.astype(jnp.float32) / (n - 1)
    h = jnp.dot(feat_ref[...].astype(jnp.bfloat16), w_ref[...].astype(jnp.bfloat16),
                preferred_element_type=jnp.float32)
    h = h * jnp.exp(-t * delta_ref[...])

    @pl.when((dr == 0) & (i == 0))
    def _():
        sum_ref[...] = jnp.zeros_like(sum_ref)

    sum_ref[...] += jnp.sum(jnp.abs(h), axis=0, keepdims=True)
    taps = jnp.where((dr == 0) | (row != 0), h, 0.0).astype(taps_ref.dtype)
    if split:
        taps = jnp.swapaxes(taps.reshape(tm // FFT_R, FFT_R, MIX_W), 0, 1)
    taps_ref[...] = taps


def hyena_filter_taps(feats, ff3):
    _, n, nf = feats.shape
    split = 2 * n == FFT_L
    tm = FFT_SB * FFT_R if split else n
    nb = n // tm
    w = ff3.reshape(nf, 2, 2, MIX_W).transpose(1, 2, 0, 3)
    deltas = jnp.abs(jnp.linspace(math.log(HY_TARGET) / HY_SLOW_PCT, math.log(HY_TARGET) / HY_FAST_PCT,
                                  MIX_W, dtype=jnp.float32)).reshape(1, MIX_W)
    if split:
        tspec = pl.BlockSpec((None, FFT_R, FFT_SB, MIX_W), lambda o, dr, i: (o, 0, dr * nb + i, 0))
        tshape = jax.ShapeDtypeStruct((2, FFT_R, FFT_R, MIX_W), jnp.bfloat16)
    else:
        tspec = pl.BlockSpec((None, tm, MIX_W), lambda o, dr, i: (o, dr * nb + i, 0))
        tshape = jax.ShapeDtypeStruct((2, 2 * n, MIX_W), jnp.bfloat16)
    return pl.pallas_call(
        functools.partial(_filter_kernel, n=n, tm=tm, split=split),
        grid=(2, 2, nb),
        in_specs=[pl.BlockSpec((None, tm, nf), lambda o, dr, i: (dr, i, 0)),
                  pl.BlockSpec((None, None, nf, MIX_W), lambda o, dr, i: (o, dr, 0, 0)),
                  pl.BlockSpec((1, MIX_W), lambda o, dr, i: (0, 0))],
        out_specs=[tspec, pl.BlockSpec((None, 1, MIX_W), lambda o, dr, i: (o, 0, 0))],
        out_shape=[tshape, jax.ShapeDtypeStruct((2, 1, MIX_W), jnp.float32)],
        compiler_params=_cparams("arbitrary", "arbitrary", "arbitrary"),
        name="hyena_filter",
    )(feats, w, deltas)


def _merge_out_kernel(yhc_ref, yhl_ref, yl_ref, yg_ref, wb_ref, g0_ref, g1_ref, g2_ref, wo_ref, *refs, n_ctx_tiles):
    *x_refs, n1_ref, gt_ref, n2_ref, sh_ref, sc_ref, wr_ref, xo_ref, h_ref, lg_ref = refs
    ys = (_pick_rows(yhc_ref, yhl_ref, n_ctx_tiles), yl_ref[...], yg_ref[...])
    acc = None
    for j, (y, g_ref) in enumerate(zip(ys, (g0_ref, g1_ref, g2_ref))):
        gate = 1.0 / (1.0 + jnp.exp(-g_ref[...].astype(jnp.float32)))
        t = gate * jnp.dot(y, wb_ref[j], preferred_element_type=jnp.float32)
        acc = t if acc is None else acc + t
    x = _pick_rows(*x_refs, n_ctx_tiles) if len(x_refs) == 2 else x_refs[0][...]
    y = jnp.dot(acc.astype(jnp.bfloat16), wo_ref[...], preferred_element_type=jnp.float32)
    y = y * lax.rsqrt(jnp.mean(y * y, axis=-1, keepdims=True) + EPS) * n1_ref[...]
    x = x + gt_ref[...] * y
    xo_ref[...] = x
    h = x * lax.rsqrt(jnp.mean(x * x, axis=-1, keepdims=True) + EPS) * n2_ref[...]
    h = (h * (1.0 + sc_ref[...]) + sh_ref[...]).astype(jnp.bfloat16)
    h_ref[...] = h
    lg_ref[...] = jnp.dot(h, wr_ref[...], preferred_element_type=jnp.float32)


def merge_out(z, y_hy_ctx, y_hy_lat, y_lru, y_hg, w_branch, w_out, layer, x, n1, gt, n2, sh, sc, w_router,
              n_ctx_rows, n_lat_rows):
    tm = ROW_T // 2
    xspecs, xs, rows, d = _row_operand(x, tm)
    n_ctx_tiles = 2 * n_ctx_rows // tm
    idx = functools.partial(_mod_index, n_ctx_tiles=n_ctx_tiles, tiles_per_batch=n_lat_rows // tm)
    row = pl.BlockSpec((tm, d), lambda i: (i, 0))
    yspec = pl.BlockSpec((tm, MIX_W), lambda i: (i, 0))
    vec = pl.BlockSpec((1, d), lambda i: (0, 0))
    mod = pl.BlockSpec((None, 1, d), lambda i: (idx(i), 0, 0))
    resident = pl.Buffered(1)

    def gspec(k):
        return pl.BlockSpec((tm, d), lambda i: (i, GATE_COL0 // d + k))

    wr = jnp.zeros((d, ROUTER_PAD), jnp.bfloat16).at[:, :N_EXPERTS].set(w_router.astype(jnp.bfloat16))
    return pl.pallas_call(
        functools.partial(_merge_out_kernel, n_ctx_tiles=n_ctx_tiles),
        grid=(rows // tm,),
        in_specs=[*_split_row_specs(tm, MIX_W, n_ctx_tiles), yspec, yspec,
                  pl.BlockSpec((None, N_BRANCH, MIX_W, d), lambda i: (layer, 0, 0, 0), pipeline_mode=resident),
                  gspec(0), gspec(1), gspec(2),
                  pl.BlockSpec((None, d, d), lambda i: (layer, 0, 0), pipeline_mode=resident),
                  *xspecs, vec, mod, vec, mod, mod,
                  pl.BlockSpec((d, ROUTER_PAD), lambda i: (0, 0), pipeline_mode=resident)],
        out_specs=[row, row, pl.BlockSpec((tm, ROUTER_PAD), lambda i: (i, 0))],
        out_shape=[jax.ShapeDtypeStruct((rows, d), jnp.float32), jax.ShapeDtypeStruct((rows, d), jnp.bfloat16),
                   jax.ShapeDtypeStruct((rows, ROUTER_PAD), jnp.float32)],
        compiler_params=_cparams("parallel", vmem=MERGE_OUT_VMEM_BYTES),
        name="merge_out",
    )(y_hy_ctx, y_hy_lat, y_lru, y_hg, w_branch, z, z, z, w_out, *xs, n1.reshape(1, d), gt, n2.reshape(1, d),
      sh, sc, wr)


def _ffn_up_kernel(x_ref, wg_ref, wu_ref, h_ref):
    x = x_ref[...]
    hg = jnp.dot(x, wg_ref[...].astype(jnp.bfloat16), preferred_element_type=jnp.float32)
    hu = jnp.dot(x, wu_ref[...].astype(jnp.bfloat16), preferred_element_type=jnp.float32)
    h_ref[...] = (hg / (1.0 + jnp.exp(-hg)) * hu).astype(h_ref.dtype)


def _ffn_down_kernel(h_ref, wd_ref, g_ref, o_ref):
    y = jnp.dot(h_ref[...], wd_ref[...].astype(jnp.bfloat16), preferred_element_type=jnp.float32)
    o_ref[...] = (y * g_ref[...]).astype(o_ref.dtype)


def expert_ffn(xe, w_gate, w_up, w_down, layer, g):
    ne, r, d = xe.shape
    dff = w_gate.shape[-1]
    hid = pl.pallas_call(
        _ffn_up_kernel,
        grid=(ne, dff // FFN_TF),
        in_specs=[pl.BlockSpec((None, r, d), lambda e, f: (e, 0, 0)),
                  pl.BlockSpec((None, None, d, FFN_TF), lambda e, f: (layer, e, 0, f)),
                  pl.BlockSpec((None, None, d, FFN_TF), lambda e, f: (layer, e, 0, f))],
        out_specs=pl.BlockSpec((None, r, FFN_TF), lambda e, f: (e, 0, f)),
        out_shape=jax.ShapeDtypeStruct((ne, r, dff), jnp.bfloat16),
        compiler_params=_cparams("parallel", "arbitrary"),
        name="ffn_up",
    )(xe, w_gate, w_up)
    return pl.pallas_call(
        _ffn_down_kernel,
        grid=(ne, d // FFN_TD),
        in_specs=[pl.BlockSpec((None, r, dff), lambda e, j: (e, 0, 0)),
                  pl.BlockSpec((None, None, dff, FFN_TD), lambda e, j: (layer, e, 0, j)),
                  pl.BlockSpec((None, r, 1), lambda e, j: (e, 0, 0))],
        out_specs=pl.BlockSpec((None, r, FFN_TD), lambda e, j: (e, 0, j)),
        out_shape=jax.ShapeDtypeStruct((ne, r, d), jnp.bfloat16),
        compiler_params=_cparams("parallel", "arbitrary"),
        name="ffn_down",
    )(hid, w_down, g)


def route(logits, row0, bsz, n):
    cap = EC_CAPACITY * n // N_EXPERTS
    aff = jax.nn.softmax(logits[row0:row0 + bsz * n, :N_EXPERTS].reshape(bsz, n, N_EXPERTS), axis=-1)
    g, idx = lax.top_k(jnp.swapaxes(aff, 1, 2), cap)
    flat = idx + (row0 + jnp.arange(bsz, dtype=idx.dtype) * n)[:, None, None]
    return (jnp.swapaxes(g, 0, 1).reshape(N_EXPERTS, bsz * cap),
            jnp.swapaxes(flat, 0, 1).reshape(N_EXPERTS, bsz * cap))


def ec_moe(h, logits, w_gate, w_up, w_down, layer, segments):
    gs, flats = zip(*(route(logits, *seg) for seg in segments))
    flat, g = lax.sort_key_val(jnp.concatenate(flats, axis=1), jnp.concatenate(gs, axis=1), dimension=1)
    return expert_ffn(h[flat], w_gate, w_up, w_down, layer, g[..., None]), flat


def moe_plan(flat, n_rows):
    ne, r = flat.shape
    nwin = r // MOE_WIN
    n_tiles = n_rows // MOE_TT
    bounds = jnp.arange(n_tiles + 1, dtype=flat.dtype) * MOE_TT
    p = jnp.sum(flat[None, :, :] < bounds[:, None, None], axis=2, dtype=jnp.int32)
    p0, p1 = p[:-1], p[1:]
    w0 = p0 // MOE_WIN
    nw = jnp.where(p1 > p0, (p1 - 1) // MOE_WIN - w0 + 1, 0)
    off = jnp.cumsum(nw, axis=1)
    q = jnp.arange(MOE_MAXQ, dtype=jnp.int32)
    e_of_q = jnp.minimum(jnp.sum(q[None, :, None] >= off[:, None, :], axis=2, dtype=jnp.int32), ne - 1)
    pick = e_of_q[:, :, None] == jnp.arange(ne, dtype=jnp.int32)
    w = q[None, :] + jnp.sum(jnp.where(pick, (w0 - (off - nw))[:, None, :], 0), axis=2)
    items = jnp.clip(e_of_q * nwin + w, 0, ne * nwin - 1)
    return items.astype(jnp.int32), off[:, -1].astype(jnp.int32)


def _combine_kernel(items_ref, count_ref, tok_ref, ye_ref, x_ref, g_ref, gt_ref, *rest, nxt, skip):
    if nxt:
        gn_ref, sh_ref, sc_ref, o_ref, u_ref, stage, sem, acc_ref = rest
    else:
        o_ref, stage, sem, acc_ref = rest
    i = pl.program_id(0)
    t = i + skip
    n = count_ref[t]
    n_groups = (n + MOE_GROUP - 1) // MOE_GROUP

    @pl.when(i == 0)
    def _():
        stage[...] = jnp.zeros_like(stage)

    def window_copy(tile, g, buf, k):
        item = items_ref[tile, jnp.minimum(MOE_GROUP * g + k, MOE_MAXQ - 1)]
        return pltpu.make_async_copy(ye_ref.at[pl.ds(item * MOE_WIN, MOE_WIN)],
                                     stage.at[buf, pl.ds(k * MOE_WIN, MOE_WIN)], sem.at[buf])

    def for_group(tile, g, buf, wait):
        for k in range(MOE_GROUP):
            @pl.when(MOE_GROUP * g + k < count_ref[tile])
            def _():
                cp = window_copy(tile, g, buf, k)
                cp.wait() if wait else cp.start()

    def start_first_groups(tile):
        for j in range(MOE_NBUF - 1):
            for_group(tile, j, j, False)

    acc_ref[...] = jnp.zeros_like(acc_ref)

    @pl.when(i == 0)
    def _():
        start_first_groups(t)

    row = t * MOE_TT + lax.broadcasted_iota(jnp.int32, (MOE_TT, MOE_GROUP * MOE_WIN), 0)

    def body(g, carry):
        buf = g % MOE_NBUF
        ahead = g + MOE_NBUF - 1

        @pl.when(ahead < n_groups)
        def _():
            for_group(t, ahead, ahead % MOE_NBUF, False)

        for_group(t, g, buf, True)
        toks = []
        for k in range(MOE_GROUP):
            qk = MOE_GROUP * g + k
            item = items_ref[t, jnp.minimum(qk, MOE_MAXQ - 1)]
            toks.append(jnp.where(qk < n, tok_ref[pl.ds(item, 1), :], -1))
        onehot = jnp.where(row == jnp.concatenate(toks, axis=1), 1.0, 0.0).astype(jnp.bfloat16)
        acc_ref[...] += jnp.dot(onehot, stage[buf], preferred_element_type=jnp.float32)
        return carry

    lax.fori_loop(0, n_groups, body, 0)

    @pl.when(i + 1 < pl.num_programs(0))
    def _():
        start_first_groups(t + 1)

    y = acc_ref[...]
    y = y * lax.rsqrt(jnp.mean(y * y, axis=-1, keepdims=True) + EPS) * g_ref[...]
    x = x_ref[...] + gt_ref[...] * y
    if nxt:
        u = x * lax.rsqrt(jnp.mean(x * x, axis=-1, keepdims=True) + EPS) * gn_ref[...]
        u_ref[...] = (u * (1.0 + sc_ref[...]) + sh_ref[...]).astype(u_ref.dtype)
    o_ref[...] = x


def moe_combine(x, ye, flat, g, gt, n_ctx_rows, n_lat_rows, nxt=None):
    rows, d = x.shape
    ne, r, _ = ye.shape
    items, count = moe_plan(flat, rows)
    n_ctx_tiles = 2 * n_ctx_rows // MOE_TT
    idx = functools.partial(_mod_index, n_ctx_tiles=n_ctx_tiles, tiles_per_batch=n_lat_rows // MOE_TT)
    skip = 0 if nxt else n_ctx_tiles
    row = pl.BlockSpec((MOE_TT, d), lambda i, *_: (i + skip, 0))
    vec = pl.BlockSpec((1, d), lambda i, *_: (0, 0))
    mod = pl.BlockSpec((None, 1, d), lambda i, *_: (idx(i + skip), 0, 0))
    out_row = pl.BlockSpec((MOE_TT, d), lambda i, *_: (i, 0))
    n_out = rows - skip * MOE_TT
    in_specs = [pl.BlockSpec((ne * r // MOE_WIN, MOE_WIN), lambda i, *_: (0, 0)),
                pl.BlockSpec(memory_space=pl.ANY), row, vec, mod]
    args = [flat.reshape(ne * r // MOE_WIN, MOE_WIN), ye.reshape(ne * r, d), x, g.reshape(1, d), gt]
    out_specs, out_shape = out_row, jax.ShapeDtypeStruct((n_out, d), jnp.float32)
    if nxt:
        in_specs += [vec, mod, mod]
        args += [nxt[0].reshape(1, d), nxt[1], nxt[2]]
        out_specs = [out_row, out_row]
        out_shape = [out_shape, jax.ShapeDtypeStruct((n_out, d), jnp.bfloat16)]
    return pl.pallas_call(
        functools.partial(_combine_kernel, nxt=bool(nxt), skip=skip),
        grid_spec=pltpu.PrefetchScalarGridSpec(
            num_scalar_prefetch=2,
            grid=(n_out // MOE_TT,),
            in_specs=in_specs, out_specs=out_specs,
            scratch_shapes=[pltpu.VMEM((MOE_NBUF, MOE_GROUP * MOE_WIN, d), jnp.bfloat16),
                            pltpu.SemaphoreType.DMA((MOE_NBUF,)),
                            pltpu.VMEM((MOE_TT, d), jnp.float32)]),
        out_shape=out_shape,
        compiler_params=_cparams("arbitrary"),
        name="moe_combine",
    )(items, count, *args)


def kernel(x, c, ctx, c_ctx, w_mod, b_mod, norm_g, w_in, hy_conv_w, hy_conv_b, hy_ff1, hy_ff1_b, hy_freq,
           hy_ff2, hy_ff2_b, hy_ff3, hy_bias, lru_conv_w, lru_conv_b, lru_wa, lru_ba, lru_wi, lru_bi,
           lru_lambda, hg_lb_logits, hg_norm_g, w_branch, w_out, w_router, w_gate, w_up, w_down):
    bsz, n_lat, d = x.shape
    n_ctx = ctx.shape[1]
    assert bsz == 2 and d == D_MODEL and n_ctx == SEQ_T and 2 * n_lat == FFT_L and n_lat % ROW_T == 0
    gam = jax.nn.softmax(hg_lb_logits.astype(jnp.float32), axis=0)
    lb_all = jnp.maximum(jnp.cumsum(gam, axis=0) - gam[:1], 0.0)
    mats = fft_matrices()
    cvec = jnp.zeros((SUBLANES, d), jnp.float32).at[:bsz].set(jax.nn.silu(c)).at[bsz].set(jax.nn.silu(c_ctx))
    n_ctx_all = bsz * n_ctx
    xa = (ctx.reshape(n_ctx_all, d), x.reshape(bsz * n_lat, d))
    w_branch_b, w_out_b = w_branch.astype(jnp.bfloat16), w_out.astype(jnp.bfloat16)
    mods, filters = [], []
    for l in range(DEPTH):
        mod = mm(cvec, w_mod, layer=l, name="adaln")[:bsz + 1] + b_mod[l]
        mods.append([mod.reshape(bsz + 1, 1, 6, d)[:, :, k] for k in range(6)])
        ffp = (hy_ff1[l], hy_ff1_b[l], hy_freq[l], hy_ff2[l], hy_ff2_b[l])
        taps_l, norm_l = hyena_filter_taps(hyena_features(n_lat, *ffp), hy_ff3[l])
        taps_c, norm_c = hyena_filter_taps(hyena_features(n_ctx, *ffp), hy_ff3[l])
        filters.append((hyena_spectra(taps_l, 1.0 / norm_l, mats), taps_c, 1.0 / norm_c))
    u = prenorm(xa, norm_g[0, 0], mods[0][0], mods[0][1], n_ctx, n_lat)
    for l in range(DEPTH):
        last = l == DEPTH - 1
        p = {'lru_conv_w': lru_conv_w[l], 'lru_conv_b': lru_conv_b[l], 'lru_wa': lru_wa[l], 'lru_ba': lru_ba[l],
             'lru_wi': lru_wi[l], 'lru_bi': lru_bi[l], 'lru_lambda': lru_lambda[l]}
        sh1, sc1, gt1, sh2, sc2, gt2 = mods[l]

        z = mm(u, w_in, jnp.bfloat16, tm=IN_PROJ_TM, tn=IN_PROJ_TN, layer=l, name="in_proj")

        y_lru = lru_branch(z, p, n_ctx, n_lat)
        y_hg = hgrn_branch(z, lb_all[l], hg_norm_g[l], n_ctx, n_lat)

        spectra, taps_c, inv_norm_c = filters[l]
        xc_l = hy_conv_latent(z, hy_conv_w[l], hy_conv_b[l], n_ctx_all, n_lat)
        y_hy_l = hyena_latent(xc_l, spectra, hy_bias[l], mats)
        xc_c = hy_conv(z, hy_conv_w[l], hy_conv_b[l], 0, n_ctx, True)
        y_hy_c = hyena_context(xc_c, taps_c, inv_norm_c, hy_bias[l])

        xa, h, logits = merge_out(z, y_hy_c.reshape(n_ctx_all, MIX_W), y_hy_l.reshape(bsz * n_lat, MIX_W), y_lru, y_hg,
                                  w_branch_b, w_out_b, l, xa, norm_g[l, 1], gt1, norm_g[l, 2], sh2, sc2,
                                  w_router[l], n_ctx, n_lat)
        segments = [(n_ctx_all, bsz, n_lat)] if last else [(n_ctx_all, bsz, n_lat), (0, bsz, n_ctx)]
        ye, flat = ec_moe(h, logits, w_gate, w_up, w_down, l, segments)
        if last:
            xa = moe_combine(xa, ye, flat, norm_g[l, 3], gt2, n_ctx, n_lat)
        else:
            nxt = (norm_g[l + 1, 0], mods[l + 1][0], mods[l + 1][1])
            xa, u = moe_combine(xa, ye, flat, norm_g[l, 3], gt2, n_ctx, n_lat, nxt)
    return xa.reshape(bsz, n_lat, d)
```

```python
import functools
import math

import numpy as np
import jax
import jax.numpy as jnp
from jax import lax
from jax.experimental import pallas as pl
from jax.experimental.pallas import tpu as pltpu

D_MODEL = 2048
DEPTH = 2
GRID_W = 64
MIX_W = D_MODEL // 2
N_BRANCH = 3
HY_BANDS = 16
HY_TARGET = 1e-2
HY_FAST_PCT = 0.3
HY_SLOW_PCT = 1.5
LRU_BLOCKS = 8
LRU_BS = MIX_W // LRU_BLOCKS
LRU_C = 8.0
HG_HEADS = 8
HG_DK = MIX_W // HG_HEADS
N_EXPERTS = 16
EC_CAPACITY = 2
EPS = 1e-6
TINY = 1e-30
LOG2E = 1.4426950408889634
HY_COL0 = 0
LRU_COL0 = 3 * MIX_W
HG_COL0 = 5 * MIX_W
GATE_COL0 = 10 * MIX_W
IN_COLS = GATE_COL0 + N_BRANCH * D_MODEL

SUBLANES = 8
LANES = 128
SEQ_T = 256
HG_C = 128
HG_LEVELS = 7
HG_HB = 8
ROW_T = 512
IN_PROJ_TM = 2816
IN_PROJ_TN = 512
FFT_R = 128
FFT_L = FFT_R * FFT_R
FFT_KB = 8
FFT_SB = 16
HY_PARTS = 4
ROUTER_PAD = LANES
FFN_TF = 256
FFN_TD = 512
MOE_TT = 512
MOE_WIN = 32
MOE_GROUP = 16
MOE_NBUF = 3
MOE_MAXQ = N_EXPERTS * (MOE_TT // MOE_WIN + 1)
VMEM_LIMIT_BYTES = 48 * 1024 * 1024
FFT_OUT_VMEM_BYTES = 56 * 1024 * 1024
MERGE_OUT_VMEM_BYTES = 56 * 1024 * 1024


def _cparams(*sem, vmem=VMEM_LIMIT_BYTES):
    return pltpu.CompilerParams(dimension_semantics=sem, vmem_limit_bytes=vmem)


def _pick_tile(n, pref):
    for t in (pref, 1024, 512, 256, 128):
        if t <= n and n % t == 0:
            return t
    return n


def _row_block(b, j, n_ctx, n_lat, reverse):
    if reverse:
        kc, kl = n_ctx - 1 - j, n_ctx + n_lat - 1 - j
    else:
        kc, kl = j, j - n_ctx
    return jnp.where(j < n_ctx, b * n_ctx + kc, 2 * n_ctx + b * n_lat + kl)


def _mm_kernel(a_ref, b_ref, o_ref):
    o_ref[...] = jnp.dot(a_ref[...].astype(jnp.bfloat16), b_ref[...].astype(jnp.bfloat16),
                         preferred_element_type=jnp.float32).astype(o_ref.dtype)


def mm(a, b, out_dtype=jnp.float32, tm=512, tn=1024, layer=None, name="mm"):
    m, k = a.shape
    n = b.shape[-1]
    tm = _pick_tile(m, tm)
    tn = _pick_tile(n, tn)
    if layer is None:
        bspec = pl.BlockSpec((k, tn), lambda i, j: (0, j))
    else:
        bspec = pl.BlockSpec((None, k, tn), lambda i, j: (layer, 0, j))
    return pl.pallas_call(
        _mm_kernel,
        grid=(m // tm, n // tn),
        in_specs=[pl.BlockSpec((tm, k), lambda i, j: (i, 0)), bspec],
        out_specs=pl.BlockSpec((tm, tn), lambda i, j: (i, j)),
        out_shape=jax.ShapeDtypeStruct((m, n), out_dtype),
        compiler_params=_cparams("parallel", "arbitrary"),
        name=name,
    )(a, b)


def _mod_index(i, n_ctx_tiles, tiles_per_batch):
    return jnp.where(i < n_ctx_tiles, 2, (i - n_ctx_tiles) // tiles_per_batch)


def _split_row_specs(tile, width, n_ctx_tiles):
    return (pl.BlockSpec((tile, width), lambda i: (jnp.minimum(i, n_ctx_tiles - 1), 0)),
            pl.BlockSpec((tile, width), lambda i: (jnp.maximum(i - n_ctx_tiles, 0), 0)))


def _pick_rows(ctx_ref, lat_ref, n_ctx_tiles):
    return jnp.where(pl.program_id(0) < n_ctx_tiles, ctx_ref[...], lat_ref[...])


def _row_operand(x, tile):
    if isinstance(x, tuple):
        d = x[0].shape[1]
        return list(_split_row_specs(tile, d, x[0].shape[0] // tile)), list(x), x[0].shape[0] + x[1].shape[0], d
    return [pl.BlockSpec((tile, x.shape[1]), lambda i: (i, 0))], [x], x.shape[0], x.shape[1]


def _prenorm_kernel(*refs, n_ctx_tiles):
    *x_refs, g_ref, sh_ref, sc_ref, u_ref = refs
    x = _pick_rows(*x_refs, n_ctx_tiles) if len(x_refs) == 2 else x_refs[0][...]
    y = x * lax.rsqrt(jnp.mean(x * x, axis=-1, keepdims=True) + EPS) * g_ref[...]
    u_ref[...] = (y * (1.0 + sc_ref[...]) + sh_ref[...]).astype(u_ref.dtype)


def prenorm(x, g, sh, sc, n_ctx_rows, n_lat_rows):
    xspecs, xs, rows, d = _row_operand(x, ROW_T)
    n_ctx_tiles = 2 * n_ctx_rows // ROW_T
    idx = functools.partial(_mod_index, n_ctx_tiles=n_ctx_tiles, tiles_per_batch=n_lat_rows // ROW_T)
    vec = pl.BlockSpec((None, 1, d), lambda i: (idx(i), 0, 0))
    return pl.pallas_call(
        functools.partial(_prenorm_kernel, n_ctx_tiles=n_ctx_tiles),
        grid=(rows // ROW_T,),
        in_specs=[*xspecs, pl.BlockSpec((1, d), lambda i: (0, 0)), vec, vec],
        out_specs=pl.BlockSpec((ROW_T, d), lambda i: (i, 0)),
        out_shape=jax.ShapeDtypeStruct((rows, d), jnp.bfloat16),
        compiler_params=_cparams("parallel"),
        name="prenorm",
    )(*xs, g.reshape(1, d), sh, sc)


def _gelu_tanh(x):
    return 0.5 * x * (1.0 + jnp.tanh(math.sqrt(2.0 / math.pi) * (x + 0.044715 * (x * x * x))))


def _conv_pos(is_ctx, shape):
    t = lax.broadcasted_iota(jnp.int32, shape, 0)
    pos = jnp.where(is_ctx, t, t & (GRID_W - 1))
    last = jnp.where(is_ctx, SEQ_T - 1, GRID_W - 1)
    return pos, last


def _lru_pass_kernel(x_ref, cw_ref, cb_ref, w_ref, gb_ref, lam_ref, *rest, reverse, final):
    if final:
        lg_ref, hprev_ref, out_ref, h_ref, a_ref, b_ref = rest
    else:
        out_ref, h_ref, a_ref, b_ref = rest
    j = pl.program_id(1)

    @pl.when(j == 0)
    def _():
        h_ref[...] = jnp.zeros_like(h_ref)

    x = x_ref[...].astype(jnp.float32)
    pos, last = _conv_pos(j == 0, x.shape)
    cw = cw_ref[...]
    xc = cb_ref[...] + cw[2:3] * x
    xc = xc + cw[0:1] * jnp.where(pos >= 2, pltpu.roll(x, 2, 0), 0.0)
    xc = xc + cw[1:2] * jnp.where(pos >= 1, pltpu.roll(x, 1, 0), 0.0)
    xc = xc + cw[3:4] * jnp.where(pos < last, pltpu.roll(x, SEQ_T - 1, 0), 0.0)

    xb = xc.astype(jnp.bfloat16)
    lam = lam_ref[...]
    sp = jnp.maximum(-lam, 0.0) + jnp.log(1.0 + jnp.exp(-jnp.abs(lam)))
    n_tiles = SEQ_T // SUBLANES
    row = lax.broadcasted_iota(jnp.int32, (1, SUBLANES, LRU_BS), 1)
    for n in range(LRU_BLOCKS):
        sl = slice(n * LRU_BS, (n + 1) * LRU_BS)
        pre = jnp.dot(xb[:, sl], w_ref[n], preferred_element_type=jnp.float32) + gb_ref[n]
        gate = 1.0 / (1.0 + jnp.exp(-pre))
        r, i = gate[:, :LRU_BS], gate[:, LRU_BS:]
        a = jnp.exp(-LRU_C * r * sp[:, sl])
        om = 1.0 - a * a
        b = om * lax.rsqrt(jnp.maximum(om, TINY)) * (i * xc[:, sl])
        a = a.reshape(n_tiles, SUBLANES, LRU_BS)
        b = b.reshape(n_tiles, SUBLANES, LRU_BS)
        for d in (1, 2, 4):
            if reverse:
                m = row < SUBLANES - d
                a_sh, b_sh = pltpu.roll(a, SUBLANES - d, 1), pltpu.roll(b, SUBLANES - d, 1)
            else:
                m = row >= d
                a_sh, b_sh = pltpu.roll(a, d, 1), pltpu.roll(b, d, 1)
            b = jnp.where(m, a * b_sh + b, b)
            a = jnp.where(m, a * a_sh, a)
        a_ref[:, sl] = a.reshape(SEQ_T, LRU_BS)
        b_ref[:, sl] = b.reshape(SEQ_T, LRU_BS)

    h = h_ref[...]
    for k in range(n_tiles):
        kk = n_tiles - 1 - k if reverse else k
        rs = slice(kk * SUBLANES, (kk + 1) * SUBLANES)
        ht = a_ref[rs, :] * h + b_ref[rs, :]
        h = ht[0:1, :] if reverse else ht[SUBLANES - 1:SUBLANES, :]
        if final:
            ht = (ht + hprev_ref[rs, :]) * _gelu_tanh(lg_ref[rs, :].astype(jnp.float32))
        out_ref[rs, :] = ht.astype(out_ref.dtype)
    h_ref[...] = h


def lru_pass(z, conv_w, conv_b, wa, ba, wi, bi, lam, n_ctx_rows, n_lat_rows, reverse, h_prev=None):
    rows = z.shape[0]
    n_lat = n_lat_rows // SEQ_T
    cb = LRU_COL0 // MIX_W
    final = h_prev is not None
    rb = functools.partial(_row_block, n_ctx=n_ctx_rows // SEQ_T, n_lat=n_lat, reverse=reverse)

    def zspec(k):
        return pl.BlockSpec((SEQ_T, MIX_W), lambda b, j: (rb(b, j), cb + k))

    row_spec = pl.BlockSpec((SEQ_T, MIX_W), lambda b, j: (rb(b, j), 0))

    def const_spec(shape):
        return pl.BlockSpec(shape, lambda b, j: (0,) * len(shape))

    w2 = jnp.concatenate([wa, wi], axis=-1).astype(jnp.bfloat16)
    gb = jnp.concatenate([ba.reshape(LRU_BLOCKS, 1, LRU_BS), bi.reshape(LRU_BLOCKS, 1, LRU_BS)], axis=-1)
    in_specs = [zspec(0), const_spec((4, MIX_W)), const_spec((1, MIX_W)),
                const_spec((LRU_BLOCKS, LRU_BS, 2 * LRU_BS)), const_spec((LRU_BLOCKS, 1, 2 * LRU_BS)),
                const_spec((1, MIX_W))]
    args = [z, conv_w, conv_b.reshape(1, MIX_W), w2, gb, lam.reshape(1, MIX_W)]
    if final:
        in_specs += [zspec(1), row_spec]
        args += [z, h_prev]
    return pl.pallas_call(
        functools.partial(_lru_pass_kernel, reverse=reverse, final=final),
        grid=(2, n_ctx_rows // SEQ_T + n_lat),
        in_specs=in_specs,
        out_specs=row_spec,
        out_shape=jax.ShapeDtypeStruct((rows, MIX_W), jnp.bfloat16 if final else jnp.float32),
        scratch_shapes=[pltpu.VMEM((1, MIX_W), jnp.float32),
                        pltpu.VMEM((SEQ_T, MIX_W), jnp.float32),
                        pltpu.VMEM((SEQ_T, MIX_W), jnp.float32)],
        compiler_params=_cparams("parallel", "arbitrary"),
        name="lru_bwd" if reverse else "lru_fwd",
    )(*args)


def lru_branch(z, p, n_ctx_rows, n_lat_rows):
    def one(d, h_prev):
        return lru_pass(z, p['lru_conv_w'], p['lru_conv_b'], p['lru_wa'][d], p['lru_ba'][d], p['lru_wi'][d],
                        p['lru_bi'][d], p['lru_lambda'][d], n_ctx_rows, n_lat_rows, d == 1, h_prev)
    return one(1, one(0, None))


def _hg_level_matrix(reverse):
    t = np.arange(HG_C)[:, None]
    s = np.arange(HG_C)[None, :]
    x = t ^ s
    lv = np.where(x > 0, np.floor(np.log2(np.maximum(x, 1))).astype(np.int32), -1)
    lv = np.where(s < t, lv, -1).astype(np.int32)
    return jnp.asarray(lv.T if reverse else lv)


def _hg_pass_kernel(lv_ref, lb_ref, q_ref, f_ref, v_ref, *rest, reverse, final):
    if final:
        og_ref, oprev_ref, g_ref, out_ref, st_ref = rest
    else:
        out_ref, st_ref = rest

    @pl.when(pl.program_id(1) == 0)
    def _():
        st_ref[...] = jnp.zeros_like(st_ref)

    lv = lv_ref[...]
    at_level = [lv == lvl for lvl in range(HG_LEVELS)]
    n_tiles = HG_C // SUBLANES
    wb = HG_HB * HG_DK
    row = lax.broadcasted_iota(jnp.int32, (1, SUBLANES, wb), 1)
    for hb in range(HG_HEADS // HG_HB):
        sl = slice(hb * wb, (hb + 1) * wb)
        heads = [(hb * HG_HB + j, slice(j * HG_DK, (j + 1) * HG_DK)) for j in range(HG_HB)]
        x = f_ref[:, sl].astype(jnp.float32)
        lb = lb_ref[:, sl]
        e = jnp.exp(-jnp.abs(x))
        inv = 1.0 / (1.0 + e)
        pos = x >= 0
        sig = jnp.where(pos, inv, e * inv)
        sigm = jnp.where(pos, e * inv, inv)
        lsig = jnp.minimum(x, 0.0) - jnp.log(1.0 + e)
        lf = jnp.where(lb > 0, jnp.log(lb + (1.0 - lb) * sig), lsig)
        kk = (1.0 - lb) * sigm
        q = q_ref[:, sl].astype(jnp.float32)
        v = v_ref[:, sl]

        c = (lf * LOG2E).reshape(n_tiles, SUBLANES, wb)
        bt = c
        att = [None] * HG_HB
        for lvl in range(HG_LEVELS):
            m = 1 << lvl
            qm = (q * jnp.exp2(c).reshape(HG_C, wb)).astype(jnp.bfloat16)
            km = (kk * jnp.exp2(bt - c).reshape(HG_C, wb)).astype(jnp.bfloat16)
            for j, (_, hs) in enumerate(heads):
                d = lax.dot_general(qm[:, hs], km[:, hs], (((1,), (1,)), ((), ())),
                                    preferred_element_type=jnp.float32)
                att[j] = jnp.where(at_level[lvl], d, 0.0 if lvl == 0 else att[j])
            if m < SUBLANES:
                upper = (row & m) != 0
                down = pltpu.roll(bt, m, 1)
                up = pltpu.roll(bt, SUBLANES - m, 1)
                if reverse:
                    c = c + jnp.where(upper, 0.0, up)
                else:
                    c = c + jnp.where(upper, down, 0.0)
                bt = bt + jnp.where(upper, down, up)
            else:
                k = m // SUBLANES
                pair = (n_tiles // (2 * k), 2, k, SUBLANES, wb)
                c5, b5 = c.reshape(pair), bt.reshape(pair)
                tot2 = b5[:, 0] + b5[:, 1]
                if reverse:
                    c = jnp.stack([c5[:, 0] + b5[:, 1], c5[:, 1]], axis=1)
                else:
                    c = jnp.stack([c5[:, 0], c5[:, 1] + b5[:, 0]], axis=1)
                c = c.reshape(n_tiles, SUBLANES, wb)
                bt = jnp.stack([tot2, tot2], axis=1).reshape(n_tiles, SUBLANES, wb)

        qm = (q * jnp.exp2(c).reshape(HG_C, wb)).astype(jnp.bfloat16)
        km = (kk * jnp.exp2(bt - c).reshape(HG_C, wb)).astype(jnp.bfloat16)
        tot = jnp.exp2(bt[0, 0:1, :])
        diag = q * kk
        for j, (h, hs) in enumerate(heads):
            osl = slice(h * HG_DK, (h + 1) * HG_DK)
            st = st_ref[h]
            vh = v[:, hs]
            o = lax.dot_general(qm[:, hs], st.astype(jnp.bfloat16), (((1,), (1,)), ((), ())),
                                preferred_element_type=jnp.float32)
            o = o + jnp.dot(att[j].astype(jnp.bfloat16), vh, preferred_element_type=jnp.float32)
            o = o + jnp.sum(diag[:, hs], axis=-1, keepdims=True) * vh.astype(jnp.float32)
            st_ref[h] = st * tot[:, hs] + lax.dot_general(vh, km[:, hs], (((0,), (0,)), ((), ())),
                                                          preferred_element_type=jnp.float32)
            if final:
                o = o + oprev_ref[:, osl]
                y = o * lax.rsqrt(jnp.mean(o * o, axis=-1, keepdims=True) + EPS) * g_ref[:, osl]
                og = og_ref[:, osl].astype(jnp.float32)
                out_ref[:, osl] = (y * (og / (1.0 + jnp.exp(-og)))).astype(out_ref.dtype)
            else:
                out_ref[:, osl] = o


def hgrn_pass(z, lb, n_ctx_rows, n_lat_rows, reverse, o_prev=None, norm_g=None):
    rows = z.shape[0]
    n_ctx, n_lat = n_ctx_rows // HG_C, n_lat_rows // HG_C
    cb = HG_COL0 // MIX_W
    final = o_prev is not None
    rb = functools.partial(_row_block, n_ctx=n_ctx, n_lat=n_lat, reverse=reverse)

    def zspec(k):
        return pl.BlockSpec((HG_C, MIX_W), lambda b, j: (rb(b, j), cb + k))

    row_spec = pl.BlockSpec((HG_C, MIX_W), lambda b, j: (rb(b, j), 0))
    vec_spec = pl.BlockSpec((1, MIX_W), lambda b, j: (0, 0))
    in_specs = [pl.BlockSpec((HG_C, HG_C), lambda b, j: (0, 0)), vec_spec,
                zspec(0), zspec(2 if reverse else 1), zspec(3)]
    args = [_hg_level_matrix(reverse), lb.reshape(1, MIX_W), z, z, z]
    if final:
        in_specs += [zspec(4), row_spec, vec_spec]
        args += [z, o_prev, norm_g.reshape(1, MIX_W)]
    return pl.pallas_call(
        functools.partial(_hg_pass_kernel, reverse=reverse, final=final),
        grid=(2, n_ctx + n_lat),
        in_specs=in_specs,
        out_specs=row_spec,
        out_shape=jax.ShapeDtypeStruct((rows, MIX_W), jnp.bfloat16 if final else jnp.float32),
        scratch_shapes=[pltpu.VMEM((HG_HEADS, HG_DK, HG_DK), jnp.float32)],
        compiler_params=_cparams("parallel", "arbitrary"),
        name="hgrn_bwd" if reverse else "hgrn_fwd",
    )(*args)


def hgrn_branch(z, lb, norm_g, n_ctx_rows, n_lat_rows):
    o_f = hgrn_pass(z, lb, n_ctx_rows, n_lat_rows, False)
    return hgrn_pass(z, lb, n_ctx_rows, n_lat_rows, True, o_f, norm_g)


def _hy_conv_kernel(x_ref, w_ref, b_ref, out_ref, *, is_ctx):
    x = x_ref[...].astype(jnp.float32)
    t = lax.broadcasted_iota(jnp.int32, x.shape, 0)
    pos = t if is_ctx else t & (GRID_W - 1)
    last = SEQ_T - 1 if is_ctx else GRID_W - 1
    w = w_ref[...]
    y = b_ref[...] + w[1:2] * x
    y = y + w[0:1] * jnp.where(pos >= 1, pltpu.roll(x, 1, 0), 0.0)
    y = y + w[2:3] * jnp.where(pos < last, pltpu.roll(x, SEQ_T - 1, 0), 0.0)
    for g in range(3):
        out_ref[g] = y[:, g * MIX_W:(g + 1) * MIX_W].astype(out_ref.dtype)


def hy_conv(z, conv_w, conv_b, block0, n_rows, is_ctx):
    nb = n_rows // SEQ_T
    return pl.pallas_call(
        functools.partial(_hy_conv_kernel, is_ctx=is_ctx),
        grid=(2, nb),
        in_specs=[pl.BlockSpec((SEQ_T, 3 * MIX_W), lambda b, j: (block0 + b * nb + j, 0)),
                  pl.BlockSpec((3, 3 * MIX_W), lambda b, j: (0, 0)),
                  pl.BlockSpec((1, 3 * MIX_W), lambda b, j: (0, 0))],
        out_specs=pl.BlockSpec((3, None, SEQ_T, MIX_W), lambda b, j: (0, b, j, 0)),
        out_shape=jax.ShapeDtypeStruct((3, 2, n_rows, MIX_W), jnp.bfloat16),
        compiler_params=_cparams("parallel", "arbitrary"),
        name="hy_conv_ctx" if is_ctx else "hy_conv_lat",
    )(z, conv_w, conv_b.reshape(1, 3 * MIX_W))


def _hy_conv_lat_kernel(*refs):
    x_refs, (w_ref, b_ref, out_ref) = refs[:HY_PARTS], refs[HY_PARTS:]
    x = jnp.concatenate([r[...] for r in x_refs], axis=0).astype(jnp.float32)
    n = x.shape[0]
    pos = lax.broadcasted_iota(jnp.int32, x.shape, 0) & (GRID_W - 1)
    w = w_ref[...]
    y = b_ref[...] + w[1:2] * x
    y = y + w[0:1] * jnp.where(pos >= 1, pltpu.roll(x, 1, 0), 0.0)
    y = y + w[2:3] * jnp.where(pos < GRID_W - 1, pltpu.roll(x, n - 1, 0), 0.0)
    y = y.astype(out_ref.dtype).reshape(n // FFT_R, FFT_R, MIX_W)
    out_ref[...] = jnp.swapaxes(y, 0, 1)


def hy_conv_latent(z, conv_w, conv_b, row0, n_rows):
    part = FFT_SB * FFT_R // HY_PARTS
    nb = n_rows // (FFT_SB * FFT_R)

    def xspec(k):
        return pl.BlockSpec((part, MIX_W), lambda g, b, j: ((row0 + b * n_rows) // part + j * HY_PARTS + k, g))

    return pl.pallas_call(
        _hy_conv_lat_kernel,
        grid=(3, 2, nb),
        in_specs=[xspec(k) for k in range(HY_PARTS)] + [
            pl.BlockSpec((3, MIX_W), lambda g, b, j: (0, g)), pl.BlockSpec((1, MIX_W), lambda g, b, j: (0, g))],
        out_specs=pl.BlockSpec((None, None, FFT_R, FFT_SB, MIX_W), lambda g, b, j: (g, b, 0, j, 0)),
        out_shape=jax.ShapeDtypeStruct((3, 2, FFT_R, n_rows // FFT_R, MIX_W), jnp.bfloat16),
        compiler_params=_cparams("parallel", "parallel", "arbitrary"),
        name="hy_conv_lat",
    )(*([z] * HY_PARTS), conv_w, conv_b.reshape(1, 3 * MIX_W))


def _cis(num, den):
    ang = (2.0 * math.pi / den) * (num % den).astype(jnp.float32)
    return jnp.cos(ang), jnp.sin(ang)


def fft_matrices():
    r = FFT_R
    i = jnp.arange(r, dtype=jnp.int32)
    c, s = _cis(i[:, None] * i[None, :], r)
    half = r // 2
    fa_data = jnp.concatenate([jnp.concatenate([c[:, :half], s[:, :half]], axis=1),
                               jnp.concatenate([-s[:, :half], c[:, :half]], axis=1)], axis=0)
    fa_taps = jnp.concatenate([c, -s], axis=0)
    ct, st = _cis(i[:, None] * i[None, :], FFT_L)
    gc = c[None] * ct[:, None, :] - s[None] * st[:, None, :]
    gs = s[None] * ct[:, None, :] + c[None] * st[:, None, :]
    gr, gi = gc, -gs
    m1 = jnp.concatenate([jnp.concatenate([gr, -gi], axis=2), jnp.concatenate([gi, gr], axis=2)], axis=1)
    er, ei = c[:half] / FFT_L, s[:half] / FFT_L
    e2 = jnp.concatenate([jnp.concatenate([er, -ei], axis=1), jnp.concatenate([ei, er], axis=1)], axis=0)
    bf = jnp.bfloat16
    return dict(fa_data=fa_data.astype(bf), fa_taps=fa_taps.astype(bf), m1=m1.astype(bf), e2=e2.astype(bf))


def _fft_a_kernel(f_ref, x_ref, o_ref, *, two):
    f = f_ref[...]
    res = []
    for s in range(FFT_SB):
        x = jnp.concatenate([x_ref[0, s], x_ref[1, s]], axis=0) if two else x_ref[s]
        res.append(jnp.dot(f, x, preferred_element_type=jnp.float32).astype(o_ref.dtype))
    o_ref[...] = jnp.swapaxes(jnp.stack(res), 0, 1).reshape(o_ref.shape)


def fft_stage_a(fmat, x, g):
    two = x.ndim == 5
    xspec = (pl.BlockSpec((None, 2, FFT_SB, FFT_R // 2, MIX_W), lambda j: (g, 0, j, 0, 0)) if two
             else pl.BlockSpec((None, FFT_SB, FFT_R, MIX_W), lambda j: (g, j, 0, 0)))
    return pl.pallas_call(
        functools.partial(_fft_a_kernel, two=two),
        grid=(FFT_R // FFT_SB,),
        in_specs=[pl.BlockSpec((2 * FFT_R, FFT_R), lambda j: (0, 0)), xspec],
        out_specs=pl.BlockSpec((2, FFT_R, FFT_SB, MIX_W), lambda j: (0, 0, j, 0)),
        out_shape=jax.ShapeDtypeStruct((2, FFT_R, FFT_R, MIX_W), jnp.bfloat16),
        compiler_params=_cparams("parallel"),
        name="fft_stage_a",
    )(fmat, x)


def _fft_mid_kernel(m1_ref, a_ref, *rest, conv):
    if conv:
        h_ref, o_ref = rest
    else:
        s_ref, o_ref = rest
    r = FFT_R
    for k in range(FFT_KB):
        a = a_ref[:, k].reshape(2 * r, MIX_W)
        x = jnp.dot(m1_ref[k], a, preferred_element_type=jnp.float32)
        if conv:
            h = h_ref[k].astype(jnp.float32)
            xr, xi, hr, hi = x[:r], x[r:], h[:r], h[r:]
            zc = jnp.concatenate([xr * hr - xi * hi, xr * hi + xi * hr], axis=0).astype(jnp.bfloat16)
            p = lax.dot_general(m1_ref[k], zc, (((0,), (0,)), ((), ())), preferred_element_type=jnp.float32)
            o_ref[:, k] = p.reshape(2, r, MIX_W).astype(o_ref.dtype)
        else:
            o_ref[k] = (x * s_ref[...]).astype(o_ref.dtype)


def fft_mid(m1, a, h=None, scale=None):
    conv = h is not None
    r = FFT_R
    mspec = pl.BlockSpec((FFT_KB, 2 * r, 2 * r), lambda j: (j, 0, 0))
    aspec = pl.BlockSpec((2, FFT_KB, r, MIX_W), lambda j: (0, j, 0, 0))
    hspec = pl.BlockSpec((FFT_KB, 2 * r, MIX_W), lambda j: (j, 0, 0))
    if conv:
        in_specs, args = [mspec, aspec, hspec], (m1, a, h)
        out_specs, out_shape = aspec, jax.ShapeDtypeStruct((2, r, r, MIX_W), jnp.bfloat16)
    else:
        in_specs, args = [mspec, aspec, pl.BlockSpec((1, MIX_W), lambda j: (0, 0))], (m1, a, scale)
        out_specs, out_shape = hspec, jax.ShapeDtypeStruct((r, 2 * r, MIX_W), jnp.bfloat16)
    return pl.pallas_call(
        functools.partial(_fft_mid_kernel, conv=conv),
        grid=(r // FFT_KB,),
        in_specs=in_specs, out_specs=out_specs, out_shape=out_shape,
        compiler_params=_cparams("parallel"),
        name="fft_mid_conv" if conv else "fft_mid_filter",
    )(*args)


def _fft_out_kernel(e_ref, p_ref, mul_ref, add_ref, bias_ref, o_ref, *, time_order):
    e = e_ref[...]
    half = FFT_R // 2
    bias = bias_ref[...]
    p = jnp.swapaxes(p_ref[...].reshape(2 * FFT_R, FFT_SB, MIX_W), 0, 1)
    res = [[], []]
    for s in range(FFT_SB):
        y = jnp.dot(e, p[s], preferred_element_type=jnp.float32)
        for b in range(2):
            yb = y[b * half:(b + 1) * half] + add_ref[b, s].astype(jnp.float32) * bias
            res[b].append((mul_ref[b, s].astype(jnp.float32) * yb).astype(o_ref.dtype))
    for b in range(2):
        out = jnp.stack(res[b])
        o_ref[b] = jnp.swapaxes(out, 0, 1) if time_order else out


def fft_out(e2, p, mul, gm, add, ga, bias, time_order):
    half = FFT_R // 2

    def bspec(g):
        return pl.BlockSpec((None, 2, FFT_SB, half, MIX_W), lambda j: (g, 0, j, 0, 0))

    if time_order:
        out_spec = pl.BlockSpec((None, 2, half, FFT_SB, MIX_W), lambda j: (0, 0, 0, j, 0))
        out_shape = jax.ShapeDtypeStruct((1, 2, half, FFT_R, MIX_W), jnp.bfloat16)
    else:
        out_spec, out_shape = bspec(0), jax.ShapeDtypeStruct((1, 2, FFT_R, half, MIX_W), jnp.bfloat16)
    return pl.pallas_call(
        functools.partial(_fft_out_kernel, time_order=time_order),
        grid=(FFT_R // FFT_SB,),
        in_specs=[pl.BlockSpec((FFT_R, 2 * FFT_R), lambda j: (0, 0)),
                  pl.BlockSpec((2, FFT_R, FFT_SB, MIX_W), lambda j: (0, 0, j, 0)), bspec(gm), bspec(ga),
                  pl.BlockSpec((1, MIX_W), lambda j: (0, 0))],
        out_specs=out_spec,
        out_shape=out_shape,
        compiler_params=_cparams("parallel", vmem=FFT_OUT_VMEM_BYTES),
        name="fft_out",
    )(e2, p, mul, add, bias)


def hyena_spectra(taps, inv_norm, mats):
    return [fft_mid(mats['m1'], fft_stage_a(mats['fa_taps'], taps, order), scale=inv_norm[order])
            for order in range(2)]


def hyena_latent(xs, spectra, bias, mats):
    u, gu = xs, 2
    for order in range(2):
        p = fft_mid(mats['m1'], fft_stage_a(mats['fa_data'], u, gu), h=spectra[order])
        u, gu = fft_out(mats['e2'], p, xs, order, u, gu, bias[order].reshape(1, MIX_W), order == 1), 0
    return u.reshape(2, FFT_L // 2, MIX_W)


def _hy_ctx_kernel(mt_ref, mf_ref, mi_ref, xc_ref, taps_ref, s_ref, bias_ref, o_ref):
    length = mi_ref.shape[0]
    n = length // 2
    u = xc_ref[2].astype(jnp.float32).reshape(length, -1)
    for order in range(2):
        h = jnp.dot(mt_ref[...], taps_ref[order], preferred_element_type=jnp.float32) * s_ref[order]
        w = jnp.dot(mf_ref[...], u.astype(jnp.bfloat16), preferred_element_type=jnp.float32)
        wr, wi, hr, hi = w[:length], w[length:], h[:length], h[length:]
        zc = jnp.concatenate([wr * hr - wi * hi, wr * hi + wi * hr], axis=0).astype(jnp.bfloat16)
        y = jnp.dot(mi_ref[...], zc, preferred_element_type=jnp.float32)
        gate = xc_ref[order].astype(jnp.float32).reshape(length, -1)
        u = gate * (y + u * bias_ref[order])
    o_ref[...] = u.reshape(2, n, -1).astype(o_ref.dtype)


def hyena_context(xc, taps, inv_norm, bias):
    n = xc.shape[2]
    length = 2 * n
    k = jnp.arange(length, dtype=jnp.int32)
    c, s = _cis(k[:, None] * k[None, :], length)
    fr, fi = c[:, :n], -s[:, :n]
    m_fwd = jnp.concatenate([jnp.concatenate([fr, -fi], axis=1), jnp.concatenate([fi, fr], axis=1)], axis=0)
    m_taps = jnp.concatenate([c, -s], axis=0)
    er, ei = c[:n] / length, s[:n] / length
    m_inv = jnp.concatenate([jnp.concatenate([er, -ei], axis=1), jnp.concatenate([ei, er], axis=1)], axis=0)
    ct = 2 * LANES

    def whole(a):
        return pl.BlockSpec(a.shape, lambda j: (0,) * a.ndim)

    mats = [m.astype(jnp.bfloat16) for m in (m_taps, m_fwd, m_inv)]
    return pl.pallas_call(
        _hy_ctx_kernel,
        grid=(MIX_W // ct,),
        in_specs=[*(whole(m) for m in mats),
                  pl.BlockSpec((3, 2, n, ct), lambda j: (0, 0, 0, j)),
                  pl.BlockSpec((2, length, ct), lambda j: (0, 0, j)),
                  pl.BlockSpec((2, 1, ct), lambda j: (0, 0, j)),
                  pl.BlockSpec((2, 1, ct), lambda j: (0, 0, j))],
        out_specs=pl.BlockSpec((2, n, ct), lambda j: (0, 0, j)),
        out_shape=jax.ShapeDtypeStruct((2, n, MIX_W), jnp.bfloat16),
        compiler_params=_cparams("parallel"),
        name="hyena_ctx",
    )(*mats, xc, taps, inv_norm, bias.reshape(2, 1, MIX_W))


def hyena_features(n, ff1, ff1_b, freq, ff2, ff2_b):
    k = jnp.arange(n, dtype=jnp.int32)
    pos = jnp.stack([k, (n - k) % n]).astype(jnp.float32)
    t = pos / (n - 1)
    fw = (2.0 * math.pi * pos / n)[..., None] * jnp.linspace(1e-4, HY_BANDS - 1, HY_BANDS, dtype=jnp.float32)
    feats = jnp.concatenate([t[..., None], jnp.cos(fw), -jnp.sin(fw)], axis=-1)
    h = jnp.sin(freq * (feats @ ff1 + ff1_b))
    return jnp.sin(freq * (h @ ff2 + ff2_b))


def _filter_kernel(feat_ref, w_ref, delta_ref, taps_ref, sum_ref, *, n, tm, split):
    dr = pl.program_id(1)
    i = pl.program_id(2)
    row = i * tm + lax.broadcasted_iota(jnp.int32, (tm, MIX_W), 0)
    lag = jnp.where(dr == 0, row, jnp.where(row == 0, 0, n - row))
    t = lag.astype(jnp.float32) / (n - 1)
    h = jnp.dot(feat_ref[...].astype(jnp.bfloat16), w_ref[...].astype(jnp.bfloat16),
                preferred_element_type=jnp.float32)
    h = h * jnp.exp(-t * delta_ref[...])

    @pl.when((dr == 0) & (i == 0))
    def _():
        sum_ref[...] = jnp.zeros_like(sum_ref)

    sum_ref[...] += jnp.sum(jnp.abs(h), axis=0, keepdims=True)
    taps = jnp.where((dr == 0) | (row != 0), h, 0.0).astype(taps_ref.dtype)
    if split:
        taps = jnp.swapaxes(taps.reshape(tm // FFT_R, FFT_R, MIX_W), 0, 1)
    taps_ref[...] = taps


def hyena_filter_taps(feats, ff3):
    _, n, nf = feats.shape
    split = 2 * n == FFT_L
    tm = FFT_SB * FFT_R if split else n
    nb = n // tm
    w = ff3.reshape(nf, 2, 2, MIX_W).transpose(1, 2, 0, 3)
    deltas = jnp.abs(jnp.linspace(math.log(HY_TARGET) / HY_SLOW_PCT, math.log(HY_TARGET) / HY_FAST_PCT,
                                  MIX_W, dtype=jnp.float32)).reshape(1, MIX_W)
    if split:
        tspec = pl.BlockSpec((None, FFT_R, FFT_SB, MIX_W), lambda o, dr, i: (o, 0, dr * nb + i, 0))
        tshape = jax.ShapeDtypeStruct((2, FFT_R, FFT_R, MIX_W), jnp.bfloat16)
    else:
        tspec = pl.BlockSpec((None, tm, MIX_W), lambda o, dr, i: (o, dr * nb + i, 0))
        tshape = jax.ShapeDtypeStruct((2, 2 * n, MIX_W), jnp.bfloat16)
    return pl.pallas_call(
        functools.partial(_filter_kernel, n=n, tm=tm, split=split),
        grid=(2, 2, nb),
        in_specs=[pl.BlockSpec((None, tm, nf), lambda o, dr, i: (dr, i, 0)),
                  pl.BlockSpec((None, None, nf, MIX_W), lambda o, dr, i: (o, dr, 0, 0)),
                  pl.BlockSpec((1, MIX_W), lambda o, dr, i: (0, 0))],
        out_specs=[tspec, pl.BlockSpec((None, 1, MIX_W), lambda o, dr, i: (o, 0, 0))],
        out_shape=[tshape, jax.ShapeDtypeStruct((2, 1, MIX_W), jnp.float32)],
        compiler_params=_cparams("arbitrary", "arbitrary", "arbitrary"),
        name="hyena_filter",
    )(feats, w, deltas)


def _merge_out_kernel(yhc_ref, yhl_ref, yl_ref, yg_ref, wb_ref, g0_ref, g1_ref, g2_ref, wo_ref, *refs, n_ctx_tiles):
    *x_refs, n1_ref, gt_ref, n2_ref, sh_ref, sc_ref, wr_ref, xo_ref, h_ref, lg_ref = refs
    ys = (_pick_rows(yhc_ref, yhl_ref, n_ctx_tiles), yl_ref[...], yg_ref[...])
    acc = None
    for j, (y, g_ref) in enumerate(zip(ys, (g0_ref, g1_ref, g2_ref))):
        gate = 1.0 / (1.0 + jnp.exp(-g_ref[...].astype(jnp.float32)))
        t = gate * jnp.dot(y, wb_ref[j], preferred_element_type=jnp.float32)
        acc = t if acc is None else acc + t
    x = _pick_rows(*x_refs, n_ctx_tiles) if len(x_refs) == 2 else x_refs[0][...]
    y = jnp.dot(acc.astype(jnp.bfloat16), wo_ref[...], preferred_element_type=jnp.float32)
    y = y * lax.rsqrt(jnp.mean(y * y, axis=-1, keepdims=True) + EPS) * n1_ref[...]
    x = x + gt_ref[...] * y
    xo_ref[...] = x
    h = x * lax.rsqrt(jnp.mean(x * x, axis=-1, keepdims=True) + EPS) * n2_ref[...]
    h = (h * (1.0 + sc_ref[...]) + sh_ref[...]).astype(jnp.bfloat16)
    h_ref[...] = h
    lg_ref[...] = jnp.dot(h, wr_ref[...], preferred_element_type=jnp.float32)


def merge_out(z, y_hy_ctx, y_hy_lat, y_lru, y_hg, w_branch, w_out, layer, x, n1, gt, n2, sh, sc, w_router,
              n_ctx_rows, n_lat_rows):
    tm = ROW_T // 2
    xspecs, xs, rows, d = _row_operand(x, tm)
    n_ctx_tiles = 2 * n_ctx_rows // tm
    idx = functools.partial(_mod_index, n_ctx_tiles=n_ctx_tiles, tiles_per_batch=n_lat_rows // tm)
    row = pl.BlockSpec((tm, d), lambda i: (i, 0))
    yspec = pl.BlockSpec((tm, MIX_W), lambda i: (i, 0))
    vec = pl.BlockSpec((1, d), lambda i: (0, 0))
    mod = pl.BlockSpec((None, 1, d), lambda i: (idx(i), 0, 0))
    resident = pl.Buffered(1)

    def gspec(k):
        return pl.BlockSpec((tm, d), lambda i: (i, GATE_COL0 // d + k))

    wr = jnp.zeros((d, ROUTER_PAD), jnp.bfloat16).at[:, :N_EXPERTS].set(w_router.astype(jnp.bfloat16))
    return pl.pallas_call(
        functools.partial(_merge_out_kernel, n_ctx_tiles=n_ctx_tiles),
        grid=(rows // tm,),
        in_specs=[*_split_row_specs(tm, MIX_W, n_ctx_tiles), yspec, yspec,
                  pl.BlockSpec((None, N_BRANCH, MIX_W, d), lambda i: (layer, 0, 0, 0), pipeline_mode=resident),
                  gspec(0), gspec(1), gspec(2),
                  pl.BlockSpec((None, d, d), lambda i: (layer, 0, 0), pipeline_mode=resident),
                  *xspecs, vec, mod, vec, mod, mod,
                  pl.BlockSpec((d, ROUTER_PAD), lambda i: (0, 0), pipeline_mode=resident)],
        out_specs=[row, row, pl.BlockSpec((tm, ROUTER_PAD), lambda i: (i, 0))],
        out_shape=[jax.ShapeDtypeStruct((rows, d), jnp.float32), jax.ShapeDtypeStruct((rows, d), jnp.bfloat16),
                   jax.ShapeDtypeStruct((rows, ROUTER_PAD), jnp.float32)],
        compiler_params=_cparams("parallel", vmem=MERGE_OUT_VMEM_BYTES),
        name="merge_out",
    )(y_hy_ctx, y_hy_lat, y_lru, y_hg, w_branch, z, z, z, w_out, *xs, n1.reshape(1, d), gt, n2.reshape(1, d),
      sh, sc, wr)


def _ffn_up_kernel(x_ref, wg_ref, wu_ref, h_ref):
    x = x_ref[...]
    hg = jnp.dot(x, wg_ref[...].astype(jnp.bfloat16), preferred_element_type=jnp.float32)
    hu = jnp.dot(x, wu_ref[...].astype(jnp.bfloat16), preferred_element_type=jnp.float32)
    h_ref[...] = (hg / (1.0 + jnp.exp(-hg)) * hu).astype(h_ref.dtype)


def _ffn_down_kernel(h_ref, wd_ref, g_ref, o_ref):
    y = jnp.dot(h_ref[...], wd_ref[...].astype(jnp.bfloat16), preferred_element_type=jnp.float32)
    o_ref[...] = (y * g_ref[...]).astype(o_ref.dtype)


def expert_ffn(xe, w_gate, w_up, w_down, layer, g):
    ne, r, d = xe.shape
    dff = w_gate.shape[-1]
    hid = pl.pallas_call(
        _ffn_up_kernel,
        grid=(ne, dff // FFN_TF),
        in_specs=[pl.BlockSpec((None, r, d), lambda e, f: (e, 0, 0)),
                  pl.BlockSpec((None, None, d, FFN_TF), lambda e, f: (layer, e, 0, f)),
                  pl.BlockSpec((None, None, d, FFN_TF), lambda e, f: (layer, e, 0, f))],
        out_specs=pl.BlockSpec((None, r, FFN_TF), lambda e, f: (e, 0, f)),
        out_shape=jax.ShapeDtypeStruct((ne, r, dff), jnp.bfloat16),
        compiler_params=_cparams("parallel", "arbitrary"),
        name="ffn_up",
    )(xe, w_gate, w_up)
    return pl.pallas_call(
        _ffn_down_kernel,
        grid=(ne, d // FFN_TD),
        in_specs=[pl.BlockSpec((None, r, dff), lambda e, j: (e, 0, 0)),
                  pl.BlockSpec((None, None, dff, FFN_TD), lambda e, j: (layer, e, 0, j)),
                  pl.BlockSpec((None, r, 1), lambda e, j: (e, 0, 0))],
        out_specs=pl.BlockSpec((None, r, FFN_TD), lambda e, j: (e, 0, j)),
        out_shape=jax.ShapeDtypeStruct((ne, r, d), jnp.bfloat16),
        compiler_params=_cparams("parallel", "arbitrary"),
        name="ffn_down",
    )(hid, w_down, g)


def route(logits, row0, bsz, n):
    cap = EC_CAPACITY * n // N_EXPERTS
    aff = jax.nn.softmax(logits[row0:row0 + bsz * n, :N_EXPERTS].reshape(bsz, n, N_EXPERTS), axis=-1)
    g, idx = lax.top_k(jnp.swapaxes(aff, 1, 2), cap)
    flat = idx + (row0 + jnp.arange(bsz, dtype=idx.dtype) * n)[:, None, None]
    return (jnp.swapaxes(g, 0, 1).reshape(N_EXPERTS, bsz * cap),
            jnp.swapaxes(flat, 0, 1).reshape(N_EXPERTS, bsz * cap))


def ec_moe(h, logits, w_gate, w_up, w_down, layer, segments):
    gs, flats = zip(*(route(logits, *seg) for seg in segments))
    flat, g = lax.sort_key_val(jnp.concatenate(flats, axis=1), jnp.concatenate(gs, axis=1), dimension=1)
    return expert_ffn(h[flat], w_gate, w_up, w_down, layer, g[..., None]), flat


def moe_plan(flat, n_rows):
    ne, r = flat.shape
    nwin = r // MOE_WIN
    n_tiles = n_rows // MOE_TT
    bounds = jnp.arange(n_tiles + 1, dtype=flat.dtype) * MOE_TT
    p = jnp.sum(flat[None, :, :] < bounds[:, None, None], axis=2, dtype=jnp.int32)
    p0, p1 = p[:-1], p[1:]
    w0 = p0 // MOE_WIN
    nw = jnp.where(p1 > p0, (p1 - 1) // MOE_WIN - w0 + 1, 0)
    off = jnp.cumsum(nw, axis=1)
    q = jnp.arange(MOE_MAXQ, dtype=jnp.int32)
    e_of_q = jnp.minimum(jnp.sum(q[None, :, None] >= off[:, None, :], axis=2, dtype=jnp.int32), ne - 1)
    pick = e_of_q[:, :, None] == jnp.arange(ne, dtype=jnp.int32)
    w = q[None, :] + jnp.sum(jnp.where(pick, (w0 - (off - nw))[:, None, :], 0), axis=2)
    items = jnp.clip(e_of_q * nwin + w, 0, ne * nwin - 1)
    return items.astype(jnp.int32), off[:, -1].astype(jnp.int32)


def _combine_kernel(items_ref, count_ref, tok_ref, ye_ref, x_ref, g_ref, gt_ref, *rest, nxt, skip):
    if nxt:
        gn_ref, sh_ref, sc_ref, o_ref, u_ref, stage, sem, acc_ref = rest
    else:
        o_ref, stage, sem, acc_ref = rest
    i = pl.program_id(0)
    t = i + skip
    n = count_ref[t]
    n_groups = (n + MOE_GROUP - 1) // MOE_GROUP

    @pl.when(i == 0)
    def _():
        stage[...] = jnp.zeros_like(stage)

    def window_copy(tile, g, buf, k):
        item = items_ref[tile, jnp.minimum(MOE_GROUP * g + k, MOE_MAXQ - 1)]
        return pltpu.make_async_copy(ye_ref.at[pl.ds(item * MOE_WIN, MOE_WIN)],
                                     stage.at[buf, pl.ds(k * MOE_WIN, MOE_WIN)], sem.at[buf])

    def for_group(tile, g, buf, wait):
        for k in range(MOE_GROUP):
            @pl.when(MOE_GROUP * g + k < count_ref[tile])
            def _():
                cp = window_copy(tile, g, buf, k)
                cp.wait() if wait else cp.start()

    def start_first_groups(tile):
        for j in range(MOE_NBUF - 1):
            for_group(tile, j, j, False)

    acc_ref[...] = jnp.zeros_like(acc_ref)

    @pl.when(i == 0)
    def _():
        start_first_groups(t)

    row = t * MOE_TT + lax.broadcasted_iota(jnp.int32, (MOE_TT, MOE_GROUP * MOE_WIN), 0)

    def body(g, carry):
        buf = g % MOE_NBUF
        ahead = g + MOE_NBUF - 1

        @pl.when(ahead < n_groups)
        def _():
            for_group(t, ahead, ahead % MOE_NBUF, False)

        for_group(t, g, buf, True)
        toks = []
        for k in range(MOE_GROUP):
            qk = MOE_GROUP * g + k
            item = items_ref[t, jnp.minimum(qk, MOE_MAXQ - 1)]
            toks.append(jnp.where(qk < n, tok_ref[pl.ds(item, 1), :], -1))
        onehot = jnp.where(row == jnp.concatenate(toks, axis=1), 1.0, 0.0).astype(jnp.bfloat16)
        acc_ref[...] += jnp.dot(onehot, stage[buf], preferred_element_type=jnp.float32)
        return carry

    lax.fori_loop(0, n_groups, body, 0)

    @pl.when(i + 1 < pl.num_programs(0))
    def _():
        start_first_groups(t + 1)

    y = acc_ref[...]
    y = y * lax.rsqrt(jnp.mean(y * y, axis=-1, keepdims=True) + EPS) * g_ref[...]
    x = x_ref[...] + gt_ref[...] * y
    if nxt:
        u = x * lax.rsqrt(jnp.mean(x * x, axis=-1, keepdims=True) + EPS) * gn_ref[...]
        u_ref[...] = (u * (1.0 + sc_ref[...]) + sh_ref[...]).astype(u_ref.dtype)
    o_ref[...] = x


def moe_combine(x, ye, flat, g, gt, n_ctx_rows, n_lat_rows, nxt=None):
    rows, d = x.shape
    ne, r, _ = ye.shape
    items, count = moe_plan(flat, rows)
    n_ctx_tiles = 2 * n_ctx_rows // MOE_TT
    idx = functools.partial(_mod_index, n_ctx_tiles=n_ctx_tiles, tiles_per_batch=n_lat_rows // MOE_TT)
    skip = 0 if nxt else n_ctx_tiles
    row = pl.BlockSpec((MOE_TT, d), lambda i, *_: (i + skip, 0))
    vec = pl.BlockSpec((1, d), lambda i, *_: (0, 0))
    mod = pl.BlockSpec((None, 1, d), lambda i, *_: (idx(i + skip), 0, 0))
    out_row = pl.BlockSpec((MOE_TT, d), lambda i, *_: (i, 0))
    n_out = rows - skip * MOE_TT
    in_specs = [pl.BlockSpec((ne * r // MOE_WIN, MOE_WIN), lambda i, *_: (0, 0)),
                pl.BlockSpec(memory_space=pl.ANY), row, vec, mod]
    args = [flat.reshape(ne * r // MOE_WIN, MOE_WIN), ye.reshape(ne * r, d), x, g.reshape(1, d), gt]
    out_specs, out_shape = out_row, jax.ShapeDtypeStruct((n_out, d), jnp.float32)
    if nxt:
        in_specs += [vec, mod, mod]
        args += [nxt[0].reshape(1, d), nxt[1], nxt[2]]
        out_specs = [out_row, out_row]
        out_shape = [out_shape, jax.ShapeDtypeStruct((n_out, d), jnp.bfloat16)]
    return pl.pallas_call(
        functools.partial(_combine_kernel, nxt=bool(nxt), skip=skip),
        grid_spec=pltpu.PrefetchScalarGridSpec(
            num_scalar_prefetch=2,
            grid=(n_out // MOE_TT,),
            in_specs=in_specs, out_specs=out_specs,
            scratch_shapes=[pltpu.VMEM((MOE_NBUF, MOE_GROUP * MOE_WIN, d), jnp.bfloat16),
                            pltpu.SemaphoreType.DMA((MOE_NBUF,)),
                            pltpu.VMEM((MOE_TT, d), jnp.float32)]),
        out_shape=out_shape,
        compiler_params=_cparams("arbitrary"),
        name="moe_combine",
    )(items, count, *args)


def kernel(x, c, ctx, c_ctx, w_mod, b_mod, norm_g, w_in, hy_conv_w, hy_conv_b, hy_ff1, hy_ff1_b, hy_freq,
           hy_ff2, hy_ff2_b, hy_ff3, hy_bias, lru_conv_w, lru_conv_b, lru_wa, lru_ba, lru_wi, lru_bi,
           lru_lambda, hg_lb_logits, hg_norm_g, w_branch, w_out, w_router, w_gate, w_up, w_down):
    bsz, n_lat, d = x.shape
    n_ctx = ctx.shape[1]
    assert bsz == 2 and d == D_MODEL and n_ctx == SEQ_T and 2 * n_lat == FFT_L and n_lat % ROW_T == 0
    gam = jax.nn.softmax(hg_lb_logits.astype(jnp.float32), axis=0)
    lb_all = jnp.maximum(jnp.cumsum(gam, axis=0) - gam[:1], 0.0)
    mats = fft_matrices()
    cvec = jnp.zeros((SUBLANES, d), jnp.float32).at[:bsz].set(jax.nn.silu(c)).at[bsz].set(jax.nn.silu(c_ctx))
    n_ctx_all = bsz * n_ctx
    xa = (ctx.reshape(n_ctx_all, d), x.reshape(bsz * n_lat, d))
    w_branch_b, w_out_b = w_branch.astype(jnp.bfloat16), w_out.astype(jnp.bfloat16)
    mods, filters = [], []
    for l in range(DEPTH):
        mod = mm(cvec, w_mod, layer=l, name="adaln")[:bsz + 1] + b_mod[l]
        mods.append([mod.reshape(bsz + 1, 1, 6, d)[:, :, k] for k in range(6)])
        ffp = (hy_ff1[l], hy_ff1_b[l], hy_freq[l], hy_ff2[l], hy_ff2_b[l])
        taps_l, norm_l = hyena_filter_taps(hyena_features(n_lat, *ffp), hy_ff3[l])
        taps_c, norm_c = hyena_filter_taps(hyena_features(n_ctx, *ffp), hy_ff3[l])
        filters.append((hyena_spectra(taps_l, 1.0 / norm_l, mats), taps_c, 1.0 / norm_c))
    u = prenorm(xa, norm_g[0, 0], mods[0][0], mods[0][1], n_ctx, n_lat)
    for l in range(DEPTH):
        last = l == DEPTH - 1
        p = {'lru_conv_w': lru_conv_w[l], 'lru_conv_b': lru_conv_b[l], 'lru_wa': lru_wa[l], 'lru_ba': lru_ba[l],
             'lru_wi': lru_wi[l], 'lru_bi': lru_bi[l], 'lru_lambda': lru_lambda[l]}
        sh1, sc1, gt1, sh2, sc2, gt2 = mods[l]

        z = mm(u, w_in, jnp.bfloat16, tm=IN_PROJ_TM, tn=IN_PROJ_TN, layer=l, name="in_proj")

        y_lru = lru_branch(z, p, n_ctx, n_lat)
        y_hg = hgrn_branch(z, lb_all[l], hg_norm_g[l], n_ctx, n_lat)

        spectra, taps_c, inv_norm_c = filters[l]
        xc_l = hy_conv_latent(z, hy_conv_w[l], hy_conv_b[l], n_ctx_all, n_lat)
        y_hy_l = hyena_latent(xc_l, spectra, hy_bias[l], mats)
        xc_c = hy_conv(z, hy_conv_w[l], hy_conv_b[l], 0, n_ctx, True)
        y_hy_c = hyena_context(xc_c, taps_c, inv_norm_c, hy_bias[l])

        xa, h, logits = merge_out(z, y_hy_c.reshape(n_ctx_all, MIX_W), y_hy_l.reshape(bsz * n_lat, MIX_W), y_lru, y_hg,
                                  w_branch_b, w_out_b, l, xa, norm_g[l, 1], gt1, norm_g[l, 2], sh2, sc2,
                                  w_router[l], n_ctx, n_lat)
        segments = [(n_ctx_all, bsz, n_lat)] if last else [(n_ctx_all, bsz, n_lat), (0, bsz, n_ctx)]
        ye, flat = ec_moe(h, logits, w_gate, w_up, w_down, l, segments)
        if last:
            xa = moe_combine(xa, ye, flat, norm_g[l, 3], gt2, n_ctx, n_lat)
        else:
            nxt = (norm_g[l + 1, 0], mods[l + 1][0], mods[l + 1][1])
            xa, u = moe_combine(xa, ye, flat, norm_g[l, 3], gt2, n_ctx, n_lat, nxt)
    return xa.reshape(bsz, n_lat, d)
```

```python
import functools
import math

import numpy as np
import jax
import jax.numpy as jnp
from jax import lax
from jax.experimental import pallas as pl
from jax.experimental.pallas import tpu as pltpu

D_MODEL = 2048
DEPTH = 2
GRID_W = 64
MIX_W = D_MODEL // 2
N_BRANCH = 3
HY_BANDS = 16
HY_TARGET = 1e-2
HY_FAST_PCT = 0.3
HY_SLOW_PCT = 1.5
LRU_BLOCKS = 8
LRU_BS = MIX_W // LRU_BLOCKS
LRU_C = 8.0
HG_HEADS = 8
HG_DK = MIX_W // HG_HEADS
N_EXPERTS = 16
EC_CAPACITY = 2
EPS = 1e-6
TINY = 1e-30
LOG2E = 1.4426950408889634
HY_COL0 = 0
LRU_COL0 = 3 * MIX_W
HG_COL0 = 5 * MIX_W
GATE_COL0 = 10 * MIX_W
IN_COLS = GATE_COL0 + N_BRANCH * D_MODEL

SUBLANES = 8
LANES = 128
SEQ_T = 256
HG_C = 128
HG_LEVELS = 7
HG_HB = 8
ROW_T = 512
IN_PROJ_TM = 2816
IN_PROJ_TN = 512
FFT_R = 128
FFT_L = FFT_R * FFT_R
FFT_KB = 8
FFT_SB = 16
HY_PARTS = 4
ROUTER_PAD = LANES
FFN_TF = 256
FFN_TD = 512
MOE_TT = 512
MOE_WIN = 32
MOE_GROUP = 16
MOE_NBUF = 4
MOE_MAXQ = N_EXPERTS * (MOE_TT // MOE_WIN + 1)
VMEM_LIMIT_BYTES = 48 * 1024 * 1024
FFT_OUT_VMEM_BYTES = 56 * 1024 * 1024
MERGE_OUT_VMEM_BYTES = 56 * 1024 * 1024


def _cparams(*sem, vmem=VMEM_LIMIT_BYTES):
    return pltpu.CompilerParams(dimension_semantics=sem, vmem_limit_bytes=vmem)


def _pick_tile(n, pref):
    for t in (pref, 1024, 512, 256, 128):
        if t <= n and n % t == 0:
            return t
    return n


def _row_block(b, j, n_ctx, n_lat, reverse):
    if reverse:
        kc, kl = n_ctx - 1 - j, n_ctx + n_lat - 1 - j
    else:
        kc, kl = j, j - n_ctx
    return jnp.where(j < n_ctx, b * n_ctx + kc, 2 * n_ctx + b * n_lat + kl)


def _mm_kernel(a_ref, b_ref, o_ref):
    o_ref[...] = jnp.dot(a_ref[...].astype(jnp.bfloat16), b_ref[...].astype(jnp.bfloat16),
                         preferred_element_type=jnp.float32).astype(o_ref.dtype)


def mm(a, b, out_dtype=jnp.float32, tm=512, tn=1024, layer=None, name="mm"):
    m, k = a.shape
    n = b.shape[-1]
    tm = _pick_tile(m, tm)
    tn = _pick_tile(n, tn)
    if layer is None:
        bspec = pl.BlockSpec((k, tn), lambda i, j: (0, j))
    else:
        bspec = pl.BlockSpec((None, k, tn), lambda i, j: (layer, 0, j))
    return pl.pallas_call(
        _mm_kernel,
        grid=(m // tm, n // tn),
        in_specs=[pl.BlockSpec((tm, k), lambda i, j: (i, 0)), bspec],
        out_specs=pl.BlockSpec((tm, tn), lambda i, j: (i, j)),
        out_shape=jax.ShapeDtypeStruct((m, n), out_dtype),
        compiler_params=_cparams("parallel", "arbitrary"),
        name=name,
    )(a, b)


def _mod_index(i, n_ctx_tiles, tiles_per_batch):
    return jnp.where(i < n_ctx_tiles, 2, (i - n_ctx_tiles) // tiles_per_batch)


def _split_row_specs(tile, width, n_ctx_tiles):
    return (pl.BlockSpec((tile, width), lambda i: (jnp.minimum(i, n_ctx_tiles - 1), 0)),
            pl.BlockSpec((tile, width), lambda i: (jnp.maximum(i - n_ctx_tiles, 0), 0)))


def _pick_rows(ctx_ref, lat_ref, n_ctx_tiles):
    return jnp.where(pl.program_id(0) < n_ctx_tiles, ctx_ref[...], lat_ref[...])


def _row_operand(x, tile):
    if isinstance(x, tuple):
        d = x[0].shape[1]
        return list(_split_row_specs(tile, d, x[0].shape[0] // tile)), list(x), x[0].shape[0] + x[1].shape[0], d
    return [pl.BlockSpec((tile, x.shape[1]), lambda i: (i, 0))], [x], x.shape[0], x.shape[1]


def _prenorm_kernel(*refs, n_ctx_tiles):
    *x_refs, g_ref, sh_ref, sc_ref, u_ref = refs
    x = _pick_rows(*x_refs, n_ctx_tiles) if len(x_refs) == 2 else x_refs[0][...]
    y = x * lax.rsqrt(jnp.mean(x * x, axis=-1, keepdims=True) + EPS) * g_ref[...]
    u_ref[...] = (y * (1.0 + sc_ref[...]) + sh_ref[...]).astype(u_ref.dtype)


def prenorm(x, g, sh, sc, n_ctx_rows, n_lat_rows):
    xspecs, xs, rows, d = _row_operand(x, ROW_T)
    n_ctx_tiles = 2 * n_ctx_rows // ROW_T
    idx = functools.partial(_mod_index, n_ctx_tiles=n_ctx_tiles, tiles_per_batch=n_lat_rows // ROW_T)
    vec = pl.BlockSpec((None, 1, d), lambda i: (idx(i), 0, 0))
    return pl.pallas_call(
        functools.partial(_prenorm_kernel, n_ctx_tiles=n_ctx_tiles),
        grid=(rows // ROW_T,),
        in_specs=[*xspecs, pl.BlockSpec((1, d), lambda i: (0, 0)), vec, vec],
        out_specs=pl.BlockSpec((ROW_T, d), lambda i: (i, 0)),
        out_shape=jax.ShapeDtypeStruct((rows, d), jnp.bfloat16),
        compiler_params=_cparams("parallel"),
        name="prenorm",
    )(*xs, g.reshape(1, d), sh, sc)


def _gelu_tanh(x):
    return 0.5 * x * (1.0 + jnp.tanh(math.sqrt(2.0 / math.pi) * (x + 0.044715 * (x * x * x))))


def _conv_pos(is_ctx, shape):
    t = lax.broadcasted_iota(jnp.int32, shape, 0)
    pos = jnp.where(is_ctx, t, t & (GRID_W - 1))
    last = jnp.where(is_ctx, SEQ_T - 1, GRID_W - 1)
    return pos, last


def _lru_pass_kernel(x_ref, cw_ref, cb_ref, w_ref, gb_ref, lam_ref, *rest, reverse, final):
    if final:
        lg_ref, hprev_ref, out_ref, h_ref, a_ref, b_ref = rest
    else:
        out_ref, h_ref, a_ref, b_ref = rest
    j = pl.program_id(1)

    @pl.when(j == 0)
    def _():
        h_ref[...] = jnp.zeros_like(h_ref)

    x = x_ref[...].astype(jnp.float32)
    pos, last = _conv_pos(j == 0, x.shape)
    cw = cw_ref[...]
    xc = cb_ref[...] + cw[2:3] * x
    xc = xc + cw[0:1] * jnp.where(pos >= 2, pltpu.roll(x, 2, 0), 0.0)
    xc = xc + cw[1:2] * jnp.where(pos >= 1, pltpu.roll(x, 1, 0), 0.0)
    xc = xc + cw[3:4] * jnp.where(pos < last, pltpu.roll(x, SEQ_T - 1, 0), 0.0)

    xb = xc.astype(jnp.bfloat16)
    lam = lam_ref[...]
    sp = jnp.maximum(-lam, 0.0) + jnp.log(1.0 + jnp.exp(-jnp.abs(lam)))
    n_tiles = SEQ_T // SUBLANES
    row = lax.broadcasted_iota(jnp.int32, (1, SUBLANES, LRU_BS), 1)
    for n in range(LRU_BLOCKS):
        sl = slice(n * LRU_BS, (n + 1) * LRU_BS)
        pre = jnp.dot(xb[:, sl], w_ref[n], preferred_element_type=jnp.float32) + gb_ref[n]
        gate = 1.0 / (1.0 + jnp.exp(-pre))
        r, i = gate[:, :LRU_BS], gate[:, LRU_BS:]
        a = jnp.exp(-LRU_C * r * sp[:, sl])
        om = 1.0 - a * a
        b = om * lax.rsqrt(jnp.maximum(om, TINY)) * (i * xc[:, sl])
        a = a.reshape(n_tiles, SUBLANES, LRU_BS)
        b = b.reshape(n_tiles, SUBLANES, LRU_BS)
        for d in (1, 2, 4):
            if reverse:
                m = row < SUBLANES - d
                a_sh, b_sh = pltpu.roll(a, SUBLANES - d, 1), pltpu.roll(b, SUBLANES - d, 1)
            else:
                m = row >= d
                a_sh, b_sh = pltpu.roll(a, d, 1), pltpu.roll(b, d, 1)
            b = jnp.where(m, a * b_sh + b, b)
            a = jnp.where(m, a * a_sh, a)
        a_ref[:, sl] = a.reshape(SEQ_T, LRU_BS)
        b_ref[:, sl] = b.reshape(SEQ_T, LRU_BS)

    h = h_ref[...]
    for k in range(n_tiles):
        kk = n_tiles - 1 - k if reverse else k
        rs = slice(kk * SUBLANES, (kk + 1) * SUBLANES)
        ht = a_ref[rs, :] * h + b_ref[rs, :]
        h = ht[0:1, :] if reverse else ht[SUBLANES - 1:SUBLANES, :]
        if final:
            ht = (ht + hprev_ref[rs, :]) * _gelu_tanh(lg_ref[rs, :].astype(jnp.float32))
        out_ref[rs, :] = ht.astype(out_ref.dtype)
    h_ref[...] = h


def lru_pass(z, conv_w, conv_b, wa, ba, wi, bi, lam, n_ctx_rows, n_lat_rows, reverse, h_prev=None):
    rows = z.shape[0]
    n_lat = n_lat_rows // SEQ_T
    cb = LRU_COL0 // MIX_W
    final = h_prev is not None
    rb = functools.partial(_row_block, n_ctx=n_ctx_rows // SEQ_T, n_lat=n_lat, reverse=reverse)

    def zspec(k):
        return pl.BlockSpec((SEQ_T, MIX_W), lambda b, j: (rb(b, j), cb + k))

    row_spec = pl.BlockSpec((SEQ_T, MIX_W), lambda b, j: (rb(b, j), 0))

    def const_spec(shape):
        return pl.BlockSpec(shape, lambda b, j: (0,) * len(shape))

    w2 = jnp.concatenate([wa, wi], axis=-1).astype(jnp.bfloat16)
    gb = jnp.concatenate([ba.reshape(LRU_BLOCKS, 1, LRU_BS), bi.reshape(LRU_BLOCKS, 1, LRU_BS)], axis=-1)
    in_specs = [zspec(0), const_spec((4, MIX_W)), const_spec((1, MIX_W)),
                const_spec((LRU_BLOCKS, LRU_BS, 2 * LRU_BS)), const_spec((LRU_BLOCKS, 1, 2 * LRU_BS)),
                const_spec((1, MIX_W))]
    args = [z, conv_w, conv_b.reshape(1, MIX_W), w2, gb, lam.reshape(1, MIX_W)]
    if final:
        in_specs += [zspec(1), row_spec]
        args += [z, h_prev]
    return pl.pallas_call(
        functools.partial(_lru_pass_kernel, reverse=reverse, final=final),
        grid=(2, n_ctx_rows // SEQ_T + n_lat),
        in_specs=in_specs,
        out_specs=row_spec,
        out_shape=jax.ShapeDtypeStruct((rows, MIX_W), jnp.bfloat16 if final else jnp.float32),
        scratch_shapes=[pltpu.VMEM((1, MIX_W), jnp.float32),
                        pltpu.VMEM((SEQ_T, MIX_W), jnp.float32),
                        pltpu.VMEM((SEQ_T, MIX_W), jnp.float32)],
        compiler_params=_cparams("parallel", "arbitrary"),
        name="lru_bwd" if reverse else "lru_fwd",
    )(*args)


def lru_branch(z, p, n_ctx_rows, n_lat_rows):
    def one(d, h_prev):
        return lru_pass(z, p['lru_conv_w'], p['lru_conv_b'], p['lru_wa'][d], p['lru_ba'][d], p['lru_wi'][d],
                        p['lru_bi'][d], p['lru_lambda'][d], n_ctx_rows, n_lat_rows, d == 1, h_prev)
    return one(1, one(0, None))


def _hg_level_matrix(reverse):
    t = np.arange(HG_C)[:, None]
    s = np.arange(HG_C)[None, :]
    x = t ^ s
    lv = np.where(x > 0, np.floor(np.log2(np.maximum(x, 1))).astype(np.int32), -1)
    lv = np.where(s < t, lv, -1).astype(np.int32)
    return jnp.asarray(lv.T if reverse else lv)


def _hg_pass_kernel(lv_ref, lb_ref, q_ref, f_ref, v_ref, *rest, reverse, final):
    if final:
        og_ref, oprev_ref, g_ref, out_ref, st_ref = rest
    else:
        out_ref, st_ref = rest

    @pl.when(pl.program_id(1) == 0)
    def _():
        st_ref[...] = jnp.zeros_like(st_ref)

    lv = lv_ref[...]
    at_level = [lv == lvl for lvl in range(HG_LEVELS)]
    n_tiles = HG_C // SUBLANES
    wb = HG_HB * HG_DK
    row = lax.broadcasted_iota(jnp.int32, (1, SUBLANES, wb), 1)
    for hb in range(HG_HEADS // HG_HB):
        sl = slice(hb * wb, (hb + 1) * wb)
        heads = [(hb * HG_HB + j, slice(j * HG_DK, (j + 1) * HG_DK)) for j in range(HG_HB)]
        x = f_ref[:, sl].astype(jnp.float32)
        lb = lb_ref[:, sl]
        e = jnp.exp(-jnp.abs(x))
        inv = 1.0 / (1.0 + e)
        pos = x >= 0
        sig = jnp.where(pos, inv, e * inv)
        sigm = jnp.where(pos, e * inv, inv)
        lsig = jnp.minimum(x, 0.0) - jnp.log(1.0 + e)
        lf = jnp.where(lb > 0, jnp.log(lb + (1.0 - lb) * sig), lsig)
        kk = (1.0 - lb) * sigm
        q = q_ref[:, sl].astype(jnp.float32)
        v = v_ref[:, sl]

        c = (lf * LOG2E).reshape(n_tiles, SUBLANES, wb)
        bt = c
        att = [None] * HG_HB
        for lvl in range(HG_LEVELS):
            m = 1 << lvl
            qm = (q * jnp.exp2(c).reshape(HG_C, wb)).astype(jnp.bfloat16)
            km = (kk * jnp.exp2(bt - c).reshape(HG_C, wb)).astype(jnp.bfloat16)
            for j, (_, hs) in enumerate(heads):
                d = lax.dot_general(qm[:, hs], km[:, hs], (((1,), (1,)), ((), ())),
                                    preferred_element_type=jnp.float32)
                att[j] = jnp.where(at_level[lvl], d, 0.0 if lvl == 0 else att[j])
            if m < SUBLANES:
                upper = (row & m) != 0
                down = pltpu.roll(bt, m, 1)
                up = pltpu.roll(bt, SUBLANES - m, 1)
                if reverse:
                    c = c + jnp.where(upper, 0.0, up)
                else:
                    c = c + jnp.where(upper, down, 0.0)
                bt = bt + jnp.where(upper, down, up)
            else:
                k = m // SUBLANES
                pair = (n_tiles // (2 * k), 2, k, SUBLANES, wb)
                c5, b5 = c.reshape(pair), bt.reshape(pair)
                tot2 = b5[:, 0] + b5[:, 1]
                if reverse:
                    c = jnp.stack([c5[:, 0] + b5[:, 1], c5[:, 1]], axis=1)
                else:
                    c = jnp.stack([c5[:, 0], c5[:, 1] + b5[:, 0]], axis=1)
                c = c.reshape(n_tiles, SUBLANES, wb)
                bt = jnp.stack([tot2, tot2], axis=1).reshape(n_tiles, SUBLANES, wb)

        qm = (q * jnp.exp2(c).reshape(HG_C, wb)).astype(jnp.bfloat16)
        km = (kk * jnp.exp2(bt - c).reshape(HG_C, wb)).astype(jnp.bfloat16)
        tot = jnp.exp2(bt[0, 0:1, :])
        diag = q * kk
        for j, (h, hs) in enumerate(heads):
            osl = slice(h * HG_DK, (h + 1) * HG_DK)
            st = st_ref[h]
            vh = v[:, hs]
            o = lax.dot_general(qm[:, hs], st.astype(jnp.bfloat16), (((1,), (1,)), ((), ())),
                                preferred_element_type=jnp.float32)
            o = o + jnp.dot(att[j].astype(jnp.bfloat16), vh, preferred_element_type=jnp.float32)
            o = o + jnp.sum(diag[:, hs], axis=-1, keepdims=True) * vh.astype(jnp.float32)
            st_ref[h] = st * tot[:, hs] + lax.dot_general(vh, km[:, hs], (((0,), (0,)), ((), ())),
                                                          preferred_element_type=jnp.float32)
            if final:
                o = o + oprev_ref[:, osl]
                y = o * lax.rsqrt(jnp.mean(o * o, axis=-1, keepdims=True) + EPS) * g_ref[:, osl]
                og = og_ref[:, osl].astype(jnp.float32)
                out_ref[:, osl] = (y * (og / (1.0 + jnp.exp(-og)))).astype(out_ref.dtype)
            else:
                out_ref[:, osl] = o


def hgrn_pass(z, lb, n_ctx_rows, n_lat_rows, reverse, o_prev=None, norm_g=None):
    rows = z.shape[0]
    n_ctx, n_lat = n_ctx_rows // HG_C, n_lat_rows // HG_C
    cb = HG_COL0 // MIX_W
    final = o_prev is not None
    rb = functools.partial(_row_block, n_ctx=n_ctx, n_lat=n_lat, reverse=reverse)

    def zspec(k):
        return pl.BlockSpec((HG_C, MIX_W), lambda b, j: (rb(b, j), cb + k))

    row_spec = pl.BlockSpec((HG_C, MIX_W), lambda b, j: (rb(b, j), 0))
    vec_spec = pl.BlockSpec((1, MIX_W), lambda b, j: (0, 0))
    in_specs = [pl.BlockSpec((HG_C, HG_C), lambda b, j: (0, 0)), vec_spec,
                zspec(0), zspec(2 if reverse else 1), zspec(3)]
    args = [_hg_level_matrix(reverse), lb.reshape(1, MIX_W), z, z, z]
    if final:
        in_specs += [zspec(4), row_spec, vec_spec]
        args += [z, o_prev, norm_g.reshape(1, MIX_W)]
    return pl.pallas_call(
        functools.partial(_hg_pass_kernel, reverse=reverse, final=final),
        grid=(2, n_ctx + n_lat),
        in_specs=in_specs,
        out_specs=row_spec,
        out_shape=jax.ShapeDtypeStruct((rows, MIX_W), jnp.bfloat16 if final else jnp.float32),
        scratch_shapes=[pltpu.VMEM((HG_HEADS, HG_DK, HG_DK), jnp.float32)],
        compiler_params=_cparams("parallel", "arbitrary"),
        name="hgrn_bwd" if reverse else "hgrn_fwd",
    )(*args)


def hgrn_branch(z, lb, norm_g, n_ctx_rows, n_lat_rows):
    o_f = hgrn_pass(z, lb, n_ctx_rows, n_lat_rows, False)
    return hgrn_pass(z, lb, n_ctx_rows, n_lat_rows, True, o_f, norm_g)


def _hy_conv_kernel(x_ref, w_ref, b_ref, out_ref, *, is_ctx):
    x = x_ref[...].astype(jnp.float32)
    t = lax.broadcasted_iota(jnp.int32, x.shape, 0)
    pos = t if is_ctx else t & (GRID_W - 1)
    last = SEQ_T - 1 if is_ctx else GRID_W - 1
    w = w_ref[...]
    y = b_ref[...] + w[1:2] * x
    y = y + w[0:1] * jnp.where(pos >= 1, pltpu.roll(x, 1, 0), 0.0)
    y = y + w[2:3] * jnp.where(pos < last, pltpu.roll(x, SEQ_T - 1, 0), 0.0)
    for g in range(3):
        out_ref[g] = y[:, g * MIX_W:(g + 1) * MIX_W].astype(out_ref.dtype)


def hy_conv(z, conv_w, conv_b, block0, n_rows, is_ctx):
    nb = n_rows // SEQ_T
    return pl.pallas_call(
        functools.partial(_hy_conv_kernel, is_ctx=is_ctx),
        grid=(2, nb),
        in_specs=[pl.BlockSpec((SEQ_T, 3 * MIX_W), lambda b, j: (block0 + b * nb + j, 0)),
                  pl.BlockSpec((3, 3 * MIX_W), lambda b, j: (0, 0)),
                  pl.BlockSpec((1, 3 * MIX_W), lambda b, j: (0, 0))],
        out_specs=pl.BlockSpec((3, None, SEQ_T, MIX_W), lambda b, j: (0, b, j, 0)),
        out_shape=jax.ShapeDtypeStruct((3, 2, n_rows, MIX_W), jnp.bfloat16),
        compiler_params=_cparams("parallel", "arbitrary"),
        name="hy_conv_ctx" if is_ctx else "hy_conv_lat",
    )(z, conv_w, conv_b.reshape(1, 3 * MIX_W))


def _hy_conv_lat_kernel(*refs):
    x_refs, (w_ref, b_ref, out_ref) = refs[:HY_PARTS], refs[HY_PARTS:]
    x = jnp.concatenate([r[...] for r in x_refs], axis=0).astype(jnp.float32)
    n = x.shape[0]
    pos = lax.broadcasted_iota(jnp.int32, x.shape, 0) & (GRID_W - 1)
    w = w_ref[...]
    y = b_ref[...] + w[1:2] * x
    y = y + w[0:1] * jnp.where(pos >= 1, pltpu.roll(x, 1, 0), 0.0)
    y = y + w[2:3] * jnp.where(pos < GRID_W - 1, pltpu.roll(x, n - 1, 0), 0.0)
    y = y.astype(out_ref.dtype).reshape(n // FFT_R, FFT_R, MIX_W)
    out_ref[...] = jnp.swapaxes(y, 0, 1)


def hy_conv_latent(z, conv_w, conv_b, row0, n_rows):
    part = FFT_SB * FFT_R // HY_PARTS
    nb = n_rows // (FFT_SB * FFT_R)

    def xspec(k):
        return pl.BlockSpec((part, MIX_W), lambda g, b, j: ((row0 + b * n_rows) // part + j * HY_PARTS + k, g))

    return pl.pallas_call(
        _hy_conv_lat_kernel,
        grid=(3, 2, nb),
        in_specs=[xspec(k) for k in range(HY_PARTS)] + [
            pl.BlockSpec((3, MIX_W), lambda g, b, j: (0, g)), pl.BlockSpec((1, MIX_W), lambda g, b, j: (0, g))],
        out_specs=pl.BlockSpec((None, None, FFT_R, FFT_SB, MIX_W), lambda g, b, j: (g, b, 0, j, 0)),
        out_shape=jax.ShapeDtypeStruct((3, 2, FFT_R, n_rows // FFT_R, MIX_W), jnp.bfloat16),
        compiler_params=_cparams("parallel", "parallel", "arbitrary"),
        name="hy_conv_lat",
    )(*([z] * HY_PARTS), conv_w, conv_b.reshape(1, 3 * MIX_W))


def _cis(num, den):
    ang = (2.0 * math.pi / den) * (num % den).astype(jnp.float32)
    return jnp.cos(ang), jnp.sin(ang)


def fft_matrices():
    r = FFT_R
    i = jnp.arange(r, dtype=jnp.int32)
    c, s = _cis(i[:, None] * i[None, :], r)
    half = r // 2
    fa_data = jnp.concatenate([jnp.concatenate([c[:, :half], s[:, :half]], axis=1),
                               jnp.concatenate([-s[:, :half], c[:, :half]], axis=1)], axis=0)
    fa_taps = jnp.concatenate([c, -s], axis=0)
    ct, st = _cis(i[:, None] * i[None, :], FFT_L)
    gc = c[None] * ct[:, None, :] - s[None] * st[:, None, :]
    gs = s[None] * ct[:, None, :] + c[None] * st[:, None, :]
    gr, gi = gc, -gs
    m1 = jnp.concatenate([jnp.concatenate([gr, -gi], axis=2), jnp.concatenate([gi, gr], axis=2)], axis=1)
    er, ei = c[:half] / FFT_L, s[:half] / FFT_L
    e2 = jnp.concatenate([jnp.concatenate([er, -ei], axis=1), jnp.concatenate([ei, er], axis=1)], axis=0)
    bf = jnp.bfloat16
    return dict(fa_data=fa_data.astype(bf), fa_taps=fa_taps.astype(bf), m1=m1.astype(bf), e2=e2.astype(bf))


def _fft_a_kernel(f_ref, x_ref, o_ref, *, two):
    f = f_ref[...]
    res = []
    for s in range(FFT_SB):
        x = jnp.concatenate([x_ref[0, s], x_ref[1, s]], axis=0) if two else x_ref[s]
        res.append(jnp.dot(f, x, preferred_element_type=jnp.float32).astype(o_ref.dtype))
    o_ref[...] = jnp.swapaxes(jnp.stack(res), 0, 1).reshape(o_ref.shape)


def fft_stage_a(fmat, x, g):
    two = x.ndim == 5
    xspec = (pl.BlockSpec((None, 2, FFT_SB, FFT_R // 2, MIX_W), lambda j: (g, 0, j, 0, 0)) if two
             else pl.BlockSpec((None, FFT_SB, FFT_R, MIX_W), lambda j: (g, j, 0, 0)))
    return pl.pallas_call(
        functools.partial(_fft_a_kernel, two=two),
        grid=(FFT_R // FFT_SB,),
        in_specs=[pl.BlockSpec((2 * FFT_R, FFT_R), lambda j: (0, 0)), xspec],
        out_specs=pl.BlockSpec((2, FFT_R, FFT_SB, MIX_W), lambda j: (0, 0, j, 0)),
        out_shape=jax.ShapeDtypeStruct((2, FFT_R, FFT_R, MIX_W), jnp.bfloat16),
        compiler_params=_cparams("parallel"),
        name="fft_stage_a",
    )(fmat, x)


def _fft_mid_kernel(m1_ref, a_ref, *rest, conv):
    if conv:
        h_ref, o_ref = rest
    else:
        s_ref, o_ref = rest
    r = FFT_R
    for k in range(FFT_KB):
        a = a_ref[:, k].reshape(2 * r, MIX_W)
        x = jnp.dot(m1_ref[k], a, preferred_element_type=jnp.float32)
        if conv:
            h = h_ref[k].astype(jnp.float32)
            xr, xi, hr, hi = x[:r], x[r:], h[:r], h[r:]
            zc = jnp.concatenate([xr * hr - xi * hi, xr * hi + xi * hr], axis=0).astype(jnp.bfloat16)
            p = lax.dot_general(m1_ref[k], zc, (((0,), (0,)), ((), ())), preferred_element_type=jnp.float32)
            o_ref[:, k] = p.reshape(2, r, MIX_W).astype(o_ref.dtype)
        else:
            o_ref[k] = (x * s_ref[...]).astype(o_ref.dtype)


def fft_mid(m1, a, h=None, scale=None):
    conv = h is not None
    r = FFT_R
    mspec = pl.BlockSpec((FFT_KB, 2 * r, 2 * r), lambda j: (j, 0, 0))
    aspec = pl.BlockSpec((2, FFT_KB, r, MIX_W), lambda j: (0, j, 0, 0))
    hspec = pl.BlockSpec((FFT_KB, 2 * r, MIX_W), lambda j: (j, 0, 0))
    if conv:
        in_specs, args = [mspec, aspec, hspec], (m1, a, h)
        out_specs, out_shape = aspec, jax.ShapeDtypeStruct((2, r, r, MIX_W), jnp.bfloat16)
    else:
        in_specs, args = [mspec, aspec, pl.BlockSpec((1, MIX_W), lambda j: (0, 0))], (m1, a, scale)
        out_specs, out_shape = hspec, jax.ShapeDtypeStruct((r, 2 * r, MIX_W), jnp.bfloat16)
    return pl.pallas_call(
        functools.partial(_fft_mid_kernel, conv=conv),
        grid=(r // FFT_KB,),
        in_specs=in_specs, out_specs=out_specs, out_shape=out_shape,
        compiler_params=_cparams("parallel"),
        name="fft_mid_conv" if conv else "fft_mid_filter",
    )(*args)


def _fft_out_kernel(e_ref, p_ref, mul_ref, add_ref, bias_ref, o_ref, *, time_order):
    e = e_ref[...]
    half = FFT_R // 2
    bias = bias_ref[...]
    p = jnp.swapaxes(p_ref[...].reshape(2 * FFT_R, FFT_SB, MIX_W), 0, 1)
    res = [[], []]
    for s in range(FFT_SB):
        y = jnp.dot(e, p[s], preferred_element_type=jnp.float32)
        for b in range(2):
            yb = y[b * half:(b + 1) * half] + add_ref[b, s].astype(jnp.float32) * bias
            res[b].append((mul_ref[b, s].astype(jnp.float32) * yb).astype(o_ref.dtype))
    for b in range(2):
        out = jnp.stack(res[b])
        o_ref[b] = jnp.swapaxes(out, 0, 1) if time_order else out


def fft_out(e2, p, mul, gm, add, ga, bias, time_order):
    half = FFT_R // 2

    def bspec(g):
        return pl.BlockSpec((None, 2, FFT_SB, half, MIX_W), lambda j: (g, 0, j, 0, 0))

    if time_order:
        out_spec = pl.BlockSpec((None, 2, half, FFT_SB, MIX_W), lambda j: (0, 0, 0, j, 0))
        out_shape = jax.ShapeDtypeStruct((1, 2, half, FFT_R, MIX_W), jnp.bfloat16)
    else:
        out_spec, out_shape = bspec(0), jax.ShapeDtypeStruct((1, 2, FFT_R, half, MIX_W), jnp.bfloat16)
    return pl.pallas_call(
        functools.partial(_fft_out_kernel, time_order=time_order),
        grid=(FFT_R // FFT_SB,),
        in_specs=[pl.BlockSpec((FFT_R, 2 * FFT_R), lambda j: (0, 0)),
                  pl.BlockSpec((2, FFT_R, FFT_SB, MIX_W), lambda j: (0, 0, j, 0)), bspec(gm), bspec(ga),
                  pl.BlockSpec((1, MIX_W), lambda j: (0, 0))],
        out_specs=out_spec,
        out_shape=out_shape,
        compiler_params=_cparams("parallel", vmem=FFT_OUT_VMEM_BYTES),
        name="fft_out",
    )(e2, p, mul, add, bias)


def hyena_spectra(taps, inv_norm, mats):
    return [fft_mid(mats['m1'], fft_stage_a(mats['fa_taps'], taps, order), scale=inv_norm[order])
            for order in range(2)]


def hyena_latent(xs, spectra, bias, mats):
    u, gu = xs, 2
    for order in range(2):
        p = fft_mid(mats['m1'], fft_stage_a(mats['fa_data'], u, gu), h=spectra[order])
        u, gu = fft_out(mats['e2'], p, xs, order, u, gu, bias[order].reshape(1, MIX_W), order == 1), 0
    return u.reshape(2, FFT_L // 2, MIX_W)


def _hy_ctx_kernel(mt_ref, mf_ref, mi_ref, xc_ref, taps_ref, s_ref, bias_ref, o_ref):
    length = mi_ref.shape[0]
    n = length // 2
    u = xc_ref[2].astype(jnp.float32).reshape(length, -1)
    for order in range(2):
        h = jnp.dot(mt_ref[...], taps_ref[order], preferred_element_type=jnp.float32) * s_ref[order]
        w = jnp.dot(mf_ref[...], u.astype(jnp.bfloat16), preferred_element_type=jnp.float32)
        wr, wi, hr, hi = w[:length], w[length:], h[:length], h[length:]
        zc = jnp.concatenate([wr * hr - wi * hi, wr * hi + wi * hr], axis=0).astype(jnp.bfloat16)
        y = jnp.dot(mi_ref[...], zc, preferred_element_type=jnp.float32)
        gate = xc_ref[order].astype(jnp.float32).reshape(length, -1)
        u = gate * (y + u * bias_ref[order])
    o_ref[...] = u.reshape(2, n, -1).astype(o_ref.dtype)


def hyena_context(xc, taps, inv_norm, bias):
    n = xc.shape[2]
    length = 2 * n
    k = jnp.arange(length, dtype=jnp.int32)
    c, s = _cis(k[:, None] * k[None, :], length)
    fr, fi = c[:, :n], -s[:, :n]
    m_fwd = jnp.concatenate([jnp.concatenate([fr, -fi], axis=1), jnp.concatenate([fi, fr], axis=1)], axis=0)
    m_taps = jnp.concatenate([c, -s], axis=0)
    er, ei = c[:n] / length, s[:n] / length
    m_inv = jnp.concatenate([jnp.concatenate([er, -ei], axis=1), jnp.concatenate([ei, er], axis=1)], axis=0)
    ct = 2 * LANES

    def whole(a):
        return pl.BlockSpec(a.shape, lambda j: (0,) * a.ndim)

    mats = [m.astype(jnp.bfloat16) for m in (m_taps, m_fwd, m_inv)]
    return pl.pallas_call(
        _hy_ctx_kernel,
        grid=(MIX_W // ct,),
        in_specs=[*(whole(m) for m in mats),
                  pl.BlockSpec((3, 2, n, ct), lambda j: (0, 0, 0, j)),
                  pl.BlockSpec((2, length, ct), lambda j: (0, 0, j)),
                  pl.BlockSpec((2, 1, ct), lambda j: (0, 0, j)),
                  pl.BlockSpec((2, 1, ct), lambda j: (0, 0, j))],
        out_specs=pl.BlockSpec((2, n, ct), lambda j: (0, 0, j)),
        out_shape=jax.ShapeDtypeStruct((2, n, MIX_W), jnp.bfloat16),
        compiler_params=_cparams("parallel"),
        name="hyena_ctx",
    )(*mats, xc, taps, inv_norm, bias.reshape(2, 1, MIX_W))


def hyena_features(n, ff1, ff1_b, freq, ff2, ff2_b):
    k = jnp.arange(n, dtype=jnp.int32)
    pos = jnp.stack([k, (n - k) % n]).astype(jnp.float32)
    t = pos / (n - 1)
    fw = (2.0 * math.pi * pos / n)[..., None] * jnp.linspace(1e-4, HY_BANDS - 1, HY_BANDS, dtype=jnp.float32)
    feats = jnp.concatenate([t[..., None], jnp.cos(fw), -jnp.sin(fw)], axis=-1)
    h = jnp.sin(freq * (feats @ ff1 + ff1_b))
    return jnp.sin(freq * (h @ ff2 + ff2_b))


def _filter_kernel(feat_ref, w_ref, delta_ref, taps_ref, sum_ref, *, n, tm, split):
    dr = pl.program_id(1)
    i = pl.program_id(2)
    row = i * tm + lax.broadcasted_iota(jnp.int32, (tm, MIX_W), 0)
    lag = jnp.where(dr == 0, row, jnp.where(row == 0, 0, n - row))
    t = lag.astype(jnp.float32) / (n - 1)
    h = jnp.dot(feat_ref[...].astype(jnp.bfloat16), w_ref[...].astype(jnp.bfloat16),
                preferred_element_type=jnp.float32)
    h = h * jnp.exp(-t * delta_ref[...])

    @pl.when((dr == 0) & (i == 0))
    def _():
        sum_ref[...] = jnp.zeros_like(sum_ref)

    sum_ref[...] += jnp.sum(jnp.abs(h), axis=0, keepdims=True)
    taps = jnp.where((dr == 0) | (row != 0), h, 0.0).astype(taps_ref.dtype)
    if split:
        taps = jnp.swapaxes(taps.reshape(tm // FFT_R, FFT_R, MIX_W), 0, 1)
    taps_ref[...] = taps


def hyena_filter_taps(feats, ff3):
    _, n, nf = feats.shape
    split = 2 * n == FFT_L
    tm = FFT_SB * FFT_R if split else n
    nb = n // tm
    w = ff3.reshape(nf, 2, 2, MIX_W).transpose(1, 2, 0, 3)
    deltas = jnp.abs(jnp.linspace(math.log(HY_TARGET) / HY_SLOW_PCT, math.log(HY_TARGET) / HY_FAST_PCT,
                                  MIX_W, dtype=jnp.float32)).reshape(1, MIX_W)
    if split:
        tspec = pl.BlockSpec((None, FFT_R, FFT_SB, MIX_W), lambda o, dr, i: (o, 0, dr * nb + i, 0))
        tshape = jax.ShapeDtypeStruct((2, FFT_R, FFT_R, MIX_W), jnp.bfloat16)
    else:
        tspec = pl.BlockSpec((None, tm, MIX_W), lambda o, dr, i: (o, dr * nb + i, 0))
        tshape = jax.ShapeDtypeStruct((2, 2 * n, MIX_W), jnp.bfloat16)
    return pl.pallas_call(
        functools.partial(_filter_kernel, n=n, tm=tm, split=split),
        grid=(2, 2, nb),
        in_specs=[pl.BlockSpec((None, tm, nf), lambda o, dr, i: (dr, i, 0)),
                  pl.BlockSpec((None, None, nf, MIX_W), lambda o, dr, i: (o, dr, 0, 0)),
                  pl.BlockSpec((1, MIX_W), lambda o, dr, i: (0, 0))],
        out_specs=[tspec, pl.BlockSpec((None, 1, MIX_W), lambda o, dr, i: (o, 0, 0))],
        out_shape=[tshape, jax.ShapeDtypeStruct((2, 1, MIX_W), jnp.float32)],
        compiler_params=_cparams("arbitrary", "arbitrary", "arbitrary"),
        name="hyena_filter",
    )(feats, w, deltas)


def _merge_out_kernel(yhc_ref, yhl_ref, yl_ref, yg_ref, wb_ref, g0_ref, g1_ref, g2_ref, wo_ref, *refs, n_ctx_tiles):
    *x_refs, n1_ref, gt_ref, n2_ref, sh_ref, sc_ref, wr_ref, xo_ref, h_ref, lg_ref = refs
    ys = (_pick_rows(yhc_ref, yhl_ref, n_ctx_tiles), yl_ref[...], yg_ref[...])
    acc = None
    for j, (y, g_ref) in enumerate(zip(ys, (g0_ref, g1_ref, g2_ref))):
        gate = 1.0 / (1.0 + jnp.exp(-g_ref[...].astype(jnp.float32)))
        t = gate * jnp.dot(y, wb_ref[j], preferred_element_type=jnp.float32)
        acc = t if acc is None else acc + t
    x = _pick_rows(*x_refs, n_ctx_tiles) if len(x_refs) == 2 else x_refs[0][...]
    y = jnp.dot(acc.astype(jnp.bfloat16), wo_ref[...], preferred_element_type=jnp.float32)
    y = y * lax.rsqrt(jnp.mean(y * y, axis=-1, keepdims=True) + EPS) * n1_ref[...]
    x = x + gt_ref[...] * y
    xo_ref[...] = x
    h = x * lax.rsqrt(jnp.mean(x * x, axis=-1, keepdims=True) + EPS) * n2_ref[...]
    h = (h * (1.0 + sc_ref[...]) + sh_ref[...]).astype(jnp.bfloat16)
    h_ref[...] = h
    lg_ref[...] = jnp.dot(h, wr_ref[...], preferred_element_type=jnp.float32)


def merge_out(z, y_hy_ctx, y_hy_lat, y_lru, y_hg, w_branch, w_out, layer, x, n1, gt, n2, sh, sc, w_router,
              n_ctx_rows, n_lat_rows):
    tm = ROW_T // 2
    xspecs, xs, rows, d = _row_operand(x, tm)
    n_ctx_tiles = 2 * n_ctx_rows // tm
    idx = functools.partial(_mod_index, n_ctx_tiles=n_ctx_tiles, tiles_per_batch=n_lat_rows // tm)
    row = pl.BlockSpec((tm, d), lambda i: (i, 0))
    yspec = pl.BlockSpec((tm, MIX_W), lambda i: (i, 0))
    vec = pl.BlockSpec((1, d), lambda i: (0, 0))
    mod = pl.BlockSpec((None, 1, d), lambda i: (idx(i), 0, 0))
    resident = pl.Buffered(1)

    def gspec(k):
        return pl.BlockSpec((tm, d), lambda i: (i, GATE_COL0 // d + k))

    wr = jnp.zeros((d, ROUTER_PAD), jnp.bfloat16).at[:, :N_EXPERTS].set(w_router.astype(jnp.bfloat16))
    return pl.pallas_call(
        functools.partial(_merge_out_kernel, n_ctx_tiles=n_ctx_tiles),
        grid=(rows // tm,),
        in_specs=[*_split_row_specs(tm, MIX_W, n_ctx_tiles), yspec, yspec,
                  pl.BlockSpec((None, N_BRANCH, MIX_W, d), lambda i: (layer, 0, 0, 0), pipeline_mode=resident),
                  gspec(0), gspec(1), gspec(2),
                  pl.BlockSpec((None, d, d), lambda i: (layer, 0, 0), pipeline_mode=resident),
                  *xspecs, vec, mod, vec, mod, mod,
                  pl.BlockSpec((d, ROUTER_PAD), lambda i: (0, 0), pipeline_mode=resident)],
        out_specs=[row, row, pl.BlockSpec((tm, ROUTER_PAD), lambda i: (i, 0))],
        out_shape=[jax.ShapeDtypeStruct((rows, d), jnp.float32), jax.ShapeDtypeStruct((rows, d), jnp.bfloat16),
                   jax.ShapeDtypeStruct((rows, ROUTER_PAD), jnp.float32)],
        compiler_params=_cparams("parallel", vmem=MERGE_OUT_VMEM_BYTES),
        name="merge_out",
    )(y_hy_ctx, y_hy_lat, y_lru, y_hg, w_branch, z, z, z, w_out, *xs, n1.reshape(1, d), gt, n2.reshape(1, d),
      sh, sc, wr)


def _ffn_up_kernel(x_ref, wg_ref, wu_ref, h_ref):
    x = x_ref[...]
    hg = jnp.dot(x, wg_ref[...].astype(jnp.bfloat16), preferred_element_type=jnp.float32)
    hu = jnp.dot(x, wu_ref[...].astype(jnp.bfloat16), preferred_element_type=jnp.float32)
    h_ref[...] = (hg / (1.0 + jnp.exp(-hg)) * hu).astype(h_ref.dtype)


def _ffn_down_kernel(h_ref, wd_ref, g_ref, o_ref):
    y = jnp.dot(h_ref[...], wd_ref[...].astype(jnp.bfloat16), preferred_element_type=jnp.float32)
    o_ref[...] = (y * g_ref[...]).astype(o_ref.dtype)


def expert_ffn(xe, w_gate, w_up, w_down, layer, g):
    ne, r, d = xe.shape
    dff = w_gate.shape[-1]
    hid = pl.pallas_call(
        _ffn_up_kernel,
        grid=(ne, dff // FFN_TF),
        in_specs=[pl.BlockSpec((None, r, d), lambda e, f: (e, 0, 0)),
                  pl.BlockSpec((None, None, d, FFN_TF), lambda e, f: (layer, e, 0, f)),
                  pl.BlockSpec((None, None, d, FFN_TF), lambda e, f: (layer, e, 0, f))],
        out_specs=pl.BlockSpec((None, r, FFN_TF), lambda e, f: (e, 0, f)),
        out_shape=jax.ShapeDtypeStruct((ne, r, dff), jnp.bfloat16),
        compiler_params=_cparams("parallel", "arbitrary"),
        name="ffn_up",
    )(xe, w_gate, w_up)
    return pl.pallas_call(
        _ffn_down_kernel,
        grid=(ne, d // FFN_TD),
        in_specs=[pl.BlockSpec((None, r, dff), lambda e, j: (e, 0, 0)),
                  pl.BlockSpec((None, None, dff, FFN_TD), lambda e, j: (layer, e, 0, j)),
                  pl.BlockSpec((None, r, 1), lambda e, j: (e, 0, 0))],
        out_specs=pl.BlockSpec((None, r, FFN_TD), lambda e, j: (e, 0, j)),
        out_shape=jax.ShapeDtypeStruct((ne, r, d), jnp.bfloat16),
        compiler_params=_cparams("parallel", "arbitrary"),
        name="ffn_down",
    )(hid, w_down, g)


def route(logits, row0, bsz, n):
    cap = EC_CAPACITY * n // N_EXPERTS
    aff = jax.nn.softmax(logits[row0:row0 + bsz * n, :N_EXPERTS].reshape(bsz, n, N_EXPERTS), axis=-1)
    g, idx = lax.top_k(jnp.swapaxes(aff, 1, 2), cap)
    flat = idx + (row0 + jnp.arange(bsz, dtype=idx.dtype) * n)[:, None, None]
    return (jnp.swapaxes(g, 0, 1).reshape(N_EXPERTS, bsz * cap),
            jnp.swapaxes(flat, 0, 1).reshape(N_EXPERTS, bsz * cap))


def ec_moe(h, logits, w_gate, w_up, w_down, layer, segments):
    gs, flats = zip(*(route(logits, *seg) for seg in segments))
    flat, g = lax.sort_key_val(jnp.concatenate(flats, axis=1), jnp.concatenate(gs, axis=1), dimension=1)
    return expert_ffn(h[flat], w_gate, w_up, w_down, layer, g[..., None]), flat


def moe_plan(flat, n_rows):
    ne, r = flat.shape
    nwin = r // MOE_WIN
    n_tiles = n_rows // MOE_TT
    bounds = jnp.arange(n_tiles + 1, dtype=flat.dtype) * MOE_TT
    p = jnp.sum(flat[None, :, :] < bounds[:, None, None], axis=2, dtype=jnp.int32)
    p0, p1 = p[:-1], p[1:]
    w0 = p0 // MOE_WIN
    nw = jnp.where(p1 > p0, (p1 - 1) // MOE_WIN - w0 + 1, 0)
    off = jnp.cumsum(nw, axis=1)
    q = jnp.arange(MOE_MAXQ, dtype=jnp.int32)
    e_of_q = jnp.minimum(jnp.sum(q[None, :, None] >= off[:, None, :], axis=2, dtype=jnp.int32), ne - 1)
    pick = e_of_q[:, :, None] == jnp.arange(ne, dtype=jnp.int32)
    w = q[None, :] + jnp.sum(jnp.where(pick, (w0 - (off - nw))[:, None, :], 0), axis=2)
    items = jnp.clip(e_of_q * nwin + w, 0, ne * nwin - 1)
    return items.astype(jnp.int32), off[:, -1].astype(jnp.int32)


def _combine_kernel(items_ref, count_ref, tok_ref, ye_ref, x_ref, g_ref, gt_ref, *rest, nxt, skip):
    if nxt:
        gn_ref, sh_ref, sc_ref, o_ref, u_ref, stage, sem, acc_ref = rest
    else:
        o_ref, stage, sem, acc_ref = rest
    i = pl.program_id(0)
    t = i + skip
    n = count_ref[t]
    n_groups = (n + MOE_GROUP - 1) // MOE_GROUP

    @pl.when(i == 0)
    def _():
        stage[...] = jnp.zeros_like(stage)

    def window_copy(tile, g, buf, k):
        item = items_ref[tile, jnp.minimum(MOE_GROUP * g + k, MOE_MAXQ - 1)]
        return pltpu.make_async_copy(ye_ref.at[pl.ds(item * MOE_WIN, MOE_WIN)],
                                     stage.at[buf, pl.ds(k * MOE_WIN, MOE_WIN)], sem.at[buf])

    def for_group(tile, g, buf, wait):
        for k in range(MOE_GROUP):
            @pl.when(MOE_GROUP * g + k < count_ref[tile])
            def _():
                cp = window_copy(tile, g, buf, k)
                cp.wait() if wait else cp.start()

    def start_first_groups(tile):
        for j in range(MOE_NBUF - 1):
            for_group(tile, j, j, False)

    acc_ref[...] = jnp.zeros_like(acc_ref)

    @pl.when(i == 0)
    def _():
        start_first_groups(t)

    row = t * MOE_TT + lax.broadcasted_iota(jnp.int32, (MOE_TT, MOE_GROUP * MOE_WIN), 0)

    def body(g, carry):
        buf = g % MOE_NBUF
        ahead = g + MOE_NBUF - 1

        @pl.when(ahead < n_groups)
        def _():
            for_group(t, ahead, ahead % MOE_NBUF, False)

        for_group(t, g, buf, True)
        toks = []
        for k in range(MOE_GROUP):
            qk = MOE_GROUP * g + k
            item = items_ref[t, jnp.minimum(qk, MOE_MAXQ - 1)]
            toks.append(jnp.where(qk < n, tok_ref[pl.ds(item, 1), :], -1))
        onehot = jnp.where(row == jnp.concatenate(toks, axis=1), 1.0, 0.0).astype(jnp.bfloat16)
        acc_ref[...] += jnp.dot(onehot, stage[buf], preferred_element_type=jnp.float32)
        return carry

    lax.fori_loop(0, n_groups, body, 0)

    @pl.when(i + 1 < pl.num_programs(0))
    def _():
        start_first_groups(t + 1)

    y = acc_ref[...]
    y = y * lax.rsqrt(jnp.mean(y * y, axis=-1, keepdims=True) + EPS) * g_ref[...]
    x = x_ref[...] + gt_ref[...] * y
    if nxt:
        u = x * lax.rsqrt(jnp.mean(x * x, axis=-1, keepdims=True) + EPS) * gn_ref[...]
        u_ref[...] = (u * (1.0 + sc_ref[...]) + sh_ref[...]).astype(u_ref.dtype)
    o_ref[...] = x


def moe_combine(x, ye, flat, g, gt, n_ctx_rows, n_lat_rows, nxt=None):
    rows, d = x.shape
    ne, r, _ = ye.shape
    items, count = moe_plan(flat, rows)
    n_ctx_tiles = 2 * n_ctx_rows // MOE_TT
    idx = functools.partial(_mod_index, n_ctx_tiles=n_ctx_tiles, tiles_per_batch=n_lat_rows // MOE_TT)
    skip = 0 if nxt else n_ctx_tiles
    row = pl.BlockSpec((MOE_TT, d), lambda i, *_: (i + skip, 0))
    vec = pl.BlockSpec((1, d), lambda i, *_: (0, 0))
    mod = pl.BlockSpec((None, 1, d), lambda i, *_: (idx(i + skip), 0, 0))
    out_row = pl.BlockSpec((MOE_TT, d), lambda i, *_: (i, 0))
    n_out = rows - skip * MOE_TT
    in_specs = [pl.BlockSpec((ne * r // MOE_WIN, MOE_WIN), lambda i, *_: (0, 0)),
                pl.BlockSpec(memory_space=pl.ANY), row, vec, mod]
    args = [flat.reshape(ne * r // MOE_WIN, MOE_WIN), ye.reshape(ne * r, d), x, g.reshape(1, d), gt]
    out_specs, out_shape = out_row, jax.ShapeDtypeStruct((n_out, d), jnp.float32)
    if nxt:
        in_specs += [vec, mod, mod]
        args += [nxt[0].reshape(1, d), nxt[1], nxt[2]]
        out_specs = [out_row, out_row]
        out_shape = [out_shape, jax.ShapeDtypeStruct((n_out, d), jnp.bfloat16)]
    return pl.pallas_call(
        functools.partial(_combine_kernel, nxt=bool(nxt), skip=skip),
        grid_spec=pltpu.PrefetchScalarGridSpec(
            num_scalar_prefetch=2,
            grid=(n_out // MOE_TT,),
            in_specs=in_specs, out_specs=out_specs,
            scratch_shapes=[pltpu.VMEM((MOE_NBUF, MOE_GROUP * MOE_WIN, d), jnp.bfloat16),
                            pltpu.SemaphoreType.DMA((MOE_NBUF,)),
                            pltpu.VMEM((MOE_TT, d), jnp.float32)]),
        out_shape=out_shape,
        compiler_params=_cparams("arbitrary"),
        name="moe_combine",
    )(items, count, *args)


def kernel(x, c, ctx, c_ctx, w_mod, b_mod, norm_g, w_in, hy_conv_w, hy_conv_b, hy_ff1, hy_ff1_b, hy_freq,
           hy_ff2, hy_ff2_b, hy_ff3, hy_bias, lru_conv_w, lru_conv_b, lru_wa, lru_ba, lru_wi, lru_bi,
           lru_lambda, hg_lb_logits, hg_norm_g, w_branch, w_out, w_router, w_gate, w_up, w_down):
    bsz, n_lat, d = x.shape
    n_ctx = ctx.shape[1]
    assert bsz == 2 and d == D_MODEL and n_ctx == SEQ_T and 2 * n_lat == FFT_L and n_lat % ROW_T == 0
    gam = jax.nn.softmax(hg_lb_logits.astype(jnp.float32), axis=0)
    lb_all = jnp.maximum(jnp.cumsum(gam, axis=0) - gam[:1], 0.0)
    mats = fft_matrices()
    cvec = jnp.zeros((SUBLANES, d), jnp.float32).at[:bsz].set(jax.nn.silu(c)).at[bsz].set(jax.nn.silu(c_ctx))
    n_ctx_all = bsz * n_ctx
    xa = (ctx.reshape(n_ctx_all, d), x.reshape(bsz * n_lat, d))
    w_branch_b, w_out_b = w_branch.astype(jnp.bfloat16), w_out.astype(jnp.bfloat16)
    mods, filters = [], []
    for l in range(DEPTH):
        mod = mm(cvec, w_mod, layer=l, name="adaln")[:bsz + 1] + b_mod[l]
        mods.append([mod.reshape(bsz + 1, 1, 6, d)[:, :, k] for k in range(6)])
        ffp = (hy_ff1[l], hy_ff1_b[l], hy_freq[l], hy_ff2[l], hy_ff2_b[l])
        taps_l, norm_l = hyena_filter_taps(hyena_features(n_lat, *ffp), hy_ff3[l])
        taps_c, norm_c = hyena_filter_taps(hyena_features(n_ctx, *ffp), hy_ff3[l])
        filters.append((hyena_spectra(taps_l, 1.0 / norm_l, mats), taps_c, 1.0 / norm_c))
    u = prenorm(xa, norm_g[0, 0], mods[0][0], mods[0][1], n_ctx, n_lat)
    for l in range(DEPTH):
        last = l == DEPTH - 1
        p = {'lru_conv_w': lru_conv_w[l], 'lru_conv_b': lru_conv_b[l], 'lru_wa': lru_wa[l], 'lru_ba': lru_ba[l],
             'lru_wi': lru_wi[l], 'lru_bi': lru_bi[l], 'lru_lambda': lru_lambda[l]}
        sh1, sc1, gt1, sh2, sc2, gt2 = mods[l]

        z = mm(u, w_in, jnp.bfloat16, tm=IN_PROJ_TM, tn=IN_PROJ_TN, layer=l, name="in_proj")

        y_lru = lru_branch(z, p, n_ctx, n_lat)
        y_hg = hgrn_branch(z, lb_all[l], hg_norm_g[l], n_ctx, n_lat)

        spectra, taps_c, inv_norm_c = filters[l]
        xc_l = hy_conv_latent(z, hy_conv_w[l], hy_conv_b[l], n_ctx_all, n_lat)
        y_hy_l = hyena_latent(xc_l, spectra, hy_bias[l], mats)
        xc_c = hy_conv(z, hy_conv_w[l], hy_conv_b[l], 0, n_ctx, True)
        y_hy_c = hyena_context(xc_c, taps_c, inv_norm_c, hy_bias[l])

        xa, h, logits = merge_out(z, y_hy_c.reshape(n_ctx_all, MIX_W), y_hy_l.reshape(bsz * n_lat, MIX_W), y_lru, y_hg,
                                  w_branch_b, w_out_b, l, xa, norm_g[l, 1], gt1, norm_g[l, 2], sh2, sc2,
                                  w_router[l], n_ctx, n_lat)
        segments = [(n_ctx_all, bsz, n_lat)] if last else [(n_ctx_all, bsz, n_lat), (0, bsz, n_ctx)]
        ye, flat = ec_moe(h, logits, w_gate, w_up, w_down, l, segments)
        if last:
            xa = moe_combine(xa, ye, flat, norm_g[l, 3], gt2, n_ctx, n_lat)
        else:
            nxt = (norm_g[l + 1, 0], mods[l + 1][0], mods[l + 1][1])
            xa, u = moe_combine(xa, ye, flat, norm_g[l, 3], gt2, n_ctx, n_lat, nxt)
    return xa.reshape(bsz, n_lat, d)
```

```python
import functools
import math

import numpy as np
import jax
import jax.numpy as jnp
from jax import lax
from jax.experimental import pallas as pl
from jax.experimental.pallas import tpu as pltpu

D_MODEL = 2048
DEPTH = 2
GRID_W = 64
MIX_W = D_MODEL // 2
N_BRANCH = 3
HY_BANDS = 16
HY_TARGET = 1e-2
HY_FAST_PCT = 0.3
HY_SLOW_PCT = 1.5
LRU_BLOCKS = 8
LRU_BS = MIX_W // LRU_BLOCKS
LRU_C = 8.0
HG_HEADS = 8
HG_DK = MIX_W // HG_HEADS
N_EXPERTS = 16
EC_CAPACITY = 2
EPS = 1e-6
TINY = 1e-30
LOG2E = 1.4426950408889634
HY_COL0 = 0
LRU_COL0 = 3 * MIX_W
HG_COL0 = 5 * MIX_W
GATE_COL0 = 10 * MIX_W
IN_COLS = GATE_COL0 + N_BRANCH * D_MODEL

SUBLANES = 8
LANES = 128
SEQ_T = 256
HG_C = 128
HG_LEVELS = 7
HG_HB = 8
ROW_T = 512
IN_PROJ_TM = 2816
IN_PROJ_TN = 512
FFT_R = 128
FFT_L = FFT_R * FFT_R
FFT_KB = 8
FFT_SB = 16
HY_PARTS = 4
ROUTER_PAD = LANES
FFN_TF = 256
FFN_TD = 512
MOE_TT = 512
MOE_WIN = 32
MOE_GROUP = 16
MOE_NBUF = 4
MOE_MAXQ = N_EXPERTS * (MOE_TT // MOE_WIN + 1)
VMEM_LIMIT_BYTES = 48 * 1024 * 1024
FFT_OUT_VMEM_BYTES = 56 * 1024 * 1024
MERGE_OUT_VMEM_BYTES = 56 * 1024 * 1024


def _cparams(*sem, vmem=VMEM_LIMIT_BYTES):
    return pltpu.CompilerParams(dimension_semantics=sem, vmem_limit_bytes=vmem)


def _pick_tile(n, pref):
    for t in (pref, 1024, 512, 256, 128):
        if t <= n and n % t == 0:
            return t
    return n


def _row_block(b, j, n_ctx, n_lat, reverse):
    if reverse:
        kc, kl = n_ctx - 1 - j, n_ctx + n_lat - 1 - j
    else:
        kc, kl = j, j - n_ctx
    return jnp.where(j < n_ctx, b * n_ctx + kc, 2 * n_ctx + b * n_lat + kl)


def _mm_kernel(a_ref, b_ref, o_ref):
    o_ref[...] = jnp.dot(a_ref[...].astype(jnp.bfloat16), b_ref[...].astype(jnp.bfloat16),
                         preferred_element_type=jnp.float32).astype(o_ref.dtype)


def mm(a, b, out_dtype=jnp.float32, tm=512, tn=1024, layer=None, name="mm"):
    m, k = a.shape
    n = b.shape[-1]
    tm = _pick_tile(m, tm)
    tn = _pick_tile(n, tn)
    if layer is None:
        bspec = pl.BlockSpec((k, tn), lambda i, j: (0, j))
    else:
        bspec = pl.BlockSpec((None, k, tn), lambda i, j: (layer, 0, j))
    return pl.pallas_call(
        _mm_kernel,
        grid=(m // tm, n // tn),
        in_specs=[pl.BlockSpec((tm, k), lambda i, j: (i, 0)), bspec],
        out_specs=pl.BlockSpec((tm, tn), lambda i, j: (i, j)),
        out_shape=jax.ShapeDtypeStruct((m, n), out_dtype),
        compiler_params=_cparams("parallel", "arbitrary"),
        name=name,
    )(a, b)


def _mod_index(i, n_ctx_tiles, tiles_per_batch):
    return jnp.where(i < n_ctx_tiles, 2, (i - n_ctx_tiles) // tiles_per_batch)


def _split_row_specs(tile, width, n_ctx_tiles):
    return (pl.BlockSpec((tile, width), lambda i: (jnp.minimum(i, n_ctx_tiles - 1), 0)),
            pl.BlockSpec((tile, width), lambda i: (jnp.maximum(i - n_ctx_tiles, 0), 0)))


def _pick_rows(ctx_ref, lat_ref, n_ctx_tiles):
    return jnp.where(pl.program_id(0) < n_ctx_tiles, ctx_ref[...], lat_ref[...])


def _row_operand(x, tile):
    if isinstance(x, tuple):
        d = x[0].shape[1]
        return list(_split_row_specs(tile, d, x[0].shape[0] // tile)), list(x), x[0].shape[0] + x[1].shape[0], d
    return [pl.BlockSpec((tile, x.shape[1]), lambda i: (i, 0))], [x], x.shape[0], x.shape[1]


def _prenorm_kernel(*refs, n_ctx_tiles):
    *x_refs, g_ref, sh_ref, sc_ref, u_ref = refs
    x = _pick_rows(*x_refs, n_ctx_tiles) if len(x_refs) == 2 else x_refs[0][...]
    y = x * lax.rsqrt(jnp.mean(x * x, axis=-1, keepdims=True) + EPS) * g_ref[...]
    u_ref[...] = (y * (1.0 + sc_ref[...]) + sh_ref[...]).astype(u_ref.dtype)


def prenorm(x, g, sh, sc, n_ctx_rows, n_lat_rows):
    xspecs, xs, rows, d = _row_operand(x, ROW_T)
    n_ctx_tiles = 2 * n_ctx_rows // ROW_T
    idx = functools.partial(_mod_index, n_ctx_tiles=n_ctx_tiles, tiles_per_batch=n_lat_rows // ROW_T)
    vec = pl.BlockSpec((None, 1, d), lambda i: (idx(i), 0, 0))
    return pl.pallas_call(
        functools.partial(_prenorm_kernel, n_ctx_tiles=n_ctx_tiles),
        grid=(rows // ROW_T,),
        in_specs=[*xspecs, pl.BlockSpec((1, d), lambda i: (0, 0)), vec, vec],
        out_specs=pl.BlockSpec((ROW_T, d), lambda i: (i, 0)),
        out_shape=jax.ShapeDtypeStruct((rows, d), jnp.bfloat16),
        compiler_params=_cparams("parallel"),
        name="prenorm",
    )(*xs, g.reshape(1, d), sh, sc)


def _gelu_tanh(x):
    return 0.5 * x * (1.0 + jnp.tanh(math.sqrt(2.0 / math.pi) * (x + 0.044715 * (x * x * x))))


def _conv_pos(is_ctx, shape):
    t = lax.broadcasted_iota(jnp.int32, shape, 0)
    pos = jnp.where(is_ctx, t, t & (GRID_W - 1))
    last = jnp.where(is_ctx, SEQ_T - 1, GRID_W - 1)
    return pos, last


def _lru_pass_kernel(x_ref, cw_ref, cb_ref, w_ref, gb_ref, lam_ref, *rest, reverse, final):
    if final:
        lg_ref, hprev_ref, out_ref, h_ref, a_ref, b_ref = rest
    else:
        out_ref, h_ref, a_ref, b_ref = rest
    j = pl.program_id(1)

    @pl.when(j == 0)
    def _():
        h_ref[...] = jnp.zeros_like(h_ref)

    x = x_ref[...].astype(jnp.float32)
    pos, last = _conv_pos(j == 0, x.shape)
    cw = cw_ref[...]
    xc = cb_ref[...] + cw[2:3] * x
    xc = xc + cw[0:1] * jnp.where(pos >= 2, pltpu.roll(x, 2, 0), 0.0)
    xc = xc + cw[1:2] * jnp.where(pos >= 1, pltpu.roll(x, 1, 0), 0.0)
    xc = xc + cw[3:4] * jnp.where(pos < last, pltpu.roll(x, SEQ_T - 1, 0), 0.0)

    xb = xc.astype(jnp.bfloat16)
    lam = lam_ref[...]
    sp = jnp.maximum(-lam, 0.0) + jnp.log(1.0 + jnp.exp(-jnp.abs(lam)))
    n_tiles = SEQ_T // SUBLANES
    row = lax.broadcasted_iota(jnp.int32, (1, SUBLANES, LRU_BS), 1)
    for n in range(LRU_BLOCKS):
        sl = slice(n * LRU_BS, (n + 1) * LRU_BS)
        pre = jnp.dot(xb[:, sl], w_ref[n], preferred_element_type=jnp.float32) + gb_ref[n]
        gate = 1.0 / (1.0 + jnp.exp(-pre))
        r, i = gate[:, :LRU_BS], gate[:, LRU_BS:]
        a = jnp.exp(-LRU_C * r * sp[:, sl])
        om = 1.0 - a * a
        b = om * lax.rsqrt(jnp.maximum(om, TINY)) * (i * xc[:, sl])
        a = a.reshape(n_tiles, SUBLANES, LRU_BS)
        b = b.reshape(n_tiles, SUBLANES, LRU_BS)
        for d in (1, 2, 4):
            if reverse:
                m = row < SUBLANES - d
                a_sh, b_sh = pltpu.roll(a, SUBLANES - d, 1), pltpu.roll(b, SUBLANES - d, 1)
            else:
                m = row >= d
                a_sh, b_sh = pltpu.roll(a, d, 1), pltpu.roll(b, d, 1)
            b = jnp.where(m, a * b_sh + b, b)
            a = jnp.where(m, a * a_sh, a)
        a_ref[:, sl] = a.reshape(SEQ_T, LRU_BS)
        b_ref[:, sl] = b.reshape(SEQ_T, LRU_BS)

    h = h_ref[...]
    for k in range(n_tiles):
        kk = n_tiles - 1 - k if reverse else k
        rs = slice(kk * SUBLANES, (kk + 1) * SUBLANES)
        ht = a_ref[rs, :] * h + b_ref[rs, :]
        h = ht[0:1, :] if reverse else ht[SUBLANES - 1:SUBLANES, :]
        if final:
            ht = (ht + hprev_ref[rs, :]) * _gelu_tanh(lg_ref[rs, :].astype(jnp.float32))
        out_ref[rs, :] = ht.astype(out_ref.dtype)
    h_ref[...] = h


def lru_pass(z, conv_w, conv_b, wa, ba, wi, bi, lam, n_ctx_rows, n_lat_rows, reverse, h_prev=None):
    rows = z.shape[0]
    n_lat = n_lat_rows // SEQ_T
    cb = LRU_COL0 // MIX_W
    final = h_prev is not None
    rb = functools.partial(_row_block, n_ctx=n_ctx_rows // SEQ_T, n_lat=n_lat, reverse=reverse)

    def zspec(k):
        return pl.BlockSpec((SEQ_T, MIX_W), lambda b, j: (rb(b, j), cb + k))

    row_spec = pl.BlockSpec((SEQ_T, MIX_W), lambda b, j: (rb(b, j), 0))

    def const_spec(shape):
        return pl.BlockSpec(shape, lambda b, j: (0,) * len(shape))

    w2 = jnp.concatenate([wa, wi], axis=-1).astype(jnp.bfloat16)
    gb = jnp.concatenate([ba.reshape(LRU_BLOCKS, 1, LRU_BS), bi.reshape(LRU_BLOCKS, 1, LRU_BS)], axis=-1)
    in_specs = [zspec(0), const_spec((4, MIX_W)), const_spec((1, MIX_W)),
                const_spec((LRU_BLOCKS, LRU_BS, 2 * LRU_BS)), const_spec((LRU_BLOCKS, 1, 2 * LRU_BS)),
                const_spec((1, MIX_W))]
    args = [z, conv_w, conv_b.reshape(1, MIX_W), w2, gb, lam.reshape(1, MIX_W)]
    if final:
        in_specs += [zspec(1), row_spec]
        args += [z, h_prev]
    return pl.pallas_call(
        functools.partial(_lru_pass_kernel, reverse=reverse, final=final),
        grid=(2, n_ctx_rows // SEQ_T + n_lat),
        in_specs=in_specs,
        out_specs=row_spec,
        out_shape=jax.ShapeDtypeStruct((rows, MIX_W), jnp.bfloat16 if final else jnp.float32),
        scratch_shapes=[pltpu.VMEM((1, MIX_W), jnp.float32),
                        pltpu.VMEM((SEQ_T, MIX_W), jnp.float32),
                        pltpu.VMEM((SEQ_T, MIX_W), jnp.float32)],
        compiler_params=_cparams("parallel", "arbitrary"),
        name="lru_bwd" if reverse else "lru_fwd",
    )(*args)


def lru_branch(z, p, n_ctx_rows, n_lat_rows):
    def one(d, h_prev):
        return lru_pass(z, p['lru_conv_w'], p['lru_conv_b'], p['lru_wa'][d], p['lru_ba'][d], p['lru_wi'][d],
                        p['lru_bi'][d], p['lru_lambda'][d], n_ctx_rows, n_lat_rows, d == 1, h_prev)
    return one(1, one(0, None))


def _hg_level_matrix(reverse):
    t = np.arange(HG_C)[:, None]
    s = np.arange(HG_C)[None, :]
    x = t ^ s
    lv = np.where(x > 0, np.floor(np.log2(np.maximum(x, 1))).astype(np.int32), -1)
    lv = np.where(s < t, lv, -1).astype(np.int32)
    return jnp.asarray(lv.T if reverse else lv)


def _hg_pass_kernel(lv_ref, lb_ref, q_ref, f_ref, v_ref, *rest, reverse, final):
    if final:
        og_ref, oprev_ref, g_ref, out_ref, st_ref = rest
    else:
        out_ref, st_ref = rest

    @pl.when(pl.program_id(1) == 0)
    def _():
        st_ref[...] = jnp.zeros_like(st_ref)

    lv = lv_ref[...]
    at_level = [lv == lvl for lvl in range(HG_LEVELS)]
    n_tiles = HG_C // SUBLANES
    wb = HG_HB * HG_DK
    row = lax.broadcasted_iota(jnp.int32, (1, SUBLANES, wb), 1)
    for hb in range(HG_HEADS // HG_HB):
        sl = slice(hb * wb, (hb + 1) * wb)
        heads = [(hb * HG_HB + j, slice(j * HG_DK, (j + 1) * HG_DK)) for j in range(HG_HB)]
        x = f_ref[:, sl].astype(jnp.float32)
        lb = lb_ref[:, sl]
        e = jnp.exp(-jnp.abs(x))
        inv = 1.0 / (1.0 + e)
        pos = x >= 0
        sig = jnp.where(pos, inv, e * inv)
        sigm = jnp.where(pos, e * inv, inv)
        lsig = jnp.minimum(x, 0.0) - jnp.log(1.0 + e)
        lf = jnp.where(lb > 0, jnp.log(lb + (1.0 - lb) * sig), lsig)
        kk = (1.0 - lb) * sigm
        q = q_ref[:, sl].astype(jnp.float32)
        v = v_ref[:, sl]

        c = (lf * LOG2E).reshape(n_tiles, SUBLANES, wb)
        bt = c
        att = [None] * HG_HB
        for lvl in range(HG_LEVELS):
            m = 1 << lvl
            qm = (q * jnp.exp2(c).reshape(HG_C, wb)).astype(jnp.bfloat16)
            km = (kk * jnp.exp2(bt - c).reshape(HG_C, wb)).astype(jnp.bfloat16)
            for j, (_, hs) in enumerate(heads):
                d = lax.dot_general(qm[:, hs], km[:, hs], (((1,), (1,)), ((), ())),
                                    preferred_element_type=jnp.float32)
                att[j] = jnp.where(at_level[lvl], d, 0.0 if lvl == 0 else att[j])
            if m < SUBLANES:
                upper = (row & m) != 0
                down = pltpu.roll(bt, m, 1)
                up = pltpu.roll(bt, SUBLANES - m, 1)
                if reverse:
                    c = c + jnp.where(upper, 0.0, up)
                else:
                    c = c + jnp.where(upper, down, 0.0)
                bt = bt + jnp.where(upper, down, up)
            else:
                k = m // SUBLANES
                pair = (n_tiles // (2 * k), 2, k, SUBLANES, wb)
                c5, b5 = c.reshape(pair), bt.reshape(pair)
                tot2 = b5[:, 0] + b5[:, 1]
                if reverse:
                    c = jnp.stack([c5[:, 0] + b5[:, 1], c5[:, 1]], axis=1)
                else:
                    c = jnp.stack([c5[:, 0], c5[:, 1] + b5[:, 0]], axis=1)
                c = c.reshape(n_tiles, SUBLANES, wb)
                bt = jnp.stack([tot2, tot2], axis=1).reshape(n_tiles, SUBLANES, wb)

        qm = (q * jnp.exp2(c).reshape(HG_C, wb)).astype(jnp.bfloat16)
        km = (kk * jnp.exp2(bt - c).reshape(HG_C, wb)).astype(jnp.bfloat16)
        tot = jnp.exp2(bt[0, 0:1, :])
        diag = q * kk
        for j, (h, hs) in enumerate(heads):
            osl = slice(h * HG_DK, (h + 1) * HG_DK)
            st = st_ref[h]
            vh = v[:, hs]
            o = lax.dot_general(qm[:, hs], st.astype(jnp.bfloat16), (((1,), (1,)), ((), ())),
                                preferred_element_type=jnp.float32)
            o = o + jnp.dot(att[j].astype(jnp.bfloat16), vh, preferred_element_type=jnp.float32)
            o = o + jnp.sum(diag[:, hs], axis=-1, keepdims=True) * vh.astype(jnp.float32)
            st_ref[h] = st * tot[:, hs] + lax.dot_general(vh, km[:, hs], (((0,), (0,)), ((), ())),
                                                          preferred_element_type=jnp.float32)
            if final:
                o = o + oprev_ref[:, osl]
                y = o * lax.rsqrt(jnp.mean(o * o, axis=-1, keepdims=True) + EPS) * g_ref[:, osl]
                og = og_ref[:, osl].astype(jnp.float32)
                out_ref[:, osl] = (y * (og / (1.0 + jnp.exp(-og)))).astype(out_ref.dtype)
            else:
                out_ref[:, osl] = o


def hgrn_pass(z, lb, n_ctx_rows, n_lat_rows, reverse, o_prev=None, norm_g=None):
    rows = z.shape[0]
    n_ctx, n_lat = n_ctx_rows // HG_C, n_lat_rows // HG_C
    cb = HG_COL0 // MIX_W
    final = o_prev is not None
    rb = functools.partial(_row_block, n_ctx=n_ctx, n_lat=n_lat, reverse=reverse)

    def zspec(k):
        return pl.BlockSpec((HG_C, MIX_W), lambda b, j: (rb(b, j), cb + k))

    row_spec = pl.BlockSpec((HG_C, MIX_W), lambda b, j: (rb(b, j), 0))
    vec_spec = pl.BlockSpec((1, MIX_W), lambda b, j: (0, 0))
    in_specs = [pl.BlockSpec((HG_C, HG_C), lambda b, j: (0, 0)), vec_spec,
                zspec(0), zspec(2 if reverse else 1), zspec(3)]
    args = [_hg_level_matrix(reverse), lb.reshape(1, MIX_W), z, z, z]
    if final:
        in_specs += [zspec(4), row_spec, vec_spec]
        args += [z, o_prev, norm_g.reshape(1, MIX_W)]
    return pl.pallas_call(
        functools.partial(_hg_pass_kernel, reverse=reverse, final=final),
        grid=(2, n_ctx + n_lat),
        in_specs=in_specs,
        out_specs=row_spec,
        out_shape=jax.ShapeDtypeStruct((rows, MIX_W), jnp.bfloat16 if final else jnp.float32),
        scratch_shapes=[pltpu.VMEM((HG_HEADS, HG_DK, HG_DK), jnp.float32)],
        compiler_params=_cparams("parallel", "arbitrary"),
        name="hgrn_bwd" if reverse else "hgrn_fwd",
    )(*args)


def hgrn_branch(z, lb, norm_g, n_ctx_rows, n_lat_rows):
    o_f = hgrn_pass(z, lb, n_ctx_rows, n_lat_rows, False)
    return hgrn_pass(z, lb, n_ctx_rows, n_lat_rows, True, o_f, norm_g)


def _hy_conv_kernel(x_ref, w_ref, b_ref, out_ref, *, is_ctx):
    x = x_ref[...].astype(jnp.float32)
    t = lax.broadcasted_iota(jnp.int32, x.shape, 0)
    pos = t if is_ctx else t & (GRID_W - 1)
    last = SEQ_T - 1 if is_ctx else GRID_W - 1
    w = w_ref[...]
    y = b_ref[...] + w[1:2] * x
    y = y + w[0:1] * jnp.where(pos >= 1, pltpu.roll(x, 1, 0), 0.0)
    y = y + w[2:3] * jnp.where(pos < last, pltpu.roll(x, SEQ_T - 1, 0), 0.0)
    for g in range(3):
        out_ref[g] = y[:, g * MIX_W:(g + 1) * MIX_W].astype(out_ref.dtype)


def hy_conv(z, conv_w, conv_b, block0, n_rows, is_ctx):
    nb = n_rows // SEQ_T
    return pl.pallas_call(
        functools.partial(_hy_conv_kernel, is_ctx=is_ctx),
        grid=(2, nb),
        in_specs=[pl.BlockSpec((SEQ_T, 3 * MIX_W), lambda b, j: (block0 + b * nb + j, 0)),
                  pl.BlockSpec((3, 3 * MIX_W), lambda b, j: (0, 0)),
                  pl.BlockSpec((1, 3 * MIX_W), lambda b, j: (0, 0))],
        out_specs=pl.BlockSpec((3, None, SEQ_T, MIX_W), lambda b, j: (0, b, j, 0)),
        out_shape=jax.ShapeDtypeStruct((3, 2, n_rows, MIX_W), jnp.bfloat16),
        compiler_params=_cparams("parallel", "arbitrary"),
        name="hy_conv_ctx" if is_ctx else "hy_conv_lat",
    )(z, conv_w, conv_b.reshape(1, 3 * MIX_W))


def _hy_conv_lat_kernel(*refs):
    x_refs, (w_ref, b_ref, out_ref) = refs[:HY_PARTS], refs[HY_PARTS:]
    x = jnp.concatenate([r[...] for r in x_refs], axis=0).astype(jnp.float32)
    n = x.shape[0]
    pos = lax.broadcasted_iota(jnp.int32, x.shape, 0) & (GRID_W - 1)
    w = w_ref[...]
    y = b_ref[...] + w[1:2] * x
    y = y + w[0:1] * jnp.where(pos >= 1, pltpu.roll(x, 1, 0), 0.0)
    y = y + w[2:3] * jnp.where(pos < GRID_W - 1, pltpu.roll(x, n - 1, 0), 0.0)
    y = y.astype(out_ref.dtype).reshape(n // FFT_R, FFT_R, MIX_W)
    out_ref[...] = jnp.swapaxes(y, 0, 1)


def hy_conv_latent(z, conv_w, conv_b, row0, n_rows):
    part = FFT_SB * FFT_R // HY_PARTS
    nb = n_rows // (FFT_SB * FFT_R)

    def xspec(k):
        return pl.BlockSpec((part, MIX_W), lambda g, b, j: ((row0 + b * n_rows) // part + j * HY_PARTS + k, g))

    return pl.pallas_call(
        _hy_conv_lat_kernel,
        grid=(3, 2, nb),
        in_specs=[xspec(k) for k in range(HY_PARTS)] + [
            pl.BlockSpec((3, MIX_W), lambda g, b, j: (0, g)), pl.BlockSpec((1, MIX_W), lambda g, b, j: (0, g))],
        out_specs=pl.BlockSpec((None, None, FFT_R, FFT_SB, MIX_W), lambda g, b, j: (g, b, 0, j, 0)),
        out_shape=jax.ShapeDtypeStruct((3, 2, FFT_R, n_rows // FFT_R, MIX_W), jnp.bfloat16),
        compiler_params=_cparams("parallel", "parallel", "arbitrary"),
        name="hy_conv_lat",
    )(*([z] * HY_PARTS), conv_w, conv_b.reshape(1, 3 * MIX_W))


def _cis(num, den):
    ang = (2.0 * math.pi / den) * (num % den).astype(jnp.float32)
    return jnp.cos(ang), jnp.sin(ang)


def fft_matrices():
    r = FFT_R
    i = jnp.arange(r, dtype=jnp.int32)
    c, s = _cis(i[:, None] * i[None, :], r)
    half = r // 2
    fa_data = jnp.concatenate([jnp.concatenate([c[:, :half], s[:, :half]], axis=1),
                               jnp.concatenate([-s[:, :half], c[:, :half]], axis=1)], axis=0)
    fa_taps = jnp.concatenate([c, -s], axis=0)
    ct, st = _cis(i[:, None] * i[None, :], FFT_L)
    gc = c[None] * ct[:, None, :] - s[None] * st[:, None, :]
    gs = s[None] * ct[:, None, :] + c[None] * st[:, None, :]
    gr, gi = gc, -gs
    m1 = jnp.concatenate([jnp.concatenate([gr, -gi], axis=2), jnp.concatenate([gi, gr], axis=2)], axis=1)
    er, ei = c[:half] / FFT_L, s[:half] / FFT_L
    e2 = jnp.concatenate([jnp.concatenate([er, -ei], axis=1), jnp.concatenate([ei, er], axis=1)], axis=0)
    bf = jnp.bfloat16
    return dict(fa_data=fa_data.astype(bf), fa_taps=fa_taps.astype(bf), m1=m1.astype(bf), e2=e2.astype(bf))


def _fft_a_kernel(f_ref, x_ref, o_ref, *, two):
    f = f_ref[...]
    res = []
    for s in range(FFT_SB):
        x = jnp.concatenate([x_ref[0, s], x_ref[1, s]], axis=0) if two else x_ref[s]
        res.append(jnp.dot(f, x, preferred_element_type=jnp.float32).astype(o_ref.dtype))
    o_ref[...] = jnp.swapaxes(jnp.stack(res), 0, 1).reshape(o_ref.shape)


def fft_stage_a(fmat, x, g):
    two = x.ndim == 5
    xspec = (pl.BlockSpec((None, 2, FFT_SB, FFT_R // 2, MIX_W), lambda j: (g, 0, j, 0, 0)) if two
             else pl.BlockSpec((None, FFT_SB, FFT_R, MIX_W), lambda j: (g, j, 0, 0)))
    return pl.pallas_call(
        functools.partial(_fft_a_kernel, two=two),
        grid=(FFT_R // FFT_SB,),
        in_specs=[pl.BlockSpec((2 * FFT_R, FFT_R), lambda j: (0, 0)), xspec],
        out_specs=pl.BlockSpec((2, FFT_R, FFT_SB, MIX_W), lambda j: (0, 0, j, 0)),
        out_shape=jax.ShapeDtypeStruct((2, FFT_R, FFT_R, MIX_W), jnp.bfloat16),
        compiler_params=_cparams("parallel"),
        name="fft_stage_a",
    )(fmat, x)


def _fft_mid_kernel(m1_ref, a_ref, *rest, conv):
    if conv:
        h_ref, o_ref = rest
    else:
        s_ref, o_ref = rest
    r = FFT_R
    for k in range(FFT_KB):
        a = a_ref[:, k].reshape(2 * r, MIX_W)
        x = jnp.dot(m1_ref[k], a, preferred_element_type=jnp.float32)
        if conv:
            h = h_ref[k].astype(jnp.float32)
            xr, xi, hr, hi = x[:r], x[r:], h[:r], h[r:]
            zc = jnp.concatenate([xr * hr - xi * hi, xr * hi + xi * hr], axis=0).astype(jnp.bfloat16)
            p = lax.dot_general(m1_ref[k], zc, (((0,), (0,)), ((), ())), preferred_element_type=jnp.float32)
            o_ref[:, k] = p.reshape(2, r, MIX_W).astype(o_ref.dtype)
        else:
            o_ref[k] = (x * s_ref[...]).astype(o_ref.dtype)


def fft_mid(m1, a, h=None, scale=None):
    conv = h is not None
    r = FFT_R
    mspec = pl.BlockSpec((FFT_KB, 2 * r, 2 * r), lambda j: (j, 0, 0))
    aspec = pl.BlockSpec((2, FFT_KB, r, MIX_W), lambda j: (0, j, 0, 0))
    hspec = pl.BlockSpec((FFT_KB, 2 * r, MIX_W), lambda j: (j, 0, 0))
    if conv:
        in_specs, args = [mspec, aspec, hspec], (m1, a, h)
        out_specs, out_shape = aspec, jax.ShapeDtypeStruct((2, r, r, MIX_W), jnp.bfloat16)
    else:
        in_specs, args = [mspec, aspec, pl.BlockSpec((1, MIX_W), lambda j: (0, 0))], (m1, a, scale)
        out_specs, out_shape = hspec, jax.ShapeDtypeStruct((r, 2 * r, MIX_W), jnp.bfloat16)
    return pl.pallas_call(
        functools.partial(_fft_mid_kernel, conv=conv),
        grid=(r // FFT_KB,),
        in_specs=in_specs, out_specs=out_specs, out_shape=out_shape,
        compiler_params=_cparams("parallel"),
        name="fft_mid_conv" if conv else "fft_mid_filter",
    )(*args)


def _fft_out_kernel(e_ref, p_ref, mul_ref, add_ref, bias_ref, o_ref, *, time_order):
    e = e_ref[...]
    half = FFT_R // 2
    bias = bias_ref[...]
    p = jnp.swapaxes(p_ref[...].reshape(2 * FFT_R, FFT_SB, MIX_W), 0, 1)
    res = [[], []]
    for s in range(FFT_SB):
        y = jnp.dot(e, p[s], preferred_element_type=jnp.float32)
        for b in range(2):
            yb = y[b * half:(b + 1) * half] + add_ref[b, s].astype(jnp.float32) * bias
            res[b].append((mul_ref[b, s].astype(jnp.float32) * yb).astype(o_ref.dtype))
    for b in range(2):
        out = jnp.stack(res[b])
        o_ref[b] = jnp.swapaxes(out, 0, 1) if time_order else out


def fft_out(e2, p, mul, gm, add, ga, bias, time_order):
    half = FFT_R // 2

    def bspec(g):
        return pl.BlockSpec((None, 2, FFT_SB, half, MIX_W), lambda j: (g, 0, j, 0, 0))

    if time_order:
        out_spec = pl.BlockSpec((None, 2, half, FFT_SB, MIX_W), lambda j: (0, 0, 0, j, 0))
        out_shape = jax.ShapeDtypeStruct((1, 2, half, FFT_R, MIX_W), jnp.bfloat16)
    else:
        out_spec, out_shape = bspec(0), jax.ShapeDtypeStruct((1, 2, FFT_R, half, MIX_W), jnp.bfloat16)
    return pl.pallas_call(
        functools.partial(_fft_out_kernel, time_order=time_order),
        grid=(FFT_R // FFT_SB,),
        in_specs=[pl.BlockSpec((FFT_R, 2 * FFT_R), lambda j: (0, 0)),
                  pl.BlockSpec((2, FFT_R, FFT_SB, MIX_W), lambda j: (0, 0, j, 0)), bspec(gm), bspec(ga),
                  pl.BlockSpec((1, MIX_W), lambda j: (0, 0))],
        out_specs=out_spec,
        out_shape=out_shape,
        compiler_params=_cparams("parallel", vmem=FFT_OUT_VMEM_BYTES),
        name="fft_out",
    )(e2, p, mul, add, bias)


def hyena_spectra(taps, inv_norm, mats):
    return [fft_mid(mats['m1'], fft_stage_a(mats['fa_taps'], taps, order), scale=inv_norm[order])
            for order in range(2)]


def hyena_latent(xs, spectra, bias, mats):
    u, gu = xs, 2
    for order in range(2):
        p = fft_mid(mats['m1'], fft_stage_a(mats['fa_data'], u, gu), h=spectra[order])
        u, gu = fft_out(mats['e2'], p, xs, order, u, gu, bias[order].reshape(1, MIX_W), order == 1), 0
    return u.reshape(2, FFT_L // 2, MIX_W)


def _hy_ctx_kernel(mt_ref, mf_ref, mi_ref, xc_ref, taps_ref, s_ref, bias_ref, o_ref):
    length = mi_ref.shape[0]
    n = length // 2
    u = xc_ref[2].astype(jnp.float32).reshape(length, -1)
    for order in range(2):
        h = jnp.dot(mt_ref[...], taps_ref[order], preferred_element_type=jnp.float32) * s_ref[order]
        w = jnp.dot(mf_ref[...], u.astype(jnp.bfloat16), preferred_element_type=jnp.float32)
        wr, wi, hr, hi = w[:length], w[length:], h[:length], h[length:]
        zc = jnp.concatenate([wr * hr - wi * hi, wr * hi + wi * hr], axis=0).astype(jnp.bfloat16)
        y = jnp.dot(mi_ref[...], zc, preferred_element_type=jnp.float32)
        gate = xc_ref[order].astype(jnp.float32).reshape(length, -1)
        u = gate * (y + u * bias_ref[order])
    o_ref[...] = u.reshape(2, n, -1).astype(o_ref.dtype)


def hyena_context(xc, taps, inv_norm, bias):
    n = xc.shape[2]
    length = 2 * n
    k = jnp.arange(length, dtype=jnp.int32)
    c, s = _cis(k[:, None] * k[None, :], length)
    fr, fi = c[:, :n], -s[:, :n]
    m_fwd = jnp.concatenate([jnp.concatenate([fr, -fi], axis=1), jnp.concatenate([fi, fr], axis=1)], axis=0)
    m_taps = jnp.concatenate([c, -s], axis=0)
    er, ei = c[:n] / length, s[:n] / length
    m_inv = jnp.concatenate([jnp.concatenate([er, -ei], axis=1), jnp.concatenate([ei, er], axis=1)], axis=0)
    ct = 2 * LANES

    def whole(a):
        return pl.BlockSpec(a.shape, lambda j: (0,) * a.ndim)

    mats = [m.astype(jnp.bfloat16) for m in (m_taps, m_fwd, m_inv)]
    return pl.pallas_call(
        _hy_ctx_kernel,
        grid=(MIX_W // ct,),
        in_specs=[*(whole(m) for m in mats),
                  pl.BlockSpec((3, 2, n, ct), lambda j: (0, 0, 0, j)),
                  pl.BlockSpec((2, length, ct), lambda j: (0, 0, j)),
                  pl.BlockSpec((2, 1, ct), lambda j: (0, 0, j)),
                  pl.BlockSpec((2, 1, ct), lambda j: (0, 0, j))],
        out_specs=pl.BlockSpec((2, n, ct), lambda j: (0, 0, j)),
        out_shape=jax.ShapeDtypeStruct((2, n, MIX_W), jnp.bfloat16),
        compiler_params=_cparams("parallel"),
        name="hyena_ctx",
    )(*mats, xc, taps, inv_norm, bias.reshape(2, 1, MIX_W))


def hyena_features(n, ff1, ff1_b, freq, ff2, ff2_b):
    k = jnp.arange(n, dtype=jnp.int32)
    pos = jnp.stack([k, (n - k) % n]).astype(jnp.float32)
    t = pos / (n - 1)
    fw = (2.0 * math.pi * pos / n)[..., None] * jnp.linspace(1e-4, HY_BANDS - 1, HY_BANDS, dtype=jnp.float32)
    feats = jnp.concatenate([t[..., None], jnp.cos(fw), -jnp.sin(fw)], axis=-1)
    h = jnp.sin(freq * (feats @ ff1 + ff1_b))
    return jnp.sin(freq * (h @ ff2 + ff2_b))


def _filter_kernel(feat_ref, w_ref, delta_ref, taps_ref, sum_ref, *, n, tm, split):
    dr = pl.program_id(1)
    i = pl.program_id(2)
    row = i * tm + lax.broadcasted_iota(jnp.int32, (tm, MIX_W), 0)
    lag = jnp.where(dr == 0, row, jnp.where(row == 0, 0, n - row))
    t = lag.astype(jnp.float32) / (n - 1)
    h = jnp.dot(feat_ref[...].astype(jnp.bfloat16), w_ref[...].astype(jnp.bfloat16),
                preferred_element_type=jnp.float32)
    h = h * jnp.exp(-t * delta_ref[...])

    @pl.when((dr == 0) & (i == 0))
    def _():
        sum_ref[...] = jnp.zeros_like(sum_ref)

    sum_ref[...] += jnp.sum(jnp.abs(h), axis=0, keepdims=True)
    taps = jnp.where((dr == 0) | (row != 0), h, 0.0).astype(taps_ref.dtype)
    if split:
        taps = jnp.swapaxes(taps.reshape(tm // FFT_R, FFT_R, MIX_W), 0, 1)
    taps_ref[...] = taps


def hyena_filter_taps(feats, ff3):
    _, n, nf = feats.shape
    split = 2 * n == FFT_L
    tm = FFT_SB * FFT_R if split else n
    nb = n // tm
    w = ff3.reshape(nf, 2, 2, MIX_W).transpose(1, 2, 0, 3)
    deltas = jnp.abs(jnp.linspace(math.log(HY_TARGET) / HY_SLOW_PCT, math.log(HY_TARGET) / HY_FAST_PCT,
                                  MIX_W, dtype=jnp.float32)).reshape(1, MIX_W)
    if split:
        tspec = pl.BlockSpec((None, FFT_R, FFT_SB, MIX_W), lambda o, dr, i: (o, 0, dr * nb + i, 0))
        tshape = jax.ShapeDtypeStruct((2, FFT_R, FFT_R, MIX_W), jnp.bfloat16)
    else:
        tspec = pl.BlockSpec((None, tm, MIX_W), lambda o, dr, i: (o, dr * nb + i, 0))
        tshape = jax.ShapeDtypeStruct((2, 2 * n, MIX_W), jnp.bfloat16)
    return pl.pallas_call(
        functools.partial(_filter_kernel, n=n, tm=tm, split=split),
        grid=(2, 2, nb),
        in_specs=[pl.BlockSpec((None, tm, nf), lambda o, dr, i: (dr, i, 0)),
                  pl.BlockSpec((None, None, nf, MIX_W), lambda o, dr, i: (o, dr, 0, 0)),
                  pl.BlockSpec((1, MIX_W), lambda o, dr, i: (0, 0))],
        out_specs=[tspec, pl.BlockSpec((None, 1, MIX_W), lambda o, dr, i: (o, 0, 0))],
        out_shape=[tshape, jax.ShapeDtypeStruct((2, 1, MIX_W), jnp.float32)],
        compiler_params=_cparams("arbitrary", "arbitrary", "arbitrary"),
        name="hyena_filter",
    )(feats, w, deltas)


def _merge_out_kernel(yhc_ref, yhl_ref, yl_ref, yg_ref, wb_ref, g0_ref, g1_ref, g2_ref, wo_ref, *refs, n_ctx_tiles):
    *x_refs, n1_ref, gt_ref, n2_ref, sh_ref, sc_ref, wr_ref, xo_ref, h_ref, lg_ref = refs
    ys = (_pick_rows(yhc_ref, yhl_ref, n_ctx_tiles), yl_ref[...], yg_ref[...])
    acc = None
    for j, (y, g_ref) in enumerate(zip(ys, (g0_ref, g1_ref, g2_ref))):
        gate = 1.0 / (1.0 + jnp.exp(-g_ref[...].astype(jnp.float32)))
        t = gate * jnp.dot(y, wb_ref[j], preferred_element_type=jnp.float32)
        acc = t if acc is None else acc + t
    x = _pick_rows(*x_refs, n_ctx_tiles) if len(x_refs) == 2 else x_refs[0][...]
    y = jnp.dot(acc.astype(jnp.bfloat16), wo_ref[...], preferred_element_type=jnp.float32)
    y = y * lax.rsqrt(jnp.mean(y * y, axis=-1, keepdims=True) + EPS) * n1_ref[...]
    x = x + gt_ref[...] * y
    xo_ref[...] = x
    h = x * lax.rsqrt(jnp.mean(x * x, axis=-1, keepdims=True) + EPS) * n2_ref[...]
    h = (h * (1.0 + sc_ref[...]) + sh_ref[...]).astype(jnp.bfloat16)
    h_ref[...] = h
    lg_ref[...] = jnp.dot(h, wr_ref[...], preferred_element_type=jnp.float32)


def merge_out(z, y_hy_ctx, y_hy_lat, y_lru, y_hg, w_branch, w_out, layer, x, n1, gt, n2, sh, sc, w_router,
              n_ctx_rows, n_lat_rows):
    tm = ROW_T // 2
    xspecs, xs, rows, d = _row_operand(x, tm)
    n_ctx_tiles = 2 * n_ctx_rows // tm
    idx = functools.partial(_mod_index, n_ctx_tiles=n_ctx_tiles, tiles_per_batch=n_lat_rows // tm)
    row = pl.BlockSpec((tm, d), lambda i: (i, 0))
    yspec = pl.BlockSpec((tm, MIX_W), lambda i: (i, 0))
    vec = pl.BlockSpec((1, d), lambda i: (0, 0))
    mod = pl.BlockSpec((None, 1, d), lambda i: (idx(i), 0, 0))
    resident = pl.Buffered(1)

    def gspec(k):
        return pl.BlockSpec((tm, d), lambda i: (i, GATE_COL0 // d + k))

    wr = jnp.zeros((d, ROUTER_PAD), jnp.bfloat16).at[:, :N_EXPERTS].set(w_router.astype(jnp.bfloat16))
    return pl.pallas_call(
        functools.partial(_merge_out_kernel, n_ctx_tiles=n_ctx_tiles),
        grid=(rows // tm,),
        in_specs=[*_split_row_specs(tm, MIX_W, n_ctx_tiles), yspec, yspec,
                  pl.BlockSpec((None, N_BRANCH, MIX_W, d), lambda i: (layer, 0, 0, 0), pipeline_mode=resident),
                  gspec(0), gspec(1), gspec(2),
                  pl.BlockSpec((None, d, d), lambda i: (layer, 0, 0), pipeline_mode=resident),
                  *xspecs, vec, mod, vec, mod, mod,
                  pl.BlockSpec((d, ROUTER_PAD), lambda i: (0, 0), pipeline_mode=resident)],
        out_specs=[row, row, pl.BlockSpec((tm, ROUTER_PAD), lambda i: (i, 0))],
        out_shape=[jax.ShapeDtypeStruct((rows, d), jnp.float32), jax.ShapeDtypeStruct((rows, d), jnp.bfloat16),
                   jax.ShapeDtypeStruct((rows, ROUTER_PAD), jnp.float32)],
        compiler_params=_cparams("parallel", vmem=MERGE_OUT_VMEM_BYTES),
        name="merge_out",
    )(y_hy_ctx, y_hy_lat, y_lru, y_hg, w_branch, z, z, z, w_out, *xs, n1.reshape(1, d), gt, n2.reshape(1, d),
      sh, sc, wr)


def _ffn_up_kernel(x_ref, wg_ref, wu_ref, h_ref):
    x = x_ref[...]
    hg = jnp.dot(x, wg_ref[...].astype(jnp.bfloat16), preferred_element_type=jnp.float32)
    hu = jnp.dot(x, wu_ref[...].astype(jnp.bfloat16), preferred_element_type=jnp.float32)
    h_ref[...] = (hg / (1.0 + jnp.exp(-hg)) * hu).astype(h_ref.dtype)


def _ffn_down_kernel(h_ref, wd_ref, g_ref, o_ref):
    y = jnp.dot(h_ref[...], wd_ref[...].astype(jnp.bfloat16), preferred_element_type=jnp.float32)
    o_ref[...] = (y * g_ref[...]).astype(o_ref.dtype)


def expert_ffn(xe, w_gate, w_up, w_down, layer, g):
    ne, r, d = xe.shape
    dff = w_gate.shape[-1]
    hid = pl.pallas_call(
        _ffn_up_kernel,
        grid=(ne, dff // FFN_TF),
        in_specs=[pl.BlockSpec((None, r, d), lambda e, f: (e, 0, 0)),
                  pl.BlockSpec((None, None, d, FFN_TF), lambda e, f: (layer, e, 0, f)),
                  pl.BlockSpec((None, None, d, FFN_TF), lambda e, f: (layer, e, 0, f))],
        out_specs=pl.BlockSpec((None, r, FFN_TF), lambda e, f: (e, 0, f)),
        out_shape=jax.ShapeDtypeStruct((ne, r, dff), jnp.bfloat16),
        compiler_params=_cparams("parallel", "arbitrary"),
        name="ffn_up",
    )(xe, w_gate, w_up)
    return pl.pallas_call(
        _ffn_down_kernel,
        grid=(ne, d // FFN_TD),
        in_specs=[pl.BlockSpec((None, r, dff), lambda e, j: (e, 0, 0)),
                  pl.BlockSpec((None, None, dff, FFN_TD), lambda e, j: (layer, e, 0, j)),
                  pl.BlockSpec((None, r, 1), lambda e, j: (e, 0, 0))],
        out_specs=pl.BlockSpec((None, r, FFN_TD), lambda e, j: (e, 0, j)),
        out_shape=jax.ShapeDtypeStruct((ne, r, d), jnp.bfloat16),
        compiler_params=_cparams("parallel", "arbitrary"),
        name="ffn_down",
    )(hid, w_down, g)


def route(logits, row0, bsz, n):
    cap = EC_CAPACITY * n // N_EXPERTS
    aff = jax.nn.softmax(logits[row0:row0 + bsz * n, :N_EXPERTS].reshape(bsz, n, N_EXPERTS), axis=-1)
    g, idx = lax.top_k(jnp.swapaxes(aff, 1, 2), cap)
    flat = idx + (row0 + jnp.arange(bsz, dtype=idx.dtype) * n)[:, None, None]
    return (jnp.swapaxes(g, 0, 1).reshape(N_EXPERTS, bsz * cap),
            jnp.swapaxes(flat, 0, 1).reshape(N_EXPERTS, bsz * cap))


def ec_moe(h, logits, w_gate, w_up, w_down, layer, segments):
    gs, flats = zip(*(route(logits, *seg) for seg in segments))
    flat, g = lax.sort_key_val(jnp.concatenate(flats, axis=1), jnp.concatenate(gs, axis=1), dimension=1)
    return expert_ffn(h[flat], w_gate, w_up, w_down, layer, g[..., None]), flat


def moe_plan(flat, n_rows):
    ne, r = flat.shape
    nwin = r // MOE_WIN
    n_tiles = n_rows // MOE_TT
    bounds = jnp.arange(n_tiles + 1, dtype=flat.dtype) * MOE_TT
    p = jnp.sum(flat[None, :, :] < bounds[:, None, None], axis=2, dtype=jnp.int32)
    p0, p1 = p[:-1], p[1:]
    w0 = p0 // MOE_WIN
    nw = jnp.where(p1 > p0, (p1 - 1) // MOE_WIN - w0 + 1, 0)
    off = jnp.cumsum(nw, axis=1)
    q = jnp.arange(MOE_MAXQ, dtype=jnp.int32)
    e_of_q = jnp.minimum(jnp.sum(q[None, :, None] >= off[:, None, :], axis=2, dtype=jnp.int32), ne - 1)
    pick = e_of_q[:, :, None] == jnp.arange(ne, dtype=jnp.int32)
    w = q[None, :] + jnp.sum(jnp.where(pick, (w0 - (off - nw))[:, None, :], 0), axis=2)
    items = jnp.clip(e_of_q * nwin + w, 0, ne * nwin - 1)
    return items.astype(jnp.int32), off[:, -1].astype(jnp.int32)


def _combine_kernel(items_ref, count_ref, tok_ref, ye_ref, x_ref, g_ref, gt_ref, *rest, nxt, skip):
    if nxt:
        gn_ref, sh_ref, sc_ref, o_ref, u_ref, stage, sem, acc_ref = rest
    else:
        o_ref, stage, sem, acc_ref = rest
    i = pl.program_id(0)
    t = i + skip
    n = count_ref[t]
    n_groups = (n + MOE_GROUP - 1) // MOE_GROUP

    @pl.when(i == 0)
    def _():
        stage[...] = jnp.zeros_like(stage)

    def window_copy(tile, g, buf, k):
        item = items_ref[tile, jnp.minimum(MOE_GROUP * g + k, MOE_MAXQ - 1)]
        return pltpu.make_async_copy(ye_ref.at[pl.ds(item * MOE_WIN, MOE_WIN)],
                                     stage.at[buf, pl.ds(k * MOE_WIN, MOE_WIN)], sem.at[buf])

    def for_group(tile, g, buf, wait):
        for k in range(MOE_GROUP):
            @pl.when(MOE_GROUP * g + k < count_ref[tile])
            def _():
                cp = window_copy(tile, g, buf, k)
                cp.wait() if wait else cp.start(priority=k % 2)

    def start_first_groups(tile):
        for j in range(MOE_NBUF - 1):
            for_group(tile, j, j, False)

    acc_ref[...] = jnp.zeros_like(acc_ref)

    @pl.when(i == 0)
    def _():
        start_first_groups(t)

    row = t * MOE_TT + lax.broadcasted_iota(jnp.int32, (MOE_TT, MOE_GROUP * MOE_WIN), 0)

    def body(g, carry):
        buf = g % MOE_NBUF
        ahead = g + MOE_NBUF - 1

        @pl.when(ahead < n_groups)
        def _():
            for_group(t, ahead, ahead % MOE_NBUF, False)

        for_group(t, g, buf, True)
        toks = []
        for k in range(MOE_GROUP):
            qk = MOE_GROUP * g + k
            item = items_ref[t, jnp.minimum(qk, MOE_MAXQ - 1)]
            toks.append(jnp.where(qk < n, tok_ref[pl.ds(item, 1), :], -1))
        onehot = jnp.where(row == jnp.concatenate(toks, axis=1), 1.0, 0.0).astype(jnp.bfloat16)
        acc_ref[...] += jnp.dot(onehot, stage[buf], preferred_element_type=jnp.float32)
        return carry

    lax.fori_loop(0, n_groups, body, 0)

    @pl.when(i + 1 < pl.num_programs(0))
    def _():
        start_first_groups(t + 1)

    y = acc_ref[...]
    y = y * lax.rsqrt(jnp.mean(y * y, axis=-1, keepdims=True) + EPS) * g_ref[...]
    x = x_ref[...] + gt_ref[...] * y
    if nxt:
        u = x * lax.rsqrt(jnp.mean(x * x, axis=-1, keepdims=True) + EPS) * gn_ref[...]
        u_ref[...] = (u * (1.0 + sc_ref[...]) + sh_ref[...]).astype(u_ref.dtype)
    o_ref[...] = x


def moe_combine(x, ye, flat, g, gt, n_ctx_rows, n_lat_rows, nxt=None):
    rows, d = x.shape
    ne, r, _ = ye.shape
    items, count = moe_plan(flat, rows)
    n_ctx_tiles = 2 * n_ctx_rows // MOE_TT
    idx = functools.partial(_mod_index, n_ctx_tiles=n_ctx_tiles, tiles_per_batch=n_lat_rows // MOE_TT)
    skip = 0 if nxt else n_ctx_tiles
    row = pl.BlockSpec((MOE_TT, d), lambda i, *_: (i + skip, 0))
    vec = pl.BlockSpec((1, d), lambda i, *_: (0, 0))
    mod = pl.BlockSpec((None, 1, d), lambda i, *_: (idx(i + skip), 0, 0))
    out_row = pl.BlockSpec((MOE_TT, d), lambda i, *_: (i, 0))
    n_out = rows - skip * MOE_TT
    in_specs = [pl.BlockSpec((ne * r // MOE_WIN, MOE_WIN), lambda i, *_: (0, 0)),
                pl.BlockSpec(memory_space=pl.ANY), row, vec, mod]
    args = [flat.reshape(ne * r // MOE_WIN, MOE_WIN), ye.reshape(ne * r, d), x, g.reshape(1, d), gt]
    out_specs, out_shape = out_row, jax.ShapeDtypeStruct((n_out, d), jnp.float32)
    if nxt:
        in_specs += [vec, mod, mod]
        args += [nxt[0].reshape(1, d), nxt[1], nxt[2]]
        out_specs = [out_row, out_row]
        out_shape = [out_shape, jax.ShapeDtypeStruct((n_out, d), jnp.bfloat16)]
    return pl.pallas_call(
        functools.partial(_combine_kernel, nxt=bool(nxt), skip=skip),
        grid_spec=pltpu.PrefetchScalarGridSpec(
            num_scalar_prefetch=2,
            grid=(n_out // MOE_TT,),
            in_specs=in_specs, out_specs=out_specs,
            scratch_shapes=[pltpu.VMEM((MOE_NBUF, MOE_GROUP * MOE_WIN, d), jnp.bfloat16),
                            pltpu.SemaphoreType.DMA((MOE_NBUF,)),
                            pltpu.VMEM((MOE_TT, d), jnp.float32)]),
        out_shape=out_shape,
        compiler_params=_cparams("arbitrary"),
        name="moe_combine",
    )(items, count, *args)


def kernel(x, c, ctx, c_ctx, w_mod, b_mod, norm_g, w_in, hy_conv_w, hy_conv_b, hy_ff1, hy_ff1_b, hy_freq,
           hy_ff2, hy_ff2_b, hy_ff3, hy_bias, lru_conv_w, lru_conv_b, lru_wa, lru_ba, lru_wi, lru_bi,
           lru_lambda, hg_lb_logits, hg_norm_g, w_branch, w_out, w_router, w_gate, w_up, w_down):
    bsz, n_lat, d = x.shape
    n_ctx = ctx.shape[1]
    assert bsz == 2 and d == D_MODEL and n_ctx == SEQ_T and 2 * n_lat == FFT_L and n_lat % ROW_T == 0
    gam = jax.nn.softmax(hg_lb_logits.astype(jnp.float32), axis=0)
    lb_all = jnp.maximum(jnp.cumsum(gam, axis=0) - gam[:1], 0.0)
    mats = fft_matrices()
    cvec = jnp.zeros((SUBLANES, d), jnp.float32).at[:bsz].set(jax.nn.silu(c)).at[bsz].set(jax.nn.silu(c_ctx))
    n_ctx_all = bsz * n_ctx
    xa = (ctx.reshape(n_ctx_all, d), x.reshape(bsz * n_lat, d))
    w_branch_b, w_out_b = w_branch.astype(jnp.bfloat16), w_out.astype(jnp.bfloat16)
    mods, filters = [], []
    for l in range(DEPTH):
        mod = mm(cvec, w_mod, layer=l, name="adaln")[:bsz + 1] + b_mod[l]
        mods.append([mod.reshape(bsz + 1, 1, 6, d)[:, :, k] for k in range(6)])
        ffp = (hy_ff1[l], hy_ff1_b[l], hy_freq[l], hy_ff2[l], hy_ff2_b[l])
        taps_l, norm_l = hyena_filter_taps(hyena_features(n_lat, *ffp), hy_ff3[l])
        taps_c, norm_c = hyena_filter_taps(hyena_features(n_ctx, *ffp), hy_ff3[l])
        filters.append((hyena_spectra(taps_l, 1.0 / norm_l, mats), taps_c, 1.0 / norm_c))
    u = prenorm(xa, norm_g[0, 0], mods[0][0], mods[0][1], n_ctx, n_lat)
    for l in range(DEPTH):
        last = l == DEPTH - 1
        p = {'lru_conv_w': lru_conv_w[l], 'lru_conv_b': lru_conv_b[l], 'lru_wa': lru_wa[l], 'lru_ba': lru_ba[l],
             'lru_wi': lru_wi[l], 'lru_bi': lru_bi[l], 'lru_lambda': lru_lambda[l]}
        sh1, sc1, gt1, sh2, sc2, gt2 = mods[l]

        z = mm(u, w_in, jnp.bfloat16, tm=IN_PROJ_TM, tn=IN_PROJ_TN, layer=l, name="in_proj")

        y_lru = lru_branch(z, p, n_ctx, n_lat)
        y_hg = hgrn_branch(z, lb_all[l], hg_norm_g[l], n_ctx, n_lat)

        spectra, taps_c, inv_norm_c = filters[l]
        xc_l = hy_conv_latent(z, hy_conv_w[l], hy_conv_b[l], n_ctx_all, n_lat)
        y_hy_l = hyena_latent(xc_l, spectra, hy_bias[l], mats)
        xc_c = hy_conv(z, hy_conv_w[l], hy_conv_b[l], 0, n_ctx, True)
        y_hy_c = hyena_context(xc_c, taps_c, inv_norm_c, hy_bias[l])

        xa, h, logits = merge_out(z, y_hy_c.reshape(n_ctx_all, MIX_W), y_hy_l.reshape(bsz * n_lat, MIX_W), y_lru, y_hg,
                                  w_branch_b, w_out_b, l, xa, norm_g[l, 1], gt1, norm_g[l, 2], sh2, sc2,
                                  w_router[l], n_ctx, n_lat)
        segments = [(n_ctx_all, bsz, n_lat)] if last else [(n_ctx_all, bsz, n_lat), (0, bsz, n_ctx)]
        ye, flat = ec_moe(h, logits, w_gate, w_up, w_down, l, segments)
        if last:
            xa = moe_combine(xa, ye, flat, norm_g[l, 3], gt2, n_ctx, n_lat)
        else:
            nxt = (norm_g[l + 1, 0], mods[l + 1][0], mods[l + 1][1])
            xa, u = moe_combine(xa, ye, flat, norm_g[l, 3], gt2, n_ctx, n_lat, nxt)
    return xa.reshape(bsz, n_lat, d)
```
